```python
import jax, jax.numpy as jnp
from jax import lax
import numpy as np

D_MODEL = 1024
BATCH = 2
SEQ = 8192
DEPTH = 2

ROPE_THETA = 10000.0
NORM_EPS = 1e-6
GN_EPS = 1e-5

RET_HEADS = 4
RET_DK = 64
RET_DV = 128
RET_CHUNK = 128

NSA_HEADS = 8
NSA_KV_HEADS = 2
NSA_DK = 64
NSA_GROUP = NSA_HEADS // NSA_KV_HEADS
CMP_BLOCK = 32
CMP_STRIDE = 16
CMP_HID = 256
SLC_BLOCK = 64
SLC_TOPK = 16
SLC_LOCAL = 2
WINDOW = 512
NSA_QBLOCK = 128
SEL_BIG = 1e9
NEG = -1e30

D_RET = RET_HEADS * RET_DV
D_NSA = NSA_HEADS * NSA_DK
D_MIX = D_RET + D_NSA

RET_QK = RET_HEADS * RET_DK
NSA_KV = NSA_KV_HEADS * NSA_DK
NSA_GATES = 3 * NSA_HEADS
SPLITS = [RET_QK, RET_QK, D_RET, D_RET, D_NSA] + [NSA_KV] * 6 + [NSA_GATES]
IN_COLS = sum(SPLITS)

D_FF = 2816
CONV_WIDTH = 3

kernel_name = 'hybrid_retention_nsa_convffn_sandwich'


def rms_norm(x, w):
    xf = x.astype(jnp.float32)
    y = xf * lax.rsqrt(jnp.mean(xf * xf, axis=-1, keepdims=True) + NORM_EPS)
    return (y * w.astype(jnp.float32)).astype(x.dtype)


def rope_angles(pos, dim):
    inv = 1.0 / (ROPE_THETA ** (jnp.arange(0, dim, 2, dtype=jnp.float32) / dim))
    ang = pos.astype(jnp.float32)[:, None] * inv[None, :]
    return jnp.cos(ang), jnp.sin(ang)


def apply_rope(x, cos, sin):
    x1, x2 = jnp.split(x, 2, axis=-1)
    c = cos[:, None, :]
    s = sin[:, None, :]
    return jnp.concatenate([x1 * c - x2 * s, x1 * s + x2 * c], axis=-1)


def masked_softmax(s, mask):
    s = jnp.where(mask, s, NEG)
    m = jnp.max(s, axis=-1, keepdims=True)
    e = jnp.where(mask, jnp.exp(s - m), 0.0)
    return e / jnp.maximum(jnp.sum(e, axis=-1, keepdims=True), 1e-30)


def retention(q, k, v, g, gn_w, cos, sin):
    f32 = jnp.float32
    B, S = q.shape[0], q.shape[1]
    H, C = RET_HEADS, RET_CHUNK
    N = S // C
    q = apply_rope(q.astype(f32), cos, sin)
    k = apply_rope(k.astype(f32), cos, sin) * (RET_DK ** -0.5)
    v = v.astype(f32).reshape(B, S, H, RET_DV)
    qc = q.reshape(B, N, C, H, RET_DK).transpose(0, 3, 1, 2, 4)
    kc = k.reshape(B, N, C, H, RET_DK).transpose(0, 3, 1, 2, 4)
    vc = v.reshape(B, N, C, H, RET_DV).transpose(0, 3, 1, 2, 4)
    log_gamma = jnp.log(1.0 - 2.0 ** (-5.0 - jnp.arange(H, dtype=f32)))
    i = jnp.arange(C, dtype=f32)
    diff = i[:, None] - i[None, :]
    decay = jnp.where(diff >= 0, jnp.exp(log_gamma[:, None, None] * jnp.maximum(diff, 0.0)), 0.0)
    scores = jnp.einsum('bhnid,bhnjd->bhnij', qc, kc) * decay[None, :, None]
    o_inner = jnp.einsum('bhnij,bhnje->bhnie', scores, vc)
    zeta = jnp.exp(log_gamma[:, None] * (C - 1.0 - i)[None, :])
    xi = jnp.exp(log_gamma[:, None] * (i + 1.0)[None, :])
    kv = jnp.einsum('bhnjd,bhnje->bhnde', kc * zeta[None, :, None, :, None], vc)
    chunk_decay = jnp.exp(log_gamma * C)[None, :, None, None]

    def step(state, kv_n):
        return state * chunk_decay + kv_n, state

    _, prev = lax.scan(step, jnp.zeros((B, H, RET_DK, RET_DV), f32), kv.transpose(2, 0, 1, 3, 4))
    prev = prev.transpose(1, 2, 0, 3, 4)
    o_cross = jnp.einsum('bhnid,bhnde->bhnie', qc, prev) * xi[None, :, None, :, None]
    o = (o_inner + o_cross).transpose(0, 2, 3, 1, 4).reshape(B, S, H, RET_DV)
    mu = jnp.mean(o, axis=-1, keepdims=True)
    var = jnp.mean(jnp.square(o - mu), axis=-1, keepdims=True)
    o = ((o - mu) * lax.rsqrt(var + GN_EPS)).reshape(B, S, D_RET) * gn_w.astype(f32)
    return jax.nn.silu(g.astype(f32)) * o


def compress(kv, pos_emb, w1, w2):
    B, S, G, D = kv.shape
    n_cmp = (S - CMP_BLOCK) // CMP_STRIDE + 1
    idx = (np.arange(n_cmp)[:, None] * CMP_STRIDE + np.arange(CMP_BLOCK)[None, :]).astype(np.int32)
    blocks = kv[:, idx] + pos_emb.astype(jnp.float32)[None, None, :, None, :]
    flat = blocks.transpose(0, 1, 3, 2, 4).reshape(B, n_cmp, G, CMP_BLOCK * D)
    hid = jax.nn.gelu(flat @ w1.astype(jnp.float32))
    return hid @ w2.astype(jnp.float32)


def nsa(q, k_cmp, v_cmp, k_slc, v_slc, k_win, v_win, gate_logits,
        cmp_k_pos, cmp_k_w1, cmp_k_w2, cmp_v_pos, cmp_v_w1, cmp_v_w2, cos, sin):
    f32 = jnp.float32
    B, S = q.shape[0], q.shape[1]
    G, R, D, QB = NSA_KV_HEADS, NSA_GROUP, NSA_DK, NSA_QBLOCK
    q = apply_rope(q.astype(f32), cos, sin) * (D ** -0.5)
    kc = compress(k_cmp.astype(f32), cmp_k_pos, cmp_k_w1, cmp_k_w2)
    vc = compress(v_cmp.astype(f32), cmp_v_pos, cmp_v_w1, cmp_v_w2)
    n_cmp = kc.shape[1]
    cmp_start = np.arange(n_cmp) * CMP_STRIDE
    cmp_end = (cmp_start + CMP_BLOCK - 1).astype(np.int32)
    ccos, csin = rope_angles(jnp.asarray(cmp_end), D)
    kc = apply_rope(kc, ccos, csin)
    ks = apply_rope(k_slc.astype(f32), cos, sin)
    kw = apply_rope(k_win.astype(f32), cos, sin)
    n_slc = S // SLC_BLOCK
    topk = min(SLC_TOPK, n_slc)
    slc_start = np.arange(n_slc) * SLC_BLOCK
    overlap = ((cmp_start[:, None] < slc_start[None, :] + SLC_BLOCK)
               & (cmp_start[:, None] + CMP_BLOCK > slc_start[None, :])).astype(np.float32)

    qg = q.reshape(B, S, G, R, D).transpose(0, 2, 3, 1, 4)
    kc_g = kc.transpose(0, 2, 1, 3)
    vc_g = vc.transpose(0, 2, 1, 3)
    ks_blk = ks.reshape(B, n_slc, SLC_BLOCK, G, D).transpose(0, 3, 1, 2, 4)
    vs_blk = v_slc.astype(f32).reshape(B, n_slc, SLC_BLOCK, G, D).transpose(0, 3, 1, 2, 4)
    pad = ((0, 0), (0, 0), (WINDOW, 0), (0, 0))
    kw_pad = jnp.pad(kw.transpose(0, 2, 1, 3), pad)
    vw_pad = jnp.pad(v_win.astype(f32).transpose(0, 2, 1, 3), pad)
    nqb = S // QB
    q_blocks = qg.reshape(B, G, R, nqb, QB, D).transpose(3, 0, 1, 2, 4, 5)
    bi = jnp.arange(B)[:, None, None, None]
    gi = jnp.arange(G)[None, :, None, None]
    jsl = jnp.arange(n_slc, dtype=jnp.int32)

    def block_fn(args):
        qb, blk = args
        t = blk * QB + jnp.arange(QB, dtype=jnp.int32)
        valid_c = jnp.asarray(cmp_end)[None, :] <= t[:, None]
        p_cmp = masked_softmax(jnp.einsum('bgrtd,bgnd->bgrtn', qb, kc_g), valid_c)
        o_cmp = jnp.einsum('bgrtn,bgnd->bgrtd', p_cmp, vc_g)
        imp = jnp.einsum('bgrtn,nj->bgtj', p_cmp, overlap)
        qblk = t // SLC_BLOCK
        blk_valid = jsl[None, :] <= qblk[:, None]
        forced = (jsl[None, :] == 0) | (blk_valid & (jsl[None, :] > qblk[:, None] - SLC_LOCAL))
        score = jnp.where(forced, SEL_BIG, jnp.where(blk_valid, imp, -SEL_BIG))
        _, sel = lax.top_k(score, topk)
        kg = ks_blk[bi, gi, sel]
        vg = vs_blk[bi, gi, sel]
        tok = sel[..., None] * SLC_BLOCK + jnp.arange(SLC_BLOCK, dtype=jnp.int32)
        smask = (tok <= t[None, None, :, None, None]).reshape(B, G, 1, QB, topk * SLC_BLOCK)
        s_slc = jnp.einsum('bgrtd,bgtksd->bgrtks', qb, kg).reshape(B, G, R, QB, topk * SLC_BLOCK)
        p_slc = masked_softmax(s_slc, smask).reshape(B, G, R, QB, topk, SLC_BLOCK)
        o_slc = jnp.einsum('bgrtks,bgtksd->bgrtd', p_slc, vg)
        kwb = lax.dynamic_slice_in_dim(kw_pad, blk * QB, WINDOW + QB, axis=2)
        vwb = lax.dynamic_slice_in_dim(vw_pad, blk * QB, WINDOW + QB, axis=2)
        kpos = blk * QB - WINDOW + jnp.arange(WINDOW + QB, dtype=jnp.int32)
        wmask = (kpos[None, :] <= t[:, None]) & (kpos[None, :] > t[:, None] - WINDOW) & (kpos[None, :] >= 0)
        p_win = masked_softmax(jnp.einsum('bgrtd,bgsd->bgrts', qb, kwb), wmask)
        o_win = jnp.einsum('bgrts,bgsd->bgrtd', p_win, vwb)
        return o_cmp, o_slc, o_win

    o_cmp, o_slc, o_win = lax.map(block_fn, (q_blocks, jnp.arange(nqb, dtype=jnp.int32)))

    def to_bshd(o):
        return o.transpose(1, 0, 4, 2, 3, 5).reshape(B, S, NSA_HEADS, D)

    gates = jax.nn.sigmoid(gate_logits.astype(f32)).reshape(B, S, 3, NSA_HEADS)[..., None]
    out = gates[:, :, 0] * to_bshd(o_cmp) + gates[:, :, 1] * to_bshd(o_slc) + gates[:, :, 2] * to_bshd(o_win)
    return out.reshape(B, S, D_NSA)


def conv_ffn(x, w_up, conv_w, w_down):
    u = x @ w_up
    c = u.shape[-1]
    u = lax.conv_general_dilated(u, conv_w[:, None, :].astype(u.dtype), window_strides=(1,),
                                 padding=[(CONV_WIDTH - 1, 0)],
                                 dimension_numbers=('NWC', 'WIO', 'NWC'),
                                 feature_group_count=c)
    gate, val = jnp.split(u, 2, axis=-1)
    return (jax.nn.gelu(gate) * val) @ w_down


def setup_inputs(seed: int = 0) -> dict:
    key = jax.random.key(seed)
    ks = jax.random.split(key, 20)
    L = DEPTH

    def nrm(k, shape, scale):
        return jax.random.normal(k, shape, jnp.float32) * scale

    def gain(k, shape):
        return 1.0 + 0.02 * jax.random.normal(k, shape, jnp.float32)

    return {
        'x': nrm(ks[0], (BATCH, SEQ, D_MODEL), 1.0),
        'norm_mix_pre': gain(ks[1], (L, D_MODEL)),
        'w_in': nrm(ks[2], (L, D_MODEL, IN_COLS), D_MODEL ** -0.5),
        'ret_gn_w': gain(ks[3], (L, D_RET)),
        'cmp_k_pos': nrm(ks[4], (L, CMP_BLOCK, NSA_DK), 0.02),
        'cmp_k_w1': nrm(ks[5], (L, CMP_BLOCK * NSA_DK, CMP_HID), (CMP_BLOCK * NSA_DK) ** -0.5),
        'cmp_k_w2': nrm(ks[6], (L, CMP_HID, NSA_DK), CMP_HID ** -0.5),
        'cmp_v_pos': nrm(ks[7], (L, CMP_BLOCK, NSA_DK), 0.02),
        'cmp_v_w1': nrm(ks[8], (L, CMP_BLOCK * NSA_DK, CMP_HID), (CMP_BLOCK * NSA_DK) ** -0.5),
        'cmp_v_w2': nrm(ks[9], (L, CMP_HID, NSA_DK), CMP_HID ** -0.5),
        'w_out': nrm(ks[10], (L, D_MIX, D_MODEL), D_MIX ** -0.5),
        'norm_mix_post': gain(ks[11], (L, D_MODEL)),
        'norm_ffn_pre': gain(ks[12], (L, D_MODEL)),
        'ffn_w_up': nrm(ks[13], (L, D_MODEL, 2 * D_FF), D_MODEL ** -0.5),
        'ffn_conv': nrm(ks[14], (L, CONV_WIDTH, 2 * D_FF), CONV_WIDTH ** -0.5),
        'ffn_w_down': nrm(ks[15], (L, D_FF, D_MODEL), D_FF ** -0.5),
        'norm_ffn_post': gain(ks[16], (L, D_MODEL)),
    }


def reference(x, norm_mix_pre, w_in, ret_gn_w, cmp_k_pos, cmp_k_w1, cmp_k_w2,
              cmp_v_pos, cmp_v_w1, cmp_v_w2, w_out, norm_mix_post, norm_ffn_pre,
              ffn_w_up, ffn_conv, ffn_w_down, norm_ffn_post):
    B, S, _ = x.shape
    cos, sin = rope_angles(jnp.arange(S, dtype=jnp.int32), NSA_DK)
    split_at = [int(v) for v in np.cumsum(SPLITS)[:-1]]
    for l in range(DEPTH):
        h = rms_norm(x, norm_mix_pre[l])
        proj = h @ w_in[l]
        (rq, rk, rv, rg, nq, kcm, vcm, ksl, vsl, kwn, vwn, ng) = jnp.split(proj, split_at, axis=-1)
        y_ret = retention(rq.reshape(B, S, RET_HEADS, RET_DK), rk.reshape(B, S, RET_HEADS, RET_DK),
                          rv, rg, ret_gn_w[l], cos, sin)
        kv = lambda a: a.reshape(B, S, NSA_KV_HEADS, NSA_DK)
        y_nsa = nsa(nq.reshape(B, S, NSA_HEADS, NSA_DK), kv(kcm), kv(vcm), kv(ksl), kv(vsl), kv(kwn), kv(vwn), ng,
                    cmp_k_pos[l], cmp_k_w1[l], cmp_k_w2[l], cmp_v_pos[l], cmp_v_w1[l], cmp_v_w2[l], cos, sin)
        mix = jnp.concatenate([y_ret, y_nsa], axis=-1).astype(x.dtype) @ w_out[l]
        x = x + rms_norm(mix, norm_mix_post[l])
        h = rms_norm(x, norm_ffn_pre[l])
        x = x + rms_norm(conv_ffn(h, ffn_w_up[l], ffn_conv[l], ffn_w_down[l]), norm_ffn_post[l])
    return x
```

```python
import functools
import math

import jax
import jax.numpy as jnp
import numpy as np
from jax import lax
from jax.experimental import pallas as pl
from jax.experimental.pallas import tpu as pltpu

F32 = jnp.float32
BF16 = jnp.bfloat16

LANES = 128
V7X_VMEM_LIMIT = 56 * 1024 * 1024

ROPE_THETA = 10000.0
NORM_EPS = 1e-6
GN_EPS = 1e-5
NEG = -1e30
SEL_BIG = 1e9

RET_HEADS = 4
RET_DK = 64
RET_DV = 128
RET_CHUNK = 128
NSA_HEADS = 8
NSA_KV_HEADS = 2
NSA_DK = 64
NSA_GROUP = NSA_HEADS // NSA_KV_HEADS
CMP_BLOCK = 32
CMP_STRIDE = 16
CMP_HID = 256
SLC_BLOCK = 64
SLC_TOPK = 16
SLC_LOCAL = 2
WINDOW = 512
NSA_QBLOCK = 128
NSA_GATES = 3 * NSA_HEADS
CONV_WIDTH = 3

D_RET = RET_HEADS * RET_DV
D_NSA = NSA_HEADS * NSA_DK
RET_QK = RET_HEADS * RET_DK
NSA_KV = NSA_KV_HEADS * NSA_DK

COL_RQ = 0
COL_RK = COL_RQ + RET_QK
COL_NQ = COL_RK + RET_QK
COL_KSL = COL_NQ + D_NSA
COL_KWN = COL_KSL + NSA_KV
ROPE_COLS = COL_KWN + NSA_KV
COL_RV = ROPE_COLS
COL_VSL = COL_RV + D_RET
COL_VWN = COL_VSL + NSA_KV
COL_RG = COL_VWN + NSA_KV
COL_KCM = COL_RG + D_RET
COL_VCM = COL_KCM + NSA_KV
COL_NG = COL_VCM + NSA_KV
IN_COLS_PAD = COL_NG + LANES

TOK_TILE = 512
RET_TILE = 512
FFN_TILE = 1024
FFN_FTILE = 256
FFN_HALO = 16
SLC_KTILE = 512


def _gelu_tanh(x):
    return 0.5 * x * (1.0 + jnp.tanh(math.sqrt(2.0 / math.pi) * (x + 0.044715 * (x * x * x))))


def _rope(p, cos, sin_signed, first_half):
    partner = jnp.where(first_half, pltpu.roll(p, LANES - 32, 1), pltpu.roll(p, 32, 1))
    return p * cos + partner * sin_signed


def _split_bf16(x):
    hi = x.astype(BF16)
    lo = (x - hi.astype(F32)).astype(BF16)
    return hi, lo


def _inproj_kernel(x_ref, g_ref, w_ref, cos_ref, sin_ref,
                   rq_ref, rk_ref, nq_ref, ksl_ref, kwn_ref, rv_ref, vsl_ref, vwn_ref,
                   rg_ref, kvc_ref, ng_ref, *, seq_len):
    tm = x_ref.shape[0]
    x = x_ref[...]
    h = (x * lax.rsqrt(jnp.mean(x * x, axis=-1, keepdims=True) + NORM_EPS) * g_ref[...]).astype(BF16)
    cos = cos_ref[...]
    sin = sin_ref[...]
    lane = lax.broadcasted_iota(jnp.int32, (tm, LANES), 1)
    first_half = (lane & 63) < 32

    def proj(c0, n):
        return jnp.dot(h, w_ref[:, c0:c0 + n], preferred_element_type=F32)

    def rope_slab(p, i):
        return _rope(p[:, i * LANES:(i + 1) * LANES], cos, sin, first_half)

    p = proj(COL_RQ, RET_QK)
    for i in range(RET_QK // LANES):
        rq_ref[:, i * LANES:(i + 1) * LANES] = rope_slab(p, i).astype(BF16)
    p = proj(COL_RK, RET_QK)
    for i in range(RET_QK // LANES):
        rk_ref[:, i * LANES:(i + 1) * LANES] = (rope_slab(p, i) * (RET_DK ** -0.5)).astype(BF16)
    p = proj(COL_NQ, D_NSA)
    for i in range(D_NSA // LANES):
        nq_ref[:, i * LANES:(i + 1) * LANES] = (rope_slab(p, i) * (NSA_DK ** -0.5)).astype(BF16)
    p = proj(COL_KSL, 2 * NSA_KV)
    ksl_ref[:, 0:LANES] = rope_slab(p, 0).astype(BF16)
    kwn_ref[...] = rope_slab(p, 1).astype(BF16)
    row = lax.broadcasted_iota(jnp.int32, (tm, LANES), 0)
    pos = (pl.program_id(0) * tm + row) % seq_len
    ksl_ref[:, LANES:2 * LANES] = jnp.where(lane == pos // SLC_BLOCK, 1.0, 0.0).astype(BF16)

    rv_ref[...] = proj(COL_RV, D_RET).astype(BF16)
    p = proj(COL_VSL, 2 * NSA_KV)
    vsl_ref[...] = p[:, 0:LANES].astype(BF16)
    vwn_ref[...] = p[:, LANES:2 * LANES].astype(BF16)
    rg_ref[...] = proj(COL_RG, D_RET)
    p = proj(COL_KCM, 2 * NSA_KV)
    kvc_ref[0] = p[:, 0:LANES]
    kvc_ref[1] = p[:, LANES:2 * LANES]
    ng_ref[...] = proj(COL_NG, LANES)


def _inproj(x2, gain, w, cos, sin, seq_len):
    T, D = x2.shape
    tm = TOK_TILE
    nt = seq_len // tm
    tok = lambda n: pl.BlockSpec((tm, n), lambda i: (i, 0))
    out_shape = (
        jax.ShapeDtypeStruct((T, RET_QK), BF16),
        jax.ShapeDtypeStruct((T, RET_QK), BF16),
        jax.ShapeDtypeStruct((T, D_NSA), BF16),
        jax.ShapeDtypeStruct((T, 2 * LANES), BF16),
        jax.ShapeDtypeStruct((T, LANES), BF16),
        jax.ShapeDtypeStruct((T, D_RET), BF16),
        jax.ShapeDtypeStruct((T, LANES), BF16),
        jax.ShapeDtypeStruct((T, LANES), BF16),
        jax.ShapeDtypeStruct((T, D_RET), F32),
        jax.ShapeDtypeStruct((2, T, LANES), F32),
        jax.ShapeDtypeStruct((T, LANES), F32),
    )
    out_specs = (tok(RET_QK), tok(RET_QK), tok(D_NSA), tok(2 * LANES), tok(LANES), tok(D_RET),
                 tok(LANES), tok(LANES), tok(D_RET),
                 pl.BlockSpec((2, tm, LANES), lambda i: (0, i, 0)), tok(LANES))
    return pl.pallas_call(
        functools.partial(_inproj_kernel, seq_len=seq_len),
        grid=(T // tm,),
        in_specs=[tok(D),
                  pl.BlockSpec((1, D), lambda i: (0, 0)),
                  pl.BlockSpec((D, IN_COLS_PAD), lambda i: (0, 0)),
                  pl.BlockSpec((tm, LANES), lambda i: (i % nt, 0)),
                  pl.BlockSpec((tm, LANES), lambda i: (i % nt, 0))],
        out_specs=out_specs,
        out_shape=out_shape,
        compiler_params=pltpu.CompilerParams(dimension_semantics=("arbitrary",),
                                             vmem_limit_bytes=V7X_VMEM_LIMIT),
        name="inproj",
    )(x2, gain, w, cos, sin)


def _retention_kernel(q_ref, k_ref, v_ref, g_ref, gnw_ref, o_ref, state_ref):
    C = RET_CHUNK
    n_chunks = q_ref.shape[0] // C

    @pl.when(pl.program_id(1) == 0)
    def _():
        state_ref[...] = jnp.zeros_like(state_ref)

    ii = lax.broadcasted_iota(jnp.int32, (C, C), 0)
    jj = lax.broadcasted_iota(jnp.int32, (C, C), 1)
    diff = (ii - jj).astype(F32)
    i_col = lax.broadcasted_iota(jnp.int32, (C, 1), 0).astype(F32)
    low_half = lax.broadcasted_iota(jnp.int32, (C, LANES), 1) < RET_DK
    nt_dims = (((1,), (1,)), ((), ()))
    tn_dims = (((0,), (0,)), ((), ()))

    for h in range(RET_HEADS):
        log_gamma = math.log(1.0 - 2.0 ** (-5.0 - h))
        decay = jnp.where(diff >= 0, jnp.exp(log_gamma * jnp.maximum(diff, 0.0)), 0.0)
        xi = jnp.exp(log_gamma * (i_col + 1.0))
        zeta = jnp.exp(log_gamma * (C - 1.0 - i_col))
        chunk_decay = math.exp(log_gamma * C)
        head_lanes = low_half if h % 2 == 0 else jnp.logical_not(low_half)
        qk_cols = slice((h // 2) * LANES, (h // 2 + 1) * LANES)
        v_cols = slice(h * RET_DV, (h + 1) * RET_DV)
        gn_w = gnw_ref[:, v_cols]
        for c in range(n_chunks):
            rows = slice(c * C, (c + 1) * C)
            qm = jnp.where(head_lanes, q_ref[rows, qk_cols].astype(F32), 0.0).astype(BF16)
            ks = k_ref[rows, qk_cols]
            v = v_ref[rows, v_cols]
            scores = lax.dot_general(qm, ks, nt_dims, preferred_element_type=F32) * decay
            o = jnp.dot(scores.astype(BF16), v, preferred_element_type=F32)
            state = state_ref[h]
            o = o + jnp.dot(qm, state.astype(BF16), preferred_element_type=F32) * xi
            kz = (ks.astype(F32) * zeta).astype(BF16)
            kv = lax.dot_general(kz, v, tn_dims, preferred_element_type=F32)
            state_ref[h] = state * chunk_decay + kv
            mu = jnp.mean(o, axis=-1, keepdims=True)
            var = jnp.mean(jnp.square(o - mu), axis=-1, keepdims=True)
            on = (o - mu) * lax.rsqrt(var + GN_EPS) * gn_w
            gate = g_ref[rows, v_cols]
            o_ref[rows, v_cols] = (gate * (1.0 / (1.0 + jnp.exp(-gate))) * on).astype(BF16)


def _retention(rq, rk, rv, rg, gn_w, batch, seq_len):
    T = rq.shape[0]
    tc = RET_TILE
    nt = seq_len // tc
    tok = lambda n: pl.BlockSpec((tc, n), lambda b, i: (b * nt + i, 0))
    return pl.pallas_call(
        _retention_kernel,
        grid=(batch, nt),
        in_specs=[tok(RET_QK), tok(RET_QK), tok(D_RET), tok(D_RET),
                  pl.BlockSpec((1, D_RET), lambda b, i: (0, 0))],
        out_specs=tok(D_RET),
        out_shape=jax.ShapeDtypeStruct((T, D_RET), BF16),
        scratch_shapes=[pltpu.VMEM((RET_HEADS, LANES, RET_DV), F32)],
        compiler_params=pltpu.CompilerParams(dimension_semantics=("arbitrary", "arbitrary"),
                                             vmem_limit_bytes=V7X_VMEM_LIMIT),
        name="retention",
    )(rq, rk, rv, rg, gn_w)


def _compress_kernel(x_ref, pa_ref, pb_ref, wa_ref, wb_ref, w2_ref, cos_ref, sin_ref, o_ref):
    ng = x_ref.shape[2]
    x = x_ref[0, 0]
    xa = (x + pa_ref[0]).astype(BF16)
    xb = (x + pb_ref[0]).astype(BF16)
    a = jnp.dot(xa, wa_ref[0], preferred_element_type=F32)
    b = jnp.dot(xb, wb_ref[0], preferred_element_type=F32)
    hid = a + pltpu.roll(b, ng - 1, 0)
    out = jnp.dot(_gelu_tanh(hid).astype(BF16), w2_ref[0], preferred_element_type=F32)
    lane = lax.broadcasted_iota(jnp.int32, out.shape, 1)
    o_ref[0, 0] = _rope(out, cos_ref[0], sin_ref[0], (lane & 63) < 32).astype(BF16)


def _compress(kvc, pos_a, pos_b, wa, wb, w2, cos, sin, batch, seq_len):
    ng = seq_len // CMP_STRIDE
    gw = CMP_STRIDE * LANES
    x = kvc.reshape(2, batch, ng, gw)
    hid = NSA_KV_HEADS * CMP_HID
    per_kv = lambda *shape: pl.BlockSpec((1,) + shape, lambda s, b: (s,) + (0,) * len(shape))
    return pl.pallas_call(
        _compress_kernel,
        grid=(2, batch),
        in_specs=[pl.BlockSpec((1, 1, ng, gw), lambda s, b: (s, b, 0, 0)),
                  per_kv(1, gw), per_kv(1, gw), per_kv(gw, hid), per_kv(gw, hid), per_kv(hid, LANES),
                  per_kv(ng, LANES), per_kv(ng, LANES)],
        out_specs=pl.BlockSpec((1, 1, ng, LANES), lambda s, b: (s, b, 0, 0)),
        out_shape=jax.ShapeDtypeStruct((2, batch, ng, LANES), BF16),
        compiler_params=pltpu.CompilerParams(dimension_semantics=("arbitrary", "arbitrary"),
                                             vmem_limit_bytes=V7X_VMEM_LIMIT),
        name="compress",
    )(x, pos_a, pos_b, wa, wb, w2, cos, sin)


def _nsa_kernel(nq_ref, ng_ref, kc_ref, vc_ref, ksl_ref, vsl_ref, kwn_ref, vwn_ref, ov_ref, ex_ref,
                o_ref, qaug_ref, *, seq_len, topk):
    QB = NSA_QBLOCK
    R = NSA_GROUP
    M = R * QB
    NC = kc_ref.shape[2]
    KT = SLC_KTILE
    WK = WINDOW + QB
    t0 = pl.program_id(1) * QB
    nt_dims = (((1,), (1,)), ((), ()))

    low_half = lax.broadcasted_iota(jnp.int32, (QB, LANES), 1) < NSA_DK
    q = nq_ref[...].astype(F32)

    def softmax_rows(s3, mask):
        s3 = jnp.where(mask[None], s3, NEG)
        m = jnp.max(s3, axis=-1, keepdims=True)
        e = jnp.where(mask[None], jnp.exp(s3 - m), 0.0)
        den = jnp.maximum(jnp.sum(e, axis=-1, keepdims=True), 1e-30)
        return e * (1.0 / den)

    o_cmp, o_slc, o_win = [], [], []
    for g in range(NSA_KV_HEADS):
        head_lanes = low_half if g == 0 else jnp.logical_not(low_half)
        for i in range(R):
            qaug_ref[g, i * QB:(i + 1) * QB, 0:LANES] = jnp.where(
                head_lanes, q[:, i * LANES:(i + 1) * LANES], 0.0).astype(BF16)
        qg = qaug_ref[g, :, 0:LANES]

        sc = lax.dot_general(qg, kc_ref[0, 0], nt_dims, preferred_element_type=F32)
        n_idx = lax.broadcasted_iota(jnp.int32, (QB, NC), 1)
        t_row = t0 + lax.broadcasted_iota(jnp.int32, (QB, NC), 0)
        p_cmp = softmax_rows(sc.reshape(R, QB, NC), n_idx * CMP_STRIDE + (CMP_BLOCK - 1) <= t_row)
        o_cmp.append(jnp.dot(p_cmp.reshape(M, NC).astype(BF16), vc_ref[0, 0], preferred_element_type=F32))

        p_hi, p_lo = _split_bf16(jnp.sum(p_cmp, axis=0))
        imp = (jnp.dot(p_hi, ov_ref[...], preferred_element_type=F32)
               + jnp.dot(p_lo, ov_ref[...], preferred_element_type=F32))
        imp_t = imp.T
        jj = lax.broadcasted_iota(jnp.int32, (LANES, QB), 0)
        q_blk = (t0 + lax.broadcasted_iota(jnp.int32, (LANES, QB), 1)) // SLC_BLOCK
        valid = jj <= q_blk
        forced = (jj == 0) | (valid & (jj > q_blk - SLC_LOCAL))
        score = jnp.where(forced, SEL_BIG, jnp.where(valid, imp_t, -SEL_BIG))
        jf = jj.astype(F32)

        def pick_round(_, carry):
            score, sel = carry
            best = jnp.max(score, axis=0, keepdims=True)
            first = jnp.min(jnp.where(score == best, jf, float(LANES)), axis=0, keepdims=True)
            pick = jf == first
            return jnp.where(pick, -jnp.inf, score), jnp.where(pick, 1.0, sel)

        _, sel = lax.fori_loop(0, topk, pick_round, (score, jnp.zeros((LANES, QB), F32)))
        bias = jnp.where((sel > 0.0) & valid, 0.0, NEG).T.astype(BF16)
        for i in range(R):
            qaug_ref[g, i * QB:(i + 1) * QB, LANES:2 * LANES] = bias

        def slc_step(k0, carry, causal):
            m_run, l_run, acc = carry
            s = lax.dot_general(qaug_ref[g], ksl_ref[0, pl.ds(k0, KT), :], nt_dims,
                                preferred_element_type=F32)
            if causal:
                k_pos = k0 + lax.broadcasted_iota(jnp.int32, (QB, KT), 1)
                t_pos = t0 + lax.broadcasted_iota(jnp.int32, (QB, KT), 0)
                s = jnp.where((k_pos <= t_pos)[None], s.reshape(R, QB, KT), NEG).reshape(M, KT)
            m_new = jnp.maximum(m_run, jnp.max(s, axis=-1, keepdims=True))
            alpha = jnp.exp(m_run - m_new)
            p = jnp.exp(s - m_new)
            l_new = alpha * l_run + jnp.sum(p, axis=-1, keepdims=True)
            acc = alpha * acc + jnp.dot(p.astype(BF16), vsl_ref[0, pl.ds(k0, KT), :],
                                        preferred_element_type=F32)
            return m_new, l_new, acc

        k_diag = (t0 // KT) * KT
        carry = (jnp.full((M, 1), NEG, F32), jnp.zeros((M, 1), F32), jnp.zeros((M, LANES), F32))
        carry = lax.fori_loop(
            0, t0 // KT, lambda kt, c: slc_step(pl.multiple_of(kt * KT, KT), c, False), carry)
        _, l_run, acc = slc_step(pl.multiple_of(k_diag, KT), carry, True)
        o_slc.append(acc * (1.0 / l_run))

        ks = pl.multiple_of(jnp.clip(t0 - WINDOW, 0, seq_len - WK), QB)
        sw = lax.dot_general(qg, kwn_ref[0, pl.ds(ks, WK), :], nt_dims, preferred_element_type=F32)
        k_pos = ks + lax.broadcasted_iota(jnp.int32, (QB, WK), 1)
        t_pos = t0 + lax.broadcasted_iota(jnp.int32, (QB, WK), 0)
        p_win = softmax_rows(sw.reshape(R, QB, WK), (k_pos <= t_pos) & (k_pos > t_pos - WINDOW))
        o_win.append(jnp.dot(p_win.reshape(M, WK).astype(BF16), vwn_ref[0, pl.ds(ks, WK), :],
                             preferred_element_type=F32))

    logits = ng_ref[...]
    g_hi, g_lo = _split_bf16(1.0 / (1.0 + jnp.exp(-logits)))
    gates = (jnp.dot(g_hi, ex_ref[...], preferred_element_type=F32)
             + jnp.dot(g_lo, ex_ref[...], preferred_element_type=F32))
    for i in range(R):
        rows = slice(i * QB, (i + 1) * QB)
        cols = slice(i * LANES, (i + 1) * LANES)
        y = jnp.zeros((QB, LANES), F32)
        for branch, o in enumerate((o_cmp, o_slc, o_win)):
            o_slab = jnp.where(low_half, o[0][rows], o[1][rows])
            y = y + gates[:, branch * D_NSA + i * LANES:branch * D_NSA + (i + 1) * LANES] * o_slab
        o_ref[:, cols] = y.astype(BF16)


def _nsa(nq, ng, kvc_cmp, ksl, vsl, kwn, vwn, overlap, expand, batch, seq_len):
    T = nq.shape[0]
    QB = NSA_QBLOCK
    nqb = seq_len // QB
    nc = seq_len // CMP_STRIDE
    topk = min(SLC_TOPK, seq_len // SLC_BLOCK)
    tok = lambda n: pl.BlockSpec((QB, n), lambda b, i: (b * nqb + i, 0))
    seq = lambda n: pl.BlockSpec((1, seq_len, n), lambda b, i: (b, 0, 0))
    return pl.pallas_call(
        functools.partial(_nsa_kernel, seq_len=seq_len, topk=topk),
        grid=(batch, nqb),
        in_specs=[tok(D_NSA), tok(LANES),
                  pl.BlockSpec((1, 1, nc, LANES), lambda b, i: (0, b, 0, 0)),
                  pl.BlockSpec((1, 1, nc, LANES), lambda b, i: (1, b, 0, 0)),
                  seq(2 * LANES), seq(LANES), seq(LANES), seq(LANES),
                  pl.BlockSpec(overlap.shape, lambda b, i: (0, 0)),
                  pl.BlockSpec(expand.shape, lambda b, i: (0, 0))],
        out_specs=tok(D_NSA),
        out_shape=jax.ShapeDtypeStruct((T, D_NSA), BF16),
        scratch_shapes=[pltpu.VMEM((NSA_KV_HEADS, NSA_GROUP * QB, 2 * LANES), BF16)],
        compiler_params=pltpu.CompilerParams(dimension_semantics=("arbitrary", "arbitrary"),
                                             vmem_limit_bytes=V7X_VMEM_LIMIT),
        name="nsa",
    )(nq, ng, kvc_cmp, kvc_cmp, ksl.reshape(batch, seq_len, -1), vsl.reshape(batch, seq_len, -1),
      kwn.reshape(batch, seq_len, -1), vwn.reshape(batch, seq_len, -1), overlap, expand)


def _outproj_kernel(yr_ref, yn_ref, x_ref, w_ref, g_ref, o_ref):
    mix = (jnp.dot(yr_ref[...], w_ref[0:D_RET, :], preferred_element_type=F32)
           + jnp.dot(yn_ref[...], w_ref[D_RET:D_RET + D_NSA, :], preferred_element_type=F32))
    n = mix * lax.rsqrt(jnp.mean(mix * mix, axis=-1, keepdims=True) + NORM_EPS) * g_ref[...]
    o_ref[...] = x_ref[...] + n


def _outproj(y_ret, y_nsa, x2, w, gain):
    T, D = x2.shape
    tm = TOK_TILE
    tok = lambda n: pl.BlockSpec((tm, n), lambda i: (i, 0))
    return pl.pallas_call(
        _outproj_kernel,
        grid=(T // tm,),
        in_specs=[tok(D_RET), tok(D_NSA), tok(D),
                  pl.BlockSpec(w.shape, lambda i: (0, 0)),
                  pl.BlockSpec((1, D), lambda i: (0, 0))],
        out_specs=tok(D),
        out_shape=jax.ShapeDtypeStruct((T, D), F32),
        compiler_params=pltpu.CompilerParams(dimension_semantics=("arbitrary",),
                                             vmem_limit_bytes=V7X_VMEM_LIMIT),
        name="outproj",
    )(y_ret, y_nsa, x2, w, gain)


def _ffn_kernel(x_ref, xp_ref, gpre_ref, wg_ref, wv_ref, cg_ref, cv_ref, wd_ref, gpost_ref, o_ref,
                h_ref, ug_ref, uv_ref, acc_ref, *, tiles_per_seq):
    tm = x_ref.shape[0]
    H = FFN_HALO
    f = pl.program_id(1)

    def normed(x):
        return x * lax.rsqrt(jnp.mean(x * x, axis=-1, keepdims=True) + NORM_EPS) * gpre_ref[...]

    @pl.when(f == 0)
    def _():
        keep = jnp.where(pl.program_id(0) % tiles_per_seq == 0, 0.0, 1.0)
        h_ref[0:H, :] = (normed(xp_ref[...]) * keep).astype(BF16)
        h_ref[H:H + tm, :] = normed(x_ref[...]).astype(BF16)
        acc_ref[...] = jnp.zeros_like(acc_ref)

    h = h_ref[...]
    ug_ref[...] = jnp.dot(h, wg_ref[...], preferred_element_type=F32)
    uv_ref[...] = jnp.dot(h, wv_ref[...], preferred_element_type=F32)

    def causal_conv(u_ref, c_ref):
        out = c_ref[CONV_WIDTH - 1:CONV_WIDTH, :] * u_ref[H:H + tm, :]
        for k in range(CONV_WIDTH - 1):
            d = CONV_WIDTH - 1 - k
            out = out + c_ref[k:k + 1, :] * u_ref[H - d:H - d + tm, :]
        return out

    act = _gelu_tanh(causal_conv(ug_ref, cg_ref)) * causal_conv(uv_ref, cv_ref)
    acc_ref[...] += jnp.dot(act.astype(BF16), wd_ref[...], preferred_element_type=F32)

    @pl.when(f == pl.num_programs(1) - 1)
    def _():
        y = acc_ref[...]
        n = y * lax.rsqrt(jnp.mean(y * y, axis=-1, keepdims=True) + NORM_EPS) * gpost_ref[...]
        o_ref[...] = x_ref[...] + n


def _ffn(x2, g_pre, w_up, conv_w, w_down, g_post, seq_len):
    T, D = x2.shape
    d_ff = w_down.shape[0]
    tm, tf, H = FFN_TILE, FFN_FTILE, FFN_HALO
    nf = d_ff // tf
    return pl.pallas_call(
        functools.partial(_ffn_kernel, tiles_per_seq=seq_len // tm),
        grid=(T // tm, nf),
        in_specs=[pl.BlockSpec((tm, D), lambda i, f: (i, 0)),
                  pl.BlockSpec((H, D), lambda i, f: (jnp.maximum(i * (tm // H) - 1, 0), 0)),
                  pl.BlockSpec((1, D), lambda i, f: (0, 0)),
                  pl.BlockSpec((D, tf), lambda i, f: (0, f)),
                  pl.BlockSpec((D, tf), lambda i, f: (0, nf + f)),
                  pl.BlockSpec((CONV_WIDTH, tf), lambda i, f: (0, f)),
                  pl.BlockSpec((CONV_WIDTH, tf), lambda i, f: (0, nf + f)),
                  pl.BlockSpec((tf, D), lambda i, f: (f, 0)),
                  pl.BlockSpec((1, D), lambda i, f: (0, 0))],
        out_specs=pl.BlockSpec((tm, D), lambda i, f: (i, 0)),
        out_shape=jax.ShapeDtypeStruct((T, D), F32),
        scratch_shapes=[pltpu.VMEM((tm + H, D), BF16), pltpu.VMEM((tm + H, tf), F32),
                        pltpu.VMEM((tm + H, tf), F32), pltpu.VMEM((tm, D), F32)],
        compiler_params=pltpu.CompilerParams(dimension_semantics=("arbitrary", "arbitrary"),
                                             vmem_limit_bytes=V7X_VMEM_LIMIT),
        name="ffn",
    )(x2, x2, g_pre, w_up, w_up, conv_w, conv_w, w_down, g_post)


_NSA_HEAD_ORDER = [g * NSA_GROUP + i for i in range(NSA_GROUP) for g in range(NSA_KV_HEADS)]


def _rope_tables(pos):
    inv = 1.0 / (ROPE_THETA ** (jnp.arange(0, NSA_DK, 2, dtype=F32) / NSA_DK))
    ang = pos.astype(F32)[:, None] * inv[None, :]
    c, s = jnp.cos(ang), jnp.sin(ang)
    return jnp.concatenate([c, c, c, c], axis=1), jnp.concatenate([-s, s, -s, s], axis=1)


def _prep_w_in(w):
    d = w.shape[0]
    splits = np.cumsum([RET_QK, RET_QK, D_RET, D_RET, D_NSA] + [NSA_KV] * 6)
    rq, rk, rv, rg, nq, kcm, vcm, ksl, vsl, kwn, vwn, ng = jnp.split(w, [int(s) for s in splits], axis=1)
    nq = nq.reshape(d, NSA_HEADS, NSA_DK)[:, np.array(_NSA_HEAD_ORDER)].reshape(d, D_NSA)
    ng = jnp.pad(ng, ((0, 0), (0, LANES - NSA_GATES)))
    return jnp.concatenate([rq, rk, nq, ksl, kwn, rv, vsl, vwn, rg, kcm, vcm, ng], axis=1).astype(BF16)


def _prep_w_out(w):
    d = w.shape[1]
    w_nsa = w[D_RET:].reshape(NSA_HEADS, NSA_DK, d)[np.array(_NSA_HEAD_ORDER)].reshape(D_NSA, d)
    return jnp.concatenate([w[:D_RET], w_nsa], axis=0).astype(BF16)


def _prep_compress(pos, w1, w2):
    G = NSA_KV_HEADS
    eye = jnp.eye(G, dtype=F32)
    half = CMP_STRIDE

    def first_layer(w_half):
        w4 = w_half.reshape(half, NSA_DK, CMP_HID)
        return jnp.einsum('ldc,gh->lgdhc', w4, eye).reshape(half * G * NSA_DK, G * CMP_HID).astype(BF16)

    def pos_row(p_half):
        return jnp.broadcast_to(p_half[:, None, :], (half, G, NSA_DK)).reshape(1, -1)

    w2x = jnp.einsum('cd,gh->gchd', w2, eye).reshape(G * CMP_HID, G * NSA_DK).astype(BF16)
    return (pos_row(pos[:half]), pos_row(pos[half:]),
            first_layer(w1[:half * NSA_DK]), first_layer(w1[half * NSA_DK:]), w2x)


def _overlap_matrix(seq_len):
    nc = seq_len // CMP_STRIDE
    cmp_start = np.arange(nc) * CMP_STRIDE
    slc_start = np.arange(LANES) * SLC_BLOCK
    ov = ((cmp_start[:, None] < slc_start[None, :] + SLC_BLOCK)
          & (cmp_start[:, None] + CMP_BLOCK > slc_start[None, :]))
    n_cmp = (seq_len - CMP_BLOCK) // CMP_STRIDE + 1
    ov &= (np.arange(nc) < n_cmp)[:, None]
    return jnp.asarray(ov.astype(np.float32), dtype=BF16)


def _gate_expand_matrix():
    ex = np.zeros((LANES, 3 * D_NSA), np.float32)
    for branch in range(3):
        for p in range(D_NSA):
            head = _NSA_HEAD_ORDER[p // NSA_DK]
            ex[branch * NSA_HEADS + head, branch * D_NSA + p] = 1.0
    return jnp.asarray(ex, dtype=BF16)


def kernel(x, norm_mix_pre, w_in, ret_gn_w, cmp_k_pos, cmp_k_w1, cmp_k_w2, cmp_v_pos, cmp_v_w1, cmp_v_w2,
           w_out, norm_mix_post, norm_ffn_pre, ffn_w_up, ffn_conv, ffn_w_down, norm_ffn_post):
    B, S, D = x.shape
    depth = w_in.shape[0]
    assert S % SLC_KTILE == 0 and S % FFN_TILE == 0 and S // SLC_BLOCK <= LANES and S >= WINDOW + NSA_QBLOCK
    assert ffn_w_down.shape[1] % FFN_FTILE == 0

    cos, sin = _rope_tables(jnp.arange(S, dtype=jnp.int32))
    nc = S // CMP_STRIDE
    ccos, csin = _rope_tables(jnp.arange(nc, dtype=jnp.int32) * CMP_STRIDE + (CMP_BLOCK - 1))
    cmp_cos = jnp.stack([ccos, jnp.ones_like(ccos)])
    cmp_sin = jnp.stack([csin, jnp.zeros_like(csin)])
    overlap = _overlap_matrix(S)
    expand = _gate_expand_matrix()

    x2 = x.reshape(B * S, D)
    for l in range(depth):
        outs = _inproj(x2, norm_mix_pre[l][None], _prep_w_in(w_in[l]), cos, sin, S)
        rq, rk, nq, ksl, kwn, rv, vsl, vwn, rg, kvc, ng = outs
        y_ret = _retention(rq, rk, rv, rg, ret_gn_w[l][None], B, S)
        kp = _prep_compress(cmp_k_pos[l], cmp_k_w1[l], cmp_k_w2[l])
        vp = _prep_compress(cmp_v_pos[l], cmp_v_w1[l], cmp_v_w2[l])
        cmp_args = [jnp.stack([a, b]) for a, b in zip(kp, vp)]
        kvc_cmp = _compress(kvc, *cmp_args, cmp_cos, cmp_sin, B, S)
        y_nsa = _nsa(nq, ng, kvc_cmp, ksl, vsl, kwn, vwn, overlap, expand, B, S)
        x2 = _outproj(y_ret, y_nsa, x2, _prep_w_out(w_out[l]), norm_mix_post[l][None])
        x2 = _ffn(x2, norm_ffn_pre[l][None], ffn_w_up[l].astype(BF16), ffn_conv[l],
                  ffn_w_down[l].astype(BF16), norm_ffn_post[l][None], S)
    return x2.reshape(B, S, D)
```

```python
import functools
import math

import jax
import jax.numpy as jnp
import numpy as np
from jax import lax
from jax.experimental import pallas as pl
from jax.experimental.pallas import tpu as pltpu

F32 = jnp.float32
BF16 = jnp.bfloat16

LANES = 128
V7X_VMEM_LIMIT = 56 * 1024 * 1024

ROPE_THETA = 10000.0
NORM_EPS = 1e-6
GN_EPS = 1e-5
NEG = -1e30
SEL_BIG = 1e9

RET_HEADS = 4
RET_DK = 64
RET_DV = 128
RET_CHUNK = 128
NSA_HEADS = 8
NSA_KV_HEADS = 2
NSA_DK = 64
NSA_GROUP = NSA_HEADS // NSA_KV_HEADS
CMP_BLOCK = 32
CMP_STRIDE = 16
CMP_HID = 256
SLC_BLOCK = 64
SLC_TOPK = 16
SLC_LOCAL = 2
WINDOW = 512
NSA_QBLOCK = 128
NSA_GATES = 3 * NSA_HEADS
CONV_WIDTH = 3

NSA_Q_SCALE = NSA_DK ** -0.5 * math.log2(math.e)

D_RET = RET_HEADS * RET_DV
D_NSA = NSA_HEADS * NSA_DK
RET_QK = RET_HEADS * RET_DK
NSA_KV = NSA_KV_HEADS * NSA_DK

COL_RQ = 0
COL_RK = COL_RQ + RET_QK
COL_NQ = COL_RK + RET_QK
COL_KSL = COL_NQ + D_NSA
COL_KWN = COL_KSL + NSA_KV
ROPE_COLS = COL_KWN + NSA_KV
COL_RV = ROPE_COLS
COL_VSL = COL_RV + D_RET
COL_VWN = COL_VSL + NSA_KV
COL_RG = COL_VWN + NSA_KV
COL_KCM = COL_RG + D_RET
COL_VCM = COL_KCM + NSA_KV
COL_NG = COL_VCM + NSA_KV
IN_COLS_PAD = COL_NG + LANES

TOK_TILE = 512
RET_TILE = 512
FFN_TILE = 1024
FFN_FTILE = 256
FFN_HALO = 16
SLC_KTILE = 512


def _gelu_tanh(x):
    return 0.5 * x * (1.0 + jnp.tanh(math.sqrt(2.0 / math.pi) * (x + 0.044715 * (x * x * x))))


def _rope(p, cos, sin_signed, first_half):
    half = NSA_DK // 2
    partner = jnp.where(first_half, pltpu.roll(p, LANES - half, 1), pltpu.roll(p, half, 1))
    return p * cos + partner * sin_signed


def _split_bf16(x):
    hi = x.astype(BF16)
    lo = (x - hi.astype(F32)).astype(BF16)
    return hi, lo


def _inproj_kernel(x_ref, g_ref, w_ref, cos_ref, sin_ref,
                   rq_ref, rk_ref, nq_ref, ksl_ref, kwn_ref, rv_ref, vsl_ref, vwn_ref,
                   rg_ref, kvc_ref, ng_ref, *, seq_len):
    tm = x_ref.shape[0]
    x = x_ref[...]
    h = (x * lax.rsqrt(jnp.mean(x * x, axis=-1, keepdims=True) + NORM_EPS) * g_ref[...]).astype(BF16)
    cos = cos_ref[...]
    sin = sin_ref[...]
    lane = lax.broadcasted_iota(jnp.int32, (tm, LANES), 1)
    first_half = (lane % NSA_DK) < NSA_DK // 2

    def proj(c0, n):
        return jnp.dot(h, w_ref[:, c0:c0 + n], preferred_element_type=F32)

    def rope_slab(p, i):
        return _rope(p[:, i * LANES:(i + 1) * LANES], cos, sin, first_half)

    p = proj(COL_RQ, RET_QK)
    for i in range(RET_QK // LANES):
        rq_ref[:, i * LANES:(i + 1) * LANES] = rope_slab(p, i).astype(BF16)
    p = proj(COL_RK, RET_QK)
    for i in range(RET_QK // LANES):
        rk_ref[:, i * LANES:(i + 1) * LANES] = (rope_slab(p, i) * (RET_DK ** -0.5)).astype(BF16)
    p = proj(COL_NQ, D_NSA)
    for i in range(D_NSA // LANES):
        nq_ref[:, i * LANES:(i + 1) * LANES] = (rope_slab(p, i) * NSA_Q_SCALE).astype(BF16)
    p = proj(COL_KSL, 2 * NSA_KV)
    ksl_ref[:, 0:LANES] = rope_slab(p, 0).astype(BF16)
    kwn_ref[...] = rope_slab(p, 1).astype(BF16)
    row = lax.broadcasted_iota(jnp.int32, (tm, LANES), 0)
    pos = (pl.program_id(0) * tm + row) % seq_len
    ksl_ref[:, LANES:2 * LANES] = jnp.where(lane == pos // SLC_BLOCK, 1.0, 0.0).astype(BF16)

    rv_ref[...] = proj(COL_RV, D_RET).astype(BF16)
    p = proj(COL_VSL, 2 * NSA_KV)
    low_half = lane < NSA_DK
    vsl_ref[:, 0:LANES] = jnp.where(low_half, p[:, 0:LANES], 1.0).astype(BF16)
    vsl_ref[:, LANES:2 * LANES] = jnp.where(low_half, 1.0, p[:, 0:LANES]).astype(BF16)
    vwn_ref[...] = p[:, LANES:2 * LANES].astype(BF16)
    rg_ref[...] = proj(COL_RG, D_RET)
    p = proj(COL_KCM, 2 * NSA_KV)
    kvc_ref[0] = p[:, 0:LANES]
    kvc_ref[1] = p[:, LANES:2 * LANES]
    ng_ref[...] = proj(COL_NG, LANES)


def _inproj(x2, gain, w, cos, sin, seq_len):
    T, D = x2.shape
    tm = TOK_TILE
    nt = seq_len // tm
    tok = lambda n: pl.BlockSpec((tm, n), lambda i: (i, 0))
    out_shape = (
        jax.ShapeDtypeStruct((T, RET_QK), BF16),
        jax.ShapeDtypeStruct((T, RET_QK), BF16),
        jax.ShapeDtypeStruct((T, D_NSA), BF16),
        jax.ShapeDtypeStruct((T, 2 * LANES), BF16),
        jax.ShapeDtypeStruct((T, LANES), BF16),
        jax.ShapeDtypeStruct((T, D_RET), BF16),
        jax.ShapeDtypeStruct((T, 2 * LANES), BF16),
        jax.ShapeDtypeStruct((T, LANES), BF16),
        jax.ShapeDtypeStruct((T, D_RET), F32),
        jax.ShapeDtypeStruct((2, T, LANES), F32),
        jax.ShapeDtypeStruct((T, LANES), F32),
    )
    out_specs = (tok(RET_QK), tok(RET_QK), tok(D_NSA), tok(2 * LANES), tok(LANES), tok(D_RET),
                 tok(2 * LANES), tok(LANES), tok(D_RET),
                 pl.BlockSpec((2, tm, LANES), lambda i: (0, i, 0)), tok(LANES))
    return pl.pallas_call(
        functools.partial(_inproj_kernel, seq_len=seq_len),
        grid=(T // tm,),
        in_specs=[tok(D),
                  pl.BlockSpec((1, D), lambda i: (0, 0)),
                  pl.BlockSpec((D, IN_COLS_PAD), lambda i: (0, 0)),
                  pl.BlockSpec((tm, LANES), lambda i: (i % nt, 0)),
                  pl.BlockSpec((tm, LANES), lambda i: (i % nt, 0))],
        out_specs=out_specs,
        out_shape=out_shape,
        compiler_params=pltpu.CompilerParams(dimension_semantics=("arbitrary",),
                                             vmem_limit_bytes=V7X_VMEM_LIMIT),
        name="inproj",
    )(x2, gain, w, cos, sin)


def _retention_kernel(q_ref, k_ref, v_ref, g_ref, gnw_ref, o_ref, state_ref):
    C = RET_CHUNK
    n_chunks = q_ref.shape[0] // C

    @pl.when(pl.program_id(1) == 0)
    def _():
        state_ref[...] = jnp.zeros_like(state_ref)

    ii = lax.broadcasted_iota(jnp.int32, (C, C), 0)
    jj = lax.broadcasted_iota(jnp.int32, (C, C), 1)
    diff = (ii - jj).astype(F32)
    i_col = lax.broadcasted_iota(jnp.int32, (C, 1), 0).astype(F32)
    low_half = lax.broadcasted_iota(jnp.int32, (C, LANES), 1) < RET_DK
    nt_dims = (((1,), (1,)), ((), ()))
    tn_dims = (((0,), (0,)), ((), ()))

    for h in range(RET_HEADS):
        log_gamma = math.log(1.0 - 2.0 ** (-5.0 - h))
        decay = jnp.where(diff >= 0, jnp.exp(log_gamma * jnp.maximum(diff, 0.0)), 0.0)
        xi = jnp.exp(log_gamma * (i_col + 1.0))
        zeta = jnp.exp(log_gamma * (C - 1.0 - i_col))
        chunk_decay = math.exp(log_gamma * C)
        head_lanes = low_half if h % 2 == 0 else jnp.logical_not(low_half)
        qk_cols = slice((h // 2) * LANES, (h // 2 + 1) * LANES)
        v_cols = slice(h * RET_DV, (h + 1) * RET_DV)
        gn_w = gnw_ref[:, v_cols]
        for c in range(n_chunks):
            rows = slice(c * C, (c + 1) * C)
            qm = jnp.where(head_lanes, q_ref[rows, qk_cols].astype(F32), 0.0).astype(BF16)
            ks = k_ref[rows, qk_cols]
            v = v_ref[rows, v_cols]
            scores = lax.dot_general(qm, ks, nt_dims, preferred_element_type=F32) * decay
            o = jnp.dot(scores.astype(BF16), v, preferred_element_type=F32)
            state = state_ref[h]
            o = o + jnp.dot(qm, state.astype(BF16), preferred_element_type=F32) * xi
            kz = (ks.astype(F32) * zeta).astype(BF16)
            kv = lax.dot_general(kz, v, tn_dims, preferred_element_type=F32)
            state_ref[h] = state * chunk_decay + kv
            mu = jnp.mean(o, axis=-1, keepdims=True)
            var = jnp.mean(jnp.square(o - mu), axis=-1, keepdims=True)
            on = (o - mu) * lax.rsqrt(var + GN_EPS) * gn_w
            gate = g_ref[rows, v_cols]
            o_ref[rows, v_cols] = (gate * (1.0 / (1.0 + jnp.exp(-gate))) * on).astype(BF16)


def _retention(rq, rk, rv, rg, gn_w, batch, seq_len):
    T = rq.shape[0]
    tc = RET_TILE
    nt = seq_len // tc
    tok = lambda n: pl.BlockSpec((tc, n), lambda b, i: (b * nt + i, 0))
    return pl.pallas_call(
        _retention_kernel,
        grid=(batch, nt),
        in_specs=[tok(RET_QK), tok(RET_QK), tok(D_RET), tok(D_RET),
                  pl.BlockSpec((1, D_RET), lambda b, i: (0, 0))],
        out_specs=tok(D_RET),
        out_shape=jax.ShapeDtypeStruct((T, D_RET), BF16),
        scratch_shapes=[pltpu.VMEM((RET_HEADS, LANES, RET_DV), F32)],
        compiler_params=pltpu.CompilerParams(dimension_semantics=("arbitrary", "arbitrary"),
                                             vmem_limit_bytes=V7X_VMEM_LIMIT),
        name="retention",
    )(rq, rk, rv, rg, gn_w)


def _compress_kernel(x_ref, pa_ref, pb_ref, wa_ref, wb_ref, w2_ref, cos_ref, sin_ref, o_ref):
    ng = x_ref.shape[2]
    x = x_ref[0, 0]
    xa = (x + pa_ref[0]).astype(BF16)
    xb = (x + pb_ref[0]).astype(BF16)
    a = jnp.dot(xa, wa_ref[0], preferred_element_type=F32)
    b = jnp.dot(xb, wb_ref[0], preferred_element_type=F32)
    hid = a + pltpu.roll(b, ng - 1, 0)
    out = jnp.dot(_gelu_tanh(hid).astype(BF16), w2_ref[0], preferred_element_type=F32)
    lane = lax.broadcasted_iota(jnp.int32, out.shape, 1)
    o_ref[0, 0] = _rope(out, cos_ref[0], sin_ref[0], (lane % NSA_DK) < NSA_DK // 2).astype(BF16)


def _compress(kvc, pos_a, pos_b, wa, wb, w2, cos, sin, batch, seq_len):
    ng = seq_len // CMP_STRIDE
    gw = CMP_STRIDE * LANES
    x = kvc.reshape(2, batch, ng, gw)
    hid = NSA_KV_HEADS * CMP_HID
    per_kv = lambda *shape: pl.BlockSpec((1,) + shape, lambda s, b: (s,) + (0,) * len(shape))
    return pl.pallas_call(
        _compress_kernel,
        grid=(2, batch),
        in_specs=[pl.BlockSpec((1, 1, ng, gw), lambda s, b: (s, b, 0, 0)),
                  per_kv(1, gw), per_kv(1, gw), per_kv(gw, hid), per_kv(gw, hid), per_kv(hid, LANES),
                  per_kv(ng, LANES), per_kv(ng, LANES)],
        out_specs=pl.BlockSpec((1, 1, ng, LANES), lambda s, b: (s, b, 0, 0)),
        out_shape=jax.ShapeDtypeStruct((2, batch, ng, LANES), BF16),
        compiler_params=pltpu.CompilerParams(dimension_semantics=("arbitrary", "arbitrary"),
                                             vmem_limit_bytes=V7X_VMEM_LIMIT),
        name="compress",
    )(x, pos_a, pos_b, wa, wb, w2, cos, sin)


def _nsa_kernel(nq_ref, ng_ref, kc_ref, vc_ref, ksl_ref, vsl_ref, kwn_ref, vwn_ref, ov_ref, ex_ref,
                o_ref, qaug_ref, s_ref, m_ref, acc_ref, *, seq_len, topk):
    QB = NSA_QBLOCK
    R = NSA_GROUP
    G = NSA_KV_HEADS
    M = R * QB
    NC = kc_ref.shape[2]
    KT = SLC_KTILE
    WK = WINDOW + QB
    t0 = pl.program_id(1) * QB
    nt_dims = (((1,), (1,)), ((), ()))

    low_half = lax.broadcasted_iota(jnp.int32, (QB, LANES), 1) < NSA_DK
    q = nq_ref[...].astype(F32)

    def softmax_rows(s3, mask):
        s3 = jnp.where(mask[None], s3, NEG)
        m = jnp.max(s3, axis=-1, keepdims=True)
        e = jnp.where(mask[None], jnp.exp2(s3 - m), 0.0)
        den = jnp.maximum(jnp.sum(e, axis=-1, keepdims=True), 1e-30)
        return e * (1.0 / den)

    for g in range(G):
        head_lanes = low_half if g == 0 else jnp.logical_not(low_half)
        for i in range(R):
            qaug_ref[g, i * QB:(i + 1) * QB, 0:LANES] = jnp.where(
                head_lanes, q[:, i * LANES:(i + 1) * LANES], 0.0).astype(BF16)

    n_idx = lax.broadcasted_iota(jnp.int32, (QB, NC), 1)
    t_row = t0 + lax.broadcasted_iota(jnp.int32, (QB, NC), 0)
    cmp_mask = n_idx * CMP_STRIDE + (CMP_BLOCK - 1) <= t_row
    jj = lax.broadcasted_iota(jnp.int32, (LANES, QB), 0)
    q_blk = (t0 + lax.broadcasted_iota(jnp.int32, (LANES, QB), 1)) // SLC_BLOCK
    valid = jj <= q_blk
    forced = (jj == 0) | (valid & (jj > q_blk - SLC_LOCAL))
    jf = jj.astype(F32)
    o_cmp, scores = [], []
    for g in range(G):
        sc = lax.dot_general(qaug_ref[g, :, 0:LANES], kc_ref[0, 0], nt_dims, preferred_element_type=F32)
        p_cmp = softmax_rows(sc.reshape(R, QB, NC), cmp_mask)
        o_cmp.append(jnp.dot(p_cmp.reshape(M, NC).astype(BF16), vc_ref[0, 0], preferred_element_type=F32))
        p_hi, p_lo = _split_bf16(jnp.sum(p_cmp, axis=0))
        imp = (jnp.dot(p_hi, ov_ref[...], preferred_element_type=F32)
               + jnp.dot(p_lo, ov_ref[...], preferred_element_type=F32))
        scores.append(jnp.where(forced, SEL_BIG, jnp.where(valid, imp.T, -SEL_BIG)))

    def pick_round(_, carry):
        out = []
        for score, sel in carry:
            best = jnp.max(score, axis=0, keepdims=True)
            first = jnp.min(jnp.where(score == best, jf, float(LANES)), axis=0, keepdims=True)
            pick = jf == first
            out.append((jnp.where(pick, -jnp.inf, score), jnp.where(pick, 1.0, sel)))
        return tuple(out)

    picked = lax.fori_loop(0, topk, pick_round,
                           tuple((s, jnp.zeros((LANES, QB), F32)) for s in scores))
    for g in range(G):
        bias = jnp.where((picked[g][1] > 0.0) & valid, 0.0, NEG).T.astype(BF16)
        for i in range(R):
            qaug_ref[g, i * QB:(i + 1) * QB, LANES:2 * LANES] = bias

    def slc_scores(k0, slot):
        for g in range(G):
            s_ref[slot, g] = lax.dot_general(qaug_ref[g], ksl_ref[0, pl.ds(k0, KT), :], nt_dims,
                                             preferred_element_type=F32)

    def slc_accumulate(k0, slot, causal):
        for g in range(G):
            s = s_ref[slot, g]
            if causal:
                k_pos = k0 + lax.broadcasted_iota(jnp.int32, (QB, KT), 1)
                t_pos = t0 + lax.broadcasted_iota(jnp.int32, (QB, KT), 0)
                s = jnp.where((k_pos <= t_pos)[None], s.reshape(R, QB, KT), NEG).reshape(M, KT)
            m_run = m_ref[g]
            m_new = jnp.maximum(m_run, jnp.max(s, axis=-1, keepdims=True))
            p = jnp.exp2(s - m_new).astype(BF16)
            pv = jnp.dot(p, vsl_ref[0, pl.ds(k0, KT), g * LANES:(g + 1) * LANES],
                         preferred_element_type=F32)
            acc_ref[g] = jnp.exp2(m_run - m_new) * acc_ref[g] + pv
            m_ref[g] = m_new

    m_ref[...] = jnp.full(m_ref.shape, NEG, F32)
    acc_ref[...] = jnp.zeros_like(acc_ref)
    n_full = t0 // KT
    slc_scores(0, 0)

    def slc_pair(j, _):
        k0 = pl.multiple_of(j * (2 * KT), 2 * KT)
        slc_scores(k0 + KT, 1)
        slc_accumulate(k0, 0, False)
        slc_scores(k0 + 2 * KT, 0)
        slc_accumulate(k0 + KT, 1, False)
        return 0

    lax.fori_loop(0, n_full // 2, slc_pair, 0)
    k_diag = pl.multiple_of(n_full * KT, KT)

    @pl.when(n_full % 2 == 0)
    def _():
        slc_accumulate(k_diag, 0, True)

    @pl.when(n_full % 2 == 1)
    def _():
        slc_scores(k_diag, 1)
        slc_accumulate(k_diag - KT, 0, False)
        slc_accumulate(k_diag, 1, True)

    o_slc = [acc_ref[g] * (1.0 / pltpu.roll(acc_ref[g], NSA_DK, 1)) for g in range(G)]

    ks = pl.multiple_of(jnp.clip(t0 - WINDOW, 0, seq_len - WK), QB)
    k_pos = ks + lax.broadcasted_iota(jnp.int32, (QB, WK), 1)
    t_pos = t0 + lax.broadcasted_iota(jnp.int32, (QB, WK), 0)
    win_mask = (k_pos <= t_pos) & (k_pos > t_pos - WINDOW)
    o_win = []
    for g in range(G):
        sw = lax.dot_general(qaug_ref[g, :, 0:LANES], kwn_ref[0, pl.ds(ks, WK), :], nt_dims,
                             preferred_element_type=F32)
        p_win = softmax_rows(sw.reshape(R, QB, WK), win_mask)
        o_win.append(jnp.dot(p_win.reshape(M, WK).astype(BF16), vwn_ref[0, pl.ds(ks, WK), :],
                             preferred_element_type=F32))

    logits = ng_ref[...]
    g_hi, g_lo = _split_bf16(1.0 / (1.0 + jnp.exp(-logits)))
    gates = (jnp.dot(g_hi, ex_ref[...], preferred_element_type=F32)
             + jnp.dot(g_lo, ex_ref[...], preferred_element_type=F32))
    for i in range(R):
        rows = slice(i * QB, (i + 1) * QB)
        cols = slice(i * LANES, (i + 1) * LANES)
        y = jnp.zeros((QB, LANES), F32)
        for branch, o in enumerate((o_cmp, o_slc, o_win)):
            o_slab = jnp.where(low_half, o[0][rows], o[1][rows])
            y = y + gates[:, branch * D_NSA + i * LANES:branch * D_NSA + (i + 1) * LANES] * o_slab
        o_ref[:, cols] = y.astype(BF16)


def _nsa(nq, ng, kvc_cmp, ksl, vsl, kwn, vwn, overlap, expand, batch, seq_len):
    T = nq.shape[0]
    QB = NSA_QBLOCK
    nqb = seq_len // QB
    nc = seq_len // CMP_STRIDE
    topk = min(SLC_TOPK, seq_len // SLC_BLOCK)
    tok = lambda n: pl.BlockSpec((QB, n), lambda b, i: (b * nqb + i, 0))
    seq = lambda n: pl.BlockSpec((1, seq_len, n), lambda b, i: (b, 0, 0))
    return pl.pallas_call(
        functools.partial(_nsa_kernel, seq_len=seq_len, topk=topk),
        grid=(batch, nqb),
        in_specs=[tok(D_NSA), tok(LANES),
                  pl.BlockSpec((1, 1, nc, LANES), lambda b, i: (0, b, 0, 0)),
                  pl.BlockSpec((1, 1, nc, LANES), lambda b, i: (1, b, 0, 0)),
                  seq(2 * LANES), seq(2 * LANES), seq(LANES), seq(LANES),
                  pl.BlockSpec(overlap.shape, lambda b, i: (0, 0)),
                  pl.BlockSpec(expand.shape, lambda b, i: (0, 0))],
        out_specs=tok(D_NSA),
        out_shape=jax.ShapeDtypeStruct((T, D_NSA), BF16),
        scratch_shapes=[pltpu.VMEM((NSA_KV_HEADS, NSA_GROUP * QB, 2 * LANES), BF16),
                        pltpu.VMEM((2, NSA_KV_HEADS, NSA_GROUP * QB, SLC_KTILE), F32),
                        pltpu.VMEM((NSA_KV_HEADS, NSA_GROUP * QB, 1), F32),
                        pltpu.VMEM((NSA_KV_HEADS, NSA_GROUP * QB, LANES), F32)],
        compiler_params=pltpu.CompilerParams(dimension_semantics=("arbitrary", "arbitrary"),
                                             vmem_limit_bytes=V7X_VMEM_LIMIT),
        name="nsa",
    )(nq, ng, kvc_cmp, kvc_cmp, ksl.reshape(batch, seq_len, -1), vsl.reshape(batch, seq_len, -1),
      kwn.reshape(batch, seq_len, -1), vwn.reshape(batch, seq_len, -1), overlap, expand)


def _outproj_kernel(yr_ref, yn_ref, x_ref, w_ref, g_ref, o_ref):
    mix = (jnp.dot(yr_ref[...], w_ref[0:D_RET, :], preferred_element_type=F32)
           + jnp.dot(yn_ref[...], w_ref[D_RET:D_RET + D_NSA, :], preferred_element_type=F32))
    n = mix * lax.rsqrt(jnp.mean(mix * mix, axis=-1, keepdims=True) + NORM_EPS) * g_ref[...]
    o_ref[...] = x_ref[...] + n


def _outproj(y_ret, y_nsa, x2, w, gain):
    T, D = x2.shape
    tm = TOK_TILE
    tok = lambda n: pl.BlockSpec((tm, n), lambda i: (i, 0))
    return pl.pallas_call(
        _outproj_kernel,
        grid=(T // tm,),
        in_specs=[tok(D_RET), tok(D_NSA), tok(D),
                  pl.BlockSpec(w.shape, lambda i: (0, 0)),
                  pl.BlockSpec((1, D), lambda i: (0, 0))],
        out_specs=tok(D),
        out_shape=jax.ShapeDtypeStruct((T, D), F32),
        compiler_params=pltpu.CompilerParams(dimension_semantics=("arbitrary",),
                                             vmem_limit_bytes=V7X_VMEM_LIMIT),
        name="outproj",
    )(y_ret, y_nsa, x2, w, gain)


def _ffn_kernel(x_ref, xp_ref, gpre_ref, wg_ref, wv_ref, cg_ref, cv_ref, wd_ref, gpost_ref, o_ref,
                h_ref, ug_ref, uv_ref, acc_ref, *, tiles_per_seq):
    tm = x_ref.shape[0]
    H = FFN_HALO
    f = pl.program_id(1)

    def normed(x):
        return x * lax.rsqrt(jnp.mean(x * x, axis=-1, keepdims=True) + NORM_EPS) * gpre_ref[...]

    @pl.when(f == 0)
    def _():
        keep = jnp.where(pl.program_id(0) % tiles_per_seq == 0, 0.0, 1.0)
        h_ref[0:H, :] = (normed(xp_ref[...]) * keep).astype(BF16)
        h_ref[H:H + tm, :] = normed(x_ref[...]).astype(BF16)
        acc_ref[...] = jnp.zeros_like(acc_ref)

    h = h_ref[...]
    ug_ref[...] = jnp.dot(h, wg_ref[...], preferred_element_type=F32)
    uv_ref[...] = jnp.dot(h, wv_ref[...], preferred_element_type=F32)

    def causal_conv(u_ref, c_ref):
        out = c_ref[CONV_WIDTH - 1:CONV_WIDTH, :] * u_ref[H:H + tm, :]
        for k in range(CONV_WIDTH - 1):
            d = CONV_WIDTH - 1 - k
            out = out + c_ref[k:k + 1, :] * u_ref[H - d:H - d + tm, :]
        return out

    act = _gelu_tanh(causal_conv(ug_ref, cg_ref)) * causal_conv(uv_ref, cv_ref)
    acc_ref[...] += jnp.dot(act.astype(BF16), wd_ref[...], preferred_element_type=F32)

    @pl.when(f == pl.num_programs(1) - 1)
    def _():
        y = acc_ref[...]
        n = y * lax.rsqrt(jnp.mean(y * y, axis=-1, keepdims=True) + NORM_EPS) * gpost_ref[...]
        o_ref[...] = x_ref[...] + n


def _ffn(x2, g_pre, w_up, conv_w, w_down, g_post, seq_len):
    T, D = x2.shape
    d_ff = w_down.shape[0]
    tm, tf, H = FFN_TILE, FFN_FTILE, FFN_HALO
    nf = d_ff // tf
    return pl.pallas_call(
        functools.partial(_ffn_kernel, tiles_per_seq=seq_len // tm),
        grid=(T // tm, nf),
        in_specs=[pl.BlockSpec((tm, D), lambda i, f: (i, 0)),
                  pl.BlockSpec((H, D), lambda i, f: (jnp.maximum(i * (tm // H) - 1, 0), 0)),
                  pl.BlockSpec((1, D), lambda i, f: (0, 0)),
                  pl.BlockSpec((D, tf), lambda i, f: (0, f)),
                  pl.BlockSpec((D, tf), lambda i, f: (0, nf + f)),
                  pl.BlockSpec((CONV_WIDTH, tf), lambda i, f: (0, f)),
                  pl.BlockSpec((CONV_WIDTH, tf), lambda i, f: (0, nf + f)),
                  pl.BlockSpec((tf, D), lambda i, f: (f, 0)),
                  pl.BlockSpec((1, D), lambda i, f: (0, 0))],
        out_specs=pl.BlockSpec((tm, D), lambda i, f: (i, 0)),
        out_shape=jax.ShapeDtypeStruct((T, D), F32),
        scratch_shapes=[pltpu.VMEM((tm + H, D), BF16), pltpu.VMEM((tm + H, tf), F32),
                        pltpu.VMEM((tm + H, tf), F32), pltpu.VMEM((tm, D), F32)],
        compiler_params=pltpu.CompilerParams(dimension_semantics=("arbitrary", "arbitrary"),
                                             vmem_limit_bytes=V7X_VMEM_LIMIT),
        name="ffn",
    )(x2, x2, g_pre, w_up, w_up, conv_w, conv_w, w_down, g_post)


_NSA_HEAD_ORDER = [g * NSA_GROUP + i for i in range(NSA_GROUP) for g in range(NSA_KV_HEADS)]


def _rope_tables(pos):
    inv = 1.0 / (ROPE_THETA ** (jnp.arange(0, NSA_DK, 2, dtype=F32) / NSA_DK))
    ang = pos.astype(F32)[:, None] * inv[None, :]
    c, s = jnp.cos(ang), jnp.sin(ang)
    return jnp.concatenate([c, c, c, c], axis=1), jnp.concatenate([-s, s, -s, s], axis=1)


def _prep_w_in(w):
    d = w.shape[0]
    splits = np.cumsum([RET_QK, RET_QK, D_RET, D_RET, D_NSA] + [NSA_KV] * 6)
    rq, rk, rv, rg, nq, kcm, vcm, ksl, vsl, kwn, vwn, ng = jnp.split(w, [int(s) for s in splits], axis=1)
    nq = nq.reshape(d, NSA_HEADS, NSA_DK)[:, np.array(_NSA_HEAD_ORDER)].reshape(d, D_NSA)
    ng = jnp.pad(ng, ((0, 0), (0, LANES - NSA_GATES)))
    return jnp.concatenate([rq, rk, nq, ksl, kwn, rv, vsl, vwn, rg, kcm, vcm, ng], axis=1).astype(BF16)


def _prep_w_out(w):
    d = w.shape[1]
    w_nsa = w[D_RET:].reshape(NSA_HEADS, NSA_DK, d)[np.array(_NSA_HEAD_ORDER)].reshape(D_NSA, d)
    return jnp.concatenate([w[:D_RET], w_nsa], axis=0).astype(BF16)


def _prep_compress(pos, w1, w2):
    G = NSA_KV_HEADS
    eye = jnp.eye(G, dtype=F32)
    half = CMP_STRIDE

    def first_layer(w_half):
        w4 = w_half.reshape(half, NSA_DK, CMP_HID)
        return jnp.einsum('ldc,gh->lgdhc', w4, eye).reshape(half * G * NSA_DK, G * CMP_HID).astype(BF16)

    def pos_row(p_half):
        return jnp.broadcast_to(p_half[:, None, :], (half, G, NSA_DK)).reshape(1, -1)

    w2x = jnp.einsum('cd,gh->gchd', w2, eye).reshape(G * CMP_HID, G * NSA_DK).astype(BF16)
    return (pos_row(pos[:half]), pos_row(pos[half:]),
            first_layer(w1[:half * NSA_DK]), first_layer(w1[half * NSA_DK:]), w2x)


def _overlap_matrix(seq_len):
    nc = seq_len // CMP_STRIDE
    cmp_start = np.arange(nc) * CMP_STRIDE
    slc_start = np.arange(LANES) * SLC_BLOCK
    ov = ((cmp_start[:, None] < slc_start[None, :] + SLC_BLOCK)
          & (cmp_start[:, None] + CMP_BLOCK > slc_start[None, :]))
    n_cmp = (seq_len - CMP_BLOCK) // CMP_STRIDE + 1
    ov &= (np.arange(nc) < n_cmp)[:, None]
    return jnp.asarray(ov.astype(np.float32), dtype=BF16)


def _gate_expand_matrix():
    ex = np.zeros((LANES, 3 * D_NSA), np.float32)
    for branch in range(3):
        for p in range(D_NSA):
            head = _NSA_HEAD_ORDER[p // NSA_DK]
            ex[branch * NSA_HEADS + head, branch * D_NSA + p] = 1.0
    return jnp.asarray(ex, dtype=BF16)


def kernel(x, norm_mix_pre, w_in, ret_gn_w, cmp_k_pos, cmp_k_w1, cmp_k_w2, cmp_v_pos, cmp_v_w1, cmp_v_w2,
           w_out, norm_mix_post, norm_ffn_pre, ffn_w_up, ffn_conv, ffn_w_down, norm_ffn_post):
    B, S, D = x.shape
    depth = w_in.shape[0]
    assert S % SLC_KTILE == 0 and S % FFN_TILE == 0 and S // SLC_BLOCK <= LANES and S >= WINDOW + NSA_QBLOCK
    assert ffn_w_down.shape[1] % FFN_FTILE == 0

    cos, sin = _rope_tables(jnp.arange(S, dtype=jnp.int32))
    nc = S // CMP_STRIDE
    ccos, csin = _rope_tables(jnp.arange(nc, dtype=jnp.int32) * CMP_STRIDE + (CMP_BLOCK - 1))
    cmp_cos = jnp.stack([ccos, jnp.ones_like(ccos)])
    cmp_sin = jnp.stack([csin, jnp.zeros_like(csin)])
    overlap = _overlap_matrix(S)
    expand = _gate_expand_matrix()

    x2 = x.reshape(B * S, D)
    for l in range(depth):
        outs = _inproj(x2, norm_mix_pre[l][None], _prep_w_in(w_in[l]), cos, sin, S)
        rq, rk, nq, ksl, kwn, rv, vsl, vwn, rg, kvc, ng = outs
        y_ret = _retention(rq, rk, rv, rg, ret_gn_w[l][None], B, S)
        kp = _prep_compress(cmp_k_pos[l], cmp_k_w1[l], cmp_k_w2[l])
        vp = _prep_compress(cmp_v_pos[l], cmp_v_w1[l], cmp_v_w2[l])
        cmp_args = [jnp.stack([a, b]) for a, b in zip(kp, vp)]
        kvc_cmp = _compress(kvc, *cmp_args, cmp_cos, cmp_sin, B, S)
        y_nsa = _nsa(nq, ng, kvc_cmp, ksl, vsl, kwn, vwn, overlap, expand, B, S)
        x2 = _outproj(y_ret, y_nsa, x2, _prep_w_out(w_out[l]), norm_mix_post[l][None])
        x2 = _ffn(x2, norm_ffn_pre[l][None], ffn_w_up[l].astype(BF16), ffn_conv[l],
                  ffn_w_down[l].astype(BF16), norm_ffn_post[l][None], S)
    return x2.reshape(B, S, D)
```

```python
import functools
import math

import jax
import jax.numpy as jnp
import numpy as np
from jax import lax
from jax.experimental import pallas as pl
from jax.experimental.pallas import tpu as pltpu

F32 = jnp.float32
BF16 = jnp.bfloat16

LANES = 128
SUBLANES = 8
V7X_VMEM_LIMIT = 56 * 1024 * 1024

ROPE_THETA = 10000.0
NORM_EPS = 1e-6
GN_EPS = 1e-5
NEG = -1e30
SEL_BIG = 1e9

RET_HEADS = 4
RET_DK = 64
RET_DV = 128
RET_CHUNK = 128
NSA_HEADS = 8
NSA_KV_HEADS = 2
NSA_DK = 64
NSA_GROUP = NSA_HEADS // NSA_KV_HEADS
CMP_BLOCK = 32
CMP_STRIDE = 16
CMP_HID = 256
SLC_BLOCK = 64
SLC_TOPK = 16
SLC_LOCAL = 2
WINDOW = 512
NSA_QBLOCK = 128
NSA_GATES = 3 * NSA_HEADS
CONV_WIDTH = 3

NSA_Q_SCALE = NSA_DK ** -0.5 * math.log2(math.e)

D_RET = RET_HEADS * RET_DV
D_NSA = NSA_HEADS * NSA_DK
RET_QK = RET_HEADS * RET_DK
NSA_KV = NSA_KV_HEADS * NSA_DK

COL_RQ = 0
COL_RK = COL_RQ + RET_QK
COL_NQ = COL_RK + RET_QK
COL_KSL = COL_NQ + D_NSA
COL_KWN = COL_KSL + NSA_KV
ROPE_COLS = COL_KWN + NSA_KV
COL_RV = ROPE_COLS
COL_VSL = COL_RV + D_RET
COL_VWN = COL_VSL + NSA_KV
COL_RG = COL_VWN + NSA_KV
COL_KCM = COL_RG + D_RET
COL_VCM = COL_KCM + NSA_KV
COL_NG = COL_VCM + NSA_KV
IN_COLS_PAD = COL_NG + LANES

TOK_TILE = 512
RET_TILE = 512
FFN_TILE = 1024
FFN_FTILE = 256
FFN_HALO = 16
FFN_ROW_CHUNKS = 4
FFN_GATE_CHUNKS = 16
FFN_LAST_SLOT = 0
SLC_KTILE = 512


GELU_C = math.sqrt(2.0 / math.pi)


def _gelu_tanh(x):
    return 0.5 * x * (1.0 + jnp.tanh(GELU_C * (x + 0.044715 * (x * x * x))))


def _rope(p, cos, sin_signed, first_half):
    half = NSA_DK // 2
    partner = jnp.where(first_half, pltpu.roll(p, LANES - half, 1), pltpu.roll(p, half, 1))
    return p * cos + partner * sin_signed


def _split_bf16(x):
    hi = x.astype(BF16)
    lo = (x - hi.astype(F32)).astype(BF16)
    return hi, lo


def _inproj_kernel(x_ref, g_ref, w_ref, cos_ref, sin_ref,
                   rq_ref, rk_ref, nq_ref, ksl_ref, kwn_ref, rv_ref, vsl_ref, vwn_ref,
                   rg_ref, kvc_ref, ng_ref, *, seq_len):
    tm = x_ref.shape[0]
    x = x_ref[...]
    h = (x * lax.rsqrt(jnp.mean(x * x, axis=-1, keepdims=True) + NORM_EPS) * g_ref[...]).astype(BF16)
    cos = cos_ref[...]
    sin = sin_ref[...]
    lane = lax.broadcasted_iota(jnp.int32, (tm, LANES), 1)
    first_half = (lane % NSA_DK) < NSA_DK // 2

    def proj(c0, n):
        return jnp.dot(h, w_ref[:, c0:c0 + n], preferred_element_type=F32)

    def rope_slab(p, i):
        return _rope(p[:, i * LANES:(i + 1) * LANES], cos, sin, first_half)

    p = proj(COL_RQ, RET_QK)
    for i in range(RET_QK // LANES):
        rq_ref[:, i * LANES:(i + 1) * LANES] = rope_slab(p, i).astype(BF16)
    p = proj(COL_RK, RET_QK)
    for i in range(RET_QK // LANES):
        rk_ref[:, i * LANES:(i + 1) * LANES] = (rope_slab(p, i) * (RET_DK ** -0.5)).astype(BF16)
    p = proj(COL_NQ, D_NSA)
    for i in range(D_NSA // LANES):
        nq_ref[:, i * LANES:(i + 1) * LANES] = (rope_slab(p, i) * NSA_Q_SCALE).astype(BF16)
    p = proj(COL_KSL, 2 * NSA_KV)
    ksl_ref[:, 0:LANES] = rope_slab(p, 0).astype(BF16)
    kwn_ref[...] = rope_slab(p, 1).astype(BF16)
    row = lax.broadcasted_iota(jnp.int32, (tm, LANES), 0)
    pos = (pl.program_id(0) * tm + row) % seq_len
    ksl_ref[:, LANES:2 * LANES] = jnp.where(lane == pos // SLC_BLOCK, 1.0, 0.0).astype(BF16)

    rv_ref[...] = proj(COL_RV, D_RET).astype(BF16)
    p = proj(COL_VSL, 2 * NSA_KV)
    low_half = lane < NSA_DK
    for v_ref, v in ((vsl_ref, p[:, 0:LANES]), (vwn_ref, p[:, LANES:2 * LANES])):
        v_ref[:, 0:LANES] = jnp.where(low_half, v, 1.0).astype(BF16)
        v_ref[:, LANES:2 * LANES] = jnp.where(low_half, 1.0, v).astype(BF16)
    rg_ref[...] = proj(COL_RG, D_RET)
    p = proj(COL_KCM, 2 * NSA_KV)
    kvc_ref[0] = p[:, 0:LANES]
    kvc_ref[1] = p[:, LANES:2 * LANES]
    ng_ref[...] = proj(COL_NG, LANES)


def _inproj(x2, gain, w, cos, sin, seq_len):
    T, D = x2.shape
    tm = TOK_TILE
    nt = seq_len // tm
    tok = lambda n: pl.BlockSpec((tm, n), lambda i: (i, 0))
    out_shape = (
        jax.ShapeDtypeStruct((T, RET_QK), BF16),
        jax.ShapeDtypeStruct((T, RET_QK), BF16),
        jax.ShapeDtypeStruct((T, D_NSA), BF16),
        jax.ShapeDtypeStruct((T, 2 * LANES), BF16),
        jax.ShapeDtypeStruct((T, LANES), BF16),
        jax.ShapeDtypeStruct((T, D_RET), BF16),
        jax.ShapeDtypeStruct((T, 2 * LANES), BF16),
        jax.ShapeDtypeStruct((T, 2 * LANES), BF16),
        jax.ShapeDtypeStruct((T, D_RET), F32),
        jax.ShapeDtypeStruct((2, T, LANES), F32),
        jax.ShapeDtypeStruct((T, LANES), F32),
    )
    out_specs = (tok(RET_QK), tok(RET_QK), tok(D_NSA), tok(2 * LANES), tok(LANES), tok(D_RET),
                 tok(2 * LANES), tok(2 * LANES), tok(D_RET),
                 pl.BlockSpec((2, tm, LANES), lambda i: (0, i, 0)), tok(LANES))
    return pl.pallas_call(
        functools.partial(_inproj_kernel, seq_len=seq_len),
        grid=(T // tm,),
        in_specs=[tok(D),
                  pl.BlockSpec((1, D), lambda i: (0, 0)),
                  pl.BlockSpec((D, IN_COLS_PAD), lambda i: (0, 0)),
                  pl.BlockSpec((tm, LANES), lambda i: (i % nt, 0)),
                  pl.BlockSpec((tm, LANES), lambda i: (i % nt, 0))],
        out_specs=out_specs,
        out_shape=out_shape,
        compiler_params=pltpu.CompilerParams(dimension_semantics=("arbitrary",),
                                             vmem_limit_bytes=V7X_VMEM_LIMIT),
        name="inproj",
    )(x2, gain, w, cos, sin)


def _retention_kernel(q_ref, k_ref, v_ref, g_ref, gnw_ref, o_ref, state_ref):
    C = RET_CHUNK
    n_chunks = q_ref.shape[0] // C

    @pl.when(pl.program_id(1) == 0)
    def _():
        state_ref[...] = jnp.zeros_like(state_ref)

    ii = lax.broadcasted_iota(jnp.int32, (C, C), 0)
    jj = lax.broadcasted_iota(jnp.int32, (C, C), 1)
    diff = (ii - jj).astype(F32)
    i_col = lax.broadcasted_iota(jnp.int32, (C, 1), 0).astype(F32)
    low_half = lax.broadcasted_iota(jnp.int32, (C, LANES), 1) < RET_DK
    nt_dims = (((1,), (1,)), ((), ()))
    tn_dims = (((0,), (0,)), ((), ()))

    for h in range(RET_HEADS):
        log_gamma = math.log(1.0 - 2.0 ** (-5.0 - h))
        decay = jnp.where(diff >= 0, jnp.exp(log_gamma * jnp.maximum(diff, 0.0)), 0.0)
        xi = jnp.exp(log_gamma * (i_col + 1.0))
        zeta = jnp.exp(log_gamma * (C - 1.0 - i_col))
        chunk_decay = math.exp(log_gamma * C)
        head_lanes = low_half if h % 2 == 0 else jnp.logical_not(low_half)
        qk_cols = slice((h // 2) * LANES, (h // 2 + 1) * LANES)
        v_cols = slice(h * RET_DV, (h + 1) * RET_DV)
        gn_w = gnw_ref[:, v_cols]
        for c in range(n_chunks):
            rows = slice(c * C, (c + 1) * C)
            qm = jnp.where(head_lanes, q_ref[rows, qk_cols].astype(F32), 0.0).astype(BF16)
            ks = k_ref[rows, qk_cols]
            v = v_ref[rows, v_cols]
            scores = lax.dot_general(qm, ks, nt_dims, preferred_element_type=F32) * decay
            o = jnp.dot(scores.astype(BF16), v, preferred_element_type=F32)
            state = state_ref[h]
            o = o + jnp.dot(qm, state.astype(BF16), preferred_element_type=F32) * xi
            kz = (ks.astype(F32) * zeta).astype(BF16)
            kv = lax.dot_general(kz, v, tn_dims, preferred_element_type=F32)
            state_ref[h] = state * chunk_decay + kv
            mu = jnp.mean(o, axis=-1, keepdims=True)
            var = jnp.mean(jnp.square(o - mu), axis=-1, keepdims=True)
            on = (o - mu) * lax.rsqrt(var + GN_EPS) * gn_w
            gate = g_ref[rows, v_cols]
            o_ref[rows, v_cols] = (gate * (1.0 / (1.0 + jnp.exp(-gate))) * on).astype(BF16)


def _retention(rq, rk, rv, rg, gn_w, batch, seq_len):
    T = rq.shape[0]
    tc = RET_TILE
    nt = seq_len // tc
    tok = lambda n: pl.BlockSpec((tc, n), lambda b, i: (b * nt + i, 0))
    return pl.pallas_call(
        _retention_kernel,
        grid=(batch, nt),
        in_specs=[tok(RET_QK), tok(RET_QK), tok(D_RET), tok(D_RET),
                  pl.BlockSpec((1, D_RET), lambda b, i: (0, 0))],
        out_specs=tok(D_RET),
        out_shape=jax.ShapeDtypeStruct((T, D_RET), BF16),
        scratch_shapes=[pltpu.VMEM((RET_HEADS, LANES, RET_DV), F32)],
        compiler_params=pltpu.CompilerParams(dimension_semantics=("arbitrary", "arbitrary"),
                                             vmem_limit_bytes=V7X_VMEM_LIMIT),
        name="retention",
    )(rq, rk, rv, rg, gn_w)


def _compress_kernel(x_ref, pa_ref, pb_ref, wa_ref, wb_ref, w2_ref, cos_ref, sin_ref, o_ref):
    ng = x_ref.shape[2]
    x = x_ref[0, 0]
    xa = (x + pa_ref[0]).astype(BF16)
    xb = (x + pb_ref[0]).astype(BF16)
    a = jnp.dot(xa, wa_ref[0], preferred_element_type=F32)
    b = jnp.dot(xb, wb_ref[0], preferred_element_type=F32)
    hid = a + pltpu.roll(b, ng - 1, 0)
    out = jnp.dot(_gelu_tanh(hid).astype(BF16), w2_ref[0], preferred_element_type=F32)
    lane = lax.broadcasted_iota(jnp.int32, out.shape, 1)
    o_ref[0, 0] = _rope(out, cos_ref[0], sin_ref[0], (lane % NSA_DK) < NSA_DK // 2).astype(BF16)


def _compress(kvc, pos_a, pos_b, wa, wb, w2, cos, sin, batch, seq_len):
    ng = seq_len // CMP_STRIDE
    gw = CMP_STRIDE * LANES
    x = kvc.reshape(2, batch, ng, gw)
    hid = NSA_KV_HEADS * CMP_HID
    per_kv = lambda *shape: pl.BlockSpec((1,) + shape, lambda s, b: (s,) + (0,) * len(shape))
    return pl.pallas_call(
        _compress_kernel,
        grid=(2, batch),
        in_specs=[pl.BlockSpec((1, 1, ng, gw), lambda s, b: (s, b, 0, 0)),
                  per_kv(1, gw), per_kv(1, gw), per_kv(gw, hid), per_kv(gw, hid), per_kv(hid, LANES),
                  per_kv(ng, LANES), per_kv(ng, LANES)],
        out_specs=pl.BlockSpec((1, 1, ng, LANES), lambda s, b: (s, b, 0, 0)),
        out_shape=jax.ShapeDtypeStruct((2, batch, ng, LANES), BF16),
        compiler_params=pltpu.CompilerParams(dimension_semantics=("arbitrary", "arbitrary"),
                                             vmem_limit_bytes=V7X_VMEM_LIMIT),
        name="compress",
    )(x, pos_a, pos_b, wa, wb, w2, cos, sin)


def _nsa_kernel(nq_ref, ng_ref, kc_ref, vc_ref, ksl_ref, vsl_ref, kwn_ref, vwn_ref, ov_ref, ex_ref,
                o_ref, qaug_ref, s_ref, m_ref, acc_ref, *, seq_len, topk):
    QB = NSA_QBLOCK
    R = NSA_GROUP
    G = NSA_KV_HEADS
    M = R * QB
    NC = kc_ref.shape[2]
    KT = SLC_KTILE
    WK = WINDOW + QB
    t0 = pl.program_id(1) * QB
    nt_dims = (((1,), (1,)), ((), ()))

    low_half = lax.broadcasted_iota(jnp.int32, (QB, LANES), 1) < NSA_DK
    q = nq_ref[...].astype(F32)

    def softmax_rows(s3, mask):
        s3 = jnp.where(mask[None], s3, NEG)
        m = jnp.max(s3, axis=-1, keepdims=True)
        e = jnp.where(mask[None], jnp.exp2(s3 - m), 0.0)
        den = jnp.maximum(jnp.sum(e, axis=-1, keepdims=True), 1e-30)
        return e * (1.0 / den)

    for g in range(G):
        head_lanes = low_half if g == 0 else jnp.logical_not(low_half)
        for i in range(R):
            qaug_ref[g, i * QB:(i + 1) * QB, 0:LANES] = jnp.where(
                head_lanes, q[:, i * LANES:(i + 1) * LANES], 0.0).astype(BF16)

    n_idx = lax.broadcasted_iota(jnp.int32, (QB, NC), 1)
    t_row = t0 + lax.broadcasted_iota(jnp.int32, (QB, NC), 0)
    cmp_mask = n_idx * CMP_STRIDE + (CMP_BLOCK - 1) <= t_row
    jj = lax.broadcasted_iota(jnp.int32, (LANES, QB), 0)
    q_blk = (t0 + lax.broadcasted_iota(jnp.int32, (LANES, QB), 1)) // SLC_BLOCK
    valid = jj <= q_blk
    forced = (jj == 0) | (valid & (jj > q_blk - SLC_LOCAL))
    jf = jj.astype(F32)
    o_cmp, scores = [], []
    for g in range(G):
        sc = lax.dot_general(qaug_ref[g, :, 0:LANES], kc_ref[0, 0], nt_dims, preferred_element_type=F32)
        p_cmp = softmax_rows(sc.reshape(R, QB, NC), cmp_mask)
        o_cmp.append(jnp.dot(p_cmp.reshape(M, NC).astype(BF16), vc_ref[0, 0], preferred_element_type=F32))
        p_hi, p_lo = _split_bf16(jnp.sum(p_cmp, axis=0))
        imp = (jnp.dot(p_hi, ov_ref[...], preferred_element_type=F32)
               + jnp.dot(p_lo, ov_ref[...], preferred_element_type=F32))
        scores.append(jnp.where(forced, -jnp.inf, jnp.where(valid, imp.T, -SEL_BIG)))

    def pick_round(_, carry):
        out = []
        for score in carry:
            best = jnp.max(score, axis=0, keepdims=True)
            first = jnp.min(jnp.where(score == best, jf, float(LANES)), axis=0, keepdims=True)
            out.append(jnp.where(jf == first, -jnp.inf, score))
        return tuple(out)

    taken = lax.fori_loop(0, topk - (1 + SLC_LOCAL), pick_round, tuple(scores))
    for g in range(G):
        bias = jnp.where((taken[g] == -jnp.inf) & valid, 0.0, NEG).T.astype(BF16)
        for i in range(R):
            qaug_ref[g, i * QB:(i + 1) * QB, LANES:2 * LANES] = bias

    def slc_scores(k0, slot):
        for g in range(G):
            s_ref[slot, g] = lax.dot_general(qaug_ref[g], ksl_ref[0, pl.ds(k0, KT), :], nt_dims,
                                             preferred_element_type=F32)

    def slc_accumulate(k0, slot, causal):
        for g in range(G):
            s = s_ref[slot, g]
            if causal:
                k_pos = k0 + lax.broadcasted_iota(jnp.int32, (QB, KT), 1)
                t_pos = t0 + lax.broadcasted_iota(jnp.int32, (QB, KT), 0)
                s = jnp.where((k_pos <= t_pos)[None], s.reshape(R, QB, KT), NEG).reshape(M, KT)
            m_run = m_ref[g]
            m_new = jnp.maximum(m_run, jnp.max(s, axis=-1, keepdims=True))
            p = jnp.exp2(s - m_new).astype(BF16)
            pv = jnp.dot(p, vsl_ref[0, pl.ds(k0, KT), g * LANES:(g + 1) * LANES],
                         preferred_element_type=F32)
            acc_ref[g] = jnp.exp2(m_run - m_new) * acc_ref[g] + pv
            m_ref[g] = m_new

    m_ref[...] = jnp.full(m_ref.shape, NEG, F32)
    acc_ref[...] = jnp.zeros_like(acc_ref)
    n_full = t0 // KT
    slc_scores(0, 0)

    def slc_pair(j, _):
        k0 = pl.multiple_of(j * (2 * KT), 2 * KT)
        slc_scores(k0 + KT, 1)
        slc_accumulate(k0, 0, False)
        slc_scores(k0 + 2 * KT, 0)
        slc_accumulate(k0 + KT, 1, False)
        return 0

    lax.fori_loop(0, n_full // 2, slc_pair, 0)
    k_diag = pl.multiple_of(n_full * KT, KT)

    @pl.when(n_full % 2 == 0)
    def _():
        slc_accumulate(k_diag, 0, True)

    @pl.when(n_full % 2 == 1)
    def _():
        slc_scores(k_diag, 1)
        slc_accumulate(k_diag - KT, 0, False)
        slc_accumulate(k_diag, 1, True)

    o_slc = [acc_ref[g] * (1.0 / pltpu.roll(acc_ref[g], NSA_DK, 1)) for g in range(G)]

    ks = pl.multiple_of(jnp.clip(t0 - WINDOW, 0, seq_len - WK), QB)
    k_pos = ks + lax.broadcasted_iota(jnp.int32, (QB, WK), 1)
    t_pos = t0 + lax.broadcasted_iota(jnp.int32, (QB, WK), 0)
    win_mask = (k_pos <= t_pos) & (k_pos > t_pos - WINDOW)
    o_win = []
    for g in range(G):
        sw = lax.dot_general(qaug_ref[g, :, 0:LANES], kwn_ref[0, pl.ds(ks, WK), :], nt_dims,
                             preferred_element_type=F32)
        sw = jnp.where(win_mask[None], sw.reshape(R, QB, WK), NEG)
        e_win = jnp.exp2(sw - jnp.max(sw, axis=-1, keepdims=True)).reshape(M, WK).astype(BF16)
        pv = jnp.dot(e_win, vwn_ref[0, pl.ds(ks, WK), g * LANES:(g + 1) * LANES],
                     preferred_element_type=F32)
        o_win.append(pv * (1.0 / pltpu.roll(pv, NSA_DK, 1)))

    logits = ng_ref[...]
    g_hi, g_lo = _split_bf16(1.0 / (1.0 + jnp.exp(-logits)))
    gates = (jnp.dot(g_hi, ex_ref[...], preferred_element_type=F32)
             + jnp.dot(g_lo, ex_ref[...], preferred_element_type=F32))
    for i in range(R):
        rows = slice(i * QB, (i + 1) * QB)
        cols = slice(i * LANES, (i + 1) * LANES)
        y = jnp.zeros((QB, LANES), F32)
        for branch, o in enumerate((o_cmp, o_slc, o_win)):
            o_slab = jnp.where(low_half, o[0][rows], o[1][rows])
            y = y + gates[:, branch * D_NSA + i * LANES:branch * D_NSA + (i + 1) * LANES] * o_slab
        o_ref[:, cols] = y.astype(BF16)


def _nsa(nq, ng, kvc_cmp, ksl, vsl, kwn, vwn, overlap, expand, batch, seq_len):
    T = nq.shape[0]
    QB = NSA_QBLOCK
    nqb = seq_len // QB
    nc = seq_len // CMP_STRIDE
    topk = min(SLC_TOPK, seq_len // SLC_BLOCK)
    tok = lambda n: pl.BlockSpec((QB, n), lambda b, i: (b * nqb + i, 0))
    seq = lambda n: pl.BlockSpec((1, seq_len, n), lambda b, i: (b, 0, 0))
    return pl.pallas_call(
        functools.partial(_nsa_kernel, seq_len=seq_len, topk=topk),
        grid=(batch, nqb),
        in_specs=[tok(D_NSA), tok(LANES),
                  pl.BlockSpec((1, 1, nc, LANES), lambda b, i: (0, b, 0, 0)),
                  pl.BlockSpec((1, 1, nc, LANES), lambda b, i: (1, b, 0, 0)),
                  seq(2 * LANES), seq(2 * LANES), seq(LANES), seq(2 * LANES),
                  pl.BlockSpec(overlap.shape, lambda b, i: (0, 0)),
                  pl.BlockSpec(expand.shape, lambda b, i: (0, 0))],
        out_specs=tok(D_NSA),
        out_shape=jax.ShapeDtypeStruct((T, D_NSA), BF16),
        scratch_shapes=[pltpu.VMEM((NSA_KV_HEADS, NSA_GROUP * QB, 2 * LANES), BF16),
                        pltpu.VMEM((2, NSA_KV_HEADS, NSA_GROUP * QB, SLC_KTILE), F32),
                        pltpu.VMEM((NSA_KV_HEADS, NSA_GROUP * QB, 1), F32),
                        pltpu.VMEM((NSA_KV_HEADS, NSA_GROUP * QB, LANES), F32)],
        compiler_params=pltpu.CompilerParams(dimension_semantics=("arbitrary", "arbitrary"),
                                             vmem_limit_bytes=V7X_VMEM_LIMIT),
        name="nsa",
    )(nq, ng, kvc_cmp, kvc_cmp, ksl.reshape(batch, seq_len, -1), vsl.reshape(batch, seq_len, -1),
      kwn.reshape(batch, seq_len, -1), vwn.reshape(batch, seq_len, -1), overlap, expand)


def _outproj_kernel(yr_ref, yn_ref, x_ref, w_ref, g_ref, o_ref):
    mix = (jnp.dot(yr_ref[...], w_ref[0:D_RET, :], preferred_element_type=F32)
           + jnp.dot(yn_ref[...], w_ref[D_RET:D_RET + D_NSA, :], preferred_element_type=F32))
    n = mix * lax.rsqrt(jnp.mean(mix * mix, axis=-1, keepdims=True) + NORM_EPS) * g_ref[...]
    o_ref[...] = x_ref[...] + n


def _outproj(y_ret, y_nsa, x2, w, gain):
    T, D = x2.shape
    tm = TOK_TILE
    tok = lambda n: pl.BlockSpec((tm, n), lambda i: (i, 0))
    return pl.pallas_call(
        _outproj_kernel,
        grid=(T // tm,),
        in_specs=[tok(D_RET), tok(D_NSA), tok(D),
                  pl.BlockSpec(w.shape, lambda i: (0, 0)),
                  pl.BlockSpec((1, D), lambda i: (0, 0))],
        out_specs=tok(D),
        out_shape=jax.ShapeDtypeStruct((T, D), F32),
        compiler_params=pltpu.CompilerParams(dimension_semantics=("arbitrary",),
                                             vmem_limit_bytes=V7X_VMEM_LIMIT),
        name="outproj",
    )(y_ret, y_nsa, x2, w, gain)


def _ffn_kernel(x_ref, xp_ref, gpre_ref, wg_ref, wv_ref, cg_ref, cv_ref, wd_ref, gpost_ref, o_ref,
                h_ref, ug_ref, uv_ref, act_ref, *, tiles_per_seq):
    tm = x_ref.shape[0]
    tf = wg_ref.shape[1]
    H = FFN_HALO
    rc = tm // FFN_ROW_CHUNKS
    rg = tm // FFN_GATE_CHUNKS
    gate_per_up = FFN_GATE_CHUNKS // FFN_ROW_CHUNKS
    f = pl.program_id(1)
    nf = pl.num_programs(1) - 1

    def normed(x):
        return x * lax.rsqrt(jnp.mean(x * x, axis=-1, keepdims=True) + NORM_EPS) * gpre_ref[...]

    def up_rows(slot, rows):
        h = h_ref[rows, :]
        ug_ref[slot, rows, :] = jnp.dot(h, wg_ref[...], preferred_element_type=F32)
        uv_ref[slot, rows, :] = jnp.dot(h, wv_ref[...], preferred_element_type=F32)

    def up_project(slot, c):
        up_rows(slot, slice(0 if c == 0 else H + c * rc, H + (c + 1) * rc))

    def causal_conv(u_ref, slot, c_ref, r0, scale):
        out = (scale * c_ref[CONV_WIDTH - 1:CONV_WIDTH, :]) * u_ref[slot, r0:r0 + rg, :]
        for k in range(CONV_WIDTH - 1):
            d = CONV_WIDTH - 1 - k
            out = out + (scale * c_ref[k:k + 1, :]) * u_ref[slot, r0 - d:r0 - d + rg, :]
        return out

    def conv_gate(slot, c):
        r0 = H + c * rg
        g = causal_conv(ug_ref, slot, cg_ref, r0, 1.0)
        v_half = causal_conv(uv_ref, slot, cv_ref, r0, 0.5)
        inner = g * (GELU_C + (GELU_C * 0.044715) * (g * g))
        act = (g + g * jnp.tanh(inner)) * v_half
        act_ref[c * rg:(c + 1) * rg, pl.ds(pl.multiple_of((f - 1) * tf, tf), tf)] = act.astype(BF16)

    @pl.when(f == 0)
    def _():
        keep = jnp.where(pl.program_id(0) % tiles_per_seq == 0, 0.0, 1.0)
        h_ref[0:H, :] = (normed(xp_ref[...]) * keep).astype(BF16)
        h_ref[H:H + tm, :] = normed(x_ref[...]).astype(BF16)
        for c in range(FFN_ROW_CHUNKS):
            up_project(0, c)

    for parity in range(2):
        @pl.when((f > 0) & (f < nf) & (f % 2 == parity))
        def _():
            for c in range(FFN_ROW_CHUNKS):
                up_project(parity, c)
                for j in range(gate_per_up):
                    conv_gate(1 - parity, c * gate_per_up + j)

    @pl.when(f == nf)
    def _():
        for c in range(FFN_GATE_CHUNKS):
            conv_gate(FFN_LAST_SLOT, c)
        y = jnp.dot(act_ref[...], wd_ref[...], preferred_element_type=F32)
        n = y * lax.rsqrt(jnp.mean(y * y, axis=-1, keepdims=True) + NORM_EPS) * gpost_ref[...]
        o_ref[...] = x_ref[...] + n


def _ffn(x2, g_pre, w_up, conv_w, w_down, g_post, seq_len):
    T, D = x2.shape
    d_ff = w_down.shape[0]
    tm, tf, H = FFN_TILE, FFN_FTILE, FFN_HALO
    nf = d_ff // tf
    assert (nf - 1) % 2 == FFN_LAST_SLOT
    up_tile = lambda f: jnp.minimum(f, nf - 1)
    cg_tile = lambda f: jnp.maximum(f - 1, 0)
    return pl.pallas_call(
        functools.partial(_ffn_kernel, tiles_per_seq=seq_len // tm),
        grid=(T // tm, nf + 1),
        in_specs=[pl.BlockSpec((tm, D), lambda i, f: (i, 0)),
                  pl.BlockSpec((H, D), lambda i, f: (jnp.maximum(i * (tm // H) - 1, 0), 0)),
                  pl.BlockSpec((1, D), lambda i, f: (0, 0)),
                  pl.BlockSpec((D, tf), lambda i, f: (0, up_tile(f))),
                  pl.BlockSpec((D, tf), lambda i, f: (0, nf + up_tile(f))),
                  pl.BlockSpec((CONV_WIDTH, tf), lambda i, f: (0, cg_tile(f))),
                  pl.BlockSpec((CONV_WIDTH, tf), lambda i, f: (0, nf + cg_tile(f))),
                  pl.BlockSpec((d_ff, D), lambda i, f: (0, 0)),
                  pl.BlockSpec((1, D), lambda i, f: (0, 0))],
        out_specs=pl.BlockSpec((tm, D), lambda i, f: (i, 0)),
        out_shape=jax.ShapeDtypeStruct((T, D), F32),
        scratch_shapes=[pltpu.VMEM((tm + H, D), BF16), pltpu.VMEM((2, tm + H, tf), F32),
                        pltpu.VMEM((2, tm + H, tf), F32), pltpu.VMEM((tm, d_ff), BF16)],
        compiler_params=pltpu.CompilerParams(dimension_semantics=("arbitrary", "arbitrary"),
                                             vmem_limit_bytes=V7X_VMEM_LIMIT),
        name="ffn",
    )(x2, x2, g_pre, w_up, w_up, conv_w, conv_w, w_down, g_post)


_NSA_HEAD_ORDER = [g * NSA_GROUP + i for i in range(NSA_GROUP) for g in range(NSA_KV_HEADS)]


def _rope_tables(pos):
    inv = 1.0 / (ROPE_THETA ** (jnp.arange(0, NSA_DK, 2, dtype=F32) / NSA_DK))
    ang = pos.astype(F32)[:, None] * inv[None, :]
    c, s = jnp.cos(ang), jnp.sin(ang)
    return jnp.concatenate([c, c, c, c], axis=1), jnp.concatenate([-s, s, -s, s], axis=1)


def _prep_w_in(w):
    d = w.shape[0]
    splits = np.cumsum([RET_QK, RET_QK, D_RET, D_RET, D_NSA] + [NSA_KV] * 6)
    rq, rk, rv, rg, nq, kcm, vcm, ksl, vsl, kwn, vwn, ng = jnp.split(w, [int(s) for s in splits], axis=1)
    nq = nq.reshape(d, NSA_HEADS, NSA_DK)[:, np.array(_NSA_HEAD_ORDER)].reshape(d, D_NSA)
    ng = jnp.pad(ng, ((0, 0), (0, LANES - NSA_GATES)))
    return jnp.concatenate([rq, rk, nq, ksl, kwn, rv, vsl, vwn, rg, kcm, vcm, ng], axis=1).astype(BF16)


def _prep_w_out(w):
    d = w.shape[1]
    w_nsa = w[D_RET:].reshape(NSA_HEADS, NSA_DK, d)[np.array(_NSA_HEAD_ORDER)].reshape(D_NSA, d)
    return jnp.concatenate([w[:D_RET], w_nsa], axis=0).astype(BF16)


def _prep_compress(pos, w1, w2):
    G = NSA_KV_HEADS
    eye = jnp.eye(G, dtype=F32)
    half = CMP_STRIDE

    def first_layer(w_half):
        w4 = w_half.reshape(half, NSA_DK, CMP_HID)
        return jnp.einsum('ldc,gh->lgdhc', w4, eye).reshape(half * G * NSA_DK, G * CMP_HID).astype(BF16)

    def pos_row(p_half):
        return jnp.broadcast_to(p_half[:, None, :], (half, G, NSA_DK)).reshape(1, -1)

    w2x = jnp.einsum('cd,gh->gchd', w2, eye).reshape(G * CMP_HID, G * NSA_DK).astype(BF16)
    return (pos_row(pos[:half]), pos_row(pos[half:]),
            first_layer(w1[:half * NSA_DK]), first_layer(w1[half * NSA_DK:]), w2x)


def _overlap_matrix(seq_len):
    nc = seq_len // CMP_STRIDE
    cmp_start = np.arange(nc) * CMP_STRIDE
    slc_start = np.arange(LANES) * SLC_BLOCK
    ov = ((cmp_start[:, None] < slc_start[None, :] + SLC_BLOCK)
          & (cmp_start[:, None] + CMP_BLOCK > slc_start[None, :]))
    n_cmp = (seq_len - CMP_BLOCK) // CMP_STRIDE + 1
    ov &= (np.arange(nc) < n_cmp)[:, None]
    return jnp.asarray(ov.astype(np.float32), dtype=BF16)


def _gate_expand_matrix():
    ex = np.zeros((LANES, 3 * D_NSA), np.float32)
    for branch in range(3):
        for p in range(D_NSA):
            head = _NSA_HEAD_ORDER[p // NSA_DK]
            ex[branch * NSA_HEADS + head, branch * D_NSA + p] = 1.0
    return jnp.asarray(ex, dtype=BF16)


def kernel(x, norm_mix_pre, w_in, ret_gn_w, cmp_k_pos, cmp_k_w1, cmp_k_w2, cmp_v_pos, cmp_v_w1, cmp_v_w2,
           w_out, norm_mix_post, norm_ffn_pre, ffn_w_up, ffn_conv, ffn_w_down, norm_ffn_post):
    B, S, D = x.shape
    depth = w_in.shape[0]
    assert S % SLC_KTILE == 0 and S % FFN_TILE == 0 and S // SLC_BLOCK <= LANES and S >= WINDOW + NSA_QBLOCK
    assert ffn_w_down.shape[1] % FFN_FTILE == 0

    cos, sin = _rope_tables(jnp.arange(S, dtype=jnp.int32))
    nc = S // CMP_STRIDE
    ccos, csin = _rope_tables(jnp.arange(nc, dtype=jnp.int32) * CMP_STRIDE + (CMP_BLOCK - 1))
    cmp_cos = jnp.stack([ccos, jnp.ones_like(ccos)])
    cmp_sin = jnp.stack([csin, jnp.zeros_like(csin)])
    overlap = _overlap_matrix(S)
    expand = _gate_expand_matrix()

    x2 = x.reshape(B * S, D)
    for l in range(depth):
        outs = _inproj(x2, norm_mix_pre[l][None], _prep_w_in(w_in[l]), cos, sin, S)
        rq, rk, nq, ksl, kwn, rv, vsl, vwn, rg, kvc, ng = outs
        y_ret = _retention(rq, rk, rv, rg, ret_gn_w[l][None], B, S)
        kp = _prep_compress(cmp_k_pos[l], cmp_k_w1[l], cmp_k_w2[l])
        vp = _prep_compress(cmp_v_pos[l], cmp_v_w1[l], cmp_v_w2[l])
        cmp_args = [jnp.stack([a, b]) for a, b in zip(kp, vp)]
        kvc_cmp = _compress(kvc, *cmp_args, cmp_cos, cmp_sin, B, S)
        y_nsa = _nsa(nq, ng, kvc_cmp, ksl, vsl, kwn, vwn, overlap, expand, B, S)
        x2 = _outproj(y_ret, y_nsa, x2, _prep_w_out(w_out[l]), norm_mix_post[l][None])
        x2 = _ffn(x2, norm_ffn_pre[l][None], ffn_w_up[l].astype(BF16), ffn_conv[l],
                  ffn_w_down[l].astype(BF16), norm_ffn_post[l][None], S)
    return x2.reshape(B, S, D)
```

```python
import functools
import math

import jax
import jax.numpy as jnp
import numpy as np
from jax import lax
from jax.experimental import pallas as pl
from jax.experimental.pallas import tpu as pltpu

F32 = jnp.float32
BF16 = jnp.bfloat16

LANES = 128
SUBLANES = 8
V7X_VMEM_LIMIT = 56 * 1024 * 1024

ROPE_THETA = 10000.0
NORM_EPS = 1e-6
GN_EPS = 1e-5
NEG = -1e30
SEL_BIG = 1e9

RET_HEADS = 4
RET_DK = 64
RET_DV = 128
RET_CHUNK = 128
NSA_HEADS = 8
NSA_KV_HEADS = 2
NSA_DK = 64
NSA_GROUP = NSA_HEADS // NSA_KV_HEADS
CMP_BLOCK = 32
CMP_STRIDE = 16
CMP_HID = 256
SLC_BLOCK = 64
SLC_TOPK = 16
SLC_LOCAL = 2
WINDOW = 512
NSA_QBLOCK = 128
NSA_GATES = 3 * NSA_HEADS
CONV_WIDTH = 3

NSA_Q_SCALE = NSA_DK ** -0.5 * math.log2(math.e)

D_RET = RET_HEADS * RET_DV
D_NSA = NSA_HEADS * NSA_DK
RET_QK = RET_HEADS * RET_DK
NSA_KV = NSA_KV_HEADS * NSA_DK

COL_RQ = 0
COL_RK = COL_RQ + RET_QK
COL_NQ = COL_RK + RET_QK
COL_KSL = COL_NQ + D_NSA
COL_KWN = COL_KSL + NSA_KV
ROPE_COLS = COL_KWN + NSA_KV
COL_RV = ROPE_COLS
COL_VSL = COL_RV + D_RET
COL_VWN = COL_VSL + NSA_KV
COL_RG = COL_VWN + NSA_KV
COL_KCM = COL_RG + D_RET
COL_VCM = COL_KCM + NSA_KV
COL_NG = COL_VCM + NSA_KV
IN_COLS_PAD = COL_NG + LANES

TOK_TILE = 512
RET_TILE = 512
FFN_TILE = 1024
FFN_FTILE = 256
FFN_HALO = 16
SLC_KTILE = 512


GELU_C = math.sqrt(2.0 / math.pi)


def _gelu_tanh(x):
    return 0.5 * x * (1.0 + jnp.tanh(GELU_C * (x + 0.044715 * (x * x * x))))


def _rope(p, cos, sin_signed, first_half):
    half = NSA_DK // 2
    partner = jnp.where(first_half, pltpu.roll(p, LANES - half, 1), pltpu.roll(p, half, 1))
    return p * cos + partner * sin_signed


def _split_bf16(x):
    hi = x.astype(BF16)
    lo = (x - hi.astype(F32)).astype(BF16)
    return hi, lo


def _inproj_kernel(x_ref, g_ref, w_ref, cos_ref, sin_ref,
                   rq_ref, rk_ref, nq_ref, ksl_ref, kwn_ref, rv_ref, vsl_ref, vwn_ref,
                   rg_ref, kvc_ref, ng_ref, *, seq_len):
    tm = x_ref.shape[0]
    x = x_ref[...]
    h = (x * lax.rsqrt(jnp.mean(x * x, axis=-1, keepdims=True) + NORM_EPS) * g_ref[...]).astype(BF16)
    cos = cos_ref[...]
    sin = sin_ref[...]
    lane = lax.broadcasted_iota(jnp.int32, (tm, LANES), 1)
    first_half = (lane % NSA_DK) < NSA_DK // 2

    def proj(c0, n):
        return jnp.dot(h, w_ref[:, c0:c0 + n], preferred_element_type=F32)

    def rope_slab(p, i):
        return _rope(p[:, i * LANES:(i + 1) * LANES], cos, sin, first_half)

    p = proj(COL_RQ, RET_QK)
    for i in range(RET_QK // LANES):
        rq_ref[:, i * LANES:(i + 1) * LANES] = rope_slab(p, i).astype(BF16)
    p = proj(COL_RK, RET_QK)
    for i in range(RET_QK // LANES):
        rk_ref[:, i * LANES:(i + 1) * LANES] = (rope_slab(p, i) * (RET_DK ** -0.5)).astype(BF16)
    p = proj(COL_NQ, D_NSA)
    for i in range(D_NSA // LANES):
        nq_ref[:, i * LANES:(i + 1) * LANES] = (rope_slab(p, i) * NSA_Q_SCALE).astype(BF16)
    p = proj(COL_KSL, 2 * NSA_KV)
    ksl_ref[:, 0:LANES] = rope_slab(p, 0).astype(BF16)
    kwn_ref[...] = rope_slab(p, 1).astype(BF16)
    row = lax.broadcasted_iota(jnp.int32, (tm, LANES), 0)
    pos = (pl.program_id(0) * tm + row) % seq_len
    ksl_ref[:, LANES:2 * LANES] = jnp.where(lane == pos // SLC_BLOCK, 1.0, 0.0).astype(BF16)

    rv_ref[...] = proj(COL_RV, D_RET).astype(BF16)
    p = proj(COL_VSL, 2 * NSA_KV)
    low_half = lane < NSA_DK
    for v_ref, v in ((vsl_ref, p[:, 0:LANES]), (vwn_ref, p[:, LANES:2 * LANES])):
        v_ref[0, 0:LANES, :] = jnp.where(low_half, v, 1.0).T.astype(BF16)
        v_ref[0, LANES:2 * LANES, :] = jnp.where(low_half, 1.0, v).T.astype(BF16)
    rg_ref[...] = proj(COL_RG, D_RET)
    p = proj(COL_KCM, 2 * NSA_KV)
    kvc_ref[0] = p[:, 0:LANES]
    kvc_ref[1] = p[:, LANES:2 * LANES]
    ng_ref[...] = proj(COL_NG, LANES)


def _inproj(x2, gain, w, cos, sin, seq_len):
    T, D = x2.shape
    tm = TOK_TILE
    nt = seq_len // tm
    tok = lambda n: pl.BlockSpec((tm, n), lambda i: (i, 0))
    tok_t = pl.BlockSpec((1, 2 * LANES, tm), lambda i: (i // nt, 0, i % nt))
    out_shape = (
        jax.ShapeDtypeStruct((T, RET_QK), BF16),
        jax.ShapeDtypeStruct((T, RET_QK), BF16),
        jax.ShapeDtypeStruct((T, D_NSA), BF16),
        jax.ShapeDtypeStruct((T, 2 * LANES), BF16),
        jax.ShapeDtypeStruct((T, LANES), BF16),
        jax.ShapeDtypeStruct((T, D_RET), BF16),
        jax.ShapeDtypeStruct((T // seq_len, 2 * LANES, seq_len), BF16),
        jax.ShapeDtypeStruct((T // seq_len, 2 * LANES, seq_len), BF16),
        jax.ShapeDtypeStruct((T, D_RET), F32),
        jax.ShapeDtypeStruct((2, T, LANES), F32),
        jax.ShapeDtypeStruct((T, LANES), F32),
    )
    out_specs = (tok(RET_QK), tok(RET_QK), tok(D_NSA), tok(2 * LANES), tok(LANES), tok(D_RET),
                 tok_t, tok_t, tok(D_RET),
                 pl.BlockSpec((2, tm, LANES), lambda i: (0, i, 0)), tok(LANES))
    return pl.pallas_call(
        functools.partial(_inproj_kernel, seq_len=seq_len),
        grid=(T // tm,),
        in_specs=[tok(D),
                  pl.BlockSpec((1, D), lambda i: (0, 0)),
                  pl.BlockSpec((D, IN_COLS_PAD), lambda i: (0, 0)),
                  pl.BlockSpec((tm, LANES), lambda i: (i % nt, 0)),
                  pl.BlockSpec((tm, LANES), lambda i: (i % nt, 0))],
        out_specs=out_specs,
        out_shape=out_shape,
        compiler_params=pltpu.CompilerParams(dimension_semantics=("arbitrary",),
                                             vmem_limit_bytes=V7X_VMEM_LIMIT),
        name="inproj",
    )(x2, gain, w, cos, sin)


def _retention_kernel(q_ref, k_ref, v_ref, g_ref, gnw_ref, o_ref, state_ref):
    C = RET_CHUNK
    n_chunks = q_ref.shape[0] // C

    @pl.when(pl.program_id(1) == 0)
    def _():
        state_ref[...] = jnp.zeros_like(state_ref)

    ii = lax.broadcasted_iota(jnp.int32, (C, C), 0)
    jj = lax.broadcasted_iota(jnp.int32, (C, C), 1)
    diff = (ii - jj).astype(F32)
    i_col = lax.broadcasted_iota(jnp.int32, (C, 1), 0).astype(F32)
    low_half = lax.broadcasted_iota(jnp.int32, (C, LANES), 1) < RET_DK
    nt_dims = (((1,), (1,)), ((), ()))
    tn_dims = (((0,), (0,)), ((), ()))

    for h in range(RET_HEADS):
        log_gamma = math.log(1.0 - 2.0 ** (-5.0 - h))
        decay = jnp.where(diff >= 0, jnp.exp(log_gamma * jnp.maximum(diff, 0.0)), 0.0)
        xi = jnp.exp(log_gamma * (i_col + 1.0))
        zeta = jnp.exp(log_gamma * (C - 1.0 - i_col))
        chunk_decay = math.exp(log_gamma * C)
        head_lanes = low_half if h % 2 == 0 else jnp.logical_not(low_half)
        qk_cols = slice((h // 2) * LANES, (h // 2 + 1) * LANES)
        v_cols = slice(h * RET_DV, (h + 1) * RET_DV)
        gn_w = gnw_ref[:, v_cols]
        for c in range(n_chunks):
            rows = slice(c * C, (c + 1) * C)
            qm = jnp.where(head_lanes, q_ref[rows, qk_cols].astype(F32), 0.0).astype(BF16)
            ks = k_ref[rows, qk_cols]
            v = v_ref[rows, v_cols]
            scores = lax.dot_general(qm, ks, nt_dims, preferred_element_type=F32) * decay
            o = jnp.dot(scores.astype(BF16), v, preferred_element_type=F32)
            state = state_ref[h]
            o = o + jnp.dot(qm, state.astype(BF16), preferred_element_type=F32) * xi
            kz = (ks.astype(F32) * zeta).astype(BF16)
            kv = lax.dot_general(kz, v, tn_dims, preferred_element_type=F32)
            state_ref[h] = state * chunk_decay + kv
            mu = jnp.mean(o, axis=-1, keepdims=True)
            var = jnp.mean(jnp.square(o - mu), axis=-1, keepdims=True)
            on = (o - mu) * lax.rsqrt(var + GN_EPS) * gn_w
            gate = g_ref[rows, v_cols]
            o_ref[rows, v_cols] = (gate * (1.0 / (1.0 + jnp.exp(-gate))) * on).astype(BF16)


def _retention(rq, rk, rv, rg, gn_w, batch, seq_len):
    T = rq.shape[0]
    tc = RET_TILE
    nt = seq_len // tc
    tok = lambda n: pl.BlockSpec((tc, n), lambda b, i: (b * nt + i, 0))
    return pl.pallas_call(
        _retention_kernel,
        grid=(batch, nt),
        in_specs=[tok(RET_QK), tok(RET_QK), tok(D_RET), tok(D_RET),
                  pl.BlockSpec((1, D_RET), lambda b, i: (0, 0))],
        out_specs=tok(D_RET),
        out_shape=jax.ShapeDtypeStruct((T, D_RET), BF16),
        scratch_shapes=[pltpu.VMEM((RET_HEADS, LANES, RET_DV), F32)],
        compiler_params=pltpu.CompilerParams(dimension_semantics=("arbitrary", "arbitrary"),
                                             vmem_limit_bytes=V7X_VMEM_LIMIT),
        name="retention",
    )(rq, rk, rv, rg, gn_w)


def _compress_kernel(x_ref, pa_ref, pb_ref, wa_ref, wb_ref, w2_ref, cos_ref, sin_ref, o_ref, ot_ref):
    ng = x_ref.shape[2]
    x = x_ref[0, 0]
    xa = (x + pa_ref[0]).astype(BF16)
    xb = (x + pb_ref[0]).astype(BF16)
    a = jnp.dot(xa, wa_ref[0], preferred_element_type=F32)
    b = jnp.dot(xb, wb_ref[0], preferred_element_type=F32)
    hid = a + pltpu.roll(b, ng - 1, 0)
    out = jnp.dot(_gelu_tanh(hid).astype(BF16), w2_ref[0], preferred_element_type=F32)
    lane = lax.broadcasted_iota(jnp.int32, out.shape, 1)
    out = _rope(out, cos_ref[0], sin_ref[0], (lane % NSA_DK) < NSA_DK // 2)
    o_ref[0, 0] = out.astype(BF16)
    ot_ref[0, 0] = out.T.astype(BF16)


def _compress(kvc, pos_a, pos_b, wa, wb, w2, cos, sin, batch, seq_len):
    ng = seq_len // CMP_STRIDE
    gw = CMP_STRIDE * LANES
    x = kvc.reshape(2, batch, ng, gw)
    hid = NSA_KV_HEADS * CMP_HID
    per_kv = lambda *shape: pl.BlockSpec((1,) + shape, lambda s, b: (s,) + (0,) * len(shape))
    return pl.pallas_call(
        _compress_kernel,
        grid=(2, batch),
        in_specs=[pl.BlockSpec((1, 1, ng, gw), lambda s, b: (s, b, 0, 0)),
                  per_kv(1, gw), per_kv(1, gw), per_kv(gw, hid), per_kv(gw, hid), per_kv(hid, LANES),
                  per_kv(ng, LANES), per_kv(ng, LANES)],
        out_specs=(pl.BlockSpec((1, 1, ng, LANES), lambda s, b: (s, b, 0, 0)),
                   pl.BlockSpec((1, 1, LANES, ng), lambda s, b: (s, b, 0, 0))),
        out_shape=(jax.ShapeDtypeStruct((2, batch, ng, LANES), BF16),
                   jax.ShapeDtypeStruct((2, batch, LANES, ng), BF16)),
        compiler_params=pltpu.CompilerParams(dimension_semantics=("arbitrary", "arbitrary"),
                                             vmem_limit_bytes=V7X_VMEM_LIMIT),
        name="compress",
    )(x, pos_a, pos_b, wa, wb, w2, cos, sin)


def _nsa_kernel(nq_ref, ng_ref, kc_ref, vc_ref, ksl_ref, vsl_ref, kwn_ref, vwn_ref, ov_ref, ex_ref,
                o_ref, qaug_ref, s_ref, m_ref, acc_ref, *, seq_len, topk):
    QB = NSA_QBLOCK
    R = NSA_GROUP
    G = NSA_KV_HEADS
    M = R * QB
    NC = kc_ref.shape[2]
    KT = SLC_KTILE
    WK = WINDOW + QB
    t0 = pl.program_id(1) * QB
    nt_dims = (((1,), (1,)), ((), ()))

    low_half = lax.broadcasted_iota(jnp.int32, (QB, LANES), 1) < NSA_DK
    q = nq_ref[...].astype(F32)

    def per_head(x):
        return jnp.concatenate([x] * R, axis=1)

    def normalized(acc, g):
        num, den = (acc[0:NSA_DK], acc[NSA_DK:NSA_DK + 1]) if g == 0 else (acc[NSA_DK:], acc[0:1])
        return num * (1.0 / den)

    for g in range(G):
        head_lanes = low_half if g == 0 else jnp.logical_not(low_half)
        for i in range(R):
            qaug_ref[g, i * QB:(i + 1) * QB, 0:LANES] = jnp.where(
                head_lanes, q[:, i * LANES:(i + 1) * LANES], 0.0).astype(BF16)

    ks = pl.multiple_of(jnp.clip(t0 - WINDOW, 0, seq_len - WK), QB)
    k_pos = ks + lax.broadcasted_iota(jnp.int32, (WK, QB), 0)
    t_pos = t0 + lax.broadcasted_iota(jnp.int32, (WK, QB), 1)
    win_mask = per_head((k_pos <= t_pos) & (k_pos > t_pos - WINDOW))
    win_scores = [lax.dot_general(kwn_ref[0, pl.ds(ks, WK), :], qaug_ref[g, :, 0:LANES], nt_dims,
                                  preferred_element_type=F32) for g in range(G)]
    cmp_scores = [lax.dot_general(kc_ref[0, 0], qaug_ref[g, :, 0:LANES], nt_dims,
                                  preferred_element_type=F32) for g in range(G)]
    o_win = []
    for g in range(G):
        sw = jnp.where(win_mask, win_scores[g], NEG)
        e_win = jnp.exp2(sw - jnp.max(sw, axis=0, keepdims=True)).astype(BF16)
        pv = jnp.dot(vwn_ref[0, g * LANES:(g + 1) * LANES, pl.ds(ks, WK)], e_win,
                     preferred_element_type=F32)
        o_win.append(normalized(pv, g))

    n_idx = lax.broadcasted_iota(jnp.int32, (NC, QB), 0)
    t_col = t0 + lax.broadcasted_iota(jnp.int32, (NC, QB), 1)
    cmp_mask = per_head(n_idx * CMP_STRIDE + (CMP_BLOCK - 1) <= t_col)
    jj = lax.broadcasted_iota(jnp.int32, (LANES, QB), 0)
    q_blk = (t0 + lax.broadcasted_iota(jnp.int32, (LANES, QB), 1)) // SLC_BLOCK
    valid = jj <= q_blk
    forced = (jj == 0) | (valid & (jj > q_blk - SLC_LOCAL))
    jf = jj.astype(F32)
    o_cmp, scores = [], []
    for g in range(G):
        sc = jnp.where(cmp_mask, cmp_scores[g], NEG)
        e = jnp.where(cmp_mask, jnp.exp2(sc - jnp.max(sc, axis=0, keepdims=True)), 0.0)
        p_cmp = e * (1.0 / jnp.maximum(jnp.sum(e, axis=0, keepdims=True), 1e-30))
        acc = jnp.dot(vc_ref[0, 0], p_cmp.astype(BF16), preferred_element_type=F32)
        o_cmp.append(acc[g * NSA_DK:(g + 1) * NSA_DK])
        p_sum = p_cmp[:, 0:QB]
        for i in range(1, R):
            p_sum = p_sum + p_cmp[:, i * QB:(i + 1) * QB]
        p_hi, p_lo = _split_bf16(p_sum)
        imp = (jnp.dot(ov_ref[...], p_hi, preferred_element_type=F32)
               + jnp.dot(ov_ref[...], p_lo, preferred_element_type=F32))
        scores.append(jnp.where(forced, -jnp.inf, jnp.where(valid, imp, -SEL_BIG)))

    def pick_round(_, carry):
        out = []
        for score in carry:
            best = jnp.max(score, axis=0, keepdims=True)
            first = jnp.min(jnp.where(score == best, jf, float(LANES)), axis=0, keepdims=True)
            out.append(jnp.where(jf == first, -jnp.inf, score))
        return tuple(out)

    taken = lax.fori_loop(0, topk - (1 + SLC_LOCAL), pick_round, tuple(scores))
    for g in range(G):
        bias = jnp.where((taken[g] == -jnp.inf) & valid, 0.0, NEG).T.astype(BF16)
        for i in range(R):
            qaug_ref[g, i * QB:(i + 1) * QB, LANES:2 * LANES] = bias

    def slc_scores(k0, slot):
        for g in range(G):
            s_ref[slot, g] = lax.dot_general(ksl_ref[0, pl.ds(k0, KT), :], qaug_ref[g], nt_dims,
                                             preferred_element_type=F32)

    def slc_accumulate(k0, slot, causal):
        for g in range(G):
            s = s_ref[slot, g]
            if causal:
                k_pos = k0 + lax.broadcasted_iota(jnp.int32, (KT, QB), 0)
                t_pos = t0 + lax.broadcasted_iota(jnp.int32, (KT, QB), 1)
                s = jnp.where(per_head(k_pos <= t_pos), s, NEG)
            m_run = m_ref[g]
            m_new = jnp.maximum(m_run, jnp.max(s, axis=0, keepdims=True))
            p = jnp.exp2(s - m_new).astype(BF16)
            pv = jnp.dot(vsl_ref[0, g * LANES:(g + 1) * LANES, pl.ds(k0, KT)], p,
                         preferred_element_type=F32)
            acc_ref[g] = jnp.exp2(m_run - m_new) * acc_ref[g] + pv
            m_ref[g] = m_new

    m_ref[...] = jnp.full(m_ref.shape, NEG, F32)
    acc_ref[...] = jnp.zeros_like(acc_ref)
    n_full = t0 // KT
    slc_scores(0, 0)

    def slc_pair(j, _):
        k0 = pl.multiple_of(j * (2 * KT), 2 * KT)
        slc_scores(k0 + KT, 1)
        slc_accumulate(k0, 0, False)
        slc_scores(k0 + 2 * KT, 0)
        slc_accumulate(k0 + KT, 1, False)
        return 0

    lax.fori_loop(0, n_full // 2, slc_pair, 0)
    k_diag = pl.multiple_of(n_full * KT, KT)

    @pl.when(n_full % 2 == 0)
    def _():
        slc_accumulate(k_diag, 0, True)

    @pl.when(n_full % 2 == 1)
    def _():
        slc_scores(k_diag, 1)
        slc_accumulate(k_diag - KT, 0, False)
        slc_accumulate(k_diag, 1, True)

    o_slc = [normalized(acc_ref[g], g) for g in range(G)]

    g_hi, g_lo = _split_bf16((1.0 / (1.0 + jnp.exp(-ng_ref[...]))).T)
    gates = (jnp.dot(ex_ref[...], g_hi, preferred_element_type=F32)
             + jnp.dot(ex_ref[...], g_lo, preferred_element_type=F32))
    branches = [jnp.concatenate(o, axis=0) for o in (o_cmp, o_slc, o_win)]
    for i in range(R):
        y = jnp.zeros((LANES, QB), F32)
        for b, o in enumerate(branches):
            y = y + gates[b * D_NSA + i * LANES:b * D_NSA + (i + 1) * LANES] * o[:, i * QB:(i + 1) * QB]
        o_ref[:, i * LANES:(i + 1) * LANES] = y.T.astype(BF16)


def _nsa(nq, ng, kvc_cmp, kvc_cmp_t, ksl, vsl, kwn, vwn, overlap, expand, batch, seq_len):
    T = nq.shape[0]
    QB = NSA_QBLOCK
    nqb = seq_len // QB
    nc = seq_len // CMP_STRIDE
    topk = min(SLC_TOPK, seq_len // SLC_BLOCK)
    tok = lambda n: pl.BlockSpec((QB, n), lambda b, i: (b * nqb + i, 0))
    seq = lambda n: pl.BlockSpec((1, seq_len, n), lambda b, i: (b, 0, 0))
    seq_t = pl.BlockSpec((1, 2 * LANES, seq_len), lambda b, i: (b, 0, 0))
    return pl.pallas_call(
        functools.partial(_nsa_kernel, seq_len=seq_len, topk=topk),
        grid=(batch, nqb),
        in_specs=[tok(D_NSA), tok(LANES),
                  pl.BlockSpec((1, 1, nc, LANES), lambda b, i: (0, b, 0, 0)),
                  pl.BlockSpec((1, 1, LANES, nc), lambda b, i: (1, b, 0, 0)),
                  seq(2 * LANES), seq_t, seq(LANES), seq_t,
                  pl.BlockSpec(overlap.shape, lambda b, i: (0, 0)),
                  pl.BlockSpec(expand.shape, lambda b, i: (0, 0))],
        out_specs=tok(D_NSA),
        out_shape=jax.ShapeDtypeStruct((T, D_NSA), BF16),
        scratch_shapes=[pltpu.VMEM((NSA_KV_HEADS, NSA_GROUP * QB, 2 * LANES), BF16),
                        pltpu.VMEM((2, NSA_KV_HEADS, SLC_KTILE, NSA_GROUP * QB), F32),
                        pltpu.VMEM((NSA_KV_HEADS, 1, NSA_GROUP * QB), F32),
                        pltpu.VMEM((NSA_KV_HEADS, LANES, NSA_GROUP * QB), F32)],
        compiler_params=pltpu.CompilerParams(dimension_semantics=("arbitrary", "arbitrary"),
                                             vmem_limit_bytes=V7X_VMEM_LIMIT),
        name="nsa",
    )(nq, ng, kvc_cmp, kvc_cmp_t, ksl.reshape(batch, seq_len, -1), vsl,
      kwn.reshape(batch, seq_len, -1), vwn, overlap, expand)


def _outproj_kernel(yr_ref, yn_ref, x_ref, w_ref, g_ref, o_ref):
    mix = (jnp.dot(yr_ref[...], w_ref[0:D_RET, :], preferred_element_type=F32)
           + jnp.dot(yn_ref[...], w_ref[D_RET:D_RET + D_NSA, :], preferred_element_type=F32))
    n = mix * lax.rsqrt(jnp.mean(mix * mix, axis=-1, keepdims=True) + NORM_EPS) * g_ref[...]
    o_ref[...] = x_ref[...] + n


def _outproj(y_ret, y_nsa, x2, w, gain):
    T, D = x2.shape
    tm = TOK_TILE
    tok = lambda n: pl.BlockSpec((tm, n), lambda i: (i, 0))
    return pl.pallas_call(
        _outproj_kernel,
        grid=(T // tm,),
        in_specs=[tok(D_RET), tok(D_NSA), tok(D),
                  pl.BlockSpec(w.shape, lambda i: (0, 0)),
                  pl.BlockSpec((1, D), lambda i: (0, 0))],
        out_specs=tok(D),
        out_shape=jax.ShapeDtypeStruct((T, D), F32),
        compiler_params=pltpu.CompilerParams(dimension_semantics=("arbitrary",),
                                             vmem_limit_bytes=V7X_VMEM_LIMIT),
        name="outproj",
    )(y_ret, y_nsa, x2, w, gain)


def _ffn_kernel(x_ref, xp_ref, gpre_ref, wg_ref, wv_ref, cg_ref, cv_ref, wd_ref, gpost_ref, o_ref,
                h_ref, ug_ref, uv_ref, acc_ref, *, tiles_per_seq):
    tm = x_ref.shape[0]
    H = FFN_HALO
    f = pl.program_id(1)

    def normed(x):
        return x * lax.rsqrt(jnp.mean(x * x, axis=-1, keepdims=True) + NORM_EPS) * gpre_ref[...]

    @pl.when(f == 0)
    def _():
        keep = jnp.where(pl.program_id(0) % tiles_per_seq == 0, 0.0, 1.0)
        h_ref[0:H, :] = (normed(xp_ref[...]) * keep).astype(BF16)
        h_ref[H:H + tm, :] = normed(x_ref[...]).astype(BF16)
        acc_ref[...] = jnp.zeros_like(acc_ref)

    h = h_ref[...]
    ug_ref[...] = jnp.dot(h, wg_ref[...], preferred_element_type=F32)
    uv_ref[...] = jnp.dot(h, wv_ref[...], preferred_element_type=F32)

    def causal_conv(u_ref, c_ref, scale):
        out = (scale * c_ref[CONV_WIDTH - 1:CONV_WIDTH, :]) * u_ref[H:H + tm, :]
        for k in range(CONV_WIDTH - 1):
            d = CONV_WIDTH - 1 - k
            out = out + (scale * c_ref[k:k + 1, :]) * u_ref[H - d:H - d + tm, :]
        return out

    g = causal_conv(ug_ref, cg_ref, 1.0)
    v_half = causal_conv(uv_ref, cv_ref, 0.5)
    inner = g * (GELU_C + (GELU_C * 0.044715) * (g * g))
    act = (g + g * jnp.tanh(inner)) * v_half
    acc_ref[...] += jnp.dot(act.astype(BF16), wd_ref[...], preferred_element_type=F32)

    @pl.when(f == pl.num_programs(1) - 1)
    def _():
        y = acc_ref[...]
        n = y * lax.rsqrt(jnp.mean(y * y, axis=-1, keepdims=True) + NORM_EPS) * gpost_ref[...]
        o_ref[...] = x_ref[...] + n


def _ffn(x2, g_pre, w_up, conv_w, w_down, g_post, seq_len):
    T, D = x2.shape
    d_ff = w_down.shape[0]
    tm, tf, H = FFN_TILE, FFN_FTILE, FFN_HALO
    nf = d_ff // tf
    return pl.pallas_call(
        functools.partial(_ffn_kernel, tiles_per_seq=seq_len // tm),
        grid=(T // tm, nf),
        in_specs=[pl.BlockSpec((tm, D), lambda i, f: (i, 0)),
                  pl.BlockSpec((H, D), lambda i, f: (jnp.maximum(i * (tm // H) - 1, 0), 0)),
                  pl.BlockSpec((1, D), lambda i, f: (0, 0)),
                  pl.BlockSpec((D, tf), lambda i, f: (0, f)),
                  pl.BlockSpec((D, tf), lambda i, f: (0, nf + f)),
                  pl.BlockSpec((CONV_WIDTH, tf), lambda i, f: (0, f)),
                  pl.BlockSpec((CONV_WIDTH, tf), lambda i, f: (0, nf + f)),
                  pl.BlockSpec((tf, D), lambda i, f: (f, 0)),
                  pl.BlockSpec((1, D), lambda i, f: (0, 0))],
        out_specs=pl.BlockSpec((tm, D), lambda i, f: (i, 0)),
        out_shape=jax.ShapeDtypeStruct((T, D), F32),
        scratch_shapes=[pltpu.VMEM((tm + H, D), BF16), pltpu.VMEM((tm + H, tf), F32),
                        pltpu.VMEM((tm + H, tf), F32), pltpu.VMEM((tm, D), F32)],
        compiler_params=pltpu.CompilerParams(dimension_semantics=("arbitrary", "arbitrary"),
                                             vmem_limit_bytes=V7X_VMEM_LIMIT),
        name="ffn",
    )(x2, x2, g_pre, w_up, w_up, conv_w, conv_w, w_down, g_post)


_NSA_HEAD_ORDER = [g * NSA_GROUP + i for i in range(NSA_GROUP) for g in range(NSA_KV_HEADS)]


def _rope_tables(pos):
    inv = 1.0 / (ROPE_THETA ** (jnp.arange(0, NSA_DK, 2, dtype=F32) / NSA_DK))
    ang = pos.astype(F32)[:, None] * inv[None, :]
    c, s = jnp.cos(ang), jnp.sin(ang)
    return jnp.concatenate([c, c, c, c], axis=1), jnp.concatenate([-s, s, -s, s], axis=1)


def _prep_w_in(w):
    d = w.shape[0]
    splits = np.cumsum([RET_QK, RET_QK, D_RET, D_RET, D_NSA] + [NSA_KV] * 6)
    rq, rk, rv, rg, nq, kcm, vcm, ksl, vsl, kwn, vwn, ng = jnp.split(w, [int(s) for s in splits], axis=1)
    nq = nq.reshape(d, NSA_HEADS, NSA_DK)[:, np.array(_NSA_HEAD_ORDER)].reshape(d, D_NSA)
    ng = jnp.pad(ng, ((0, 0), (0, LANES - NSA_GATES)))
    return jnp.concatenate([rq, rk, nq, ksl, kwn, rv, vsl, vwn, rg, kcm, vcm, ng], axis=1).astype(BF16)


def _prep_w_out(w):
    d = w.shape[1]
    w_nsa = w[D_RET:].reshape(NSA_HEADS, NSA_DK, d)[np.array(_NSA_HEAD_ORDER)].reshape(D_NSA, d)
    return jnp.concatenate([w[:D_RET], w_nsa], axis=0).astype(BF16)


def _prep_compress(pos, w1, w2):
    G = NSA_KV_HEADS
    eye = jnp.eye(G, dtype=F32)
    half = CMP_STRIDE

    def first_layer(w_half):
        w4 = w_half.reshape(half, NSA_DK, CMP_HID)
        return jnp.einsum('ldc,gh->lgdhc', w4, eye).reshape(half * G * NSA_DK, G * CMP_HID).astype(BF16)

    def pos_row(p_half):
        return jnp.broadcast_to(p_half[:, None, :], (half, G, NSA_DK)).reshape(1, -1)

    w2x = jnp.einsum('cd,gh->gchd', w2, eye).reshape(G * CMP_HID, G * NSA_DK).astype(BF16)
    return (pos_row(pos[:half]), pos_row(pos[half:]),
            first_layer(w1[:half * NSA_DK]), first_layer(w1[half * NSA_DK:]), w2x)


def _overlap_matrix(seq_len):
    nc = seq_len // CMP_STRIDE
    cmp_start = np.arange(nc) * CMP_STRIDE
    slc_start = np.arange(LANES) * SLC_BLOCK
    ov = ((cmp_start[:, None] < slc_start[None, :] + SLC_BLOCK)
          & (cmp_start[:, None] + CMP_BLOCK > slc_start[None, :]))
    n_cmp = (seq_len - CMP_BLOCK) // CMP_STRIDE + 1
    ov &= (np.arange(nc) < n_cmp)[:, None]
    return jnp.asarray(ov.T.astype(np.float32), dtype=BF16)


def _gate_expand_matrix():
    ex = np.zeros((LANES, 3 * D_NSA), np.float32)
    for branch in range(3):
        for p in range(D_NSA):
            head = _NSA_HEAD_ORDER[p // NSA_DK]
            ex[branch * NSA_HEADS + head, branch * D_NSA + p] = 1.0
    return jnp.asarray(ex.T, dtype=BF16)


def kernel(x, norm_mix_pre, w_in, ret_gn_w, cmp_k_pos, cmp_k_w1, cmp_k_w2, cmp_v_pos, cmp_v_w1, cmp_v_w2,
           w_out, norm_mix_post, norm_ffn_pre, ffn_w_up, ffn_conv, ffn_w_down, norm_ffn_post):
    B, S, D = x.shape
    depth = w_in.shape[0]
    assert S % SLC_KTILE == 0 and S % FFN_TILE == 0 and S // SLC_BLOCK <= LANES and S >= WINDOW + NSA_QBLOCK
    assert ffn_w_down.shape[1] % FFN_FTILE == 0

    cos, sin = _rope_tables(jnp.arange(S, dtype=jnp.int32))
    nc = S // CMP_STRIDE
    ccos, csin = _rope_tables(jnp.arange(nc, dtype=jnp.int32) * CMP_STRIDE + (CMP_BLOCK - 1))
    cmp_cos = jnp.stack([ccos, jnp.ones_like(ccos)])
    cmp_sin = jnp.stack([csin, jnp.zeros_like(csin)])
    overlap = _overlap_matrix(S)
    expand = _gate_expand_matrix()

    x2 = x.reshape(B * S, D)
    for l in range(depth):
        outs = _inproj(x2, norm_mix_pre[l][None], _prep_w_in(w_in[l]), cos, sin, S)
        rq, rk, nq, ksl, kwn, rv, vsl, vwn, rg, kvc, ng = outs
        y_ret = _retention(rq, rk, rv, rg, ret_gn_w[l][None], B, S)
        kp = _prep_compress(cmp_k_pos[l], cmp_k_w1[l], cmp_k_w2[l])
        vp = _prep_compress(cmp_v_pos[l], cmp_v_w1[l], cmp_v_w2[l])
        cmp_args = [jnp.stack([a, b]) for a, b in zip(kp, vp)]
        kvc_cmp, kvc_cmp_t = _compress(kvc, *cmp_args, cmp_cos, cmp_sin, B, S)
        y_nsa = _nsa(nq, ng, kvc_cmp, kvc_cmp_t, ksl, vsl, kwn, vwn, overlap, expand, B, S)
        x2 = _outproj(y_ret, y_nsa, x2, _prep_w_out(w_out[l]), norm_mix_post[l][None])
        x2 = _ffn(x2, norm_ffn_pre[l][None], ffn_w_up[l].astype(BF16), ffn_conv[l],
                  ffn_w_down[l].astype(BF16), norm_ffn_post[l][None], S)
    return x2.reshape(B, S, D)
```

```python
import functools
import math

import jax
import jax.numpy as jnp
import numpy as np
from jax import lax
from jax.experimental import pallas as pl
from jax.experimental.pallas import tpu as pltpu

F32 = jnp.float32
BF16 = jnp.bfloat16

LANES = 128
SUBLANES = 8
V7X_VMEM_LIMIT = 56 * 1024 * 1024

ROPE_THETA = 10000.0
NORM_EPS = 1e-6
GN_EPS = 1e-5
NEG = -1e30
SEL_BIG = 1e9

RET_HEADS = 4
RET_DK = 64
RET_DV = 128
RET_CHUNK = 128
NSA_HEADS = 8
NSA_KV_HEADS = 2
NSA_DK = 64
NSA_GROUP = NSA_HEADS // NSA_KV_HEADS
CMP_BLOCK = 32
CMP_STRIDE = 16
CMP_HID = 256
SLC_BLOCK = 64
SLC_TOPK = 16
SLC_LOCAL = 2
WINDOW = 512
NSA_QBLOCK = 128
NSA_GATES = 3 * NSA_HEADS
CONV_WIDTH = 3

NSA_Q_SCALE = NSA_DK ** -0.5 * math.log2(math.e)

D_RET = RET_HEADS * RET_DV
D_NSA = NSA_HEADS * NSA_DK
RET_QK = RET_HEADS * RET_DK
NSA_KV = NSA_KV_HEADS * NSA_DK

COL_RQ = 0
COL_RK = COL_RQ + RET_QK
COL_NQ = COL_RK + RET_QK
COL_KSL = COL_NQ + D_NSA
COL_KWN = COL_KSL + NSA_KV
ROPE_COLS = COL_KWN + NSA_KV
COL_RV = ROPE_COLS
COL_VSL = COL_RV + D_RET
COL_VWN = COL_VSL + NSA_KV
COL_RG = COL_VWN + NSA_KV
COL_KCM = COL_RG + D_RET
COL_VCM = COL_KCM + NSA_KV
COL_NG = COL_VCM + NSA_KV
IN_COLS_PAD = COL_NG + LANES

TOK_TILE = 512
RET_TILE = 512
FFN_TILE = 1024
FFN_FTILE = 256
FFN_HALO = 16
FFN_ROW_CHUNKS = 4
SLC_KTILE = 512


GELU_C = math.sqrt(2.0 / math.pi)


def _gelu_tanh(x):
    return 0.5 * x * (1.0 + jnp.tanh(GELU_C * (x + 0.044715 * (x * x * x))))


def _rope(p, cos, sin_signed, first_half):
    half = NSA_DK // 2
    partner = jnp.where(first_half, pltpu.roll(p, LANES - half, 1), pltpu.roll(p, half, 1))
    return p * cos + partner * sin_signed


def _split_bf16(x):
    hi = x.astype(BF16)
    lo = (x - hi.astype(F32)).astype(BF16)
    return hi, lo


def _inproj_kernel(x_ref, g_ref, w_ref, cos_ref, sin_ref,
                   rq_ref, rk_ref, nq_ref, ksl_ref, kwn_ref, rv_ref, vsl_ref, vwn_ref,
                   rg_ref, kvc_ref, ng_ref, stage_ref, *, seq_len):
    tm = x_ref.shape[0]
    x = x_ref[...]
    h = (x * lax.rsqrt(jnp.mean(x * x, axis=-1, keepdims=True) + NORM_EPS) * g_ref[...]).astype(BF16)
    cos = cos_ref[...]
    sin = sin_ref[...]
    lane = lax.broadcasted_iota(jnp.int32, (tm, LANES), 1)
    first_half = (lane % NSA_DK) < NSA_DK // 2

    def proj(c0, n):
        return jnp.dot(h, w_ref[:, c0:c0 + n], preferred_element_type=F32)

    def rope_slab(p, i):
        return _rope(p[:, i * LANES:(i + 1) * LANES], cos, sin, first_half)

    p = proj(COL_RQ, RET_QK)
    for i in range(RET_QK // LANES):
        rq_ref[:, i * LANES:(i + 1) * LANES] = rope_slab(p, i).astype(BF16)
    p = proj(COL_RK, RET_QK)
    for i in range(RET_QK // LANES):
        rk_ref[:, i * LANES:(i + 1) * LANES] = (rope_slab(p, i) * (RET_DK ** -0.5)).astype(BF16)
    p = proj(COL_NQ, D_NSA)
    for i in range(D_NSA // LANES):
        nq_ref[:, i * LANES:(i + 1) * LANES] = (rope_slab(p, i) * NSA_Q_SCALE).astype(BF16)
    p = proj(COL_KSL, 2 * NSA_KV)
    ksl_ref[:, 0:LANES] = rope_slab(p, 0).astype(BF16)
    kwn_ref[...] = rope_slab(p, 1).astype(BF16)
    row = lax.broadcasted_iota(jnp.int32, (tm, LANES), 0)
    pos = (pl.program_id(0) * tm + row) % seq_len
    ksl_ref[:, LANES:2 * LANES] = jnp.where(lane == pos // SLC_BLOCK, 1.0, 0.0).astype(BF16)

    rv_ref[...] = proj(COL_RV, D_RET).astype(BF16)
    p = proj(COL_VSL, 2 * NSA_KV)
    low_half = lane < NSA_DK
    for v_ref, v in ((vsl_ref, p[:, 0:LANES]), (vwn_ref, p[:, LANES:2 * LANES])):
        v_ref[0, 0:LANES, :] = jnp.where(low_half, v, 1.0).T.astype(BF16)
        v_ref[0, LANES:2 * LANES, :] = jnp.where(low_half, 1.0, v).T.astype(BF16)
    rg_ref[...] = proj(COL_RG, D_RET)
    p = proj(COL_KCM, 2 * NSA_KV)
    for s in range(2):
        stage_ref[...] = p[:, s * LANES:(s + 1) * LANES]
        for l in range(CMP_STRIDE):
            kvc_ref[s, :, l * LANES:(l + 1) * LANES] = stage_ref[pl.ds(l, tm // CMP_STRIDE, stride=CMP_STRIDE), :]
    ng_ref[...] = proj(COL_NG, LANES)


def _inproj(x2, gain, w, cos, sin, seq_len):
    T, D = x2.shape
    tm = TOK_TILE
    nt = seq_len // tm
    tok = lambda n: pl.BlockSpec((tm, n), lambda i: (i, 0))
    tok_t = pl.BlockSpec((1, 2 * LANES, tm), lambda i: (i // nt, 0, i % nt))
    out_shape = (
        jax.ShapeDtypeStruct((T, RET_QK), BF16),
        jax.ShapeDtypeStruct((T, RET_QK), BF16),
        jax.ShapeDtypeStruct((T, D_NSA), BF16),
        jax.ShapeDtypeStruct((T, 2 * LANES), BF16),
        jax.ShapeDtypeStruct((T, LANES), BF16),
        jax.ShapeDtypeStruct((T, D_RET), BF16),
        jax.ShapeDtypeStruct((T // seq_len, 2 * LANES, seq_len), BF16),
        jax.ShapeDtypeStruct((T // seq_len, 2 * LANES, seq_len), BF16),
        jax.ShapeDtypeStruct((T, D_RET), F32),
        jax.ShapeDtypeStruct((2, T // CMP_STRIDE, CMP_STRIDE * LANES), F32),
        jax.ShapeDtypeStruct((T, LANES), F32),
    )
    out_specs = (tok(RET_QK), tok(RET_QK), tok(D_NSA), tok(2 * LANES), tok(LANES), tok(D_RET),
                 tok_t, tok_t, tok(D_RET),
                 pl.BlockSpec((2, tm // CMP_STRIDE, CMP_STRIDE * LANES), lambda i: (0, i, 0)), tok(LANES))
    return pl.pallas_call(
        functools.partial(_inproj_kernel, seq_len=seq_len),
        grid=(T // tm,),
        in_specs=[tok(D),
                  pl.BlockSpec((1, D), lambda i: (0, 0)),
                  pl.BlockSpec((D, IN_COLS_PAD), lambda i: (0, 0)),
                  pl.BlockSpec((tm, LANES), lambda i: (i % nt, 0)),
                  pl.BlockSpec((tm, LANES), lambda i: (i % nt, 0))],
        out_specs=out_specs,
        out_shape=out_shape,
        scratch_shapes=[pltpu.VMEM((tm, LANES), F32)],
        compiler_params=pltpu.CompilerParams(dimension_semantics=("arbitrary",),
                                             vmem_limit_bytes=V7X_VMEM_LIMIT),
        name="inproj",
    )(x2, gain, w, cos, sin)


def _retention_kernel(q_ref, k_ref, v_ref, g_ref, gnw_ref, o_ref, state_ref):
    C = RET_CHUNK
    n_chunks = q_ref.shape[0] // C

    @pl.when(pl.program_id(1) == 0)
    def _():
        state_ref[...] = jnp.zeros_like(state_ref)

    ii = lax.broadcasted_iota(jnp.int32, (C, C), 0)
    jj = lax.broadcasted_iota(jnp.int32, (C, C), 1)
    diff = (ii - jj).astype(F32)
    i_col = lax.broadcasted_iota(jnp.int32, (C, 1), 0).astype(F32)
    low_half = lax.broadcasted_iota(jnp.int32, (C, LANES), 1) < RET_DK
    nt_dims = (((1,), (1,)), ((), ()))
    tn_dims = (((0,), (0,)), ((), ()))

    for h in range(RET_HEADS):
        log_gamma = math.log(1.0 - 2.0 ** (-5.0 - h))
        decay = jnp.where(diff >= 0, jnp.exp(log_gamma * jnp.maximum(diff, 0.0)), 0.0)
        xi = jnp.exp(log_gamma * (i_col + 1.0))
        zeta = jnp.exp(log_gamma * (C - 1.0 - i_col))
        chunk_decay = math.exp(log_gamma * C)
        head_lanes = low_half if h % 2 == 0 else jnp.logical_not(low_half)
        qk_cols = slice((h // 2) * LANES, (h // 2 + 1) * LANES)
        v_cols = slice(h * RET_DV, (h + 1) * RET_DV)
        gn_w = gnw_ref[:, v_cols]
        for c in range(n_chunks):
            rows = slice(c * C, (c + 1) * C)
            qm = jnp.where(head_lanes, q_ref[rows, qk_cols].astype(F32), 0.0).astype(BF16)
            ks = k_ref[rows, qk_cols]
            v = v_ref[rows, v_cols]
            scores = lax.dot_general(qm, ks, nt_dims, preferred_element_type=F32) * decay
            o = jnp.dot(scores.astype(BF16), v, preferred_element_type=F32)
            state = state_ref[h]
            o = o + jnp.dot(qm, state.astype(BF16), preferred_element_type=F32) * xi
            kz = (ks.astype(F32) * zeta).astype(BF16)
            kv = lax.dot_general(kz, v, tn_dims, preferred_element_type=F32)
            state_ref[h] = state * chunk_decay + kv
            mu = jnp.mean(o, axis=-1, keepdims=True)
            var = jnp.mean(jnp.square(o - mu), axis=-1, keepdims=True)
            on = (o - mu) * lax.rsqrt(var + GN_EPS) * gn_w
            gate = g_ref[rows, v_cols]
            o_ref[rows, v_cols] = (gate * (1.0 / (1.0 + jnp.exp(-gate))) * on).astype(BF16)


def _retention(rq, rk, rv, rg, gn_w, batch, seq_len):
    T = rq.shape[0]
    tc = RET_TILE
    nt = seq_len // tc
    tok = lambda n: pl.BlockSpec((tc, n), lambda b, i: (b * nt + i, 0))
    return pl.pallas_call(
        _retention_kernel,
        grid=(batch, nt),
        in_specs=[tok(RET_QK), tok(RET_QK), tok(D_RET), tok(D_RET),
                  pl.BlockSpec((1, D_RET), lambda b, i: (0, 0))],
        out_specs=tok(D_RET),
        out_shape=jax.ShapeDtypeStruct((T, D_RET), BF16),
        scratch_shapes=[pltpu.VMEM((RET_HEADS, LANES, RET_DV), F32)],
        compiler_params=pltpu.CompilerParams(dimension_semantics=("arbitrary", "arbitrary"),
                                             vmem_limit_bytes=V7X_VMEM_LIMIT),
        name="retention",
    )(rq, rk, rv, rg, gn_w)


def _compress_kernel(x_ref, pa_ref, pb_ref, wa_ref, wb_ref, w2_ref, cos_ref, sin_ref, o_ref, ot_ref):
    ng = x_ref.shape[2]
    x = x_ref[0, 0]
    xa = (x + pa_ref[0]).astype(BF16)
    xb = (x + pb_ref[0]).astype(BF16)
    a = jnp.dot(xa, wa_ref[0], preferred_element_type=F32)
    b = jnp.dot(xb, wb_ref[0], preferred_element_type=F32)
    hid = a + pltpu.roll(b, ng - 1, 0)
    out = jnp.dot(_gelu_tanh(hid).astype(BF16), w2_ref[0], preferred_element_type=F32)
    lane = lax.broadcasted_iota(jnp.int32, out.shape, 1)
    out = _rope(out, cos_ref[0], sin_ref[0], (lane % NSA_DK) < NSA_DK // 2)
    o_ref[0, 0] = out.astype(BF16)
    ot_ref[0, 0] = out.T.astype(BF16)


def _compress(kvc, pos_a, pos_b, wa, wb, w2, cos, sin, batch, seq_len):
    ng = seq_len // CMP_STRIDE
    gw = CMP_STRIDE * LANES
    x = kvc.reshape(2, batch, ng, gw)
    hid = NSA_KV_HEADS * CMP_HID
    per_kv = lambda *shape: pl.BlockSpec((1,) + shape, lambda s, b: (s,) + (0,) * len(shape))
    return pl.pallas_call(
        _compress_kernel,
        grid=(2, batch),
        in_specs=[pl.BlockSpec((1, 1, ng, gw), lambda s, b: (s, b, 0, 0)),
                  per_kv(1, gw), per_kv(1, gw), per_kv(gw, hid), per_kv(gw, hid), per_kv(hid, LANES),
                  per_kv(ng, LANES), per_kv(ng, LANES)],
        out_specs=(pl.BlockSpec((1, 1, ng, LANES), lambda s, b: (s, b, 0, 0)),
                   pl.BlockSpec((1, 1, LANES, ng), lambda s, b: (s, b, 0, 0))),
        out_shape=(jax.ShapeDtypeStruct((2, batch, ng, LANES), BF16),
                   jax.ShapeDtypeStruct((2, batch, LANES, ng), BF16)),
        compiler_params=pltpu.CompilerParams(dimension_semantics=("arbitrary", "arbitrary"),
                                             vmem_limit_bytes=V7X_VMEM_LIMIT),
        name="compress",
    )(x, pos_a, pos_b, wa, wb, w2, cos, sin)


def _nsa_kernel(nq_ref, ng_ref, kc_ref, vc_ref, ksl_ref, vsl_ref, kwn_ref, vwn_ref, ov_ref, ex_ref,
                o_ref, qaug_ref, s_ref, m_ref, acc_ref, *, seq_len, topk):
    QB = NSA_QBLOCK
    R = NSA_GROUP
    G = NSA_KV_HEADS
    M = R * QB
    NC = kc_ref.shape[2]
    KT = SLC_KTILE
    WK = WINDOW + QB
    t0 = pl.program_id(1) * QB
    nt_dims = (((1,), (1,)), ((), ()))

    low_half = lax.broadcasted_iota(jnp.int32, (QB, LANES), 1) < NSA_DK
    q = nq_ref[...].astype(F32)

    def per_head(x):
        return jnp.concatenate([x] * R, axis=1)

    def normalized(acc, g):
        num, den = (acc[0:NSA_DK], acc[NSA_DK:NSA_DK + 1]) if g == 0 else (acc[NSA_DK:], acc[0:1])
        return num * (1.0 / den)

    for g in range(G):
        head_lanes = low_half if g == 0 else jnp.logical_not(low_half)
        for i in range(R):
            qaug_ref[g, i * QB:(i + 1) * QB, 0:LANES] = jnp.where(
                head_lanes, q[:, i * LANES:(i + 1) * LANES], 0.0).astype(BF16)

    ks = pl.multiple_of(jnp.clip(t0 - WINDOW, 0, seq_len - WK), QB)
    k_pos = ks + lax.broadcasted_iota(jnp.int32, (WK, QB), 0)
    t_pos = t0 + lax.broadcasted_iota(jnp.int32, (WK, QB), 1)
    win_mask = per_head((k_pos <= t_pos) & (k_pos > t_pos - WINDOW))
    win_scores = [lax.dot_general(kwn_ref[0, pl.ds(ks, WK), :], qaug_ref[g, :, 0:LANES], nt_dims,
                                  preferred_element_type=F32) for g in range(G)]
    cmp_scores = [lax.dot_general(kc_ref[0, 0], qaug_ref[g, :, 0:LANES], nt_dims,
                                  preferred_element_type=F32) for g in range(G)]
    o_win = []
    for g in range(G):
        sw = jnp.where(win_mask, win_scores[g], NEG)
        e_win = jnp.exp2(sw - jnp.max(sw, axis=0, keepdims=True)).astype(BF16)
        pv = jnp.dot(vwn_ref[0, g * LANES:(g + 1) * LANES, pl.ds(ks, WK)], e_win,
                     preferred_element_type=F32)
        o_win.append(normalized(pv, g))

    n_idx = lax.broadcasted_iota(jnp.int32, (NC, QB), 0)
    t_col = t0 + lax.broadcasted_iota(jnp.int32, (NC, QB), 1)
    cmp_mask = per_head(n_idx * CMP_STRIDE + (CMP_BLOCK - 1) <= t_col)
    jj = lax.broadcasted_iota(jnp.int32, (LANES, QB), 0)
    q_blk = (t0 + lax.broadcasted_iota(jnp.int32, (LANES, QB), 1)) // SLC_BLOCK
    valid = jj <= q_blk
    forced = (jj == 0) | (valid & (jj > q_blk - SLC_LOCAL))
    jf = jj.astype(F32)
    o_cmp, scores = [], []
    for g in range(G):
        sc = jnp.where(cmp_mask, cmp_scores[g], NEG)
        e = jnp.where(cmp_mask, jnp.exp2(sc - jnp.max(sc, axis=0, keepdims=True)), 0.0)
        p_cmp = e * (1.0 / jnp.maximum(jnp.sum(e, axis=0, keepdims=True), 1e-30))
        acc = jnp.dot(vc_ref[0, 0], p_cmp.astype(BF16), preferred_element_type=F32)
        o_cmp.append(acc[g * NSA_DK:(g + 1) * NSA_DK])
        p_sum = p_cmp[:, 0:QB]
        for i in range(1, R):
            p_sum = p_sum + p_cmp[:, i * QB:(i + 1) * QB]
        p_hi, p_lo = _split_bf16(p_sum)
        imp = (jnp.dot(ov_ref[...], p_hi, preferred_element_type=F32)
               + jnp.dot(ov_ref[...], p_lo, preferred_element_type=F32))
        scores.append(jnp.where(forced, -jnp.inf, jnp.where(valid, imp, -SEL_BIG)))

    def pick_round(_, carry):
        out = []
        for score in carry:
            best = jnp.max(score, axis=0, keepdims=True)
            first = jnp.min(jnp.where(score == best, jf, float(LANES)), axis=0, keepdims=True)
            out.append(jnp.where(jf == first, -jnp.inf, score))
        return tuple(out)

    taken = lax.fori_loop(0, topk - (1 + SLC_LOCAL), pick_round, tuple(scores))
    for g in range(G):
        bias = jnp.where((taken[g] == -jnp.inf) & valid, 0.0, NEG).T.astype(BF16)
        for i in range(R):
            qaug_ref[g, i * QB:(i + 1) * QB, LANES:2 * LANES] = bias

    def slc_scores(k0, slot):
        for g in range(G):
            s_ref[slot, g] = lax.dot_general(ksl_ref[0, pl.ds(k0, KT), :], qaug_ref[g], nt_dims,
                                             preferred_element_type=F32)

    def slc_accumulate(k0, slot, causal):
        for g in range(G):
            s = s_ref[slot, g]
            if causal:
                k_pos = k0 + lax.broadcasted_iota(jnp.int32, (KT, QB), 0)
                t_pos = t0 + lax.broadcasted_iota(jnp.int32, (KT, QB), 1)
                s = jnp.where(per_head(k_pos <= t_pos), s, NEG)
            m_run = m_ref[g]
            m_new = jnp.maximum(m_run, jnp.max(s, axis=0, keepdims=True))
            p = jnp.exp2(s - m_new).astype(BF16)
            pv = jnp.dot(vsl_ref[0, g * LANES:(g + 1) * LANES, pl.ds(k0, KT)], p,
                         preferred_element_type=F32)
            acc_ref[g] = jnp.exp2(m_run - m_new) * acc_ref[g] + pv
            m_ref[g] = m_new

    m_ref[...] = jnp.full(m_ref.shape, NEG, F32)
    acc_ref[...] = jnp.zeros_like(acc_ref)
    n_full = t0 // KT
    slc_scores(0, 0)

    def slc_pair(j, _):
        k0 = pl.multiple_of(j * (2 * KT), 2 * KT)
        slc_scores(k0 + KT, 1)
        slc_accumulate(k0, 0, False)
        slc_scores(k0 + 2 * KT, 0)
        slc_accumulate(k0 + KT, 1, False)
        return 0

    lax.fori_loop(0, n_full // 2, slc_pair, 0)
    k_diag = pl.multiple_of(n_full * KT, KT)

    @pl.when(n_full % 2 == 0)
    def _():
        slc_accumulate(k_diag, 0, True)

    @pl.when(n_full % 2 == 1)
    def _():
        slc_scores(k_diag, 1)
        slc_accumulate(k_diag - KT, 0, False)
        slc_accumulate(k_diag, 1, True)

    o_slc = [normalized(acc_ref[g], g) for g in range(G)]

    g_hi, g_lo = _split_bf16((1.0 / (1.0 + jnp.exp(-ng_ref[...]))).T)
    gates = (jnp.dot(ex_ref[...], g_hi, preferred_element_type=F32)
             + jnp.dot(ex_ref[...], g_lo, preferred_element_type=F32))
    branches = [jnp.concatenate(o, axis=0) for o in (o_cmp, o_slc, o_win)]
    for i in range(R):
        y = jnp.zeros((LANES, QB), F32)
        for b, o in enumerate(branches):
            y = y + gates[b * D_NSA + i * LANES:b * D_NSA + (i + 1) * LANES] * o[:, i * QB:(i + 1) * QB]
        o_ref[:, i * LANES:(i + 1) * LANES] = y.T.astype(BF16)


def _nsa(nq, ng, kvc_cmp, kvc_cmp_t, ksl, vsl, kwn, vwn, overlap, expand, batch, seq_len):
    T = nq.shape[0]
    QB = NSA_QBLOCK
    nqb = seq_len // QB
    nc = seq_len // CMP_STRIDE
    topk = min(SLC_TOPK, seq_len // SLC_BLOCK)
    tok = lambda n: pl.BlockSpec((QB, n), lambda b, i: (b * nqb + i, 0))
    seq = lambda n: pl.BlockSpec((1, seq_len, n), lambda b, i: (b, 0, 0))
    seq_t = pl.BlockSpec((1, 2 * LANES, seq_len), lambda b, i: (b, 0, 0))
    return pl.pallas_call(
        functools.partial(_nsa_kernel, seq_len=seq_len, topk=topk),
        grid=(batch, nqb),
        in_specs=[tok(D_NSA), tok(LANES),
                  pl.BlockSpec((1, 1, nc, LANES), lambda b, i: (0, b, 0, 0)),
                  pl.BlockSpec((1, 1, LANES, nc), lambda b, i: (1, b, 0, 0)),
                  seq(2 * LANES), seq_t, seq(LANES), seq_t,
                  pl.BlockSpec(overlap.shape, lambda b, i: (0, 0)),
                  pl.BlockSpec(expand.shape, lambda b, i: (0, 0))],
        out_specs=tok(D_NSA),
        out_shape=jax.ShapeDtypeStruct((T, D_NSA), BF16),
        scratch_shapes=[pltpu.VMEM((NSA_KV_HEADS, NSA_GROUP * QB, 2 * LANES), BF16),
                        pltpu.VMEM((2, NSA_KV_HEADS, SLC_KTILE, NSA_GROUP * QB), F32),
                        pltpu.VMEM((NSA_KV_HEADS, 1, NSA_GROUP * QB), F32),
                        pltpu.VMEM((NSA_KV_HEADS, LANES, NSA_GROUP * QB), F32)],
        compiler_params=pltpu.CompilerParams(dimension_semantics=("arbitrary", "arbitrary"),
                                             vmem_limit_bytes=V7X_VMEM_LIMIT),
        name="nsa",
    )(nq, ng, kvc_cmp, kvc_cmp_t, ksl.reshape(batch, seq_len, -1), vsl,
      kwn.reshape(batch, seq_len, -1), vwn, overlap, expand)


def _outproj_kernel(yr_ref, yn_ref, x_ref, w_ref, g_ref, o_ref):
    mix = (jnp.dot(yr_ref[...], w_ref[0:D_RET, :], preferred_element_type=F32)
           + jnp.dot(yn_ref[...], w_ref[D_RET:D_RET + D_NSA, :], preferred_element_type=F32))
    n = mix * lax.rsqrt(jnp.mean(mix * mix, axis=-1, keepdims=True) + NORM_EPS) * g_ref[...]
    o_ref[...] = x_ref[...] + n


def _outproj(y_ret, y_nsa, x2, w, gain):
    T, D = x2.shape
    tm = TOK_TILE
    tok = lambda n: pl.BlockSpec((tm, n), lambda i: (i, 0))
    return pl.pallas_call(
        _outproj_kernel,
        grid=(T // tm,),
        in_specs=[tok(D_RET), tok(D_NSA), tok(D),
                  pl.BlockSpec(w.shape, lambda i: (0, 0)),
                  pl.BlockSpec((1, D), lambda i: (0, 0))],
        out_specs=tok(D),
        out_shape=jax.ShapeDtypeStruct((T, D), F32),
        compiler_params=pltpu.CompilerParams(dimension_semantics=("arbitrary",),
                                             vmem_limit_bytes=V7X_VMEM_LIMIT),
        name="outproj",
    )(y_ret, y_nsa, x2, w, gain)


def _ffn_kernel(x_ref, xp_ref, gpre_ref, wg_ref, wv_ref, cg_ref, cv_ref, wd_ref, gpost_ref, o_ref,
                h_ref, ug_ref, uv_ref, acc_ref, *, tiles_per_seq):
    tm = x_ref.shape[0]
    H = FFN_HALO
    f = pl.program_id(1)

    def normed(x):
        return x * lax.rsqrt(jnp.mean(x * x, axis=-1, keepdims=True) + NORM_EPS) * gpre_ref[...]

    @pl.when(f == 0)
    def _():
        keep = jnp.where(pl.program_id(0) % tiles_per_seq == 0, 0.0, 1.0)
        h_ref[0:H, :] = (normed(xp_ref[...]) * keep).astype(BF16)
        h_ref[H:H + tm, :] = normed(x_ref[...]).astype(BF16)
        acc_ref[...] = jnp.zeros_like(acc_ref)

    rc = tm // FFN_ROW_CHUNKS

    def up_project(c):
        rows = slice(0 if c == 0 else H + c * rc, H + (c + 1) * rc)
        h = h_ref[rows, :]
        ug_ref[rows, :] = jnp.dot(h, wg_ref[...], preferred_element_type=F32)
        uv_ref[rows, :] = jnp.dot(h, wv_ref[...], preferred_element_type=F32)

    def causal_conv(u_ref, c_ref, r0, scale):
        out = (scale * c_ref[CONV_WIDTH - 1:CONV_WIDTH, :]) * u_ref[r0:r0 + rc, :]
        for k in range(CONV_WIDTH - 1):
            d = CONV_WIDTH - 1 - k
            out = out + (scale * c_ref[k:k + 1, :]) * u_ref[r0 - d:r0 - d + rc, :]
        return out

    def gate_and_down(c):
        g = causal_conv(ug_ref, cg_ref, H + c * rc, 1.0)
        v_half = causal_conv(uv_ref, cv_ref, H + c * rc, 0.5)
        inner = g * (GELU_C + (GELU_C * 0.044715) * (g * g))
        act = (g + g * jnp.tanh(inner)) * v_half
        acc_ref[c * rc:(c + 1) * rc, :] += jnp.dot(act.astype(BF16), wd_ref[...],
                                                   preferred_element_type=F32)

    up_project(0)
    for c in range(FFN_ROW_CHUNKS):
        if c + 1 < FFN_ROW_CHUNKS:
            up_project(c + 1)
        gate_and_down(c)

    @pl.when(f == pl.num_programs(1) - 1)
    def _():
        y = acc_ref[...]
        n = y * lax.rsqrt(jnp.mean(y * y, axis=-1, keepdims=True) + NORM_EPS) * gpost_ref[...]
        o_ref[...] = x_ref[...] + n


def _ffn(x2, g_pre, w_up, conv_w, w_down, g_post, seq_len):
    T, D = x2.shape
    d_ff = w_down.shape[0]
    tm, tf, H = FFN_TILE, FFN_FTILE, FFN_HALO
    nf = d_ff // tf
    return pl.pallas_call(
        functools.partial(_ffn_kernel, tiles_per_seq=seq_len // tm),
        grid=(T // tm, nf),
        in_specs=[pl.BlockSpec((tm, D), lambda i, f: (i, 0)),
                  pl.BlockSpec((H, D), lambda i, f: (jnp.maximum(i * (tm // H) - 1, 0), 0)),
                  pl.BlockSpec((1, D), lambda i, f: (0, 0)),
                  pl.BlockSpec((D, tf), lambda i, f: (0, f)),
                  pl.BlockSpec((D, tf), lambda i, f: (0, nf + f)),
                  pl.BlockSpec((CONV_WIDTH, tf), lambda i, f: (0, f)),
                  pl.BlockSpec((CONV_WIDTH, tf), lambda i, f: (0, nf + f)),
                  pl.BlockSpec((tf, D), lambda i, f: (f, 0)),
                  pl.BlockSpec((1, D), lambda i, f: (0, 0))],
        out_specs=pl.BlockSpec((tm, D), lambda i, f: (i, 0)),
        out_shape=jax.ShapeDtypeStruct((T, D), F32),
        scratch_shapes=[pltpu.VMEM((tm + H, D), BF16), pltpu.VMEM((tm + H, tf), F32),
                        pltpu.VMEM((tm + H, tf), F32), pltpu.VMEM((tm, D), F32)],
        compiler_params=pltpu.CompilerParams(dimension_semantics=("arbitrary", "arbitrary"),
                                             vmem_limit_bytes=V7X_VMEM_LIMIT),
        name="ffn",
    )(x2, x2, g_pre, w_up, w_up, conv_w, conv_w, w_down, g_post)


_NSA_HEAD_ORDER = [g * NSA_GROUP + i for i in range(NSA_GROUP) for g in range(NSA_KV_HEADS)]


def _rope_tables(pos):
    inv = 1.0 / (ROPE_THETA ** (jnp.arange(0, NSA_DK, 2, dtype=F32) / NSA_DK))
    ang = pos.astype(F32)[:, None] * inv[None, :]
    c, s = jnp.cos(ang), jnp.sin(ang)
    return jnp.concatenate([c, c, c, c], axis=1), jnp.concatenate([-s, s, -s, s], axis=1)


def _prep_w_in(w):
    lead = w.shape[:-1]
    splits = np.cumsum([RET_QK, RET_QK, D_RET, D_RET, D_NSA] + [NSA_KV] * 6)
    rq, rk, rv, rg, nq, kcm, vcm, ksl, vsl, kwn, vwn, ng = jnp.split(w, [int(s) for s in splits], axis=-1)
    nq = nq.reshape(lead + (NSA_HEADS, NSA_DK))[..., np.array(_NSA_HEAD_ORDER), :].reshape(lead + (D_NSA,))
    ng = jnp.pad(ng, [(0, 0)] * len(lead) + [(0, LANES - NSA_GATES)])
    return jnp.concatenate([rq, rk, nq, ksl, kwn, rv, vsl, vwn, rg, kcm, vcm, ng], axis=-1).astype(BF16)


def _prep_w_out(w):
    layers, _, d = w.shape
    w_nsa = w[:, D_RET:].reshape(layers, NSA_HEADS, NSA_DK, d)[:, np.array(_NSA_HEAD_ORDER)]
    return jnp.concatenate([w[:, :D_RET], w_nsa.reshape(layers, D_NSA, d)], axis=1).astype(BF16)


def _prep_compress(pos, w1, w2):
    lead = w1.shape[:-2]
    half = CMP_STRIDE

    def block_diag(w, axis):
        z = jnp.zeros_like(w)
        return jnp.stack([jnp.concatenate([w, z], axis=-1), jnp.concatenate([z, w], axis=-1)], axis=axis)

    def first_layer(w_half):
        w4 = w_half.reshape(lead + (half, NSA_DK, CMP_HID))
        return block_diag(w4, -3).reshape(lead + (half * NSA_KV_HEADS * NSA_DK, -1)).astype(BF16)

    def pos_row(p_half):
        rows = jnp.broadcast_to(p_half[..., :, None, :], lead + (half, NSA_KV_HEADS, NSA_DK))
        return rows.reshape(lead + (1, -1))

    w2x = block_diag(w2, -3).reshape(lead + (NSA_KV_HEADS * CMP_HID, -1)).astype(BF16)
    n1 = half * NSA_DK
    return (pos_row(pos[..., :half, :]), pos_row(pos[..., half:, :]),
            first_layer(w1[..., :n1, :]), first_layer(w1[..., n1:, :]), w2x)


def _overlap_matrix(seq_len):
    nc = seq_len // CMP_STRIDE
    cmp_start = np.arange(nc) * CMP_STRIDE
    slc_start = np.arange(LANES) * SLC_BLOCK
    ov = ((cmp_start[:, None] < slc_start[None, :] + SLC_BLOCK)
          & (cmp_start[:, None] + CMP_BLOCK > slc_start[None, :]))
    n_cmp = (seq_len - CMP_BLOCK) // CMP_STRIDE + 1
    ov &= (np.arange(nc) < n_cmp)[:, None]
    return jnp.asarray(ov.T.astype(np.float32), dtype=BF16)


def _gate_expand_matrix():
    ex = np.zeros((LANES, 3 * D_NSA), np.float32)
    for branch in range(3):
        for p in range(D_NSA):
            head = _NSA_HEAD_ORDER[p // NSA_DK]
            ex[branch * NSA_HEADS + head, branch * D_NSA + p] = 1.0
    return jnp.asarray(ex.T, dtype=BF16)


def kernel(x, norm_mix_pre, w_in, ret_gn_w, cmp_k_pos, cmp_k_w1, cmp_k_w2, cmp_v_pos, cmp_v_w1, cmp_v_w2,
           w_out, norm_mix_post, norm_ffn_pre, ffn_w_up, ffn_conv, ffn_w_down, norm_ffn_post):
    B, S, D = x.shape
    depth = w_in.shape[0]
    assert S % SLC_KTILE == 0 and S % FFN_TILE == 0 and S // SLC_BLOCK <= LANES and S >= WINDOW + NSA_QBLOCK
    assert ffn_w_down.shape[1] % FFN_FTILE == 0

    cos, sin = _rope_tables(jnp.arange(S, dtype=jnp.int32))
    nc = S // CMP_STRIDE
    ccos, csin = _rope_tables(jnp.arange(nc, dtype=jnp.int32) * CMP_STRIDE + (CMP_BLOCK - 1))
    cmp_cos = jnp.stack([ccos, jnp.ones_like(ccos)])
    cmp_sin = jnp.stack([csin, jnp.zeros_like(csin)])
    overlap = _overlap_matrix(S)
    expand = _gate_expand_matrix()

    w_in_p = _prep_w_in(w_in)
    w_out_p = _prep_w_out(w_out)
    w_up_p = ffn_w_up.astype(BF16)
    w_down_p = ffn_w_down.astype(BF16)
    cmp_p = _prep_compress(jnp.stack([cmp_k_pos, cmp_v_pos], axis=1), jnp.stack([cmp_k_w1, cmp_v_w1], axis=1),
                           jnp.stack([cmp_k_w2, cmp_v_w2], axis=1))

    x2 = x.reshape(B * S, D)
    for l in range(depth):
        outs = _inproj(x2, norm_mix_pre[l][None], w_in_p[l], cos, sin, S)
        rq, rk, nq, ksl, kwn, rv, vsl, vwn, rg, kvc, ng = outs
        y_ret = _retention(rq, rk, rv, rg, ret_gn_w[l][None], B, S)
        kvc_cmp, kvc_cmp_t = _compress(kvc, *[a[l] for a in cmp_p], cmp_cos, cmp_sin, B, S)
        y_nsa = _nsa(nq, ng, kvc_cmp, kvc_cmp_t, ksl, vsl, kwn, vwn, overlap, expand, B, S)
        x2 = _outproj(y_ret, y_nsa, x2, w_out_p[l], norm_mix_post[l][None])
        x2 = _ffn(x2, norm_ffn_pre[l][None], w_up_p[l], ffn_conv[l], w_down_p[l], norm_ffn_post[l][None], S)
    return x2.reshape(B, S, D)
```

```python
import functools
import math

import jax
import jax.numpy as jnp
import numpy as np
from jax import lax
from jax.experimental import pallas as pl
from jax.experimental.pallas import tpu as pltpu

F32 = jnp.float32
BF16 = jnp.bfloat16

LANES = 128
SUBLANES = 8
V7X_VMEM_LIMIT = 56 * 1024 * 1024

ROPE_THETA = 10000.0
NORM_EPS = 1e-6
GN_EPS = 1e-5
NEG = -1e30
SEL_BIG = 1e9

RET_HEADS = 4
RET_DK = 64
RET_DV = 128
RET_CHUNK = 128
NSA_HEADS = 8
NSA_KV_HEADS = 2
NSA_DK = 64
NSA_GROUP = NSA_HEADS // NSA_KV_HEADS
CMP_BLOCK = 32
CMP_STRIDE = 16
CMP_HID = 256
SLC_BLOCK = 64
SLC_TOPK = 16
SLC_LOCAL = 2
WINDOW = 512
NSA_QBLOCK = 128
NSA_GATES = 3 * NSA_HEADS
CONV_WIDTH = 3

NSA_Q_SCALE = NSA_DK ** -0.5 * math.log2(math.e)

D_RET = RET_HEADS * RET_DV
D_NSA = NSA_HEADS * NSA_DK
RET_QK = RET_HEADS * RET_DK
NSA_KV = NSA_KV_HEADS * NSA_DK

COL_RQ = 0
COL_RK = COL_RQ + RET_QK
COL_NQ = COL_RK + RET_QK
COL_KSL = COL_NQ + D_NSA
COL_KWN = COL_KSL + NSA_KV
ROPE_COLS = COL_KWN + NSA_KV
COL_RV = ROPE_COLS
COL_VSL = COL_RV + D_RET
COL_VWN = COL_VSL + NSA_KV
COL_RG = COL_VWN + NSA_KV
COL_KCM = COL_RG + D_RET
COL_VCM = COL_KCM + NSA_KV
COL_NG = COL_VCM + NSA_KV
IN_COLS_PAD = COL_NG + LANES

TOK_TILE = 512
RET_TILE = 512
FFN_TILE = 1024
FFN_FTILE = 256
FFN_HALO = 16
FFN_ROW_CHUNKS = 4
SLC_KTILE = 512
SLC_UNROLL = 4


GELU_C = math.sqrt(2.0 / math.pi)


def _gelu_tanh(x):
    return 0.5 * x * (1.0 + jnp.tanh(GELU_C * (x + 0.044715 * (x * x * x))))


def _rope(p, cos, sin_signed, first_half):
    half = NSA_DK // 2
    partner = jnp.where(first_half, pltpu.roll(p, LANES - half, 1), pltpu.roll(p, half, 1))
    return p * cos + partner * sin_signed


def _split_bf16(x):
    hi = x.astype(BF16)
    lo = (x - hi.astype(F32)).astype(BF16)
    return hi, lo


def _inproj_kernel(x_ref, g_ref, w_ref, cos_ref, sin_ref,
                   rq_ref, rk_ref, nq_ref, ksl_ref, kwn_ref, rv_ref, vsl_ref, vwn_ref,
                   rg_ref, kvc_ref, ng_ref, stage_ref, *, seq_len):
    tm = x_ref.shape[0]
    x = x_ref[...]
    h = (x * lax.rsqrt(jnp.mean(x * x, axis=-1, keepdims=True) + NORM_EPS) * g_ref[...]).astype(BF16)
    cos = cos_ref[...]
    sin = sin_ref[...]
    lane = lax.broadcasted_iota(jnp.int32, (tm, LANES), 1)
    first_half = (lane % NSA_DK) < NSA_DK // 2

    def proj(c0, n):
        return jnp.dot(h, w_ref[:, c0:c0 + n], preferred_element_type=F32)

    def rope_slab(p, i):
        return _rope(p[:, i * LANES:(i + 1) * LANES], cos, sin, first_half)

    p = proj(COL_RQ, RET_QK)
    for i in range(RET_QK // LANES):
        rq_ref[:, i * LANES:(i + 1) * LANES] = rope_slab(p, i).astype(BF16)
    p = proj(COL_RK, RET_QK)
    for i in range(RET_QK // LANES):
        rk_ref[:, i * LANES:(i + 1) * LANES] = (rope_slab(p, i) * (RET_DK ** -0.5)).astype(BF16)
    p = proj(COL_NQ, D_NSA)
    for i in range(D_NSA // LANES):
        nq_ref[:, i * LANES:(i + 1) * LANES] = (rope_slab(p, i) * NSA_Q_SCALE).astype(BF16)
    p = proj(COL_KSL, 2 * NSA_KV)
    ksl_ref[:, 0:LANES] = rope_slab(p, 0).astype(BF16)
    kwn_ref[...] = rope_slab(p, 1).astype(BF16)
    row = lax.broadcasted_iota(jnp.int32, (tm, LANES), 0)
    pos = (pl.program_id(0) * tm + row) % seq_len
    ksl_ref[:, LANES:2 * LANES] = jnp.where(lane == pos // SLC_BLOCK, 1.0, 0.0).astype(BF16)

    rv_ref[...] = proj(COL_RV, D_RET).astype(BF16)
    p = proj(COL_VSL, 2 * NSA_KV)
    low_half = lane < NSA_DK
    for v_ref, v in ((vsl_ref, p[:, 0:LANES]), (vwn_ref, p[:, LANES:2 * LANES])):
        v_ref[0, 0:LANES, :] = jnp.where(low_half, v, 1.0).T.astype(BF16)
        v_ref[0, LANES:2 * LANES, :] = jnp.where(low_half, 1.0, v).T.astype(BF16)
    rg_ref[...] = proj(COL_RG, D_RET)
    p = proj(COL_KCM, 2 * NSA_KV)
    for s in range(2):
        stage_ref[...] = p[:, s * LANES:(s + 1) * LANES]
        for l in range(CMP_STRIDE):
            kvc_ref[s, :, l * LANES:(l + 1) * LANES] = stage_ref[pl.ds(l, tm // CMP_STRIDE, stride=CMP_STRIDE), :]
    ng_ref[...] = proj(COL_NG, LANES)


def _inproj(x2, gain, w, cos, sin, seq_len):
    T, D = x2.shape
    tm = TOK_TILE
    nt = seq_len // tm
    tok = lambda n: pl.BlockSpec((tm, n), lambda i: (i, 0))
    tok_t = pl.BlockSpec((1, 2 * LANES, tm), lambda i: (i // nt, 0, i % nt))
    out_shape = (
        jax.ShapeDtypeStruct((T, RET_QK), BF16),
        jax.ShapeDtypeStruct((T, RET_QK), BF16),
        jax.ShapeDtypeStruct((T, D_NSA), BF16),
        jax.ShapeDtypeStruct((T, 2 * LANES), BF16),
        jax.ShapeDtypeStruct((T, LANES), BF16),
        jax.ShapeDtypeStruct((T, D_RET), BF16),
        jax.ShapeDtypeStruct((T // seq_len, 2 * LANES, seq_len), BF16),
        jax.ShapeDtypeStruct((T // seq_len, 2 * LANES, seq_len), BF16),
        jax.ShapeDtypeStruct((T, D_RET), F32),
        jax.ShapeDtypeStruct((2, T // CMP_STRIDE, CMP_STRIDE * LANES), F32),
        jax.ShapeDtypeStruct((T, LANES), F32),
    )
    out_specs = (tok(RET_QK), tok(RET_QK), tok(D_NSA), tok(2 * LANES), tok(LANES), tok(D_RET),
                 tok_t, tok_t, tok(D_RET),
                 pl.BlockSpec((2, tm // CMP_STRIDE, CMP_STRIDE * LANES), lambda i: (0, i, 0)), tok(LANES))
    return pl.pallas_call(
        functools.partial(_inproj_kernel, seq_len=seq_len),
        grid=(T // tm,),
        in_specs=[tok(D),
                  pl.BlockSpec((1, D), lambda i: (0, 0)),
                  pl.BlockSpec((D, IN_COLS_PAD), lambda i: (0, 0)),
                  pl.BlockSpec((tm, LANES), lambda i: (i % nt, 0)),
                  pl.BlockSpec((tm, LANES), lambda i: (i % nt, 0))],
        out_specs=out_specs,
        out_shape=out_shape,
        scratch_shapes=[pltpu.VMEM((tm, LANES), F32)],
        compiler_params=pltpu.CompilerParams(dimension_semantics=("arbitrary",),
                                             vmem_limit_bytes=V7X_VMEM_LIMIT),
        name="inproj",
    )(x2, gain, w, cos, sin)


def _retention_kernel(q_ref, k_ref, v_ref, g_ref, gnw_ref, o_ref, state_ref):
    C = RET_CHUNK
    n_chunks = q_ref.shape[0] // C

    @pl.when(pl.program_id(1) == 0)
    def _():
        state_ref[...] = jnp.zeros_like(state_ref)

    ii = lax.broadcasted_iota(jnp.int32, (C, C), 0)
    jj = lax.broadcasted_iota(jnp.int32, (C, C), 1)
    diff = (ii - jj).astype(F32)
    i_col = lax.broadcasted_iota(jnp.int32, (C, 1), 0).astype(F32)
    low_half = lax.broadcasted_iota(jnp.int32, (C, LANES), 1) < RET_DK
    nt_dims = (((1,), (1,)), ((), ()))
    tn_dims = (((0,), (0,)), ((), ()))

    for h in range(RET_HEADS):
        log_gamma = math.log(1.0 - 2.0 ** (-5.0 - h))
        decay = jnp.where(diff >= 0, jnp.exp(log_gamma * jnp.maximum(diff, 0.0)), 0.0)
        xi = jnp.exp(log_gamma * (i_col + 1.0))
        zeta = jnp.exp(log_gamma * (C - 1.0 - i_col))
        chunk_decay = math.exp(log_gamma * C)
        head_lanes = low_half if h % 2 == 0 else jnp.logical_not(low_half)
        qk_cols = slice((h // 2) * LANES, (h // 2 + 1) * LANES)
        v_cols = slice(h * RET_DV, (h + 1) * RET_DV)
        gn_w = gnw_ref[:, v_cols]
        for c in range(n_chunks):
            rows = slice(c * C, (c + 1) * C)
            qm = jnp.where(head_lanes, q_ref[rows, qk_cols].astype(F32), 0.0).astype(BF16)
            ks = k_ref[rows, qk_cols]
            v = v_ref[rows, v_cols]
            scores = lax.dot_general(qm, ks, nt_dims, preferred_element_type=F32) * decay
            o = jnp.dot(scores.astype(BF16), v, preferred_element_type=F32)
            state = state_ref[h]
            o = o + jnp.dot(qm, state.astype(BF16), preferred_element_type=F32) * xi
            kz = (ks.astype(F32) * zeta).astype(BF16)
            kv = lax.dot_general(kz, v, tn_dims, preferred_element_type=F32)
            state_ref[h] = state * chunk_decay + kv
            mu = jnp.mean(o, axis=-1, keepdims=True)
            var = jnp.mean(jnp.square(o - mu), axis=-1, keepdims=True)
            on = (o - mu) * lax.rsqrt(var + GN_EPS) * gn_w
            gate = g_ref[rows, v_cols]
            o_ref[rows, v_cols] = (gate * (1.0 / (1.0 + jnp.exp(-gate))) * on).astype(BF16)


def _retention(rq, rk, rv, rg, gn_w, batch, seq_len):
    T = rq.shape[0]
    tc = RET_TILE
    nt = seq_len // tc
    tok = lambda n: pl.BlockSpec((tc, n), lambda b, i: (b * nt + i, 0))
    return pl.pallas_call(
        _retention_kernel,
        grid=(batch, nt),
        in_specs=[tok(RET_QK), tok(RET_QK), tok(D_RET), tok(D_RET),
                  pl.BlockSpec((1, D_RET), lambda b, i: (0, 0))],
        out_specs=tok(D_RET),
        out_shape=jax.ShapeDtypeStruct((T, D_RET), BF16),
        scratch_shapes=[pltpu.VMEM((RET_HEADS, LANES, RET_DV), F32)],
        compiler_params=pltpu.CompilerParams(dimension_semantics=("arbitrary", "arbitrary"),
                                             vmem_limit_bytes=V7X_VMEM_LIMIT),
        name="retention",
    )(rq, rk, rv, rg, gn_w)


def _compress_kernel(x_ref, pa_ref, pb_ref, wa_ref, wb_ref, w2_ref, cos_ref, sin_ref, o_ref, ot_ref):
    ng = x_ref.shape[2]
    x = x_ref[0, 0]
    xa = (x + pa_ref[0]).astype(BF16)
    xb = (x + pb_ref[0]).astype(BF16)
    a = jnp.dot(xa, wa_ref[0], preferred_element_type=F32)
    b = jnp.dot(xb, wb_ref[0], preferred_element_type=F32)
    hid = a + pltpu.roll(b, ng - 1, 0)
    out = jnp.dot(_gelu_tanh(hid).astype(BF16), w2_ref[0], preferred_element_type=F32)
    lane = lax.broadcasted_iota(jnp.int32, out.shape, 1)
    out = _rope(out, cos_ref[0], sin_ref[0], (lane % NSA_DK) < NSA_DK // 2)
    o_ref[0, 0] = out.astype(BF16)
    ot_ref[0, 0] = out.T.astype(BF16)


def _compress(kvc, pos_a, pos_b, wa, wb, w2, cos, sin, batch, seq_len):
    ng = seq_len // CMP_STRIDE
    gw = CMP_STRIDE * LANES
    x = kvc.reshape(2, batch, ng, gw)
    hid = NSA_KV_HEADS * CMP_HID
    per_kv = lambda *shape: pl.BlockSpec((1,) + shape, lambda s, b: (s,) + (0,) * len(shape))
    return pl.pallas_call(
        _compress_kernel,
        grid=(2, batch),
        in_specs=[pl.BlockSpec((1, 1, ng, gw), lambda s, b: (s, b, 0, 0)),
                  per_kv(1, gw), per_kv(1, gw), per_kv(gw, hid), per_kv(gw, hid), per_kv(hid, LANES),
                  per_kv(ng, LANES), per_kv(ng, LANES)],
        out_specs=(pl.BlockSpec((1, 1, ng, LANES), lambda s, b: (s, b, 0, 0)),
                   pl.BlockSpec((1, 1, LANES, ng), lambda s, b: (s, b, 0, 0))),
        out_shape=(jax.ShapeDtypeStruct((2, batch, ng, LANES), BF16),
                   jax.ShapeDtypeStruct((2, batch, LANES, ng), BF16)),
        compiler_params=pltpu.CompilerParams(dimension_semantics=("arbitrary", "arbitrary"),
                                             vmem_limit_bytes=V7X_VMEM_LIMIT),
        name="compress",
    )(x, pos_a, pos_b, wa, wb, w2, cos, sin)


def _nsa_kernel(nq_ref, ng_ref, kc_ref, vc_ref, ksl_ref, vsl_ref, kwn_ref, vwn_ref, ov_ref, ex_ref,
                o_ref, qaug_ref, s_ref, m_ref, acc_ref, out_ref, score_ref, *, seq_len, topk):
    QB = NSA_QBLOCK
    R = NSA_GROUP
    G = NSA_KV_HEADS
    M = R * QB
    NC = kc_ref.shape[2]
    KT = SLC_KTILE
    WK = WINDOW + QB
    t0 = pl.program_id(1) * QB
    nt_dims = (((1,), (1,)), ((), ()))

    low_half = lax.broadcasted_iota(jnp.int32, (QB, LANES), 1) < NSA_DK
    q = nq_ref[...].astype(F32)

    def per_head(x):
        return jnp.concatenate([x] * R, axis=1)

    def normalized(acc, g):
        num, den = (acc[0:NSA_DK], acc[NSA_DK:NSA_DK + 1]) if g == 0 else (acc[NSA_DK:], acc[0:1])
        return num * (1.0 / den)

    for g in range(G):
        head_lanes = low_half if g == 0 else jnp.logical_not(low_half)
        for i in range(R):
            qaug_ref[g, i * QB:(i + 1) * QB, 0:LANES] = jnp.where(
                head_lanes, q[:, i * LANES:(i + 1) * LANES], 0.0).astype(BF16)

    ks = pl.multiple_of(jnp.clip(t0 - WINDOW, 0, seq_len - WK), QB)
    k_pos = ks + lax.broadcasted_iota(jnp.int32, (WK, QB), 0)
    t_pos = t0 + lax.broadcasted_iota(jnp.int32, (WK, QB), 1)
    win_mask = per_head((k_pos <= t_pos) & (k_pos > t_pos - WINDOW))
    jj = lax.broadcasted_iota(jnp.int32, (LANES, QB), 0)
    q_blk = (t0 + lax.broadcasted_iota(jnp.int32, (LANES, QB), 1)) // SLC_BLOCK
    valid = jj <= q_blk
    forced = (jj == 0) | (valid & (jj > q_blk - SLC_LOCAL))
    jf = jj.astype(F32)

    def window_and_compressed(nc):
        win_scores = [lax.dot_general(kwn_ref[0, pl.ds(ks, WK), :], qaug_ref[g, :, 0:LANES], nt_dims,
                                      preferred_element_type=F32) for g in range(G)]
        cmp_scores = [lax.dot_general(kc_ref[0, 0, 0:nc, :], qaug_ref[g, :, 0:LANES], nt_dims,
                                      preferred_element_type=F32) for g in range(G)]
        for g in range(G):
            sw = jnp.where(win_mask, win_scores[g], NEG)
            e_win = jnp.exp2(sw - jnp.max(sw, axis=0, keepdims=True)).astype(BF16)
            pv = jnp.dot(vwn_ref[0, g * LANES:(g + 1) * LANES, pl.ds(ks, WK)], e_win,
                         preferred_element_type=F32)
            out_ref[1, g * NSA_DK:(g + 1) * NSA_DK, :] = normalized(pv, g)

        n_idx = lax.broadcasted_iota(jnp.int32, (nc, QB), 0)
        t_col = t0 + lax.broadcasted_iota(jnp.int32, (nc, QB), 1)
        cmp_mask = per_head(n_idx * CMP_STRIDE + (CMP_BLOCK - 1) <= t_col)
        for g in range(G):
            sc = jnp.where(cmp_mask, cmp_scores[g], NEG)
            e = jnp.where(cmp_mask, jnp.exp2(sc - jnp.max(sc, axis=0, keepdims=True)), 0.0)
            p_cmp = e * (1.0 / jnp.maximum(jnp.sum(e, axis=0, keepdims=True), 1e-30))
            acc = jnp.dot(vc_ref[0, 0, :, 0:nc], p_cmp.astype(BF16), preferred_element_type=F32)
            out_ref[0, g * NSA_DK:(g + 1) * NSA_DK, :] = acc[g * NSA_DK:(g + 1) * NSA_DK]
            p_sum = p_cmp[:, 0:QB]
            for i in range(1, R):
                p_sum = p_sum + p_cmp[:, i * QB:(i + 1) * QB]
            p_hi, p_lo = _split_bf16(p_sum)
            imp = (jnp.dot(ov_ref[:, 0:nc], p_hi, preferred_element_type=F32)
                   + jnp.dot(ov_ref[:, 0:nc], p_lo, preferred_element_type=F32))
            score_ref[g] = jnp.where(forced, -jnp.inf, jnp.where(valid, imp, -SEL_BIG))

    n_visible = (t0 + QB - CMP_BLOCK) // CMP_STRIDE + 1
    n_variants = NC // LANES
    for v in range(n_variants):
        @pl.when(jnp.minimum((n_visible - 1) // LANES, n_variants - 1) == v)
        def _():
            window_and_compressed((v + 1) * LANES)

    def pick_round(_, carry):
        out = []
        for score in carry:
            best = jnp.max(score, axis=0, keepdims=True)
            first = jnp.min(jnp.where(score == best, jf, float(LANES)), axis=0, keepdims=True)
            out.append(jnp.where(jf == first, -jnp.inf, score))
        return tuple(out)

    n_rounds = jnp.where((t0 + QB - 1) // SLC_BLOCK < topk, 0, topk - (1 + SLC_LOCAL))
    taken = lax.fori_loop(0, n_rounds, pick_round, tuple(score_ref[g] for g in range(G)))
    for g in range(G):
        selected = ((taken[g] == -jnp.inf) | (q_blk < topk)) & valid
        bias = jnp.where(selected, 0.0, NEG).T.astype(BF16)
        for i in range(R):
            qaug_ref[g, i * QB:(i + 1) * QB, LANES:2 * LANES] = bias

    def slc_scores(k0, slot):
        for g in range(G):
            s_ref[slot, g] = lax.dot_general(ksl_ref[0, pl.ds(k0, KT), :], qaug_ref[g], nt_dims,
                                             preferred_element_type=F32)

    def slc_accumulate(k0, slot, causal):
        for g in range(G):
            s = s_ref[slot, g]
            if causal:
                k_pos = k0 + lax.broadcasted_iota(jnp.int32, (KT, QB), 0)
                t_pos = t0 + lax.broadcasted_iota(jnp.int32, (KT, QB), 1)
                s = jnp.where(per_head(k_pos <= t_pos), s, NEG)
            m_run = m_ref[g]
            m_new = jnp.maximum(m_run, jnp.max(s, axis=0, keepdims=True))
            p = jnp.exp2(s - m_new).astype(BF16)
            pv = jnp.dot(vsl_ref[0, g * LANES:(g + 1) * LANES, pl.ds(k0, KT)], p,
                         preferred_element_type=F32)
            acc_ref[g] = jnp.exp2(m_run - m_new) * acc_ref[g] + pv
            m_ref[g] = m_new

    m_ref[...] = jnp.full(m_ref.shape, NEG, F32)
    acc_ref[...] = jnp.zeros_like(acc_ref)
    n_full = t0 // KT
    U = SLC_UNROLL
    slc_scores(0, 0)

    def slc_run(k0, n_tiles, last_is_diagonal):
        for u in range(n_tiles):
            causal = last_is_diagonal and u == n_tiles - 1
            if not causal:
                slc_scores(k0 + (u + 1) * KT, (u + 1) % 2)
            slc_accumulate(k0 + u * KT, u % 2, causal)

    def slc_group(j, _):
        slc_run(pl.multiple_of(j * (U * KT), U * KT), U, False)
        return 0

    lax.fori_loop(0, n_full // U, slc_group, 0)
    k_rest = pl.multiple_of((n_full // U) * (U * KT), U * KT)
    for r in range(U):
        @pl.when(n_full % U == r)
        def _():
            slc_run(k_rest, r + 1, True)

    o_slc = [normalized(acc_ref[g], g) for g in range(G)]

    g_hi, g_lo = _split_bf16((1.0 / (1.0 + jnp.exp(-ng_ref[...]))).T)
    gates = (jnp.dot(ex_ref[...], g_hi, preferred_element_type=F32)
             + jnp.dot(ex_ref[...], g_lo, preferred_element_type=F32))
    branches = [out_ref[0], jnp.concatenate(o_slc, axis=0), out_ref[1]]
    for i in range(R):
        y = jnp.zeros((LANES, QB), F32)
        for b, o in enumerate(branches):
            y = y + gates[b * D_NSA + i * LANES:b * D_NSA + (i + 1) * LANES] * o[:, i * QB:(i + 1) * QB]
        o_ref[:, i * LANES:(i + 1) * LANES] = y.T.astype(BF16)


def _nsa(nq, ng, kvc_cmp, kvc_cmp_t, ksl, vsl, kwn, vwn, overlap, expand, batch, seq_len):
    T = nq.shape[0]
    QB = NSA_QBLOCK
    nqb = seq_len // QB
    nc = seq_len // CMP_STRIDE
    topk = min(SLC_TOPK, seq_len // SLC_BLOCK)
    tok = lambda n: pl.BlockSpec((QB, n), lambda b, i: (b * nqb + i, 0))
    seq = lambda n: pl.BlockSpec((1, seq_len, n), lambda b, i: (b, 0, 0))
    seq_t = pl.BlockSpec((1, 2 * LANES, seq_len), lambda b, i: (b, 0, 0))
    return pl.pallas_call(
        functools.partial(_nsa_kernel, seq_len=seq_len, topk=topk),
        grid=(batch, nqb),
        in_specs=[tok(D_NSA), tok(LANES),
                  pl.BlockSpec((1, 1, nc, LANES), lambda b, i: (0, b, 0, 0)),
                  pl.BlockSpec((1, 1, LANES, nc), lambda b, i: (1, b, 0, 0)),
                  seq(2 * LANES), seq_t, seq(LANES), seq_t,
                  pl.BlockSpec(overlap.shape, lambda b, i: (0, 0)),
                  pl.BlockSpec(expand.shape, lambda b, i: (0, 0))],
        out_specs=tok(D_NSA),
        out_shape=jax.ShapeDtypeStruct((T, D_NSA), BF16),
        scratch_shapes=[pltpu.VMEM((NSA_KV_HEADS, NSA_GROUP * QB, 2 * LANES), BF16),
                        pltpu.VMEM((2, NSA_KV_HEADS, SLC_KTILE, NSA_GROUP * QB), F32),
                        pltpu.VMEM((NSA_KV_HEADS, 1, NSA_GROUP * QB), F32),
                        pltpu.VMEM((NSA_KV_HEADS, LANES, NSA_GROUP * QB), F32),
                        pltpu.VMEM((2, LANES, NSA_GROUP * QB), F32),
                        pltpu.VMEM((NSA_KV_HEADS, LANES, QB), F32)],
        compiler_params=pltpu.CompilerParams(dimension_semantics=("arbitrary", "arbitrary"),
                                             vmem_limit_bytes=V7X_VMEM_LIMIT),
        name="nsa",
    )(nq, ng, kvc_cmp, kvc_cmp_t, ksl.reshape(batch, seq_len, -1), vsl,
      kwn.reshape(batch, seq_len, -1), vwn, overlap, expand)


def _outproj_kernel(yr_ref, yn_ref, x_ref, w_ref, g_ref, o_ref):
    mix = (jnp.dot(yr_ref[...], w_ref[0:D_RET, :], preferred_element_type=F32)
           + jnp.dot(yn_ref[...], w_ref[D_RET:D_RET + D_NSA, :], preferred_element_type=F32))
    n = mix * lax.rsqrt(jnp.mean(mix * mix, axis=-1, keepdims=True) + NORM_EPS) * g_ref[...]
    o_ref[...] = x_ref[...] + n


def _outproj(y_ret, y_nsa, x2, w, gain):
    T, D = x2.shape
    tm = TOK_TILE
    tok = lambda n: pl.BlockSpec((tm, n), lambda i: (i, 0))
    return pl.pallas_call(
        _outproj_kernel,
        grid=(T // tm,),
        in_specs=[tok(D_RET), tok(D_NSA), tok(D),
                  pl.BlockSpec(w.shape, lambda i: (0, 0)),
                  pl.BlockSpec((1, D), lambda i: (0, 0))],
        out_specs=tok(D),
        out_shape=jax.ShapeDtypeStruct((T, D), F32),
        compiler_params=pltpu.CompilerParams(dimension_semantics=("arbitrary",),
                                             vmem_limit_bytes=V7X_VMEM_LIMIT),
        name="outproj",
    )(y_ret, y_nsa, x2, w, gain)


def _ffn_kernel(x_ref, xp_ref, gpre_ref, wg_ref, wv_ref, cg_ref, cv_ref, wd_ref, gpost_ref, o_ref,
                h_ref, ug_ref, uv_ref, acc_ref, *, tiles_per_seq):
    tm = x_ref.shape[0]
    H = FFN_HALO
    f = pl.program_id(1)

    def normed(x):
        return x * lax.rsqrt(jnp.mean(x * x, axis=-1, keepdims=True) + NORM_EPS) * gpre_ref[...]

    @pl.when(f == 0)
    def _():
        keep = jnp.where(pl.program_id(0) % tiles_per_seq == 0, 0.0, 1.0)
        h_ref[0:H, :] = (normed(xp_ref[...]) * keep).astype(BF16)
        h_ref[H:H + tm, :] = normed(x_ref[...]).astype(BF16)
        acc_ref[...] = jnp.zeros_like(acc_ref)

    rc = tm // FFN_ROW_CHUNKS

    def up_project(c):
        rows = slice(0 if c == 0 else H + c * rc, H + (c + 1) * rc)
        h = h_ref[rows, :]
        ug_ref[rows, :] = jnp.dot(h, wg_ref[...], preferred_element_type=F32)
        uv_ref[rows, :] = jnp.dot(h, wv_ref[...], preferred_element_type=F32)

    def causal_conv(u_ref, c_ref, r0, scale):
        out = (scale * c_ref[CONV_WIDTH - 1:CONV_WIDTH, :]) * u_ref[r0:r0 + rc, :]
        for k in range(CONV_WIDTH - 1):
            d = CONV_WIDTH - 1 - k
            out = out + (scale * c_ref[k:k + 1, :]) * u_ref[r0 - d:r0 - d + rc, :]
        return out

    def gate_and_down(c):
        g = causal_conv(ug_ref, cg_ref, H + c * rc, 1.0)
        v_half = causal_conv(uv_ref, cv_ref, H + c * rc, 0.5)
        inner = g * (GELU_C + (GELU_C * 0.044715) * (g * g))
        act = (g + g * jnp.tanh(inner)) * v_half
        acc_ref[c * rc:(c + 1) * rc, :] += jnp.dot(act.astype(BF16), wd_ref[...],
                                                   preferred_element_type=F32)

    up_project(0)
    for c in range(FFN_ROW_CHUNKS):
        if c + 1 < FFN_ROW_CHUNKS:
            up_project(c + 1)
        gate_and_down(c)

    @pl.when(f == pl.num_programs(1) - 1)
    def _():
        y = acc_ref[...]
        n = y * lax.rsqrt(jnp.mean(y * y, axis=-1, keepdims=True) + NORM_EPS) * gpost_ref[...]
        o_ref[...] = x_ref[...] + n


def _ffn(x2, g_pre, w_up, conv_w, w_down, g_post, seq_len):
    T, D = x2.shape
    d_ff = w_down.shape[0]
    tm, tf, H = FFN_TILE, FFN_FTILE, FFN_HALO
    nf = d_ff // tf
    return pl.pallas_call(
        functools.partial(_ffn_kernel, tiles_per_seq=seq_len // tm),
        grid=(T // tm, nf),
        in_specs=[pl.BlockSpec((tm, D), lambda i, f: (i, 0)),
                  pl.BlockSpec((H, D), lambda i, f: (jnp.maximum(i * (tm // H) - 1, 0), 0)),
                  pl.BlockSpec((1, D), lambda i, f: (0, 0)),
                  pl.BlockSpec((D, tf), lambda i, f: (0, f)),
                  pl.BlockSpec((D, tf), lambda i, f: (0, nf + f)),
                  pl.BlockSpec((CONV_WIDTH, tf), lambda i, f: (0, f)),
                  pl.BlockSpec((CONV_WIDTH, tf), lambda i, f: (0, nf + f)),
                  pl.BlockSpec((tf, D), lambda i, f: (f, 0)),
                  pl.BlockSpec((1, D), lambda i, f: (0, 0))],
        out_specs=pl.BlockSpec((tm, D), lambda i, f: (i, 0)),
        out_shape=jax.ShapeDtypeStruct((T, D), F32),
        scratch_shapes=[pltpu.VMEM((tm + H, D), BF16), pltpu.VMEM((tm + H, tf), F32),
                        pltpu.VMEM((tm + H, tf), F32), pltpu.VMEM((tm, D), F32)],
        compiler_params=pltpu.CompilerParams(dimension_semantics=("arbitrary", "arbitrary"),
                                             vmem_limit_bytes=V7X_VMEM_LIMIT),
        name="ffn",
    )(x2, x2, g_pre, w_up, w_up, conv_w, conv_w, w_down, g_post)


_NSA_HEAD_ORDER = [g * NSA_GROUP + i for i in range(NSA_GROUP) for g in range(NSA_KV_HEADS)]


def _rope_tables(pos):
    inv = 1.0 / (ROPE_THETA ** (jnp.arange(0, NSA_DK, 2, dtype=F32) / NSA_DK))
    ang = pos.astype(F32)[:, None] * inv[None, :]
    c, s = jnp.cos(ang), jnp.sin(ang)
    return jnp.concatenate([c, c, c, c], axis=1), jnp.concatenate([-s, s, -s, s], axis=1)


def _prep_w_in(w):
    lead = w.shape[:-1]
    splits = np.cumsum([RET_QK, RET_QK, D_RET, D_RET, D_NSA] + [NSA_KV] * 6)
    rq, rk, rv, rg, nq, kcm, vcm, ksl, vsl, kwn, vwn, ng = jnp.split(w, [int(s) for s in splits], axis=-1)
    nq = nq.reshape(lead + (NSA_HEADS, NSA_DK))[..., np.array(_NSA_HEAD_ORDER), :].reshape(lead + (D_NSA,))
    ng = jnp.pad(ng, [(0, 0)] * len(lead) + [(0, LANES - NSA_GATES)])
    return jnp.concatenate([rq, rk, nq, ksl, kwn, rv, vsl, vwn, rg, kcm, vcm, ng], axis=-1).astype(BF16)


def _prep_w_out(w):
    layers, _, d = w.shape
    w_nsa = w[:, D_RET:].reshape(layers, NSA_HEADS, NSA_DK, d)[:, np.array(_NSA_HEAD_ORDER)]
    return jnp.concatenate([w[:, :D_RET], w_nsa.reshape(layers, D_NSA, d)], axis=1).astype(BF16)


def _prep_compress(pos, w1, w2):
    lead = w1.shape[:-2]
    half = CMP_STRIDE

    def block_diag(w, axis):
        z = jnp.zeros_like(w)
        return jnp.stack([jnp.concatenate([w, z], axis=-1), jnp.concatenate([z, w], axis=-1)], axis=axis)

    def first_layer(w_half):
        w4 = w_half.reshape(lead + (half, NSA_DK, CMP_HID))
        return block_diag(w4, -3).reshape(lead + (half * NSA_KV_HEADS * NSA_DK, -1)).astype(BF16)

    def pos_row(p_half):
        rows = jnp.broadcast_to(p_half[..., :, None, :], lead + (half, NSA_KV_HEADS, NSA_DK))
        return rows.reshape(lead + (1, -1))

    w2x = block_diag(w2, -3).reshape(lead + (NSA_KV_HEADS * CMP_HID, -1)).astype(BF16)
    n1 = half * NSA_DK
    return (pos_row(pos[..., :half, :]), pos_row(pos[..., half:, :]),
            first_layer(w1[..., :n1, :]), first_layer(w1[..., n1:, :]), w2x)


def _overlap_matrix(seq_len):
    nc = seq_len // CMP_STRIDE
    cmp_start = np.arange(nc) * CMP_STRIDE
    slc_start = np.arange(LANES) * SLC_BLOCK
    ov = ((cmp_start[:, None] < slc_start[None, :] + SLC_BLOCK)
          & (cmp_start[:, None] + CMP_BLOCK > slc_start[None, :]))
    n_cmp = (seq_len - CMP_BLOCK) // CMP_STRIDE + 1
    ov &= (np.arange(nc) < n_cmp)[:, None]
    return jnp.asarray(ov.T.astype(np.float32), dtype=BF16)


def _gate_expand_matrix():
    ex = np.zeros((LANES, 3 * D_NSA), np.float32)
    for branch in range(3):
        for p in range(D_NSA):
            head = _NSA_HEAD_ORDER[p // NSA_DK]
            ex[branch * NSA_HEADS + head, branch * D_NSA + p] = 1.0
    return jnp.asarray(ex.T, dtype=BF16)


def kernel(x, norm_mix_pre, w_in, ret_gn_w, cmp_k_pos, cmp_k_w1, cmp_k_w2, cmp_v_pos, cmp_v_w1, cmp_v_w2,
           w_out, norm_mix_post, norm_ffn_pre, ffn_w_up, ffn_conv, ffn_w_down, norm_ffn_post):
    B, S, D = x.shape
    depth = w_in.shape[0]
    assert S % SLC_KTILE == 0 and S % FFN_TILE == 0 and S // SLC_BLOCK <= LANES and S >= WINDOW + NSA_QBLOCK
    assert ffn_w_down.shape[1] % FFN_FTILE == 0

    cos, sin = _rope_tables(jnp.arange(S, dtype=jnp.int32))
    nc = S // CMP_STRIDE
    ccos, csin = _rope_tables(jnp.arange(nc, dtype=jnp.int32) * CMP_STRIDE + (CMP_BLOCK - 1))
    cmp_cos = jnp.stack([ccos, jnp.ones_like(ccos)])
    cmp_sin = jnp.stack([csin, jnp.zeros_like(csin)])
    overlap = _overlap_matrix(S)
    expand = _gate_expand_matrix()

    w_in_p = _prep_w_in(w_in)
    w_out_p = _prep_w_out(w_out)
    w_up_p = ffn_w_up.astype(BF16)
    w_down_p = ffn_w_down.astype(BF16)
    cmp_p = _prep_compress(jnp.stack([cmp_k_pos, cmp_v_pos], axis=1), jnp.stack([cmp_k_w1, cmp_v_w1], axis=1),
                           jnp.stack([cmp_k_w2, cmp_v_w2], axis=1))

    x2 = x.reshape(B * S, D)
    for l in range(depth):
        outs = _inproj(x2, norm_mix_pre[l][None], w_in_p[l], cos, sin, S)
        rq, rk, nq, ksl, kwn, rv, vsl, vwn, rg, kvc, ng = outs
        y_ret = _retention(rq, rk, rv, rg, ret_gn_w[l][None], B, S)
        kvc_cmp, kvc_cmp_t = _compress(kvc, *[a[l] for a in cmp_p], cmp_cos, cmp_sin, B, S)
        y_nsa = _nsa(nq, ng, kvc_cmp, kvc_cmp_t, ksl, vsl, kwn, vwn, overlap, expand, B, S)
        x2 = _outproj(y_ret, y_nsa, x2, w_out_p[l], norm_mix_post[l][None])
        x2 = _ffn(x2, norm_ffn_pre[l][None], w_up_p[l], ffn_conv[l], w_down_p[l], norm_ffn_post[l][None], S)
    return x2.reshape(B, S, D)
```

```python
import functools
import math

import jax
import jax.numpy as jnp
import numpy as np
from jax import lax
from jax.experimental import pallas as pl
from jax.experimental.pallas import tpu as pltpu

F32 = jnp.float32
BF16 = jnp.bfloat16

LANES = 128
SUBLANES = 8
V7X_VMEM_LIMIT = 56 * 1024 * 1024

ROPE_THETA = 10000.0
NORM_EPS = 1e-6
GN_EPS = 1e-5
NEG = -1e30
SEL_BIG = 1e9

RET_HEADS = 4
RET_DK = 64
RET_DV = 128
RET_CHUNK = 128
NSA_HEADS = 8
NSA_KV_HEADS = 2
NSA_DK = 64
NSA_GROUP = NSA_HEADS // NSA_KV_HEADS
CMP_BLOCK = 32
CMP_STRIDE = 16
CMP_HID = 256
SLC_BLOCK = 64
SLC_TOPK = 16
SLC_LOCAL = 2
WINDOW = 512
NSA_QBLOCK = 128
NSA_GATES = 3 * NSA_HEADS
CONV_WIDTH = 3

NSA_Q_SCALE = NSA_DK ** -0.5 * math.log2(math.e)

D_RET = RET_HEADS * RET_DV
D_NSA = NSA_HEADS * NSA_DK
RET_QK = RET_HEADS * RET_DK
NSA_KV = NSA_KV_HEADS * NSA_DK

COL_RQ = 0
COL_RK = COL_RQ + RET_QK
COL_NQ = COL_RK + RET_QK
COL_KSL = COL_NQ + D_NSA
COL_KWN = COL_KSL + NSA_KV
ROPE_COLS = COL_KWN + NSA_KV
COL_RV = ROPE_COLS
COL_VSL = COL_RV + D_RET
COL_VWN = COL_VSL + NSA_KV
COL_RG = COL_VWN + NSA_KV
COL_KCM = COL_RG + D_RET
COL_VCM = COL_KCM + NSA_KV
COL_NG = COL_VCM + NSA_KV
IN_COLS_PAD = COL_NG + LANES

TOK_TILE = 512
RET_TILE = 512
FFN_TILE = 1024
FFN_FTILE = 256
FFN_HALO = 16
FFN_ROW_CHUNKS = 4
SLC_KTILE = 512
SLC_UNROLL = 4


GELU_C = math.sqrt(2.0 / math.pi)


def _gelu_tanh(x):
    return 0.5 * x * (1.0 + jnp.tanh(GELU_C * (x + 0.044715 * (x * x * x))))


def _rope(p, cos, sin_signed, first_half):
    half = NSA_DK // 2
    partner = jnp.where(first_half, pltpu.roll(p, LANES - half, 1), pltpu.roll(p, half, 1))
    return p * cos + partner * sin_signed


def _split_bf16(x):
    hi = x.astype(BF16)
    lo = (x - hi.astype(F32)).astype(BF16)
    return hi, lo


def _inproj_kernel(x_ref, g_ref, w_ref, cos_ref, sin_ref,
                   rq_ref, rk_ref, nq_ref, ksl_ref, kwn_ref, rv_ref, vsl_ref, vwn_ref,
                   rg_ref, kvc_ref, ng_ref, stage_ref, *, seq_len):
    tm = x_ref.shape[0]
    x = x_ref[...]
    h = (x * lax.rsqrt(jnp.mean(x * x, axis=-1, keepdims=True) + NORM_EPS) * g_ref[...]).astype(BF16)
    cos = cos_ref[...]
    sin = sin_ref[...]
    lane = lax.broadcasted_iota(jnp.int32, (tm, LANES), 1)
    first_half = (lane % NSA_DK) < NSA_DK // 2

    def proj(c0, n):
        return jnp.dot(h, w_ref[:, c0:c0 + n], preferred_element_type=F32)

    def rope_slab(p, i):
        return _rope(p[:, i * LANES:(i + 1) * LANES], cos, sin, first_half)

    p = proj(COL_RQ, RET_QK)
    for i in range(RET_QK // LANES):
        rq_ref[:, i * LANES:(i + 1) * LANES] = rope_slab(p, i).astype(BF16)
    p = proj(COL_RK, RET_QK)
    for i in range(RET_QK // LANES):
        rk_ref[:, i * LANES:(i + 1) * LANES] = (rope_slab(p, i) * (RET_DK ** -0.5)).astype(BF16)
    p = proj(COL_NQ, D_NSA)
    for i in range(D_NSA // LANES):
        nq_ref[:, i * LANES:(i + 1) * LANES] = (rope_slab(p, i) * NSA_Q_SCALE).astype(BF16)
    p = proj(COL_KSL, 2 * NSA_KV)
    ksl_ref[:, 0:LANES] = rope_slab(p, 0).astype(BF16)
    kwn_ref[...] = rope_slab(p, 1).astype(BF16)
    row = lax.broadcasted_iota(jnp.int32, (tm, LANES), 0)
    pos = (pl.program_id(0) * tm + row) % seq_len
    ksl_ref[:, LANES:2 * LANES] = jnp.where(lane == pos // SLC_BLOCK, 1.0, 0.0).astype(BF16)

    rv_ref[...] = proj(COL_RV, D_RET).astype(BF16)
    p = proj(COL_VSL, 2 * NSA_KV)
    low_half = lane < NSA_DK
    for v_ref, v in ((vsl_ref, p[:, 0:LANES]), (vwn_ref, p[:, LANES:2 * LANES])):
        v_ref[0, 0:LANES, :] = jnp.where(low_half, v, 1.0).T.astype(BF16)
        v_ref[0, LANES:2 * LANES, :] = jnp.where(low_half, 1.0, v).T.astype(BF16)
    rg_ref[...] = proj(COL_RG, D_RET)
    p = proj(COL_KCM, 2 * NSA_KV)
    for s in range(2):
        stage_ref[...] = p[:, s * LANES:(s + 1) * LANES]
        for l in range(CMP_STRIDE):
            kvc_ref[s, :, l * LANES:(l + 1) * LANES] = stage_ref[pl.ds(l, tm // CMP_STRIDE, stride=CMP_STRIDE), :]
    ng_ref[...] = proj(COL_NG, LANES)


def _inproj(x2, gain, w, layer, cos, sin, seq_len):
    T, D = x2.shape
    tm = TOK_TILE
    nt = seq_len // tm
    tok = lambda n: pl.BlockSpec((tm, n), lambda i: (i, 0))
    tok_t = pl.BlockSpec((1, 2 * LANES, tm), lambda i: (i // nt, 0, i % nt))
    out_shape = (
        jax.ShapeDtypeStruct((T, RET_QK), BF16),
        jax.ShapeDtypeStruct((T, RET_QK), BF16),
        jax.ShapeDtypeStruct((T, D_NSA), BF16),
        jax.ShapeDtypeStruct((T, 2 * LANES), BF16),
        jax.ShapeDtypeStruct((T, LANES), BF16),
        jax.ShapeDtypeStruct((T, D_RET), BF16),
        jax.ShapeDtypeStruct((T // seq_len, 2 * LANES, seq_len), BF16),
        jax.ShapeDtypeStruct((T // seq_len, 2 * LANES, seq_len), BF16),
        jax.ShapeDtypeStruct((T, D_RET), F32),
        jax.ShapeDtypeStruct((2, T // CMP_STRIDE, CMP_STRIDE * LANES), F32),
        jax.ShapeDtypeStruct((T, LANES), F32),
    )
    out_specs = (tok(RET_QK), tok(RET_QK), tok(D_NSA), tok(2 * LANES), tok(LANES), tok(D_RET),
                 tok_t, tok_t, tok(D_RET),
                 pl.BlockSpec((2, tm // CMP_STRIDE, CMP_STRIDE * LANES), lambda i: (0, i, 0)), tok(LANES))
    return pl.pallas_call(
        functools.partial(_inproj_kernel, seq_len=seq_len),
        grid=(T // tm,),
        in_specs=[tok(D),
                  pl.BlockSpec((1, D), lambda i: (0, 0)),
                  pl.BlockSpec((None, D, IN_COLS_PAD), lambda i: (layer, 0, 0)),
                  pl.BlockSpec((tm, LANES), lambda i: (i % nt, 0)),
                  pl.BlockSpec((tm, LANES), lambda i: (i % nt, 0))],
        out_specs=out_specs,
        out_shape=out_shape,
        scratch_shapes=[pltpu.VMEM((tm, LANES), F32)],
        compiler_params=pltpu.CompilerParams(dimension_semantics=("arbitrary",),
                                             vmem_limit_bytes=V7X_VMEM_LIMIT),
        name="inproj",
    )(x2, gain, w, cos, sin)


def _retention_kernel(q_ref, k_ref, v_ref, g_ref, gnw_ref, o_ref, state_ref, vbd_ref):
    C = RET_CHUNK
    n_chunks = q_ref.shape[0] // C

    @pl.when(pl.program_id(1) == 0)
    def _():
        state_ref[...] = jnp.zeros_like(state_ref)

    vbd_ref[...] = jnp.zeros_like(vbd_ref)

    ii = lax.broadcasted_iota(jnp.int32, (C, C), 0)
    jj = lax.broadcasted_iota(jnp.int32, (C, C), 1)
    diff = (ii - jj).astype(F32)
    i_col = lax.broadcasted_iota(jnp.int32, (C, 1), 0).astype(F32)
    low_half = lax.broadcasted_iota(jnp.int32, (C, LANES), 1) < RET_DK
    low_cols = lax.broadcasted_iota(jnp.int32, (C, 2 * RET_DV), 1) < RET_DV
    own_block = ((lax.broadcasted_iota(jnp.int32, (LANES, 2 * RET_DV), 0) < RET_DK)
                 == (lax.broadcasted_iota(jnp.int32, (LANES, 2 * RET_DV), 1) < RET_DV))
    nt_dims = (((1,), (1,)), ((), ()))
    tn_dims = (((0,), (0,)), ((), ()))

    for pair in range(RET_HEADS // 2):
        lg0, lg1 = (math.log(1.0 - 2.0 ** (-5.0 - h)) for h in (2 * pair, 2 * pair + 1))
        decay = [jnp.where(diff >= 0, jnp.exp(lg * jnp.maximum(diff, 0.0)), 0.0) for lg in (lg0, lg1)]
        xi = jnp.where(low_cols, jnp.exp(lg0 * (i_col + 1.0)), jnp.exp(lg1 * (i_col + 1.0)))
        zeta = jnp.where(low_half, jnp.exp(lg0 * (C - 1.0 - i_col)), jnp.exp(lg1 * (C - 1.0 - i_col)))
        chunk_decay = jnp.where(low_cols[0:1], math.exp(lg0 * C), math.exp(lg1 * C))
        qk_cols = slice(pair * LANES, (pair + 1) * LANES)
        v_cols = slice(2 * pair * RET_DV, 2 * (pair + 1) * RET_DV)
        for c in range(n_chunks):
            rows = slice(c * C, (c + 1) * C)
            q = q_ref[rows, qk_cols]
            ks = k_ref[rows, qk_cols]
            v = v_ref[rows, v_cols]
            qf = q.astype(F32)
            q_stack = jnp.concatenate([jnp.where(low_half, qf, 0.0), jnp.where(low_half, 0.0, qf)],
                                      axis=0).astype(BF16)
            s = lax.dot_general(q_stack, ks, nt_dims, preferred_element_type=F32)
            s_pair = jnp.concatenate([s[0:C] * decay[0], s[C:2 * C] * decay[1]], axis=1).astype(BF16)
            vbd_ref[pair, 0:C, 0:RET_DV] = v[:, 0:RET_DV]
            vbd_ref[pair, C:2 * C, RET_DV:2 * RET_DV] = v[:, RET_DV:2 * RET_DV]
            o = jnp.dot(s_pair, vbd_ref[pair], preferred_element_type=F32)
            state = state_ref[pair]
            o = o + jnp.dot(q, state.astype(BF16), preferred_element_type=F32) * xi
            kz = (ks.astype(F32) * zeta).astype(BF16)
            kv = lax.dot_general(kz, v, tn_dims, preferred_element_type=F32)
            state_ref[pair] = state * chunk_decay + jnp.where(own_block, kv, 0.0)
            for e in range(2):
                cols = slice((2 * pair + e) * RET_DV, (2 * pair + e + 1) * RET_DV)
                oh = o[:, e * RET_DV:(e + 1) * RET_DV]
                mu = jnp.mean(oh, axis=-1, keepdims=True)
                var = jnp.mean(jnp.square(oh - mu), axis=-1, keepdims=True)
                on = (oh - mu) * lax.rsqrt(var + GN_EPS) * gnw_ref[:, cols]
                gate = g_ref[rows, cols]
                o_ref[rows, cols] = (gate * (1.0 / (1.0 + jnp.exp(-gate))) * on).astype(BF16)


def _retention(rq, rk, rv, rg, gn_w, batch, seq_len):
    T = rq.shape[0]
    tc = RET_TILE
    nt = seq_len // tc
    tok = lambda n: pl.BlockSpec((tc, n), lambda b, i: (b * nt + i, 0))
    return pl.pallas_call(
        _retention_kernel,
        grid=(batch, nt),
        in_specs=[tok(RET_QK), tok(RET_QK), tok(D_RET), tok(D_RET),
                  pl.BlockSpec((1, D_RET), lambda b, i: (0, 0))],
        out_specs=tok(D_RET),
        out_shape=jax.ShapeDtypeStruct((T, D_RET), BF16),
        scratch_shapes=[pltpu.VMEM((RET_HEADS // 2, LANES, 2 * RET_DV), F32),
                        pltpu.VMEM((RET_HEADS // 2, 2 * RET_CHUNK, 2 * RET_DV), BF16)],
        compiler_params=pltpu.CompilerParams(dimension_semantics=("arbitrary", "arbitrary"),
                                             vmem_limit_bytes=V7X_VMEM_LIMIT),
        name="retention",
    )(rq, rk, rv, rg, gn_w)


def _compress_kernel(x_ref, pa_ref, pb_ref, wa_ref, wb_ref, w2_ref, cos_ref, sin_ref, o_ref, ot_ref):
    ng = x_ref.shape[2]
    x = x_ref[0, 0]
    xa = (x + pa_ref[0]).astype(BF16)
    xb = (x + pb_ref[0]).astype(BF16)
    a = jnp.dot(xa, wa_ref[0], preferred_element_type=F32)
    b = jnp.dot(xb, wb_ref[0], preferred_element_type=F32)
    hid = a + pltpu.roll(b, ng - 1, 0)
    out = jnp.dot(_gelu_tanh(hid).astype(BF16), w2_ref[0], preferred_element_type=F32)
    lane = lax.broadcasted_iota(jnp.int32, out.shape, 1)
    out = _rope(out, cos_ref[0], sin_ref[0], (lane % NSA_DK) < NSA_DK // 2)
    o_ref[0, 0] = out.astype(BF16)
    ot_ref[0, 0] = out.T.astype(BF16)


def _compress(kvc, pos_a, pos_b, wa, wb, w2, layer, cos, sin, batch, seq_len):
    ng = seq_len // CMP_STRIDE
    gw = CMP_STRIDE * LANES
    x = kvc.reshape(2, batch, ng, gw)
    hid = NSA_KV_HEADS * CMP_HID
    per_kv = lambda *shape: pl.BlockSpec((1,) + shape, lambda s, b: (s,) + (0,) * len(shape))
    per_lkv = lambda *shape: pl.BlockSpec((None, 1) + shape, lambda s, b: (layer, s) + (0,) * len(shape))
    return pl.pallas_call(
        _compress_kernel,
        grid=(2, batch),
        in_specs=[pl.BlockSpec((1, 1, ng, gw), lambda s, b: (s, b, 0, 0)),
                  per_lkv(1, gw), per_lkv(1, gw), per_lkv(gw, hid), per_lkv(gw, hid), per_lkv(hid, LANES),
                  per_kv(ng, LANES), per_kv(ng, LANES)],
        out_specs=(pl.BlockSpec((1, 1, ng, LANES), lambda s, b: (s, b, 0, 0)),
                   pl.BlockSpec((1, 1, LANES, ng), lambda s, b: (s, b, 0, 0))),
        out_shape=(jax.ShapeDtypeStruct((2, batch, ng, LANES), BF16),
                   jax.ShapeDtypeStruct((2, batch, LANES, ng), BF16)),
        compiler_params=pltpu.CompilerParams(dimension_semantics=("arbitrary", "arbitrary"),
                                             vmem_limit_bytes=V7X_VMEM_LIMIT),
        name="compress",
    )(x, pos_a, pos_b, wa, wb, w2, cos, sin)


def _nsa_kernel(nq_ref, ng_ref, kc_ref, vc_ref, ksl_ref, vsl_ref, kwn_ref, vwn_ref, ov_ref, ex_ref,
                o_ref, qaug_ref, s_ref, m_ref, acc_ref, out_ref, score_ref, *, seq_len, topk):
    QB = NSA_QBLOCK
    R = NSA_GROUP
    G = NSA_KV_HEADS
    M = R * QB
    NC = kc_ref.shape[2]
    KT = SLC_KTILE
    WK = WINDOW + QB
    t0 = pl.program_id(1) * QB
    nt_dims = (((1,), (1,)), ((), ()))

    low_half = lax.broadcasted_iota(jnp.int32, (QB, LANES), 1) < NSA_DK
    q = nq_ref[...].astype(F32)

    def per_head(x):
        return jnp.concatenate([x] * R, axis=1)

    def normalized(acc, g):
        num, den = (acc[0:NSA_DK], acc[NSA_DK:NSA_DK + 1]) if g == 0 else (acc[NSA_DK:], acc[0:1])
        return num * (1.0 / den)

    for g in range(G):
        head_lanes = low_half if g == 0 else jnp.logical_not(low_half)
        for i in range(R):
            qaug_ref[g, i * QB:(i + 1) * QB, 0:LANES] = jnp.where(
                head_lanes, q[:, i * LANES:(i + 1) * LANES], 0.0).astype(BF16)

    ks = pl.multiple_of(jnp.clip(t0 - WINDOW, 0, seq_len - WK), QB)
    k_pos = ks + lax.broadcasted_iota(jnp.int32, (WK, QB), 0)
    t_pos = t0 + lax.broadcasted_iota(jnp.int32, (WK, QB), 1)
    win_mask = per_head((k_pos <= t_pos) & (k_pos > t_pos - WINDOW))
    jj = lax.broadcasted_iota(jnp.int32, (LANES, QB), 0)
    q_blk = (t0 + lax.broadcasted_iota(jnp.int32, (LANES, QB), 1)) // SLC_BLOCK
    valid = jj <= q_blk
    forced = (jj == 0) | (valid & (jj > q_blk - SLC_LOCAL))
    jf = jj.astype(F32)

    def window_and_compressed(nc):
        win_scores = [lax.dot_general(kwn_ref[0, pl.ds(ks, WK), :], qaug_ref[g, :, 0:LANES], nt_dims,
                                      preferred_element_type=F32) for g in range(G)]
        cmp_scores = [lax.dot_general(kc_ref[0, 0, 0:nc, :], qaug_ref[g, :, 0:LANES], nt_dims,
                                      preferred_element_type=F32) for g in range(G)]
        for g in range(G):
            sw = jnp.where(win_mask, win_scores[g], NEG)
            e_win = jnp.exp2(sw - jnp.max(sw, axis=0, keepdims=True)).astype(BF16)
            pv = jnp.dot(vwn_ref[0, g * LANES:(g + 1) * LANES, pl.ds(ks, WK)], e_win,
                         preferred_element_type=F32)
            out_ref[1, g * NSA_DK:(g + 1) * NSA_DK, :] = normalized(pv, g)

        n_idx = lax.broadcasted_iota(jnp.int32, (nc, QB), 0)
        t_col = t0 + lax.broadcasted_iota(jnp.int32, (nc, QB), 1)
        cmp_mask = per_head(n_idx * CMP_STRIDE + (CMP_BLOCK - 1) <= t_col)
        for g in range(G):
            sc = jnp.where(cmp_mask, cmp_scores[g], NEG)
            e = jnp.where(cmp_mask, jnp.exp2(sc - jnp.max(sc, axis=0, keepdims=True)), 0.0)
            p_cmp = e * (1.0 / jnp.maximum(jnp.sum(e, axis=0, keepdims=True), 1e-30))
            acc = jnp.dot(vc_ref[0, 0, :, 0:nc], p_cmp.astype(BF16), preferred_element_type=F32)
            out_ref[0, g * NSA_DK:(g + 1) * NSA_DK, :] = acc[g * NSA_DK:(g + 1) * NSA_DK]
            p_sum = p_cmp[:, 0:QB]
            for i in range(1, R):
                p_sum = p_sum + p_cmp[:, i * QB:(i + 1) * QB]
            p_hi, p_lo = _split_bf16(p_sum)
            imp = (jnp.dot(ov_ref[:, 0:nc], p_hi, preferred_element_type=F32)
                   + jnp.dot(ov_ref[:, 0:nc], p_lo, preferred_element_type=F32))
            score_ref[g] = jnp.where(forced, -jnp.inf, jnp.where(valid, imp, -SEL_BIG))

    n_visible = (t0 + QB - CMP_BLOCK) // CMP_STRIDE + 1
    n_variants = NC // LANES
    for v in range(n_variants):
        @pl.when(jnp.minimum((n_visible - 1) // LANES, n_variants - 1) == v)
        def _():
            window_and_compressed((v + 1) * LANES)

    def pick_round(_, carry):
        out = []
        for score in carry:
            best = jnp.max(score, axis=0, keepdims=True)
            first = jnp.min(jnp.where(score == best, jf, float(LANES)), axis=0, keepdims=True)
            out.append(jnp.where(jf == first, -jnp.inf, score))
        return tuple(out)

    n_rounds = jnp.where((t0 + QB - 1) // SLC_BLOCK < topk, 0, topk - (1 + SLC_LOCAL))
    taken = lax.fori_loop(0, n_rounds, pick_round, tuple(score_ref[g] for g in range(G)))
    for g in range(G):
        selected = ((taken[g] == -jnp.inf) | (q_blk < topk)) & valid
        bias = jnp.where(selected, 0.0, NEG).T.astype(BF16)
        for i in range(R):
            qaug_ref[g, i * QB:(i + 1) * QB, LANES:2 * LANES] = bias

    def slc_scores(k0, slot):
        for g in range(G):
            s_ref[slot, g] = lax.dot_general(ksl_ref[0, pl.ds(k0, KT), :], qaug_ref[g], nt_dims,
                                             preferred_element_type=F32)

    def slc_accumulate(k0, slot, causal):
        for g in range(G):
            s = s_ref[slot, g]
            if causal:
                k_pos = k0 + lax.broadcasted_iota(jnp.int32, (KT, QB), 0)
                t_pos = t0 + lax.broadcasted_iota(jnp.int32, (KT, QB), 1)
                s = jnp.where(per_head(k_pos <= t_pos), s, NEG)
            m_run = m_ref[g]
            m_new = jnp.maximum(m_run, jnp.max(s, axis=0, keepdims=True))
            p = jnp.exp2(s - m_new).astype(BF16)
            pv = jnp.dot(vsl_ref[0, g * LANES:(g + 1) * LANES, pl.ds(k0, KT)], p,
                         preferred_element_type=F32)
            acc_ref[g] = jnp.exp2(m_run - m_new) * acc_ref[g] + pv
            m_ref[g] = m_new

    m_ref[...] = jnp.full(m_ref.shape, NEG, F32)
    acc_ref[...] = jnp.zeros_like(acc_ref)
    n_full = t0 // KT
    U = SLC_UNROLL
    slc_scores(0, 0)

    def slc_run(k0, n_tiles, last_is_diagonal):
        for u in range(n_tiles):
            causal = last_is_diagonal and u == n_tiles - 1
            if not causal:
                slc_scores(k0 + (u + 1) * KT, (u + 1) % 2)
            slc_accumulate(k0 + u * KT, u % 2, causal)

    def slc_group(j, _):
        slc_run(pl.multiple_of(j * (U * KT), U * KT), U, False)
        return 0

    lax.fori_loop(0, n_full // U, slc_group, 0)
    k_rest = pl.multiple_of((n_full // U) * (U * KT), U * KT)
    for r in range(U):
        @pl.when(n_full % U == r)
        def _():
            slc_run(k_rest, r + 1, True)

    o_slc = [normalized(acc_ref[g], g) for g in range(G)]

    g_hi, g_lo = _split_bf16((1.0 / (1.0 + jnp.exp(-ng_ref[...]))).T)
    gates = (jnp.dot(ex_ref[...], g_hi, preferred_element_type=F32)
             + jnp.dot(ex_ref[...], g_lo, preferred_element_type=F32))
    branches = [out_ref[0], jnp.concatenate(o_slc, axis=0), out_ref[1]]
    for i in range(R):
        y = jnp.zeros((LANES, QB), F32)
        for b, o in enumerate(branches):
            y = y + gates[b * D_NSA + i * LANES:b * D_NSA + (i + 1) * LANES] * o[:, i * QB:(i + 1) * QB]
        o_ref[:, i * LANES:(i + 1) * LANES] = y.T.astype(BF16)


def _nsa(nq, ng, kvc_cmp, kvc_cmp_t, ksl, vsl, kwn, vwn, overlap, expand, batch, seq_len):
    T = nq.shape[0]
    QB = NSA_QBLOCK
    nqb = seq_len // QB
    nc = seq_len // CMP_STRIDE
    topk = min(SLC_TOPK, seq_len // SLC_BLOCK)
    tok = lambda n: pl.BlockSpec((QB, n), lambda b, i: (b * nqb + i, 0))
    seq = lambda n: pl.BlockSpec((1, seq_len, n), lambda b, i: (b, 0, 0))
    seq_t = pl.BlockSpec((1, 2 * LANES, seq_len), lambda b, i: (b, 0, 0))
    return pl.pallas_call(
        functools.partial(_nsa_kernel, seq_len=seq_len, topk=topk),
        grid=(batch, nqb),
        in_specs=[tok(D_NSA), tok(LANES),
                  pl.BlockSpec((1, 1, nc, LANES), lambda b, i: (0, b, 0, 0)),
                  pl.BlockSpec((1, 1, LANES, nc), lambda b, i: (1, b, 0, 0)),
                  seq(2 * LANES), seq_t, seq(LANES), seq_t,
                  pl.BlockSpec(overlap.shape, lambda b, i: (0, 0)),
                  pl.BlockSpec(expand.shape, lambda b, i: (0, 0))],
        out_specs=tok(D_NSA),
        out_shape=jax.ShapeDtypeStruct((T, D_NSA), BF16),
        scratch_shapes=[pltpu.VMEM((NSA_KV_HEADS, NSA_GROUP * QB, 2 * LANES), BF16),
                        pltpu.VMEM((2, NSA_KV_HEADS, SLC_KTILE, NSA_GROUP * QB), F32),
                        pltpu.VMEM((NSA_KV_HEADS, 1, NSA_GROUP * QB), F32),
                        pltpu.VMEM((NSA_KV_HEADS, LANES, NSA_GROUP * QB), F32),
                        pltpu.VMEM((2, LANES, NSA_GROUP * QB), F32),
                        pltpu.VMEM((NSA_KV_HEADS, LANES, QB), F32)],
        compiler_params=pltpu.CompilerParams(dimension_semantics=("arbitrary", "arbitrary"),
                                             vmem_limit_bytes=V7X_VMEM_LIMIT),
        name="nsa",
    )(nq, ng, kvc_cmp, kvc_cmp_t, ksl.reshape(batch, seq_len, -1), vsl,
      kwn.reshape(batch, seq_len, -1), vwn, overlap, expand)


def _outproj_kernel(yr_ref, yn_ref, x_ref, w_ref, g_ref, o_ref):
    mix = (jnp.dot(yr_ref[...], w_ref[0:D_RET, :], preferred_element_type=F32)
           + jnp.dot(yn_ref[...], w_ref[D_RET:D_RET + D_NSA, :], preferred_element_type=F32))
    n = mix * lax.rsqrt(jnp.mean(mix * mix, axis=-1, keepdims=True) + NORM_EPS) * g_ref[...]
    o_ref[...] = x_ref[...] + n


def _outproj(y_ret, y_nsa, x2, w, layer, gain):
    T, D = x2.shape
    tm = TOK_TILE
    tok = lambda n: pl.BlockSpec((tm, n), lambda i: (i, 0))
    return pl.pallas_call(
        _outproj_kernel,
        grid=(T // tm,),
        in_specs=[tok(D_RET), tok(D_NSA), tok(D),
                  pl.BlockSpec((None,) + w.shape[1:], lambda i: (layer, 0, 0)),
                  pl.BlockSpec((1, D), lambda i: (0, 0))],
        out_specs=tok(D),
        out_shape=jax.ShapeDtypeStruct((T, D), F32),
        compiler_params=pltpu.CompilerParams(dimension_semantics=("arbitrary",),
                                             vmem_limit_bytes=V7X_VMEM_LIMIT),
        name="outproj",
    )(y_ret, y_nsa, x2, w, gain)


def _ffn_kernel(x_ref, xp_ref, gpre_ref, wg_ref, wv_ref, cg_ref, cv_ref, wd_ref, gpost_ref, o_ref,
                h_ref, ug_ref, uv_ref, acc_ref, *, tiles_per_seq):
    tm = x_ref.shape[0]
    H = FFN_HALO
    f = pl.program_id(1)

    def normed(x):
        return x * lax.rsqrt(jnp.mean(x * x, axis=-1, keepdims=True) + NORM_EPS) * gpre_ref[...]

    @pl.when(f == 0)
    def _():
        keep = jnp.where(pl.program_id(0) % tiles_per_seq == 0, 0.0, 1.0)
        h_ref[0:H, :] = (normed(xp_ref[...]) * keep).astype(BF16)
        h_ref[H:H + tm, :] = normed(x_ref[...]).astype(BF16)
        acc_ref[...] = jnp.zeros_like(acc_ref)

    rc = tm // FFN_ROW_CHUNKS

    def up_project(c):
        rows = slice(0 if c == 0 else H + c * rc, H + (c + 1) * rc)
        h = h_ref[rows, :]
        ug_ref[rows, :] = jnp.dot(h, wg_ref[...], preferred_element_type=F32)
        uv_ref[rows, :] = jnp.dot(h, wv_ref[...], preferred_element_type=F32)

    def causal_conv(u_ref, c_ref, r0, scale):
        out = (scale * c_ref[CONV_WIDTH - 1:CONV_WIDTH, :]) * u_ref[r0:r0 + rc, :]
        for k in range(CONV_WIDTH - 1):
            d = CONV_WIDTH - 1 - k
            out = out + (scale * c_ref[k:k + 1, :]) * u_ref[r0 - d:r0 - d + rc, :]
        return out

    def gate_and_down(c):
        g = causal_conv(ug_ref, cg_ref, H + c * rc, 1.0)
        v_half = causal_conv(uv_ref, cv_ref, H + c * rc, 0.5)
        inner = g * (GELU_C + (GELU_C * 0.044715) * (g * g))
        act = (g + g * jnp.tanh(inner)) * v_half
        acc_ref[c * rc:(c + 1) * rc, :] += jnp.dot(act.astype(BF16), wd_ref[...],
                                                   preferred_element_type=F32)

    up_project(0)
    for c in range(FFN_ROW_CHUNKS):
        if c + 1 < FFN_ROW_CHUNKS:
            up_project(c + 1)
        gate_and_down(c)

    @pl.when(f == pl.num_programs(1) - 1)
    def _():
        y = acc_ref[...]
        n = y * lax.rsqrt(jnp.mean(y * y, axis=-1, keepdims=True) + NORM_EPS) * gpost_ref[...]
        o_ref[...] = x_ref[...] + n


def _ffn(x2, g_pre, w_up, conv_w, w_down, layer, g_post, seq_len):
    T, D = x2.shape
    d_ff = w_down.shape[1]
    tm, tf, H = FFN_TILE, FFN_FTILE, FFN_HALO
    nf = d_ff // tf
    return pl.pallas_call(
        functools.partial(_ffn_kernel, tiles_per_seq=seq_len // tm),
        grid=(T // tm, nf),
        in_specs=[pl.BlockSpec((tm, D), lambda i, f: (i, 0)),
                  pl.BlockSpec((H, D), lambda i, f: (jnp.maximum(i * (tm // H) - 1, 0), 0)),
                  pl.BlockSpec((1, D), lambda i, f: (0, 0)),
                  pl.BlockSpec((None, D, tf), lambda i, f: (layer, 0, f)),
                  pl.BlockSpec((None, D, tf), lambda i, f: (layer, 0, nf + f)),
                  pl.BlockSpec((None, CONV_WIDTH, tf), lambda i, f: (layer, 0, f)),
                  pl.BlockSpec((None, CONV_WIDTH, tf), lambda i, f: (layer, 0, nf + f)),
                  pl.BlockSpec((None, tf, D), lambda i, f: (layer, f, 0)),
                  pl.BlockSpec((1, D), lambda i, f: (0, 0))],
        out_specs=pl.BlockSpec((tm, D), lambda i, f: (i, 0)),
        out_shape=jax.ShapeDtypeStruct((T, D), F32),
        scratch_shapes=[pltpu.VMEM((tm + H, D), BF16), pltpu.VMEM((tm + H, tf), F32),
                        pltpu.VMEM((tm + H, tf), F32), pltpu.VMEM((tm, D), F32)],
        compiler_params=pltpu.CompilerParams(dimension_semantics=("arbitrary", "arbitrary"),
                                             vmem_limit_bytes=V7X_VMEM_LIMIT),
        name="ffn",
    )(x2, x2, g_pre, w_up, w_up, conv_w, conv_w, w_down, g_post)


_NSA_HEAD_ORDER = [g * NSA_GROUP + i for i in range(NSA_GROUP) for g in range(NSA_KV_HEADS)]


def _rope_tables(pos):
    inv = 1.0 / (ROPE_THETA ** (jnp.arange(0, NSA_DK, 2, dtype=F32) / NSA_DK))
    ang = pos.astype(F32)[:, None] * inv[None, :]
    c, s = jnp.cos(ang), jnp.sin(ang)
    return jnp.concatenate([c, c, c, c], axis=1), jnp.concatenate([-s, s, -s, s], axis=1)


def _prep_w_in(w):
    lead = w.shape[:-1]
    splits = np.cumsum([RET_QK, RET_QK, D_RET, D_RET, D_NSA] + [NSA_KV] * 6)
    rq, rk, rv, rg, nq, kcm, vcm, ksl, vsl, kwn, vwn, ng = jnp.split(w, [int(s) for s in splits], axis=-1)
    nq = nq.reshape(lead + (NSA_HEADS, NSA_DK))[..., np.array(_NSA_HEAD_ORDER), :].reshape(lead + (D_NSA,))
    ng = jnp.pad(ng, [(0, 0)] * len(lead) + [(0, LANES - NSA_GATES)])
    return jnp.concatenate([rq, rk, nq, ksl, kwn, rv, vsl, vwn, rg, kcm, vcm, ng], axis=-1).astype(BF16)


def _prep_w_out(w):
    layers, _, d = w.shape
    w_nsa = w[:, D_RET:].reshape(layers, NSA_HEADS, NSA_DK, d)[:, np.array(_NSA_HEAD_ORDER)]
    return jnp.concatenate([w[:, :D_RET], w_nsa.reshape(layers, D_NSA, d)], axis=1).astype(BF16)


def _prep_compress(pos, w1, w2):
    lead = w1.shape[:-2]
    half = CMP_STRIDE

    def block_diag(w, axis):
        z = jnp.zeros_like(w)
        return jnp.stack([jnp.concatenate([w, z], axis=-1), jnp.concatenate([z, w], axis=-1)], axis=axis)

    def first_layer(w_half):
        w4 = w_half.reshape(lead + (half, NSA_DK, CMP_HID))
        return block_diag(w4, -3).reshape(lead + (half * NSA_KV_HEADS * NSA_DK, -1)).astype(BF16)

    def pos_row(p_half):
        rows = jnp.broadcast_to(p_half[..., :, None, :], lead + (half, NSA_KV_HEADS, NSA_DK))
        return rows.reshape(lead + (1, -1))

    w2x = block_diag(w2, -3).reshape(lead + (NSA_KV_HEADS * CMP_HID, -1)).astype(BF16)
    n1 = half * NSA_DK
    return (pos_row(pos[..., :half, :]), pos_row(pos[..., half:, :]),
            first_layer(w1[..., :n1, :]), first_layer(w1[..., n1:, :]), w2x)


def _overlap_matrix(seq_len):
    nc = seq_len // CMP_STRIDE
    cmp_start = np.arange(nc) * CMP_STRIDE
    slc_start = np.arange(LANES) * SLC_BLOCK
    ov = ((cmp_start[:, None] < slc_start[None, :] + SLC_BLOCK)
          & (cmp_start[:, None] + CMP_BLOCK > slc_start[None, :]))
    n_cmp = (seq_len - CMP_BLOCK) // CMP_STRIDE + 1
    ov &= (np.arange(nc) < n_cmp)[:, None]
    return jnp.asarray(ov.T.astype(np.float32), dtype=BF16)


def _gate_expand_matrix():
    ex = np.zeros((LANES, 3 * D_NSA), np.float32)
    for branch in range(3):
        for p in range(D_NSA):
            head = _NSA_HEAD_ORDER[p // NSA_DK]
            ex[branch * NSA_HEADS + head, branch * D_NSA + p] = 1.0
    return jnp.asarray(ex.T, dtype=BF16)


def kernel(x, norm_mix_pre, w_in, ret_gn_w, cmp_k_pos, cmp_k_w1, cmp_k_w2, cmp_v_pos, cmp_v_w1, cmp_v_w2,
           w_out, norm_mix_post, norm_ffn_pre, ffn_w_up, ffn_conv, ffn_w_down, norm_ffn_post):
    B, S, D = x.shape
    depth = w_in.shape[0]
    assert S % SLC_KTILE == 0 and S % FFN_TILE == 0 and S // SLC_BLOCK <= LANES and S >= WINDOW + NSA_QBLOCK
    assert ffn_w_down.shape[1] % FFN_FTILE == 0

    cos, sin = _rope_tables(jnp.arange(S, dtype=jnp.int32))
    nc = S // CMP_STRIDE
    ccos, csin = _rope_tables(jnp.arange(nc, dtype=jnp.int32) * CMP_STRIDE + (CMP_BLOCK - 1))
    cmp_cos = jnp.stack([ccos, jnp.ones_like(ccos)])
    cmp_sin = jnp.stack([csin, jnp.zeros_like(csin)])
    overlap = _overlap_matrix(S)
    expand = _gate_expand_matrix()

    w_in_p = _prep_w_in(w_in)
    w_out_p = _prep_w_out(w_out)
    w_up_p = ffn_w_up.astype(BF16)
    w_down_p = ffn_w_down.astype(BF16)
    cmp_p = _prep_compress(jnp.stack([cmp_k_pos, cmp_v_pos], axis=1), jnp.stack([cmp_k_w1, cmp_v_w1], axis=1),
                           jnp.stack([cmp_k_w2, cmp_v_w2], axis=1))

    x2 = x.reshape(B * S, D)
    for l in range(depth):
        outs = _inproj(x2, norm_mix_pre[l][None], w_in_p, l, cos, sin, S)
        rq, rk, nq, ksl, kwn, rv, vsl, vwn, rg, kvc, ng = outs
        y_ret = _retention(rq, rk, rv, rg, ret_gn_w[l][None], B, S)
        kvc_cmp, kvc_cmp_t = _compress(kvc, *cmp_p, l, cmp_cos, cmp_sin, B, S)
        y_nsa = _nsa(nq, ng, kvc_cmp, kvc_cmp_t, ksl, vsl, kwn, vwn, overlap, expand, B, S)
        x2 = _outproj(y_ret, y_nsa, x2, w_out_p, l, norm_mix_post[l][None])
        x2 = _ffn(x2, norm_ffn_pre[l][None], w_up_p, ffn_conv, w_down_p, l, norm_ffn_post[l][None], S)
    return x2.reshape(B, S, D)
```

```python
import functools
import math

import jax
import jax.numpy as jnp
import numpy as np
from jax import lax
from jax.experimental import pallas as pl
from jax.experimental.pallas import tpu as pltpu

F32 = jnp.float32
BF16 = jnp.bfloat16

LANES = 128
SUBLANES = 8
V7X_VMEM_LIMIT = 56 * 1024 * 1024

ROPE_THETA = 10000.0
NORM_EPS = 1e-6
GN_EPS = 1e-5
NEG = -1e30
SEL_BIG = 1e9

RET_HEADS = 4
RET_DK = 64
RET_DV = 128
RET_CHUNK = 128
NSA_HEADS = 8
NSA_KV_HEADS = 2
NSA_DK = 64
NSA_GROUP = NSA_HEADS // NSA_KV_HEADS
CMP_BLOCK = 32
CMP_STRIDE = 16
CMP_HID = 256
SLC_BLOCK = 64
SLC_TOPK = 16
SLC_LOCAL = 2
WINDOW = 512
NSA_QBLOCK = 128
NSA_TILES_PER_STEP = 4
NSA_GATES = 3 * NSA_HEADS
CONV_WIDTH = 3

NSA_Q_SCALE = NSA_DK ** -0.5 * math.log2(math.e)

D_RET = RET_HEADS * RET_DV
D_NSA = NSA_HEADS * NSA_DK
RET_QK = RET_HEADS * RET_DK
NSA_KV = NSA_KV_HEADS * NSA_DK

COL_RQ = 0
COL_RK = COL_RQ + RET_QK
COL_NQ = COL_RK + RET_QK
COL_KSL = COL_NQ + D_NSA
COL_KWN = COL_KSL + NSA_KV
ROPE_COLS = COL_KWN + NSA_KV
COL_RV = ROPE_COLS
COL_VSL = COL_RV + D_RET
COL_VWN = COL_VSL + NSA_KV
COL_RG = COL_VWN + NSA_KV
COL_KCM = COL_RG + D_RET
COL_VCM = COL_KCM + NSA_KV
COL_NG = COL_VCM + NSA_KV
IN_COLS_PAD = COL_NG + LANES

TOK_TILE = 512
RET_TILE = 512
FFN_TILE = 1024
FFN_FTILE = 256
FFN_HALO = 16
FFN_ROW_CHUNKS = 4
SLC_KTILE = 512
SLC_UNROLL = 4


GELU_C = math.sqrt(2.0 / math.pi)


def _gelu_tanh(x):
    return 0.5 * x * (1.0 + jnp.tanh(GELU_C * (x + 0.044715 * (x * x * x))))


def _rope(p, cos, sin_signed, first_half):
    half = NSA_DK // 2
    partner = jnp.where(first_half, pltpu.roll(p, LANES - half, 1), pltpu.roll(p, half, 1))
    return p * cos + partner * sin_signed


def _split_bf16(x):
    hi = x.astype(BF16)
    lo = (x - hi.astype(F32)).astype(BF16)
    return hi, lo


def _inproj_kernel(x_ref, g_ref, w_ref, cos_ref, sin_ref,
                   rq_ref, rk_ref, nq_ref, ksl_ref, kwn_ref, rv_ref, vsl_ref, vwn_ref,
                   rg_ref, kvc_ref, ng_ref, stage_ref, *, seq_len):
    tm = x_ref.shape[0]
    x = x_ref[...]
    h = (x * lax.rsqrt(jnp.mean(x * x, axis=-1, keepdims=True) + NORM_EPS) * g_ref[...]).astype(BF16)
    cos = cos_ref[...]
    sin = sin_ref[...]
    lane = lax.broadcasted_iota(jnp.int32, (tm, LANES), 1)
    first_half = (lane % NSA_DK) < NSA_DK // 2

    def proj(c0, n):
        return jnp.dot(h, w_ref[:, c0:c0 + n], preferred_element_type=F32)

    def rope_slab(p, i):
        return _rope(p[:, i * LANES:(i + 1) * LANES], cos, sin, first_half)

    p = proj(COL_RQ, RET_QK)
    for i in range(RET_QK // LANES):
        rq_ref[:, i * LANES:(i + 1) * LANES] = rope_slab(p, i).astype(BF16)
    p = proj(COL_RK, RET_QK)
    for i in range(RET_QK // LANES):
        rk_ref[:, i * LANES:(i + 1) * LANES] = (rope_slab(p, i) * (RET_DK ** -0.5)).astype(BF16)
    p = proj(COL_NQ, D_NSA)
    for i in range(D_NSA // LANES):
        nq_ref[:, i * LANES:(i + 1) * LANES] = (rope_slab(p, i) * NSA_Q_SCALE).astype(BF16)
    p = proj(COL_KSL, 2 * NSA_KV)
    ksl_ref[:, 0:LANES] = rope_slab(p, 0).astype(BF16)
    kwn_ref[...] = rope_slab(p, 1).astype(BF16)
    row = lax.broadcasted_iota(jnp.int32, (tm, LANES), 0)
    pos = (pl.program_id(0) * tm + row) % seq_len
    ksl_ref[:, LANES:2 * LANES] = jnp.where(lane == pos // SLC_BLOCK, 1.0, 0.0).astype(BF16)

    rv_ref[...] = proj(COL_RV, D_RET).astype(BF16)
    p = proj(COL_VSL, 2 * NSA_KV)
    low_half = lane < NSA_DK
    for v_ref, v in ((vsl_ref, p[:, 0:LANES]), (vwn_ref, p[:, LANES:2 * LANES])):
        v_ref[0, 0:LANES, :] = jnp.where(low_half, v, 1.0).T.astype(BF16)
        v_ref[0, LANES:2 * LANES, :] = jnp.where(low_half, 1.0, v).T.astype(BF16)
    rg_ref[...] = proj(COL_RG, D_RET)
    p = proj(COL_KCM, 2 * NSA_KV)
    for s in range(2):
        stage_ref[...] = p[:, s * LANES:(s + 1) * LANES]
        for l in range(CMP_STRIDE):
            kvc_ref[s, :, l * LANES:(l + 1) * LANES] = stage_ref[pl.ds(l, tm // CMP_STRIDE, stride=CMP_STRIDE), :]
    ng_ref[...] = proj(COL_NG, LANES)


def _inproj(x2, gain, w, layer, cos, sin, seq_len):
    T, D = x2.shape
    tm = TOK_TILE
    nt = seq_len // tm
    tok = lambda n: pl.BlockSpec((tm, n), lambda i: (i, 0))
    tok_t = pl.BlockSpec((1, 2 * LANES, tm), lambda i: (i // nt, 0, i % nt))
    out_shape = (
        jax.ShapeDtypeStruct((T, RET_QK), BF16),
        jax.ShapeDtypeStruct((T, RET_QK), BF16),
        jax.ShapeDtypeStruct((T, D_NSA), BF16),
        jax.ShapeDtypeStruct((T, 2 * LANES), BF16),
        jax.ShapeDtypeStruct((T, LANES), BF16),
        jax.ShapeDtypeStruct((T, D_RET), BF16),
        jax.ShapeDtypeStruct((T // seq_len, 2 * LANES, seq_len), BF16),
        jax.ShapeDtypeStruct((T // seq_len, 2 * LANES, seq_len), BF16),
        jax.ShapeDtypeStruct((T, D_RET), F32),
        jax.ShapeDtypeStruct((2, T // CMP_STRIDE, CMP_STRIDE * LANES), F32),
        jax.ShapeDtypeStruct((T, LANES), F32),
    )
    out_specs = (tok(RET_QK), tok(RET_QK), tok(D_NSA), tok(2 * LANES), tok(LANES), tok(D_RET),
                 tok_t, tok_t, tok(D_RET),
                 pl.BlockSpec((2, tm // CMP_STRIDE, CMP_STRIDE * LANES), lambda i: (0, i, 0)), tok(LANES))
    return pl.pallas_call(
        functools.partial(_inproj_kernel, seq_len=seq_len),
        grid=(T // tm,),
        in_specs=[tok(D),
                  pl.BlockSpec((1, D), lambda i: (0, 0)),
                  pl.BlockSpec((None, D, IN_COLS_PAD), lambda i: (layer, 0, 0)),
                  pl.BlockSpec((tm, LANES), lambda i: (i % nt, 0)),
                  pl.BlockSpec((tm, LANES), lambda i: (i % nt, 0))],
        out_specs=out_specs,
        out_shape=out_shape,
        scratch_shapes=[pltpu.VMEM((tm, LANES), F32)],
        compiler_params=pltpu.CompilerParams(dimension_semantics=("arbitrary",),
                                             vmem_limit_bytes=V7X_VMEM_LIMIT),
        name="inproj",
    )(x2, gain, w, cos, sin)


def _retention_kernel(q_ref, k_ref, v_ref, g_ref, gnw_ref, o_ref, state_ref, vbd_ref):
    C = RET_CHUNK
    n_chunks = q_ref.shape[0] // C

    @pl.when(pl.program_id(1) == 0)
    def _():
        state_ref[...] = jnp.zeros_like(state_ref)

    vbd_ref[...] = jnp.zeros_like(vbd_ref)

    ii = lax.broadcasted_iota(jnp.int32, (C, C), 0)
    jj = lax.broadcasted_iota(jnp.int32, (C, C), 1)
    diff = (ii - jj).astype(F32)
    i_col = lax.broadcasted_iota(jnp.int32, (C, 1), 0).astype(F32)
    low_half = lax.broadcasted_iota(jnp.int32, (C, LANES), 1) < RET_DK
    low_cols = lax.broadcasted_iota(jnp.int32, (C, 2 * RET_DV), 1) < RET_DV
    own_block = ((lax.broadcasted_iota(jnp.int32, (LANES, 2 * RET_DV), 0) < RET_DK)
                 == (lax.broadcasted_iota(jnp.int32, (LANES, 2 * RET_DV), 1) < RET_DV))
    nt_dims = (((1,), (1,)), ((), ()))
    tn_dims = (((0,), (0,)), ((), ()))

    for pair in range(RET_HEADS // 2):
        lg0, lg1 = (math.log(1.0 - 2.0 ** (-5.0 - h)) for h in (2 * pair, 2 * pair + 1))
        decay = [jnp.where(diff >= 0, jnp.exp(lg * jnp.maximum(diff, 0.0)), 0.0) for lg in (lg0, lg1)]
        xi = jnp.where(low_cols, jnp.exp(lg0 * (i_col + 1.0)), jnp.exp(lg1 * (i_col + 1.0)))
        zeta = jnp.where(low_half, jnp.exp(lg0 * (C - 1.0 - i_col)), jnp.exp(lg1 * (C - 1.0 - i_col)))
        chunk_decay = jnp.where(low_cols[0:1], math.exp(lg0 * C), math.exp(lg1 * C))
        qk_cols = slice(pair * LANES, (pair + 1) * LANES)
        v_cols = slice(2 * pair * RET_DV, 2 * (pair + 1) * RET_DV)
        for c in range(n_chunks):
            rows = slice(c * C, (c + 1) * C)
            q = q_ref[rows, qk_cols]
            ks = k_ref[rows, qk_cols]
            v = v_ref[rows, v_cols]
            qf = q.astype(F32)
            q_stack = jnp.concatenate([jnp.where(low_half, qf, 0.0), jnp.where(low_half, 0.0, qf)],
                                      axis=0).astype(BF16)
            s = lax.dot_general(q_stack, ks, nt_dims, preferred_element_type=F32)
            s_pair = jnp.concatenate([s[0:C] * decay[0], s[C:2 * C] * decay[1]], axis=1).astype(BF16)
            vbd_ref[pair, 0:C, 0:RET_DV] = v[:, 0:RET_DV]
            vbd_ref[pair, C:2 * C, RET_DV:2 * RET_DV] = v[:, RET_DV:2 * RET_DV]
            o = jnp.dot(s_pair, vbd_ref[pair], preferred_element_type=F32)
            state = state_ref[pair]
            o = o + jnp.dot(q, state.astype(BF16), preferred_element_type=F32) * xi
            kz = (ks.astype(F32) * zeta).astype(BF16)
            kv = lax.dot_general(kz, v, tn_dims, preferred_element_type=F32)
            state_ref[pair] = state * chunk_decay + jnp.where(own_block, kv, 0.0)
            for e in range(2):
                cols = slice((2 * pair + e) * RET_DV, (2 * pair + e + 1) * RET_DV)
                oh = o[:, e * RET_DV:(e + 1) * RET_DV]
                mu = jnp.mean(oh, axis=-1, keepdims=True)
                var = jnp.mean(jnp.square(oh - mu), axis=-1, keepdims=True)
                on = (oh - mu) * lax.rsqrt(var + GN_EPS) * gnw_ref[:, cols]
                gate = g_ref[rows, cols]
                o_ref[rows, cols] = (gate * (1.0 / (1.0 + jnp.exp(-gate))) * on).astype(BF16)


def _retention(rq, rk, rv, rg, gn_w, batch, seq_len):
    T = rq.shape[0]
    tc = RET_TILE
    nt = seq_len // tc
    tok = lambda n: pl.BlockSpec((tc, n), lambda b, i: (b * nt + i, 0))
    return pl.pallas_call(
        _retention_kernel,
        grid=(batch, nt),
        in_specs=[tok(RET_QK), tok(RET_QK), tok(D_RET), tok(D_RET),
                  pl.BlockSpec((1, D_RET), lambda b, i: (0, 0))],
        out_specs=tok(D_RET),
        out_shape=jax.ShapeDtypeStruct((T, D_RET), BF16),
        scratch_shapes=[pltpu.VMEM((RET_HEADS // 2, LANES, 2 * RET_DV), F32),
                        pltpu.VMEM((RET_HEADS // 2, 2 * RET_CHUNK, 2 * RET_DV), BF16)],
        compiler_params=pltpu.CompilerParams(dimension_semantics=("arbitrary", "arbitrary"),
                                             vmem_limit_bytes=V7X_VMEM_LIMIT),
        name="retention",
    )(rq, rk, rv, rg, gn_w)


def _compress_kernel(x_ref, pa_ref, pb_ref, wa_ref, wb_ref, w2_ref, cos_ref, sin_ref, o_ref, ot_ref):
    ng = x_ref.shape[2]
    x = x_ref[0, 0]
    xa = (x + pa_ref[0]).astype(BF16)
    xb = (x + pb_ref[0]).astype(BF16)
    a = jnp.dot(xa, wa_ref[0], preferred_element_type=F32)
    b = jnp.dot(xb, wb_ref[0], preferred_element_type=F32)
    hid = a + pltpu.roll(b, ng - 1, 0)
    out = jnp.dot(_gelu_tanh(hid).astype(BF16), w2_ref[0], preferred_element_type=F32)
    lane = lax.broadcasted_iota(jnp.int32, out.shape, 1)
    out = _rope(out, cos_ref[0], sin_ref[0], (lane % NSA_DK) < NSA_DK // 2)
    o_ref[0, 0] = out.astype(BF16)
    ot_ref[0, 0] = out.T.astype(BF16)


def _compress(kvc, pos_a, pos_b, wa, wb, w2, layer, cos, sin, batch, seq_len):
    ng = seq_len // CMP_STRIDE
    gw = CMP_STRIDE * LANES
    x = kvc.reshape(2, batch, ng, gw)
    hid = NSA_KV_HEADS * CMP_HID
    per_kv = lambda *shape: pl.BlockSpec((1,) + shape, lambda s, b: (s,) + (0,) * len(shape))
    per_lkv = lambda *shape: pl.BlockSpec((None, 1) + shape, lambda s, b: (layer, s) + (0,) * len(shape))
    return pl.pallas_call(
        _compress_kernel,
        grid=(2, batch),
        in_specs=[pl.BlockSpec((1, 1, ng, gw), lambda s, b: (s, b, 0, 0)),
                  per_lkv(1, gw), per_lkv(1, gw), per_lkv(gw, hid), per_lkv(gw, hid), per_lkv(hid, LANES),
                  per_kv(ng, LANES), per_kv(ng, LANES)],
        out_specs=(pl.BlockSpec((1, 1, ng, LANES), lambda s, b: (s, b, 0, 0)),
                   pl.BlockSpec((1, 1, LANES, ng), lambda s, b: (s, b, 0, 0))),
        out_shape=(jax.ShapeDtypeStruct((2, batch, ng, LANES), BF16),
                   jax.ShapeDtypeStruct((2, batch, LANES, ng), BF16)),
        compiler_params=pltpu.CompilerParams(dimension_semantics=("arbitrary", "arbitrary"),
                                             vmem_limit_bytes=V7X_VMEM_LIMIT),
        name="compress",
    )(x, pos_a, pos_b, wa, wb, w2, cos, sin)


def _nsa_kernel(nq_ref, ng_ref, *refs, seq_len, topk):
    o_ref = refs[8]

    def one_tile(j, carry):
        rows = pl.ds(pl.multiple_of(j * NSA_QBLOCK, NSA_QBLOCK), NSA_QBLOCK)
        t0 = (pl.program_id(1) * NSA_TILES_PER_STEP + j) * NSA_QBLOCK
        _nsa_tile(t0, nq_ref.at[rows], ng_ref.at[rows], *refs[:8], o_ref.at[rows], *refs[9:],
                  seq_len=seq_len, topk=topk)
        return carry

    lax.fori_loop(0, NSA_TILES_PER_STEP, one_tile, 0)


def _nsa_tile(t0, nq_ref, ng_ref, kc_ref, vc_ref, ksl_ref, vsl_ref, kwn_ref, vwn_ref, ov_ref, ex_ref,
              o_ref, qaug_ref, s_ref, m_ref, acc_ref, out_ref, score_ref, *, seq_len, topk):
    QB = NSA_QBLOCK
    R = NSA_GROUP
    G = NSA_KV_HEADS
    M = R * QB
    NC = kc_ref.shape[2]
    KT = SLC_KTILE
    WK = WINDOW + QB
    nt_dims = (((1,), (1,)), ((), ()))

    low_half = lax.broadcasted_iota(jnp.int32, (QB, LANES), 1) < NSA_DK
    q = nq_ref[...].astype(F32)

    def per_head(x):
        return jnp.concatenate([x] * R, axis=1)

    def normalized(acc, g):
        num, den = (acc[0:NSA_DK], acc[NSA_DK:NSA_DK + 1]) if g == 0 else (acc[NSA_DK:], acc[0:1])
        return num * (1.0 / den)

    for g in range(G):
        head_lanes = low_half if g == 0 else jnp.logical_not(low_half)
        for i in range(R):
            qaug_ref[g, i * QB:(i + 1) * QB, 0:LANES] = jnp.where(
                head_lanes, q[:, i * LANES:(i + 1) * LANES], 0.0).astype(BF16)

    ks = pl.multiple_of(jnp.clip(t0 - WINDOW, 0, seq_len - WK), QB)
    k_pos = ks + lax.broadcasted_iota(jnp.int32, (WK, QB), 0)
    t_pos = t0 + lax.broadcasted_iota(jnp.int32, (WK, QB), 1)
    win_mask = per_head((k_pos <= t_pos) & (k_pos > t_pos - WINDOW))
    jj = lax.broadcasted_iota(jnp.int32, (LANES, QB), 0)
    q_blk = (t0 + lax.broadcasted_iota(jnp.int32, (LANES, QB), 1)) // SLC_BLOCK
    valid = jj <= q_blk
    forced = (jj == 0) | (valid & (jj > q_blk - SLC_LOCAL))
    jf = jj.astype(F32)

    def window_and_compressed(nc):
        win_scores = [lax.dot_general(kwn_ref[0, pl.ds(ks, WK), :], qaug_ref[g, :, 0:LANES], nt_dims,
                                      preferred_element_type=F32) for g in range(G)]
        cmp_scores = [lax.dot_general(kc_ref[0, 0, 0:nc, :], qaug_ref[g, :, 0:LANES], nt_dims,
                                      preferred_element_type=F32) for g in range(G)]
        for g in range(G):
            sw = jnp.where(win_mask, win_scores[g], NEG)
            e_win = jnp.exp2(sw - jnp.max(sw, axis=0, keepdims=True)).astype(BF16)
            pv = jnp.dot(vwn_ref[0, g * LANES:(g + 1) * LANES, pl.ds(ks, WK)], e_win,
                         preferred_element_type=F32)
            out_ref[1, g * NSA_DK:(g + 1) * NSA_DK, :] = normalized(pv, g)

        n_idx = lax.broadcasted_iota(jnp.int32, (nc, QB), 0)
        t_col = t0 + lax.broadcasted_iota(jnp.int32, (nc, QB), 1)
        cmp_mask = per_head(n_idx * CMP_STRIDE + (CMP_BLOCK - 1) <= t_col)
        for g in range(G):
            sc = jnp.where(cmp_mask, cmp_scores[g], NEG)
            e = jnp.exp2(sc - jnp.max(sc, axis=0, keepdims=True))
            if nc == LANES:
                e = jnp.where(cmp_mask, e, 0.0)
            p_cmp = e * (1.0 / jnp.maximum(jnp.sum(e, axis=0, keepdims=True), 1e-30))
            acc = jnp.dot(vc_ref[0, 0, :, 0:nc], p_cmp.astype(BF16), preferred_element_type=F32)
            out_ref[0, g * NSA_DK:(g + 1) * NSA_DK, :] = acc[g * NSA_DK:(g + 1) * NSA_DK]
            p_sum = p_cmp[:, 0:QB]
            for i in range(1, R):
                p_sum = p_sum + p_cmp[:, i * QB:(i + 1) * QB]
            p_hi, p_lo = _split_bf16(p_sum)
            imp = (jnp.dot(ov_ref[:, 0:nc], p_hi, preferred_element_type=F32)
                   + jnp.dot(ov_ref[:, 0:nc], p_lo, preferred_element_type=F32))
            score_ref[g] = jnp.where(forced, -jnp.inf, jnp.where(valid, imp, -SEL_BIG))

    n_visible = (t0 + QB - CMP_BLOCK) // CMP_STRIDE + 1
    n_variants = NC // LANES
    for v in range(n_variants):
        @pl.when(jnp.minimum((n_visible - 1) // LANES, n_variants - 1) == v)
        def _():
            window_and_compressed((v + 1) * LANES)

    def pick_round(_, carry):
        out = []
        for score in carry:
            best = jnp.max(score, axis=0, keepdims=True)
            first = jnp.min(jnp.where(score == best, jf, float(LANES)), axis=0, keepdims=True)
            out.append(jnp.where(jf == first, -jnp.inf, score))
        return tuple(out)

    n_rounds = jnp.where((t0 + QB - 1) // SLC_BLOCK < topk, 0, topk - (1 + SLC_LOCAL))
    taken = lax.fori_loop(0, n_rounds, pick_round, tuple(score_ref[g] for g in range(G)))
    for g in range(G):
        selected = ((taken[g] == -jnp.inf) | (q_blk < topk)) & valid
        bias = jnp.where(selected, 0.0, NEG).T.astype(BF16)
        for i in range(R):
            qaug_ref[g, i * QB:(i + 1) * QB, LANES:2 * LANES] = bias

    def slc_scores(k0, slot):
        for g in range(G):
            s_ref[slot, g] = lax.dot_general(ksl_ref[0, pl.ds(k0, KT), :], qaug_ref[g], nt_dims,
                                             preferred_element_type=F32)

    def slc_accumulate(k0, slot, causal):
        for g in range(G):
            s = s_ref[slot, g]
            if causal:
                k_pos = k0 + lax.broadcasted_iota(jnp.int32, (KT, QB), 0)
                t_pos = t0 + lax.broadcasted_iota(jnp.int32, (KT, QB), 1)
                s = jnp.where(per_head(k_pos <= t_pos), s, NEG)
            m_run = m_ref[g]
            m_new = jnp.maximum(m_run, jnp.max(s, axis=0, keepdims=True))
            p = jnp.exp2(s - m_new).astype(BF16)
            pv = jnp.dot(vsl_ref[0, g * LANES:(g + 1) * LANES, pl.ds(k0, KT)], p,
                         preferred_element_type=F32)
            acc_ref[g] = jnp.exp2(m_run - m_new) * acc_ref[g] + pv
            m_ref[g] = m_new

    m_ref[...] = jnp.full(m_ref.shape, NEG, F32)
    acc_ref[...] = jnp.zeros_like(acc_ref)
    n_full = t0 // KT
    U = SLC_UNROLL
    slc_scores(0, 0)

    def slc_run(k0, n_tiles, last_is_diagonal):
        for u in range(n_tiles):
            causal = last_is_diagonal and u == n_tiles - 1
            if not causal:
                slc_scores(k0 + (u + 1) * KT, (u + 1) % 2)
            slc_accumulate(k0 + u * KT, u % 2, causal)

    def slc_group(j, _):
        slc_run(pl.multiple_of(j * (U * KT), U * KT), U, False)
        return 0

    lax.fori_loop(0, n_full // U, slc_group, 0)
    k_rest = pl.multiple_of((n_full // U) * (U * KT), U * KT)
    for r in range(U):
        @pl.when(n_full % U == r)
        def _():
            slc_run(k_rest, r + 1, True)

    o_slc = [normalized(acc_ref[g], g) for g in range(G)]

    g_hi, g_lo = _split_bf16((1.0 / (1.0 + jnp.exp(-ng_ref[...]))).T)
    gates = (jnp.dot(ex_ref[...], g_hi, preferred_element_type=F32)
             + jnp.dot(ex_ref[...], g_lo, preferred_element_type=F32))
    branches = [out_ref[0], jnp.concatenate(o_slc, axis=0), out_ref[1]]
    for i in range(R):
        y = jnp.zeros((LANES, QB), F32)
        for b, o in enumerate(branches):
            y = y + gates[b * D_NSA + i * LANES:b * D_NSA + (i + 1) * LANES] * o[:, i * QB:(i + 1) * QB]
        o_ref[:, i * LANES:(i + 1) * LANES] = y.T.astype(BF16)


def _nsa(nq, ng, kvc_cmp, kvc_cmp_t, ksl, vsl, kwn, vwn, overlap, expand, batch, seq_len):
    T = nq.shape[0]
    QB = NSA_QBLOCK
    rows = QB * NSA_TILES_PER_STEP
    nqb = seq_len // rows
    nc = seq_len // CMP_STRIDE
    topk = min(SLC_TOPK, seq_len // SLC_BLOCK)
    tok = lambda n: pl.BlockSpec((rows, n), lambda b, i: (b * nqb + i, 0))
    seq = lambda n: pl.BlockSpec((1, seq_len, n), lambda b, i: (b, 0, 0))
    seq_t = pl.BlockSpec((1, 2 * LANES, seq_len), lambda b, i: (b, 0, 0))
    return pl.pallas_call(
        functools.partial(_nsa_kernel, seq_len=seq_len, topk=topk),
        grid=(batch, nqb),
        in_specs=[tok(D_NSA), tok(LANES),
                  pl.BlockSpec((1, 1, nc, LANES), lambda b, i: (0, b, 0, 0)),
                  pl.BlockSpec((1, 1, LANES, nc), lambda b, i: (1, b, 0, 0)),
                  seq(2 * LANES), seq_t, seq(LANES), seq_t,
                  pl.BlockSpec(overlap.shape, lambda b, i: (0, 0)),
                  pl.BlockSpec(expand.shape, lambda b, i: (0, 0))],
        out_specs=tok(D_NSA),
        out_shape=jax.ShapeDtypeStruct((T, D_NSA), BF16),
        scratch_shapes=[pltpu.VMEM((NSA_KV_HEADS, NSA_GROUP * QB, 2 * LANES), BF16),
                        pltpu.VMEM((2, NSA_KV_HEADS, SLC_KTILE, NSA_GROUP * QB), F32),
                        pltpu.VMEM((NSA_KV_HEADS, 1, NSA_GROUP * QB), F32),
                        pltpu.VMEM((NSA_KV_HEADS, LANES, NSA_GROUP * QB), F32),
                        pltpu.VMEM((2, LANES, NSA_GROUP * QB), F32),
                        pltpu.VMEM((NSA_KV_HEADS, LANES, QB), F32)],
        compiler_params=pltpu.CompilerParams(dimension_semantics=("arbitrary", "arbitrary"),
                                             vmem_limit_bytes=V7X_VMEM_LIMIT),
        name="nsa",
    )(nq, ng, kvc_cmp, kvc_cmp_t, ksl.reshape(batch, seq_len, -1), vsl,
      kwn.reshape(batch, seq_len, -1), vwn, overlap, expand)


def _outproj_kernel(yr_ref, yn_ref, x_ref, w_ref, g_ref, o_ref):
    mix = (jnp.dot(yr_ref[...], w_ref[0:D_RET, :], preferred_element_type=F32)
           + jnp.dot(yn_ref[...], w_ref[D_RET:D_RET + D_NSA, :], preferred_element_type=F32))
    n = mix * lax.rsqrt(jnp.mean(mix * mix, axis=-1, keepdims=True) + NORM_EPS) * g_ref[...]
    o_ref[...] = x_ref[...] + n


def _outproj(y_ret, y_nsa, x2, w, layer, gain):
    T, D = x2.shape
    tm = TOK_TILE
    tok = lambda n: pl.BlockSpec((tm, n), lambda i: (i, 0))
    return pl.pallas_call(
        _outproj_kernel,
        grid=(T // tm,),
        in_specs=[tok(D_RET), tok(D_NSA), tok(D),
                  pl.BlockSpec((None,) + w.shape[1:], lambda i: (layer, 0, 0)),
                  pl.BlockSpec((1, D), lambda i: (0, 0))],
        out_specs=tok(D),
        out_shape=jax.ShapeDtypeStruct((T, D), F32),
        compiler_params=pltpu.CompilerParams(dimension_semantics=("arbitrary",),
                                             vmem_limit_bytes=V7X_VMEM_LIMIT),
        name="outproj",
    )(y_ret, y_nsa, x2, w, gain)


def _ffn_kernel(x_ref, xp_ref, gpre_ref, wg_ref, wv_ref, cg_ref, cv_ref, wd_ref, gpost_ref, o_ref,
                h_ref, ug_ref, uv_ref, acc_ref, *, tiles_per_seq):
    tm = x_ref.shape[0]
    H = FFN_HALO
    f = pl.program_id(1)

    def normed(x):
        return x * lax.rsqrt(jnp.mean(x * x, axis=-1, keepdims=True) + NORM_EPS) * gpre_ref[...]

    @pl.when(f == 0)
    def _():
        keep = jnp.where(pl.program_id(0) % tiles_per_seq == 0, 0.0, 1.0)
        h_ref[0:H, :] = (normed(xp_ref[...]) * keep).astype(BF16)
        h_ref[H:H + tm, :] = normed(x_ref[...]).astype(BF16)
        acc_ref[...] = jnp.zeros_like(acc_ref)

    rc = tm // FFN_ROW_CHUNKS

    def up_project(c):
        rows = slice(0 if c == 0 else H + c * rc, H + (c + 1) * rc)
        h = h_ref[rows, :]
        ug_ref[rows, :] = jnp.dot(h, wg_ref[...], preferred_element_type=F32)
        uv_ref[rows, :] = jnp.dot(h, wv_ref[...], preferred_element_type=F32)

    def causal_conv(u_ref, c_ref, r0, scale):
        out = (scale * c_ref[CONV_WIDTH - 1:CONV_WIDTH, :]) * u_ref[r0:r0 + rc, :]
        for k in range(CONV_WIDTH - 1):
            d = CONV_WIDTH - 1 - k
            out = out + (scale * c_ref[k:k + 1, :]) * u_ref[r0 - d:r0 - d + rc, :]
        return out

    def gate_and_down(c):
        g = causal_conv(ug_ref, cg_ref, H + c * rc, 1.0)
        v_half = causal_conv(uv_ref, cv_ref, H + c * rc, 0.5)
        inner = g * (GELU_C + (GELU_C * 0.044715) * (g * g))
        act = (g + g * jnp.tanh(inner)) * v_half
        acc_ref[c * rc:(c + 1) * rc, :] += jnp.dot(act.astype(BF16), wd_ref[...],
                                                   preferred_element_type=F32)

    up_project(0)
    for c in range(FFN_ROW_CHUNKS):
        if c + 1 < FFN_ROW_CHUNKS:
            up_project(c + 1)
        gate_and_down(c)

    @pl.when(f == pl.num_programs(1) - 1)
    def _():
        y = acc_ref[...]
        n = y * lax.rsqrt(jnp.mean(y * y, axis=-1, keepdims=True) + NORM_EPS) * gpost_ref[...]
        o_ref[...] = x_ref[...] + n


def _ffn(x2, g_pre, w_up, conv_w, w_down, layer, g_post, seq_len):
    T, D = x2.shape
    d_ff = w_down.shape[1]
    tm, tf, H = FFN_TILE, FFN_FTILE, FFN_HALO
    nf = d_ff // tf
    return pl.pallas_call(
        functools.partial(_ffn_kernel, tiles_per_seq=seq_len // tm),
        grid=(T // tm, nf),
        in_specs=[pl.BlockSpec((tm, D), lambda i, f: (i, 0)),
                  pl.BlockSpec((H, D), lambda i, f: (jnp.maximum(i * (tm // H) - 1, 0), 0)),
                  pl.BlockSpec((1, D), lambda i, f: (0, 0)),
                  pl.BlockSpec((None, D, tf), lambda i, f: (layer, 0, f)),
                  pl.BlockSpec((None, D, tf), lambda i, f: (layer, 0, nf + f)),
                  pl.BlockSpec((None, CONV_WIDTH, tf), lambda i, f: (layer, 0, f)),
                  pl.BlockSpec((None, CONV_WIDTH, tf), lambda i, f: (layer, 0, nf + f)),
                  pl.BlockSpec((None, tf, D), lambda i, f: (layer, f, 0)),
                  pl.BlockSpec((1, D), lambda i, f: (0, 0))],
        out_specs=pl.BlockSpec((tm, D), lambda i, f: (i, 0)),
        out_shape=jax.ShapeDtypeStruct((T, D), F32),
        scratch_shapes=[pltpu.VMEM((tm + H, D), BF16), pltpu.VMEM((tm + H, tf), F32),
                        pltpu.VMEM((tm + H, tf), F32), pltpu.VMEM((tm, D), F32)],
        compiler_params=pltpu.CompilerParams(dimension_semantics=("arbitrary", "arbitrary"),
                                             vmem_limit_bytes=V7X_VMEM_LIMIT),
        name="ffn",
    )(x2, x2, g_pre, w_up, w_up, conv_w, conv_w, w_down, g_post)


_NSA_HEAD_ORDER = [g * NSA_GROUP + i for i in range(NSA_GROUP) for g in range(NSA_KV_HEADS)]


def _rope_tables(pos):
    inv = 1.0 / (ROPE_THETA ** (jnp.arange(0, NSA_DK, 2, dtype=F32) / NSA_DK))
    ang = pos.astype(F32)[:, None] * inv[None, :]
    c, s = jnp.cos(ang), jnp.sin(ang)
    return jnp.concatenate([c, c, c, c], axis=1), jnp.concatenate([-s, s, -s, s], axis=1)


def _prep_w_in(w):
    lead = w.shape[:-1]
    splits = np.cumsum([RET_QK, RET_QK, D_RET, D_RET, D_NSA] + [NSA_KV] * 6)
    rq, rk, rv, rg, nq, kcm, vcm, ksl, vsl, kwn, vwn, ng = jnp.split(w, [int(s) for s in splits], axis=-1)
    nq = nq.reshape(lead + (NSA_HEADS, NSA_DK))[..., np.array(_NSA_HEAD_ORDER), :].reshape(lead + (D_NSA,))
    ng = jnp.pad(ng, [(0, 0)] * len(lead) + [(0, LANES - NSA_GATES)])
    return jnp.concatenate([rq, rk, nq, ksl, kwn, rv, vsl, vwn, rg, kcm, vcm, ng], axis=-1).astype(BF16)


def _prep_w_out(w):
    layers, _, d = w.shape
    w_nsa = w[:, D_RET:].reshape(layers, NSA_HEADS, NSA_DK, d)[:, np.array(_NSA_HEAD_ORDER)]
    return jnp.concatenate([w[:, :D_RET], w_nsa.reshape(layers, D_NSA, d)], axis=1).astype(BF16)


def _prep_compress(pos, w1, w2):
    lead = w1.shape[:-2]
    half = CMP_STRIDE

    def block_diag(w, axis):
        z = jnp.zeros_like(w)
        return jnp.stack([jnp.concatenate([w, z], axis=-1), jnp.concatenate([z, w], axis=-1)], axis=axis)

    def first_layer(w_half):
        w4 = w_half.reshape(lead + (half, NSA_DK, CMP_HID))
        return block_diag(w4, -3).reshape(lead + (half * NSA_KV_HEADS * NSA_DK, -1)).astype(BF16)

    def pos_row(p_half):
        rows = jnp.broadcast_to(p_half[..., :, None, :], lead + (half, NSA_KV_HEADS, NSA_DK))
        return rows.reshape(lead + (1, -1))

    w2x = block_diag(w2, -3).reshape(lead + (NSA_KV_HEADS * CMP_HID, -1)).astype(BF16)
    n1 = half * NSA_DK
    return (pos_row(pos[..., :half, :]), pos_row(pos[..., half:, :]),
            first_layer(w1[..., :n1, :]), first_layer(w1[..., n1:, :]), w2x)


def _overlap_matrix(seq_len):
    nc = seq_len // CMP_STRIDE
    cmp_start = np.arange(nc) * CMP_STRIDE
    slc_start = np.arange(LANES) * SLC_BLOCK
    ov = ((cmp_start[:, None] < slc_start[None, :] + SLC_BLOCK)
          & (cmp_start[:, None] + CMP_BLOCK > slc_start[None, :]))
    n_cmp = (seq_len - CMP_BLOCK) // CMP_STRIDE + 1
    ov &= (np.arange(nc) < n_cmp)[:, None]
    return jnp.asarray(ov.T.astype(np.float32), dtype=BF16)


def _gate_expand_matrix():
    ex = np.zeros((LANES, 3 * D_NSA), np.float32)
    for branch in range(3):
        for p in range(D_NSA):
            head = _NSA_HEAD_ORDER[p // NSA_DK]
            ex[branch * NSA_HEADS + head, branch * D_NSA + p] = 1.0
    return jnp.asarray(ex.T, dtype=BF16)


def kernel(x, norm_mix_pre, w_in, ret_gn_w, cmp_k_pos, cmp_k_w1, cmp_k_w2, cmp_v_pos, cmp_v_w1, cmp_v_w2,
           w_out, norm_mix_post, norm_ffn_pre, ffn_w_up, ffn_conv, ffn_w_down, norm_ffn_post):
    B, S, D = x.shape
    depth = w_in.shape[0]
    assert S % SLC_KTILE == 0 and S % FFN_TILE == 0 and S // SLC_BLOCK <= LANES and S >= WINDOW + NSA_QBLOCK
    assert ffn_w_down.shape[1] % FFN_FTILE == 0

    cos, sin = _rope_tables(jnp.arange(S, dtype=jnp.int32))
    nc = S // CMP_STRIDE
    ccos, csin = _rope_tables(jnp.arange(nc, dtype=jnp.int32) * CMP_STRIDE + (CMP_BLOCK - 1))
    cmp_cos = jnp.stack([ccos, jnp.ones_like(ccos)])
    cmp_sin = jnp.stack([csin, jnp.zeros_like(csin)])
    overlap = _overlap_matrix(S)
    expand = _gate_expand_matrix()

    w_in_p = _prep_w_in(w_in)
    w_out_p = _prep_w_out(w_out)
    w_up_p = ffn_w_up.astype(BF16)
    w_down_p = ffn_w_down.astype(BF16)
    cmp_p = _prep_compress(jnp.stack([cmp_k_pos, cmp_v_pos], axis=1), jnp.stack([cmp_k_w1, cmp_v_w1], axis=1),
                           jnp.stack([cmp_k_w2, cmp_v_w2], axis=1))

    x2 = x.reshape(B * S, D)
    for l in range(depth):
        outs = _inproj(x2, norm_mix_pre[l][None], w_in_p, l, cos, sin, S)
        rq, rk, nq, ksl, kwn, rv, vsl, vwn, rg, kvc, ng = outs
        y_ret = _retention(rq, rk, rv, rg, ret_gn_w[l][None], B, S)
        kvc_cmp, kvc_cmp_t = _compress(kvc, *cmp_p, l, cmp_cos, cmp_sin, B, S)
        y_nsa = _nsa(nq, ng, kvc_cmp, kvc_cmp_t, ksl, vsl, kwn, vwn, overlap, expand, B, S)
        x2 = _outproj(y_ret, y_nsa, x2, w_out_p, l, norm_mix_post[l][None])
        x2 = _ffn(x2, norm_ffn_pre[l][None], w_up_p, ffn_conv, w_down_p, l, norm_ffn_post[l][None], S)
    return x2.reshape(B, S, D)
```

```python
import functools
import math

import jax
import jax.numpy as jnp
import numpy as np
from jax import lax
from jax.experimental import pallas as pl
from jax.experimental.pallas import tpu as pltpu

F32 = jnp.float32
BF16 = jnp.bfloat16

LANES = 128
SUBLANES = 8
V7X_VMEM_LIMIT = 56 * 1024 * 1024

ROPE_THETA = 10000.0
NORM_EPS = 1e-6
GN_EPS = 1e-5
NEG = -1e30
SEL_BIG = 1e9

RET_HEADS = 4
RET_DK = 64
RET_DV = 128
RET_CHUNK = 128
NSA_HEADS = 8
NSA_KV_HEADS = 2
NSA_DK = 64
NSA_GROUP = NSA_HEADS // NSA_KV_HEADS
CMP_BLOCK = 32
CMP_STRIDE = 16
CMP_HID = 256
SLC_BLOCK = 64
SLC_TOPK = 16
SLC_LOCAL = 2
WINDOW = 512
NSA_QBLOCK = 128
NSA_TILES_PER_STEP = 4
NSA_GATES = 3 * NSA_HEADS
CONV_WIDTH = 3

NSA_Q_SCALE = NSA_DK ** -0.5 * math.log2(math.e)

D_RET = RET_HEADS * RET_DV
D_NSA = NSA_HEADS * NSA_DK
RET_QK = RET_HEADS * RET_DK
NSA_KV = NSA_KV_HEADS * NSA_DK

COL_RQ = 0
COL_RK = COL_RQ + RET_QK
COL_NQ = COL_RK + RET_QK
COL_KSL = COL_NQ + D_NSA
COL_KWN = COL_KSL + NSA_KV
ROPE_COLS = COL_KWN + NSA_KV
COL_RV = ROPE_COLS
COL_VSL = COL_RV + D_RET
COL_VWN = COL_VSL + NSA_KV
COL_RG = COL_VWN + NSA_KV
COL_KCM = COL_RG + D_RET
COL_VCM = COL_KCM + NSA_KV
COL_NG = COL_VCM + NSA_KV
IN_COLS_PAD = COL_NG + LANES

TOK_TILE = 512
RET_TILE = 512
FFN_TILE = 1024
FFN_FTILE = 256
FFN_HALO = 16
FFN_ROW_CHUNKS = 4
SLC_KTILE = 512
SLC_UNROLL = 4


GELU_C = math.sqrt(2.0 / math.pi)


def _gelu_tanh(x):
    return 0.5 * x * (1.0 + jnp.tanh(GELU_C * (x + 0.044715 * (x * x * x))))


def _rope(p, cos, sin_signed, first_half):
    half = NSA_DK // 2
    partner = jnp.where(first_half, pltpu.roll(p, LANES - half, 1), pltpu.roll(p, half, 1))
    return p * cos + partner * sin_signed


def _split_bf16(x):
    hi = x.astype(BF16)
    lo = (x - hi.astype(F32)).astype(BF16)
    return hi, lo


def _inproj_kernel(x_ref, g_ref, w_ref, cos_ref, sin_ref,
                   rq_ref, rk_ref, nq_ref, ksl_ref, kwn_ref, rv_ref, vsl_ref, vwn_ref,
                   rg_ref, kvc_ref, ng_ref, stage_ref, *, seq_len):
    tm = x_ref.shape[0]
    x = x_ref[...]
    h = (x * lax.rsqrt(jnp.mean(x * x, axis=-1, keepdims=True) + NORM_EPS) * g_ref[...]).astype(BF16)
    cos = cos_ref[...]
    sin = sin_ref[...]
    lane = lax.broadcasted_iota(jnp.int32, (tm, LANES), 1)
    first_half = (lane % NSA_DK) < NSA_DK // 2

    def proj(c0, n):
        return jnp.dot(h, w_ref[:, c0:c0 + n], preferred_element_type=F32)

    def rope_slab(p, i):
        return _rope(p[:, i * LANES:(i + 1) * LANES], cos, sin, first_half)

    p = proj(COL_RQ, RET_QK)
    for i in range(RET_QK // LANES):
        rq_ref[:, i * LANES:(i + 1) * LANES] = rope_slab(p, i).astype(BF16)
    p = proj(COL_RK, RET_QK)
    for i in range(RET_QK // LANES):
        rk_ref[:, i * LANES:(i + 1) * LANES] = (rope_slab(p, i) * (RET_DK ** -0.5)).astype(BF16)
    p = proj(COL_NQ, D_NSA)
    for i in range(D_NSA // LANES):
        nq_ref[:, i * LANES:(i + 1) * LANES] = (rope_slab(p, i) * NSA_Q_SCALE).astype(BF16)
    p = proj(COL_KSL, 2 * NSA_KV)
    ksl_ref[:, 0:LANES] = rope_slab(p, 0).astype(BF16)
    kwn_ref[...] = rope_slab(p, 1).astype(BF16)
    row = lax.broadcasted_iota(jnp.int32, (tm, LANES), 0)
    pos = (pl.program_id(0) * tm + row) % seq_len
    ksl_ref[:, LANES:2 * LANES] = jnp.where(lane == pos // SLC_BLOCK, 1.0, 0.0).astype(BF16)

    rv_ref[...] = proj(COL_RV, D_RET).astype(BF16)
    p = proj(COL_VSL, 2 * NSA_KV)
    low_half = lane < NSA_DK
    for v_ref, v in ((vsl_ref, p[:, 0:LANES]), (vwn_ref, p[:, LANES:2 * LANES])):
        v_ref[0, 0:LANES, :] = jnp.where(low_half, v, 1.0).T.astype(BF16)
        v_ref[0, LANES:2 * LANES, :] = jnp.where(low_half, 1.0, v).T.astype(BF16)
    rg_ref[...] = proj(COL_RG, D_RET)
    p = proj(COL_KCM, 2 * NSA_KV)
    for s in range(2):
        stage_ref[...] = p[:, s * LANES:(s + 1) * LANES]
        for l in range(CMP_STRIDE):
            kvc_ref[s, :, l * LANES:(l + 1) * LANES] = stage_ref[pl.ds(l, tm // CMP_STRIDE, stride=CMP_STRIDE), :]
    ng_ref[...] = proj(COL_NG, LANES)


def _inproj(x2, gain, w, layer, cos, sin, seq_len):
    T, D = x2.shape
    tm = TOK_TILE
    nt = seq_len // tm
    tok = lambda n: pl.BlockSpec((tm, n), lambda i: (i, 0))
    tok_t = pl.BlockSpec((1, 2 * LANES, tm), lambda i: (i // nt, 0, i % nt))
    out_shape = (
        jax.ShapeDtypeStruct((T, RET_QK), BF16),
        jax.ShapeDtypeStruct((T, RET_QK), BF16),
        jax.ShapeDtypeStruct((T, D_NSA), BF16),
        jax.ShapeDtypeStruct((T, 2 * LANES), BF16),
        jax.ShapeDtypeStruct((T, LANES), BF16),
        jax.ShapeDtypeStruct((T, D_RET), BF16),
        jax.ShapeDtypeStruct((T // seq_len, 2 * LANES, seq_len), BF16),
        jax.ShapeDtypeStruct((T // seq_len, 2 * LANES, seq_len), BF16),
        jax.ShapeDtypeStruct((T, D_RET), F32),
        jax.ShapeDtypeStruct((2, T // CMP_STRIDE, CMP_STRIDE * LANES), F32),
        jax.ShapeDtypeStruct((T, LANES), F32),
    )
    out_specs = (tok(RET_QK), tok(RET_QK), tok(D_NSA), tok(2 * LANES), tok(LANES), tok(D_RET),
                 tok_t, tok_t, tok(D_RET),
                 pl.BlockSpec((2, tm // CMP_STRIDE, CMP_STRIDE * LANES), lambda i: (0, i, 0)), tok(LANES))
    return pl.pallas_call(
        functools.partial(_inproj_kernel, seq_len=seq_len),
        grid=(T // tm,),
        in_specs=[tok(D),
                  pl.BlockSpec((1, D), lambda i: (0, 0)),
                  pl.BlockSpec((None, D, IN_COLS_PAD), lambda i: (layer, 0, 0)),
                  pl.BlockSpec((tm, LANES), lambda i: (i % nt, 0)),
                  pl.BlockSpec((tm, LANES), lambda i: (i % nt, 0))],
        out_specs=out_specs,
        out_shape=out_shape,
        scratch_shapes=[pltpu.VMEM((tm, LANES), F32)],
        compiler_params=pltpu.CompilerParams(dimension_semantics=("arbitrary",),
                                             vmem_limit_bytes=V7X_VMEM_LIMIT),
        name="inproj",
    )(x2, gain, w, cos, sin)


def _retention_kernel(q_ref, k_ref, v_ref, g_ref, gnw_ref, o_ref, state_ref, vbd_ref):
    C = RET_CHUNK
    n_chunks = q_ref.shape[0] // C

    @pl.when(pl.program_id(1) == 0)
    def _():
        state_ref[...] = jnp.zeros_like(state_ref)

    vbd_ref[...] = jnp.zeros_like(vbd_ref)

    ii = lax.broadcasted_iota(jnp.int32, (C, C), 0)
    jj = lax.broadcasted_iota(jnp.int32, (C, C), 1)
    diff = (ii - jj).astype(F32)
    i_col = lax.broadcasted_iota(jnp.int32, (C, 1), 0).astype(F32)
    low_half = lax.broadcasted_iota(jnp.int32, (C, LANES), 1) < RET_DK
    low_cols = lax.broadcasted_iota(jnp.int32, (C, 2 * RET_DV), 1) < RET_DV
    own_block = ((lax.broadcasted_iota(jnp.int32, (LANES, 2 * RET_DV), 0) < RET_DK)
                 == (lax.broadcasted_iota(jnp.int32, (LANES, 2 * RET_DV), 1) < RET_DV))
    nt_dims = (((1,), (1,)), ((), ()))
    tn_dims = (((0,), (0,)), ((), ()))

    for pair in range(RET_HEADS // 2):
        lg0, lg1 = (math.log(1.0 - 2.0 ** (-5.0 - h)) for h in (2 * pair, 2 * pair + 1))
        decay = [jnp.where(diff >= 0, jnp.exp(lg * jnp.maximum(diff, 0.0)), 0.0) for lg in (lg0, lg1)]
        xi = jnp.where(low_cols, jnp.exp(lg0 * (i_col + 1.0)), jnp.exp(lg1 * (i_col + 1.0)))
        zeta = jnp.where(low_half, jnp.exp(lg0 * (C - 1.0 - i_col)), jnp.exp(lg1 * (C - 1.0 - i_col)))
        chunk_decay = jnp.where(low_cols[0:1], math.exp(lg0 * C), math.exp(lg1 * C))
        qk_cols = slice(pair * LANES, (pair + 1) * LANES)
        v_cols = slice(2 * pair * RET_DV, 2 * (pair + 1) * RET_DV)
        for c in range(n_chunks):
            rows = slice(c * C, (c + 1) * C)
            q = q_ref[rows, qk_cols]
            ks = k_ref[rows, qk_cols]
            v = v_ref[rows, v_cols]
            qf = q.astype(F32)
            q_stack = jnp.concatenate([jnp.where(low_half, qf, 0.0), jnp.where(low_half, 0.0, qf)],
                                      axis=0).astype(BF16)
            s = lax.dot_general(q_stack, ks, nt_dims, preferred_element_type=F32)
            s_pair = jnp.concatenate([s[0:C] * decay[0], s[C:2 * C] * decay[1]], axis=1).astype(BF16)
            vbd_ref[pair, 0:C, 0:RET_DV] = v[:, 0:RET_DV]
            vbd_ref[pair, C:2 * C, RET_DV:2 * RET_DV] = v[:, RET_DV:2 * RET_DV]
            o = jnp.dot(s_pair, vbd_ref[pair], preferred_element_type=F32)
            state = state_ref[pair]
            o = o + jnp.dot(q, state.astype(BF16), preferred_element_type=F32) * xi
            kz = (ks.astype(F32) * zeta).astype(BF16)
            kv = lax.dot_general(kz, v, tn_dims, preferred_element_type=F32)
            state_ref[pair] = state * chunk_decay + jnp.where(own_block, kv, 0.0)
            for e in range(2):
                cols = slice((2 * pair + e) * RET_DV, (2 * pair + e + 1) * RET_DV)
                oh = o[:, e * RET_DV:(e + 1) * RET_DV]
                mu = jnp.mean(oh, axis=-1, keepdims=True)
                var = jnp.mean(jnp.square(oh - mu), axis=-1, keepdims=True)
                on = (oh - mu) * lax.rsqrt(var + GN_EPS) * gnw_ref[:, cols]
                gate = g_ref[rows, cols]
                o_ref[rows, cols] = (gate * (1.0 / (1.0 + jnp.exp(-gate))) * on).astype(BF16)


def _retention(rq, rk, rv, rg, gn_w, batch, seq_len):
    T = rq.shape[0]
    tc = RET_TILE
    nt = seq_len // tc
    tok = lambda n: pl.BlockSpec((tc, n), lambda b, i: (b * nt + i, 0))
    return pl.pallas_call(
        _retention_kernel,
        grid=(batch, nt),
        in_specs=[tok(RET_QK), tok(RET_QK), tok(D_RET), tok(D_RET),
                  pl.BlockSpec((1, D_RET), lambda b, i: (0, 0))],
        out_specs=tok(D_RET),
        out_shape=jax.ShapeDtypeStruct((T, D_RET), BF16),
        scratch_shapes=[pltpu.VMEM((RET_HEADS // 2, LANES, 2 * RET_DV), F32),
                        pltpu.VMEM((RET_HEADS // 2, 2 * RET_CHUNK, 2 * RET_DV), BF16)],
        compiler_params=pltpu.CompilerParams(dimension_semantics=("arbitrary", "arbitrary"),
                                             vmem_limit_bytes=V7X_VMEM_LIMIT),
        name="retention",
    )(rq, rk, rv, rg, gn_w)


def _compress_kernel(x_ref, pa_ref, pb_ref, wa_ref, wb_ref, w2_ref, cos_ref, sin_ref, o_ref, ot_ref):
    ng = x_ref.shape[2]
    x = x_ref[0, 0]
    xa = (x + pa_ref[0]).astype(BF16)
    xb = (x + pb_ref[0]).astype(BF16)
    a = jnp.dot(xa, wa_ref[0], preferred_element_type=F32)
    b = jnp.dot(xb, wb_ref[0], preferred_element_type=F32)
    hid = a + pltpu.roll(b, ng - 1, 0)
    out = jnp.dot(_gelu_tanh(hid).astype(BF16), w2_ref[0], preferred_element_type=F32)
    lane = lax.broadcasted_iota(jnp.int32, out.shape, 1)
    out = _rope(out, cos_ref[0], sin_ref[0], (lane % NSA_DK) < NSA_DK // 2)
    o_ref[0, 0] = out.astype(BF16)
    ot_ref[0, 0] = out.T.astype(BF16)


def _compress(kvc, pos_a, pos_b, wa, wb, w2, layer, cos, sin, batch, seq_len):
    ng = seq_len // CMP_STRIDE
    gw = CMP_STRIDE * LANES
    x = kvc.reshape(2, batch, ng, gw)
    hid = NSA_KV_HEADS * CMP_HID
    per_kv = lambda *shape: pl.BlockSpec((1,) + shape, lambda s, b: (s,) + (0,) * len(shape))
    per_lkv = lambda *shape: pl.BlockSpec((None, 1) + shape, lambda s, b: (layer, s) + (0,) * len(shape))
    return pl.pallas_call(
        _compress_kernel,
        grid=(2, batch),
        in_specs=[pl.BlockSpec((1, 1, ng, gw), lambda s, b: (s, b, 0, 0)),
                  per_lkv(1, gw), per_lkv(1, gw), per_lkv(gw, hid), per_lkv(gw, hid), per_lkv(hid, LANES),
                  per_kv(ng, LANES), per_kv(ng, LANES)],
        out_specs=(pl.BlockSpec((1, 1, ng, LANES), lambda s, b: (s, b, 0, 0)),
                   pl.BlockSpec((1, 1, LANES, ng), lambda s, b: (s, b, 0, 0))),
        out_shape=(jax.ShapeDtypeStruct((2, batch, ng, LANES), BF16),
                   jax.ShapeDtypeStruct((2, batch, LANES, ng), BF16)),
        compiler_params=pltpu.CompilerParams(dimension_semantics=("arbitrary", "arbitrary"),
                                             vmem_limit_bytes=V7X_VMEM_LIMIT),
        name="compress",
    )(x, pos_a, pos_b, wa, wb, w2, cos, sin)


def _nsa_kernel(nq_ref, ng_ref, *refs, seq_len, topk):
    o_ref = refs[8]

    def one_tile(j, carry):
        rows = pl.ds(pl.multiple_of(j * NSA_QBLOCK, NSA_QBLOCK), NSA_QBLOCK)
        t0 = (pl.program_id(1) * NSA_TILES_PER_STEP + j) * NSA_QBLOCK
        _nsa_tile(t0, nq_ref.at[rows], ng_ref.at[rows], *refs[:8], o_ref.at[rows], *refs[9:],
                  seq_len=seq_len, topk=topk)
        return carry

    lax.fori_loop(0, NSA_TILES_PER_STEP, one_tile, 0)


def _nsa_tile(t0, nq_ref, ng_ref, kc_ref, vc_ref, ksl_ref, vsl_ref, kwn_ref, vwn_ref, ov_ref, ex_ref,
              o_ref, qaug_ref, s_ref, m_ref, acc_ref, out_ref, *, seq_len, topk):
    QB = NSA_QBLOCK
    R = NSA_GROUP
    G = NSA_KV_HEADS
    M = R * QB
    NC = kc_ref.shape[2]
    KT = SLC_KTILE
    WK = WINDOW + QB
    nt_dims = (((1,), (1,)), ((), ()))

    low_half = lax.broadcasted_iota(jnp.int32, (QB, LANES), 1) < NSA_DK
    q = nq_ref[...].astype(F32)

    def per_head(x):
        return jnp.concatenate([x] * R, axis=1)

    def normalized(acc, g):
        num, den = (acc[0:NSA_DK], acc[NSA_DK:NSA_DK + 1]) if g == 0 else (acc[NSA_DK:], acc[0:1])
        return num * (1.0 / den)

    for g in range(G):
        head_lanes = low_half if g == 0 else jnp.logical_not(low_half)
        for i in range(R):
            qaug_ref[g, i * QB:(i + 1) * QB, 0:LANES] = jnp.where(
                head_lanes, q[:, i * LANES:(i + 1) * LANES], 0.0).astype(BF16)

    ks = pl.multiple_of(jnp.clip(t0 - WINDOW, 0, seq_len - WK), QB)
    k_pos = ks + lax.broadcasted_iota(jnp.int32, (WK, QB), 0)
    t_pos = t0 + lax.broadcasted_iota(jnp.int32, (WK, QB), 1)
    win_mask = per_head((k_pos <= t_pos) & (k_pos > t_pos - WINDOW))
    jj = lax.broadcasted_iota(jnp.int32, (LANES, QB), 0)
    q_blk = (t0 + lax.broadcasted_iota(jnp.int32, (LANES, QB), 1)) // SLC_BLOCK
    valid = jj <= q_blk
    forced = (jj == 0) | (valid & (jj > q_blk - SLC_LOCAL))
    jf = jj.astype(F32)

    def pick_round(score):
        best = jnp.max(score, axis=0, keepdims=True)
        first = jnp.min(jnp.where(score == best, jf, float(LANES)), axis=0, keepdims=True)
        return jnp.where(jf == first, -jnp.inf, score)

    def compressed_select_window(nc):
        win_scores = [lax.dot_general(kwn_ref[0, pl.ds(ks, WK), :], qaug_ref[g, :, 0:LANES], nt_dims,
                                      preferred_element_type=F32) for g in range(G)]
        cmp_scores = [lax.dot_general(kc_ref[0, 0, 0:nc, :], qaug_ref[g, :, 0:LANES], nt_dims,
                                      preferred_element_type=F32) for g in range(G)]

        scores = []
        n_idx = lax.broadcasted_iota(jnp.int32, (nc, QB), 0)
        t_col = t0 + lax.broadcasted_iota(jnp.int32, (nc, QB), 1)
        cmp_mask = per_head(n_idx * CMP_STRIDE + (CMP_BLOCK - 1) <= t_col)
        for g in range(G):
            sc = jnp.where(cmp_mask, cmp_scores[g], NEG)
            e = jnp.exp2(sc - jnp.max(sc, axis=0, keepdims=True))
            if nc == LANES:
                e = jnp.where(cmp_mask, e, 0.0)
            p_cmp = e * (1.0 / jnp.maximum(jnp.sum(e, axis=0, keepdims=True), 1e-30))
            acc = jnp.dot(vc_ref[0, 0, :, 0:nc], p_cmp.astype(BF16), preferred_element_type=F32)
            out_ref[0, g * NSA_DK:(g + 1) * NSA_DK, :] = acc[g * NSA_DK:(g + 1) * NSA_DK]
            p_sum = p_cmp[:, 0:QB]
            for i in range(1, R):
                p_sum = p_sum + p_cmp[:, i * QB:(i + 1) * QB]
            p_hi, p_lo = _split_bf16(p_sum)
            imp = (jnp.dot(ov_ref[:, 0:nc], p_hi, preferred_element_type=F32)
                   + jnp.dot(ov_ref[:, 0:nc], p_lo, preferred_element_type=F32))
            scores.append(jnp.where(forced, -jnp.inf, jnp.where(valid, imp, -SEL_BIG)))

        for _ in range(topk - (1 + SLC_LOCAL)):
            scores = [pick_round(score) for score in scores]
        for g in range(G):
            selected = ((scores[g] == -jnp.inf) | (q_blk < topk)) & valid
            bias = jnp.where(selected, 0.0, NEG).T.astype(BF16)
            for i in range(R):
                qaug_ref[g, i * QB:(i + 1) * QB, LANES:2 * LANES] = bias

        for g in range(G):
            sw = jnp.where(win_mask, win_scores[g], NEG)
            e_win = jnp.exp2(sw - jnp.max(sw, axis=0, keepdims=True)).astype(BF16)
            pv = jnp.dot(vwn_ref[0, g * LANES:(g + 1) * LANES, pl.ds(ks, WK)], e_win,
                         preferred_element_type=F32)
            out_ref[1, g * NSA_DK:(g + 1) * NSA_DK, :] = normalized(pv, g)

    n_visible = (t0 + QB - CMP_BLOCK) // CMP_STRIDE + 1
    n_variants = NC // LANES
    for v in range(n_variants):
        @pl.when(jnp.minimum((n_visible - 1) // LANES, n_variants - 1) == v)
        def _():
            compressed_select_window((v + 1) * LANES)

    def slc_scores(k0, slot):
        for g in range(G):
            s_ref[slot, g] = lax.dot_general(ksl_ref[0, pl.ds(k0, KT), :], qaug_ref[g], nt_dims,
                                             preferred_element_type=F32)

    def slc_accumulate(k0, slot, causal):
        for g in range(G):
            s = s_ref[slot, g]
            if causal:
                k_pos = k0 + lax.broadcasted_iota(jnp.int32, (KT, QB), 0)
                t_pos = t0 + lax.broadcasted_iota(jnp.int32, (KT, QB), 1)
                s = jnp.where(per_head(k_pos <= t_pos), s, NEG)
            m_run = m_ref[g]
            m_new = jnp.maximum(m_run, jnp.max(s, axis=0, keepdims=True))
            p = jnp.exp2(s - m_new).astype(BF16)
            pv = jnp.dot(vsl_ref[0, g * LANES:(g + 1) * LANES, pl.ds(k0, KT)], p,
                         preferred_element_type=F32)
            acc_ref[g] = jnp.exp2(m_run - m_new) * acc_ref[g] + pv
            m_ref[g] = m_new

    m_ref[...] = jnp.full(m_ref.shape, NEG, F32)
    acc_ref[...] = jnp.zeros_like(acc_ref)
    n_full = t0 // KT
    U = SLC_UNROLL
    slc_scores(0, 0)

    def slc_run(k0, n_tiles, last_is_diagonal):
        for u in range(n_tiles):
            causal = last_is_diagonal and u == n_tiles - 1
            if not causal:
                slc_scores(k0 + (u + 1) * KT, (u + 1) % 2)
            slc_accumulate(k0 + u * KT, u % 2, causal)

    def slc_group(j, _):
        slc_run(pl.multiple_of(j * (U * KT), U * KT), U, False)
        return 0

    lax.fori_loop(0, n_full // U, slc_group, 0)
    k_rest = pl.multiple_of((n_full // U) * (U * KT), U * KT)
    for r in range(U):
        @pl.when(n_full % U == r)
        def _():
            slc_run(k_rest, r + 1, True)

    o_slc = [normalized(acc_ref[g], g) for g in range(G)]

    g_hi, g_lo = _split_bf16((1.0 / (1.0 + jnp.exp(-ng_ref[...]))).T)
    gates = (jnp.dot(ex_ref[...], g_hi, preferred_element_type=F32)
             + jnp.dot(ex_ref[...], g_lo, preferred_element_type=F32))
    branches = [out_ref[0], jnp.concatenate(o_slc, axis=0), out_ref[1]]
    for i in range(R):
        y = jnp.zeros((LANES, QB), F32)
        for b, o in enumerate(branches):
            y = y + gates[b * D_NSA + i * LANES:b * D_NSA + (i + 1) * LANES] * o[:, i * QB:(i + 1) * QB]
        o_ref[:, i * LANES:(i + 1) * LANES] = y.T.astype(BF16)


def _nsa(nq, ng, kvc_cmp, kvc_cmp_t, ksl, vsl, kwn, vwn, overlap, expand, batch, seq_len):
    T = nq.shape[0]
    QB = NSA_QBLOCK
    rows = QB * NSA_TILES_PER_STEP
    nqb = seq_len // rows
    nc = seq_len // CMP_STRIDE
    topk = min(SLC_TOPK, seq_len // SLC_BLOCK)
    tok = lambda n: pl.BlockSpec((rows, n), lambda b, i: (b * nqb + i, 0))
    seq = lambda n: pl.BlockSpec((1, seq_len, n), lambda b, i: (b, 0, 0))
    seq_t = pl.BlockSpec((1, 2 * LANES, seq_len), lambda b, i: (b, 0, 0))
    return pl.pallas_call(
        functools.partial(_nsa_kernel, seq_len=seq_len, topk=topk),
        grid=(batch, nqb),
        in_specs=[tok(D_NSA), tok(LANES),
                  pl.BlockSpec((1, 1, nc, LANES), lambda b, i: (0, b, 0, 0)),
                  pl.BlockSpec((1, 1, LANES, nc), lambda b, i: (1, b, 0, 0)),
                  seq(2 * LANES), seq_t, seq(LANES), seq_t,
                  pl.BlockSpec(overlap.shape, lambda b, i: (0, 0)),
                  pl.BlockSpec(expand.shape, lambda b, i: (0, 0))],
        out_specs=tok(D_NSA),
        out_shape=jax.ShapeDtypeStruct((T, D_NSA), BF16),
        scratch_shapes=[pltpu.VMEM((NSA_KV_HEADS, NSA_GROUP * QB, 2 * LANES), BF16),
                        pltpu.VMEM((2, NSA_KV_HEADS, SLC_KTILE, NSA_GROUP * QB), F32),
                        pltpu.VMEM((NSA_KV_HEADS, 1, NSA_GROUP * QB), F32),
                        pltpu.VMEM((NSA_KV_HEADS, LANES, NSA_GROUP * QB), F32),
                        pltpu.VMEM((2, LANES, NSA_GROUP * QB), F32)],
        compiler_params=pltpu.CompilerParams(dimension_semantics=("arbitrary", "arbitrary"),
                                             vmem_limit_bytes=V7X_VMEM_LIMIT),
        name="nsa",
    )(nq, ng, kvc_cmp, kvc_cmp_t, ksl.reshape(batch, seq_len, -1), vsl,
      kwn.reshape(batch, seq_len, -1), vwn, overlap, expand)


def _ffn_kernel(yr_ref, yrp_ref, yn_ref, ynp_ref, x_ref, xp_ref, wo_ref, gmix_ref, gpre_ref,
                wg_ref, wv_ref, cg_ref, cv_ref, wd_ref, gpost_ref, o_ref,
                h_ref, ug_ref, uv_ref, acc_ref, xmid_ref, *, tiles_per_seq):
    tm = x_ref.shape[0]
    H = FFN_HALO
    f = pl.program_id(1)

    def rms(x, gain_ref):
        return x * lax.rsqrt(jnp.mean(x * x, axis=-1, keepdims=True) + NORM_EPS) * gain_ref[...]

    @pl.when(f == 0)
    def _():
        mix = (jnp.dot(jnp.concatenate([yrp_ref[...], yr_ref[...]], axis=0), wo_ref[0:D_RET, :],
                       preferred_element_type=F32)
               + jnp.dot(jnp.concatenate([ynp_ref[...], yn_ref[...]], axis=0),
                         wo_ref[D_RET:D_RET + D_NSA, :], preferred_element_type=F32))
        post = rms(mix, gmix_ref)
        x_mid = x_ref[...] + post[H:H + tm]
        xmid_ref[...] = x_mid
        keep = jnp.where(pl.program_id(0) % tiles_per_seq == 0, 0.0, 1.0)
        h_ref[0:H, :] = (rms(xp_ref[...] + post[0:H], gpre_ref) * keep).astype(BF16)
        h_ref[H:H + tm, :] = rms(x_mid, gpre_ref).astype(BF16)
        acc_ref[...] = jnp.zeros_like(acc_ref)

    rc = tm // FFN_ROW_CHUNKS

    def up_project(c):
        rows = slice(0 if c == 0 else H + c * rc, H + (c + 1) * rc)
        h = h_ref[rows, :]
        ug_ref[rows, :] = jnp.dot(h, wg_ref[...], preferred_element_type=F32)
        uv_ref[rows, :] = jnp.dot(h, wv_ref[...], preferred_element_type=F32)

    def causal_conv(u_ref, c_ref, r0, scale):
        out = (scale * c_ref[CONV_WIDTH - 1:CONV_WIDTH, :]) * u_ref[r0:r0 + rc, :]
        for k in range(CONV_WIDTH - 1):
            d = CONV_WIDTH - 1 - k
            out = out + (scale * c_ref[k:k + 1, :]) * u_ref[r0 - d:r0 - d + rc, :]
        return out

    def gate_and_down(c):
        g = causal_conv(ug_ref, cg_ref, H + c * rc, 1.0)
        v_half = causal_conv(uv_ref, cv_ref, H + c * rc, 0.5)
        inner = g * (GELU_C + (GELU_C * 0.044715) * (g * g))
        act = (g + g * jnp.tanh(inner)) * v_half
        acc_ref[c * rc:(c + 1) * rc, :] += jnp.dot(act.astype(BF16), wd_ref[...],
                                                   preferred_element_type=F32)

    up_project(0)
    for c in range(FFN_ROW_CHUNKS):
        if c + 1 < FFN_ROW_CHUNKS:
            up_project(c + 1)
        gate_and_down(c)

    @pl.when(f == pl.num_programs(1) - 1)
    def _():
        o_ref[...] = xmid_ref[...] + rms(acc_ref[...], gpost_ref)


def _outproj_ffn(y_ret, y_nsa, x2, w_out, g_mix, g_pre, w_up, conv_w, w_down, layer, g_post, seq_len):
    T, D = x2.shape
    d_ff = w_down.shape[1]
    tm, tf, H = FFN_TILE, FFN_FTILE, FFN_HALO
    nf = d_ff // tf
    tile = lambda n: pl.BlockSpec((tm, n), lambda i, f: (i, 0))
    halo = lambda n: pl.BlockSpec((H, n), lambda i, f: (jnp.maximum(i * (tm // H) - 1, 0), 0))
    gain = pl.BlockSpec((1, D), lambda i, f: (0, 0))
    return pl.pallas_call(
        functools.partial(_ffn_kernel, tiles_per_seq=seq_len // tm),
        grid=(T // tm, nf),
        in_specs=[tile(D_RET), halo(D_RET), tile(D_NSA), halo(D_NSA), tile(D), halo(D),
                  pl.BlockSpec((None,) + w_out.shape[1:], lambda i, f: (layer, 0, 0)),
                  gain, gain,
                  pl.BlockSpec((None, D, tf), lambda i, f: (layer, 0, f)),
                  pl.BlockSpec((None, D, tf), lambda i, f: (layer, 0, nf + f)),
                  pl.BlockSpec((None, CONV_WIDTH, tf), lambda i, f: (layer, 0, f)),
                  pl.BlockSpec((None, CONV_WIDTH, tf), lambda i, f: (layer, 0, nf + f)),
                  pl.BlockSpec((None, tf, D), lambda i, f: (layer, f, 0)),
                  gain],
        out_specs=tile(D),
        out_shape=jax.ShapeDtypeStruct((T, D), F32),
        scratch_shapes=[pltpu.VMEM((tm + H, D), BF16), pltpu.VMEM((tm + H, tf), F32),
                        pltpu.VMEM((tm + H, tf), F32), pltpu.VMEM((tm, D), F32),
                        pltpu.VMEM((tm, D), F32)],
        compiler_params=pltpu.CompilerParams(dimension_semantics=("arbitrary", "arbitrary"),
                                             vmem_limit_bytes=V7X_VMEM_LIMIT),
        name="ffn",
    )(y_ret, y_ret, y_nsa, y_nsa, x2, x2, w_out, g_mix, g_pre, w_up, w_up, conv_w, conv_w, w_down, g_post)


_NSA_HEAD_ORDER = [g * NSA_GROUP + i for i in range(NSA_GROUP) for g in range(NSA_KV_HEADS)]


def _rope_tables(pos):
    inv = 1.0 / (ROPE_THETA ** (jnp.arange(0, NSA_DK, 2, dtype=F32) / NSA_DK))
    ang = pos.astype(F32)[:, None] * inv[None, :]
    c, s = jnp.cos(ang), jnp.sin(ang)
    return jnp.concatenate([c, c, c, c], axis=1), jnp.concatenate([-s, s, -s, s], axis=1)


def _prep_w_in(w):
    lead = w.shape[:-1]
    splits = np.cumsum([RET_QK, RET_QK, D_RET, D_RET, D_NSA] + [NSA_KV] * 6)
    rq, rk, rv, rg, nq, kcm, vcm, ksl, vsl, kwn, vwn, ng = jnp.split(w, [int(s) for s in splits], axis=-1)
    nq = nq.reshape(lead + (NSA_HEADS, NSA_DK))[..., np.array(_NSA_HEAD_ORDER), :].reshape(lead + (D_NSA,))
    ng = jnp.pad(ng, [(0, 0)] * len(lead) + [(0, LANES - NSA_GATES)])
    return jnp.concatenate([rq, rk, nq, ksl, kwn, rv, vsl, vwn, rg, kcm, vcm, ng], axis=-1).astype(BF16)


def _prep_w_out(w):
    layers, _, d = w.shape
    w_nsa = w[:, D_RET:].reshape(layers, NSA_HEADS, NSA_DK, d)[:, np.array(_NSA_HEAD_ORDER)]
    return jnp.concatenate([w[:, :D_RET], w_nsa.reshape(layers, D_NSA, d)], axis=1).astype(BF16)


def _prep_compress(pos, w1, w2):
    lead = w1.shape[:-2]
    half = CMP_STRIDE

    def block_diag(w, axis):
        z = jnp.zeros_like(w)
        return jnp.stack([jnp.concatenate([w, z], axis=-1), jnp.concatenate([z, w], axis=-1)], axis=axis)

    def first_layer(w_half):
        w4 = w_half.reshape(lead + (half, NSA_DK, CMP_HID))
        return block_diag(w4, -3).reshape(lead + (half * NSA_KV_HEADS * NSA_DK, -1)).astype(BF16)

    def pos_row(p_half):
        rows = jnp.broadcast_to(p_half[..., :, None, :], lead + (half, NSA_KV_HEADS, NSA_DK))
        return rows.reshape(lead + (1, -1))

    w2x = block_diag(w2, -3).reshape(lead + (NSA_KV_HEADS * CMP_HID, -1)).astype(BF16)
    n1 = half * NSA_DK
    return (pos_row(pos[..., :half, :]), pos_row(pos[..., half:, :]),
            first_layer(w1[..., :n1, :]), first_layer(w1[..., n1:, :]), w2x)


def _overlap_matrix(seq_len):
    nc = seq_len // CMP_STRIDE
    cmp_start = np.arange(nc) * CMP_STRIDE
    slc_start = np.arange(LANES) * SLC_BLOCK
    ov = ((cmp_start[:, None] < slc_start[None, :] + SLC_BLOCK)
          & (cmp_start[:, None] + CMP_BLOCK > slc_start[None, :]))
    n_cmp = (seq_len - CMP_BLOCK) // CMP_STRIDE + 1
    ov &= (np.arange(nc) < n_cmp)[:, None]
    return jnp.asarray(ov.T.astype(np.float32), dtype=BF16)


def _gate_expand_matrix():
    ex = np.zeros((LANES, 3 * D_NSA), np.float32)
    for branch in range(3):
        for p in range(D_NSA):
            head = _NSA_HEAD_ORDER[p // NSA_DK]
            ex[branch * NSA_HEADS + head, branch * D_NSA + p] = 1.0
    return jnp.asarray(ex.T, dtype=BF16)


def kernel(x, norm_mix_pre, w_in, ret_gn_w, cmp_k_pos, cmp_k_w1, cmp_k_w2, cmp_v_pos, cmp_v_w1, cmp_v_w2,
           w_out, norm_mix_post, norm_ffn_pre, ffn_w_up, ffn_conv, ffn_w_down, norm_ffn_post):
    B, S, D = x.shape
    depth = w_in.shape[0]
    assert S % SLC_KTILE == 0 and S % FFN_TILE == 0 and S // SLC_BLOCK <= LANES and S >= WINDOW + NSA_QBLOCK
    assert ffn_w_down.shape[1] % FFN_FTILE == 0

    cos, sin = _rope_tables(jnp.arange(S, dtype=jnp.int32))
    nc = S // CMP_STRIDE
    ccos, csin = _rope_tables(jnp.arange(nc, dtype=jnp.int32) * CMP_STRIDE + (CMP_BLOCK - 1))
    cmp_cos = jnp.stack([ccos, jnp.ones_like(ccos)])
    cmp_sin = jnp.stack([csin, jnp.zeros_like(csin)])
    overlap = _overlap_matrix(S)
    expand = _gate_expand_matrix()

    w_in_p = _prep_w_in(w_in)
    w_out_p = _prep_w_out(w_out)
    w_up_p = ffn_w_up.astype(BF16)
    w_down_p = ffn_w_down.astype(BF16)
    cmp_p = _prep_compress(jnp.stack([cmp_k_pos, cmp_v_pos], axis=1), jnp.stack([cmp_k_w1, cmp_v_w1], axis=1),
                           jnp.stack([cmp_k_w2, cmp_v_w2], axis=1))

    x2 = x.reshape(B * S, D)
    for l in range(depth):
        outs = _inproj(x2, norm_mix_pre[l][None], w_in_p, l, cos, sin, S)
        rq, rk, nq, ksl, kwn, rv, vsl, vwn, rg, kvc, ng = outs
        y_ret = _retention(rq, rk, rv, rg, ret_gn_w[l][None], B, S)
        kvc_cmp, kvc_cmp_t = _compress(kvc, *cmp_p, l, cmp_cos, cmp_sin, B, S)
        y_nsa = _nsa(nq, ng, kvc_cmp, kvc_cmp_t, ksl, vsl, kwn, vwn, overlap, expand, B, S)
        x2 = _outproj_ffn(y_ret, y_nsa, x2, w_out_p, norm_mix_post[l][None], norm_ffn_pre[l][None],
                          w_up_p, ffn_conv, w_down_p, l, norm_ffn_post[l][None], S)
    return x2.reshape(B, S, D)
```

```python
import functools
import math

import jax
import jax.numpy as jnp
import numpy as np
from jax import lax
from jax.experimental import pallas as pl
from jax.experimental.pallas import tpu as pltpu

F32 = jnp.float32
BF16 = jnp.bfloat16

LANES = 128
SUBLANES = 8
V7X_VMEM_LIMIT = 56 * 1024 * 1024

ROPE_THETA = 10000.0
NORM_EPS = 1e-6
GN_EPS = 1e-5
NEG = -1e30
SEL_BIG = 1e9

RET_HEADS = 4
RET_DK = 64
RET_DV = 128
RET_CHUNK = 128
NSA_HEADS = 8
NSA_KV_HEADS = 2
NSA_DK = 64
NSA_GROUP = NSA_HEADS // NSA_KV_HEADS
CMP_BLOCK = 32
CMP_STRIDE = 16
CMP_HID = 256
SLC_BLOCK = 64
SLC_TOPK = 16
SLC_LOCAL = 2
WINDOW = 512
NSA_QBLOCK = 128
NSA_TILES_PER_STEP = 4
NSA_GATES = 3 * NSA_HEADS
CONV_WIDTH = 3

NSA_Q_SCALE = NSA_DK ** -0.5 * math.log2(math.e)

D_RET = RET_HEADS * RET_DV
D_NSA = NSA_HEADS * NSA_DK
RET_QK = RET_HEADS * RET_DK
NSA_KV = NSA_KV_HEADS * NSA_DK

COL_RQ = 0
COL_RK = COL_RQ + RET_QK
COL_NQ = COL_RK + RET_QK
COL_KSL = COL_NQ + D_NSA
COL_KWN = COL_KSL + NSA_KV
ROPE_COLS = COL_KWN + NSA_KV
COL_RV = ROPE_COLS
COL_VSL = COL_RV + D_RET
COL_VWN = COL_VSL + NSA_KV
COL_RG = COL_VWN + NSA_KV
COL_KCM = COL_RG + D_RET
COL_VCM = COL_KCM + NSA_KV
COL_NG = COL_VCM + NSA_KV
IN_COLS_PAD = COL_NG + LANES

TOK_TILE = 512
RET_TILE = 512
FFN_TILE = 1024
FFN_FTILE = 256
FFN_HALO = 16
FFN_ROW_CHUNKS = 4
SLC_KTILE = 512
SLC_UNROLL = 4


GELU_C = math.sqrt(2.0 / math.pi)


def _gelu_tanh(x):
    return 0.5 * x * (1.0 + jnp.tanh(GELU_C * (x + 0.044715 * (x * x * x))))


def _rope(p, cos, sin_signed, first_half):
    half = NSA_DK // 2
    partner = jnp.where(first_half, pltpu.roll(p, LANES - half, 1), pltpu.roll(p, half, 1))
    return p * cos + partner * sin_signed


def _split_bf16(x):
    hi = x.astype(BF16)
    lo = (x - hi.astype(F32)).astype(BF16)
    return hi, lo


def _inproj_kernel(x_ref, g_ref, w_ref, cos_ref, sin_ref,
                   rq_ref, rk_ref, nq_ref, ksl_ref, kwn_ref, rv_ref, vsl_ref, vwn_ref,
                   rg_ref, kvc_ref, ng_ref, stage_ref, *, seq_len):
    tm = x_ref.shape[0]
    x = x_ref[...]
    h = (x * lax.rsqrt(jnp.mean(x * x, axis=-1, keepdims=True) + NORM_EPS) * g_ref[...]).astype(BF16)
    cos = cos_ref[...]
    sin = sin_ref[...]
    lane = lax.broadcasted_iota(jnp.int32, (tm, LANES), 1)
    first_half = (lane % NSA_DK) < NSA_DK // 2

    def proj(c0, n):
        return jnp.dot(h, w_ref[:, c0:c0 + n], preferred_element_type=F32)

    def rope_slab(p, i):
        return _rope(p[:, i * LANES:(i + 1) * LANES], cos, sin, first_half)

    p = proj(COL_RQ, RET_QK)
    for i in range(RET_QK // LANES):
        rq_ref[:, i * LANES:(i + 1) * LANES] = rope_slab(p, i).astype(BF16)
    p = proj(COL_RK, RET_QK)
    for i in range(RET_QK // LANES):
        rk_ref[:, i * LANES:(i + 1) * LANES] = (rope_slab(p, i) * (RET_DK ** -0.5)).astype(BF16)
    p = proj(COL_NQ, D_NSA)
    for i in range(D_NSA // LANES):
        nq_ref[:, i * LANES:(i + 1) * LANES] = (rope_slab(p, i) * NSA_Q_SCALE).astype(BF16)
    p = proj(COL_KSL, 2 * NSA_KV)
    ksl_ref[:, 0:LANES] = rope_slab(p, 0).astype(BF16)
    kwn_ref[...] = rope_slab(p, 1).astype(BF16)
    row = lax.broadcasted_iota(jnp.int32, (tm, LANES), 0)
    pos = (pl.program_id(0) * tm + row) % seq_len
    ksl_ref[:, LANES:2 * LANES] = jnp.where(lane == pos // SLC_BLOCK, 1.0, 0.0).astype(BF16)

    rv_ref[...] = proj(COL_RV, D_RET).astype(BF16)
    p = proj(COL_VSL, 2 * NSA_KV)
    low_half = lane < NSA_DK
    for v_ref, v in ((vsl_ref, p[:, 0:LANES]), (vwn_ref, p[:, LANES:2 * LANES])):
        v_ref[0, 0:LANES, :] = jnp.where(low_half, v, 1.0).T.astype(BF16)
        v_ref[0, LANES:2 * LANES, :] = jnp.where(low_half, 1.0, v).T.astype(BF16)
    rg_ref[...] = proj(COL_RG, D_RET)
    p = proj(COL_KCM, 2 * NSA_KV)
    for s in range(2):
        stage_ref[...] = p[:, s * LANES:(s + 1) * LANES]
        for l in range(CMP_STRIDE):
            kvc_ref[s, :, l * LANES:(l + 1) * LANES] = stage_ref[pl.ds(l, tm // CMP_STRIDE, stride=CMP_STRIDE), :]
    ng_ref[...] = proj(COL_NG, LANES)


def _inproj(x2, gain, w, layer, cos, sin, seq_len):
    T, D = x2.shape
    tm = TOK_TILE
    nt = seq_len // tm
    tok = lambda n: pl.BlockSpec((tm, n), lambda i: (i, 0))
    tok_t = pl.BlockSpec((1, 2 * LANES, tm), lambda i: (i // nt, 0, i % nt))
    out_shape = (
        jax.ShapeDtypeStruct((T, RET_QK), BF16),
        jax.ShapeDtypeStruct((T, RET_QK), BF16),
        jax.ShapeDtypeStruct((T, D_NSA), BF16),
        jax.ShapeDtypeStruct((T, 2 * LANES), BF16),
        jax.ShapeDtypeStruct((T, LANES), BF16),
        jax.ShapeDtypeStruct((T, D_RET), BF16),
        jax.ShapeDtypeStruct((T // seq_len, 2 * LANES, seq_len), BF16),
        jax.ShapeDtypeStruct((T // seq_len, 2 * LANES, seq_len), BF16),
        jax.ShapeDtypeStruct((T, D_RET), F32),
        jax.ShapeDtypeStruct((2, T // CMP_STRIDE, CMP_STRIDE * LANES), F32),
        jax.ShapeDtypeStruct((T, LANES), F32),
    )
    out_specs = (tok(RET_QK), tok(RET_QK), tok(D_NSA), tok(2 * LANES), tok(LANES), tok(D_RET),
                 tok_t, tok_t, tok(D_RET),
                 pl.BlockSpec((2, tm // CMP_STRIDE, CMP_STRIDE * LANES), lambda i: (0, i, 0)), tok(LANES))
    return pl.pallas_call(
        functools.partial(_inproj_kernel, seq_len=seq_len),
        grid=(T // tm,),
        in_specs=[tok(D),
                  pl.BlockSpec((1, D), lambda i: (0, 0)),
                  pl.BlockSpec((None, D, IN_COLS_PAD), lambda i: (layer, 0, 0)),
                  pl.BlockSpec((tm, LANES), lambda i: (i % nt, 0)),
                  pl.BlockSpec((tm, LANES), lambda i: (i % nt, 0))],
        out_specs=out_specs,
        out_shape=out_shape,
        scratch_shapes=[pltpu.VMEM((tm, LANES), F32)],
        compiler_params=pltpu.CompilerParams(dimension_semantics=("arbitrary",),
                                             vmem_limit_bytes=V7X_VMEM_LIMIT),
        name="inproj",
    )(x2, gain, w, cos, sin)


def _retention_kernel(q_ref, k_ref, v_ref, g_ref, gnw_ref, o_ref, state_ref, vbd_ref):
    C = RET_CHUNK
    n_chunks = q_ref.shape[0] // C

    @pl.when(pl.program_id(1) == 0)
    def _():
        state_ref[...] = jnp.zeros_like(state_ref)

    vbd_ref[...] = jnp.zeros_like(vbd_ref)

    ii = lax.broadcasted_iota(jnp.int32, (C, C), 0)
    jj = lax.broadcasted_iota(jnp.int32, (C, C), 1)
    diff = (ii - jj).astype(F32)
    i_col = lax.broadcasted_iota(jnp.int32, (C, 1), 0).astype(F32)
    low_half = lax.broadcasted_iota(jnp.int32, (C, LANES), 1) < RET_DK
    low_cols = lax.broadcasted_iota(jnp.int32, (C, 2 * RET_DV), 1) < RET_DV
    own_block = ((lax.broadcasted_iota(jnp.int32, (LANES, 2 * RET_DV), 0) < RET_DK)
                 == (lax.broadcasted_iota(jnp.int32, (LANES, 2 * RET_DV), 1) < RET_DV))
    nt_dims = (((1,), (1,)), ((), ()))
    tn_dims = (((0,), (0,)), ((), ()))

    for pair in range(RET_HEADS // 2):
        lg0, lg1 = (math.log(1.0 - 2.0 ** (-5.0 - h)) for h in (2 * pair, 2 * pair + 1))
        decay = [jnp.where(diff >= 0, jnp.exp(lg * jnp.maximum(diff, 0.0)), 0.0) for lg in (lg0, lg1)]
        xi = jnp.where(low_cols, jnp.exp(lg0 * (i_col + 1.0)), jnp.exp(lg1 * (i_col + 1.0)))
        zeta = jnp.where(low_half, jnp.exp(lg0 * (C - 1.0 - i_col)), jnp.exp(lg1 * (C - 1.0 - i_col)))
        chunk_decay = jnp.where(low_cols[0:1], math.exp(lg0 * C), math.exp(lg1 * C))
        qk_cols = slice(pair * LANES, (pair + 1) * LANES)
        v_cols = slice(2 * pair * RET_DV, 2 * (pair + 1) * RET_DV)
        for c in range(n_chunks):
            rows = slice(c * C, (c + 1) * C)
            q = q_ref[rows, qk_cols]
            ks = k_ref[rows, qk_cols]
            v = v_ref[rows, v_cols]
            qf = q.astype(F32)
            q_stack = jnp.concatenate([jnp.where(low_half, qf, 0.0), jnp.where(low_half, 0.0, qf)],
                                      axis=0).astype(BF16)
            s = lax.dot_general(q_stack, ks, nt_dims, preferred_element_type=F32)
            s_pair = jnp.concatenate([s[0:C] * decay[0], s[C:2 * C] * decay[1]], axis=1).astype(BF16)
            vbd_ref[pair, 0:C, 0:RET_DV] = v[:, 0:RET_DV]
            vbd_ref[pair, C:2 * C, RET_DV:2 * RET_DV] = v[:, RET_DV:2 * RET_DV]
            o = jnp.dot(s_pair, vbd_ref[pair], preferred_element_type=F32)
            state = state_ref[pair]
            o = o + jnp.dot(q, state.astype(BF16), preferred_element_type=F32) * xi
            kz = (ks.astype(F32) * zeta).astype(BF16)
            kv = lax.dot_general(kz, v, tn_dims, preferred_element_type=F32)
            state_ref[pair] = state * chunk_decay + jnp.where(own_block, kv, 0.0)
            for e in range(2):
                cols = slice((2 * pair + e) * RET_DV, (2 * pair + e + 1) * RET_DV)
                oh = o[:, e * RET_DV:(e + 1) * RET_DV]
                mu = jnp.mean(oh, axis=-1, keepdims=True)
                var = jnp.mean(jnp.square(oh - mu), axis=-1, keepdims=True)
                on = (oh - mu) * lax.rsqrt(var + GN_EPS) * gnw_ref[:, cols]
                gate = g_ref[rows, cols]
                o_ref[rows, cols] = (gate * (1.0 / (1.0 + jnp.exp(-gate))) * on).astype(BF16)


def _retention(rq, rk, rv, rg, gn_w, batch, seq_len):
    T = rq.shape[0]
    tc = RET_TILE
    nt = seq_len // tc
    tok = lambda n: pl.BlockSpec((tc, n), lambda b, i: (b * nt + i, 0))
    return pl.pallas_call(
        _retention_kernel,
        grid=(batch, nt),
        in_specs=[tok(RET_QK), tok(RET_QK), tok(D_RET), tok(D_RET),
                  pl.BlockSpec((1, D_RET), lambda b, i: (0, 0))],
        out_specs=tok(D_RET),
        out_shape=jax.ShapeDtypeStruct((T, D_RET), BF16),
        scratch_shapes=[pltpu.VMEM((RET_HEADS // 2, LANES, 2 * RET_DV), F32),
                        pltpu.VMEM((RET_HEADS // 2, 2 * RET_CHUNK, 2 * RET_DV), BF16)],
        compiler_params=pltpu.CompilerParams(dimension_semantics=("arbitrary", "arbitrary"),
                                             vmem_limit_bytes=V7X_VMEM_LIMIT),
        name="retention",
    )(rq, rk, rv, rg, gn_w)


def _compress_kernel(x_ref, pa_ref, pb_ref, wa_ref, wb_ref, w2_ref, cos_ref, sin_ref, o_ref, ot_ref):
    ng = x_ref.shape[2]
    x = x_ref[0, 0]
    xa = (x + pa_ref[0]).astype(BF16)
    xb = (x + pb_ref[0]).astype(BF16)
    a = jnp.dot(xa, wa_ref[0], preferred_element_type=F32)
    b = jnp.dot(xb, wb_ref[0], preferred_element_type=F32)
    hid = a + pltpu.roll(b, ng - 1, 0)
    out = jnp.dot(_gelu_tanh(hid).astype(BF16), w2_ref[0], preferred_element_type=F32)
    lane = lax.broadcasted_iota(jnp.int32, out.shape, 1)
    out = _rope(out, cos_ref[0], sin_ref[0], (lane % NSA_DK) < NSA_DK // 2)
    o_ref[0, 0] = out.astype(BF16)
    ot_ref[0, 0] = out.T.astype(BF16)


def _compress(kvc, pos_a, pos_b, wa, wb, w2, layer, cos, sin, batch, seq_len):
    ng = seq_len // CMP_STRIDE
    gw = CMP_STRIDE * LANES
    x = kvc.reshape(2, batch, ng, gw)
    hid = NSA_KV_HEADS * CMP_HID
    per_kv = lambda *shape: pl.BlockSpec((1,) + shape, lambda s, b: (s,) + (0,) * len(shape))
    per_lkv = lambda *shape: pl.BlockSpec((None, 1) + shape, lambda s, b: (layer, s) + (0,) * len(shape))
    return pl.pallas_call(
        _compress_kernel,
        grid=(2, batch),
        in_specs=[pl.BlockSpec((1, 1, ng, gw), lambda s, b: (s, b, 0, 0)),
                  per_lkv(1, gw), per_lkv(1, gw), per_lkv(gw, hid), per_lkv(gw, hid), per_lkv(hid, LANES),
                  per_kv(ng, LANES), per_kv(ng, LANES)],
        out_specs=(pl.BlockSpec((1, 1, ng, LANES), lambda s, b: (s, b, 0, 0)),
                   pl.BlockSpec((1, 1, LANES, ng), lambda s, b: (s, b, 0, 0))),
        out_shape=(jax.ShapeDtypeStruct((2, batch, ng, LANES), BF16),
                   jax.ShapeDtypeStruct((2, batch, LANES, ng), BF16)),
        compiler_params=pltpu.CompilerParams(dimension_semantics=("arbitrary", "arbitrary"),
                                             vmem_limit_bytes=V7X_VMEM_LIMIT),
        name="compress",
    )(x, pos_a, pos_b, wa, wb, w2, cos, sin)


def _nsa_kernel(nq_ref, ng_ref, *refs, seq_len, topk):
    o_ref = refs[8]

    def one_tile(j, carry):
        rows = pl.ds(pl.multiple_of(j * NSA_QBLOCK, NSA_QBLOCK), NSA_QBLOCK)
        t0 = (pl.program_id(1) * NSA_TILES_PER_STEP + j) * NSA_QBLOCK
        _nsa_tile(t0, nq_ref.at[rows], ng_ref.at[rows], *refs[:8], o_ref.at[rows], *refs[9:],
                  seq_len=seq_len, topk=topk)
        return carry

    lax.fori_loop(0, NSA_TILES_PER_STEP, one_tile, 0)


def _nsa_tile(t0, nq_ref, ng_ref, kc_ref, vc_ref, ksl_ref, vsl_ref, kwn_ref, vwn_ref, ov_ref, ex_ref,
              o_ref, qaug_ref, s_ref, m_ref, acc_ref, out_ref, *, seq_len, topk):
    QB = NSA_QBLOCK
    R = NSA_GROUP
    G = NSA_KV_HEADS
    M = R * QB
    NC = kc_ref.shape[2]
    KT = SLC_KTILE
    WK = WINDOW + QB
    nt_dims = (((1,), (1,)), ((), ()))

    low_half = lax.broadcasted_iota(jnp.int32, (QB, LANES), 1) < NSA_DK
    q = nq_ref[...].astype(F32)

    def per_head(x):
        return jnp.concatenate([x] * R, axis=1)

    def normalized(acc, g):
        num, den = (acc[0:NSA_DK], acc[NSA_DK:NSA_DK + 1]) if g == 0 else (acc[NSA_DK:], acc[0:1])
        return num * (1.0 / den)

    for g in range(G):
        head_lanes = low_half if g == 0 else jnp.logical_not(low_half)
        for i in range(R):
            qaug_ref[g, i * QB:(i + 1) * QB, 0:LANES] = jnp.where(
                head_lanes, q[:, i * LANES:(i + 1) * LANES], 0.0).astype(BF16)

    ks = pl.multiple_of(jnp.clip(t0 - WINDOW, 0, seq_len - WK), QB)
    k_pos = ks + lax.broadcasted_iota(jnp.int32, (WK, QB), 0)
    t_pos = t0 + lax.broadcasted_iota(jnp.int32, (WK, QB), 1)
    win_mask = per_head((k_pos <= t_pos) & (k_pos > t_pos - WINDOW))
    jj = lax.broadcasted_iota(jnp.int32, (LANES, QB), 0)
    q_blk = (t0 + lax.broadcasted_iota(jnp.int32, (LANES, QB), 1)) // SLC_BLOCK
    valid = jj <= q_blk
    forced = (jj == 0) | (valid & (jj > q_blk - SLC_LOCAL))
    jf = jj.astype(F32)

    def pick_round(score):
        best = jnp.max(score, axis=0, keepdims=True)
        first = jnp.min(jnp.where(score == best, jf, float(LANES)), axis=0, keepdims=True)
        return jnp.where(jf == first, -jnp.inf, score)

    def slc_scores(k0, slot):
        for g in range(G):
            s_ref[slot, g] = lax.dot_general(ksl_ref[0, pl.ds(k0, KT), :], qaug_ref[g], nt_dims,
                                             preferred_element_type=F32)

    def compressed_select_window(nc):
        win_scores = [lax.dot_general(kwn_ref[0, pl.ds(ks, WK), :], qaug_ref[g, :, 0:LANES], nt_dims,
                                      preferred_element_type=F32) for g in range(G)]
        cmp_scores = [lax.dot_general(kc_ref[0, 0, 0:nc, :], qaug_ref[g, :, 0:LANES], nt_dims,
                                      preferred_element_type=F32) for g in range(G)]

        scores = []
        n_idx = lax.broadcasted_iota(jnp.int32, (nc, QB), 0)
        t_col = t0 + lax.broadcasted_iota(jnp.int32, (nc, QB), 1)
        cmp_mask = per_head(n_idx * CMP_STRIDE + (CMP_BLOCK - 1) <= t_col)
        for g in range(G):
            sc = jnp.where(cmp_mask, cmp_scores[g], NEG)
            e = jnp.exp2(sc - jnp.max(sc, axis=0, keepdims=True))
            if nc == LANES:
                e = jnp.where(cmp_mask, e, 0.0)
            p_cmp = e * (1.0 / jnp.maximum(jnp.sum(e, axis=0, keepdims=True), 1e-30))
            acc = jnp.dot(vc_ref[0, 0, :, 0:nc], p_cmp.astype(BF16), preferred_element_type=F32)
            out_ref[0, g * NSA_DK:(g + 1) * NSA_DK, :] = acc[g * NSA_DK:(g + 1) * NSA_DK]
            p_sum = p_cmp[:, 0:QB]
            for i in range(1, R):
                p_sum = p_sum + p_cmp[:, i * QB:(i + 1) * QB]
            p_hi, p_lo = _split_bf16(p_sum)
            imp = (jnp.dot(ov_ref[:, 0:nc], p_hi, preferred_element_type=F32)
                   + jnp.dot(ov_ref[:, 0:nc], p_lo, preferred_element_type=F32))
            scores.append(jnp.where(forced, -jnp.inf, jnp.where(valid, imp, -SEL_BIG)))

        for _ in range(topk - (1 + SLC_LOCAL)):
            scores = [pick_round(score) for score in scores]
        for g in range(G):
            selected = ((scores[g] == -jnp.inf) | (q_blk < topk)) & valid
            bias = jnp.where(selected, 0.0, NEG).T.astype(BF16)
            for i in range(R):
                qaug_ref[g, i * QB:(i + 1) * QB, LANES:2 * LANES] = bias
        slc_scores(0, 0)

        for g in range(G):
            sw = jnp.where(win_mask, win_scores[g], NEG)
            e_win = jnp.exp2(sw - jnp.max(sw, axis=0, keepdims=True)).astype(BF16)
            pv = jnp.dot(vwn_ref[0, g * LANES:(g + 1) * LANES, pl.ds(ks, WK)], e_win,
                         preferred_element_type=F32)
            out_ref[1, g * NSA_DK:(g + 1) * NSA_DK, :] = normalized(pv, g)

    n_visible = (t0 + QB - CMP_BLOCK) // CMP_STRIDE + 1
    n_variants = NC // LANES
    for v in range(n_variants):
        @pl.when(jnp.minimum((n_visible - 1) // LANES, n_variants - 1) == v)
        def _():
            compressed_select_window((v + 1) * LANES)

    def slc_accumulate(k0, slot, causal):
        for g in range(G):
            s = s_ref[slot, g]
            if causal:
                k_pos = k0 + lax.broadcasted_iota(jnp.int32, (KT, QB), 0)
                t_pos = t0 + lax.broadcasted_iota(jnp.int32, (KT, QB), 1)
                s = jnp.where(per_head(k_pos <= t_pos), s, NEG)
            m_run = m_ref[g]
            m_new = jnp.maximum(m_run, jnp.max(s, axis=0, keepdims=True))
            p = jnp.exp2(s - m_new).astype(BF16)
            pv = jnp.dot(vsl_ref[0, g * LANES:(g + 1) * LANES, pl.ds(k0, KT)], p,
                         preferred_element_type=F32)
            acc_ref[g] = jnp.exp2(m_run - m_new) * acc_ref[g] + pv
            m_ref[g] = m_new

    m_ref[...] = jnp.full(m_ref.shape, NEG, F32)
    acc_ref[...] = jnp.zeros_like(acc_ref)
    n_full = t0 // KT
    U = SLC_UNROLL

    def slc_run(k0, n_tiles, last_is_diagonal):
        for u in range(n_tiles):
            causal = last_is_diagonal and u == n_tiles - 1
            if not causal:
                slc_scores(k0 + (u + 1) * KT, (u + 1) % 2)
            slc_accumulate(k0 + u * KT, u % 2, causal)

    def slc_group(j, _):
        slc_run(pl.multiple_of(j * (U * KT), U * KT), U, False)
        return 0

    lax.fori_loop(0, n_full // U, slc_group, 0)
    k_rest = pl.multiple_of((n_full // U) * (U * KT), U * KT)
    for r in range(U):
        @pl.when(n_full % U == r)
        def _():
            slc_run(k_rest, r + 1, True)

    o_slc = [normalized(acc_ref[g], g) for g in range(G)]

    g_hi, g_lo = _split_bf16((1.0 / (1.0 + jnp.exp(-ng_ref[...]))).T)
    gates = (jnp.dot(ex_ref[...], g_hi, preferred_element_type=F32)
             + jnp.dot(ex_ref[...], g_lo, preferred_element_type=F32))
    branches = [out_ref[0], jnp.concatenate(o_slc, axis=0), out_ref[1]]
    for i in range(R):
        y = jnp.zeros((LANES, QB), F32)
        for b, o in enumerate(branches):
            y = y + gates[b * D_NSA + i * LANES:b * D_NSA + (i + 1) * LANES] * o[:, i * QB:(i + 1) * QB]
        o_ref[:, i * LANES:(i + 1) * LANES] = y.T.astype(BF16)


def _nsa(nq, ng, kvc_cmp, kvc_cmp_t, ksl, vsl, kwn, vwn, overlap, expand, batch, seq_len):
    T = nq.shape[0]
    QB = NSA_QBLOCK
    rows = QB * NSA_TILES_PER_STEP
    nqb = seq_len // rows
    nc = seq_len // CMP_STRIDE
    topk = min(SLC_TOPK, seq_len // SLC_BLOCK)
    tok = lambda n: pl.BlockSpec((rows, n), lambda b, i: (b * nqb + i, 0))
    seq = lambda n: pl.BlockSpec((1, seq_len, n), lambda b, i: (b, 0, 0))
    seq_t = pl.BlockSpec((1, 2 * LANES, seq_len), lambda b, i: (b, 0, 0))
    return pl.pallas_call(
        functools.partial(_nsa_kernel, seq_len=seq_len, topk=topk),
        grid=(batch, nqb),
        in_specs=[tok(D_NSA), tok(LANES),
                  pl.BlockSpec((1, 1, nc, LANES), lambda b, i: (0, b, 0, 0)),
                  pl.BlockSpec((1, 1, LANES, nc), lambda b, i: (1, b, 0, 0)),
                  seq(2 * LANES), seq_t, seq(LANES), seq_t,
                  pl.BlockSpec(overlap.shape, lambda b, i: (0, 0)),
                  pl.BlockSpec(expand.shape, lambda b, i: (0, 0))],
        out_specs=tok(D_NSA),
        out_shape=jax.ShapeDtypeStruct((T, D_NSA), BF16),
        scratch_shapes=[pltpu.VMEM((NSA_KV_HEADS, NSA_GROUP * QB, 2 * LANES), BF16),
                        pltpu.VMEM((2, NSA_KV_HEADS, SLC_KTILE, NSA_GROUP * QB), F32),
                        pltpu.VMEM((NSA_KV_HEADS, 1, NSA_GROUP * QB), F32),
                        pltpu.VMEM((NSA_KV_HEADS, LANES, NSA_GROUP * QB), F32),
                        pltpu.VMEM((2, LANES, NSA_GROUP * QB), F32)],
        compiler_params=pltpu.CompilerParams(dimension_semantics=("arbitrary", "arbitrary"),
                                             vmem_limit_bytes=V7X_VMEM_LIMIT),
        name="nsa",
    )(nq, ng, kvc_cmp, kvc_cmp_t, ksl.reshape(batch, seq_len, -1), vsl,
      kwn.reshape(batch, seq_len, -1), vwn, overlap, expand)


def _ffn_kernel(yr_ref, yrp_ref, yn_ref, ynp_ref, x_ref, xp_ref, wo_ref, gmix_ref, gpre_ref,
                wg_ref, wv_ref, cg_ref, cv_ref, wd_ref, gpost_ref, o_ref,
                h_ref, ug_ref, uv_ref, acc_ref, xmid_ref, *, tiles_per_seq):
    tm = x_ref.shape[0]
    H = FFN_HALO
    f = pl.program_id(1)

    def rms(x, gain_ref):
        return x * lax.rsqrt(jnp.mean(x * x, axis=-1, keepdims=True) + NORM_EPS) * gain_ref[...]

    @pl.when(f == 0)
    def _():
        rc0 = tm // FFN_ROW_CHUNKS
        keep = jnp.where(pl.program_id(0) % tiles_per_seq == 0, 0.0, 1.0)
        for c in range(FFN_ROW_CHUNKS):
            rows = slice(c * rc0, (c + 1) * rc0)
            y_ret, y_nsa = yr_ref[rows, :], yn_ref[rows, :]
            if c == 0:
                y_ret = jnp.concatenate([yrp_ref[...], y_ret], axis=0)
                y_nsa = jnp.concatenate([ynp_ref[...], y_nsa], axis=0)
            mix = (jnp.dot(y_ret, wo_ref[0:D_RET, :], preferred_element_type=F32)
                   + jnp.dot(y_nsa, wo_ref[D_RET:D_RET + D_NSA, :], preferred_element_type=F32))
            post = rms(mix, gmix_ref)
            if c == 0:
                h_ref[0:H, :] = (rms(xp_ref[...] + post[0:H], gpre_ref) * keep).astype(BF16)
                post = post[H:H + rc0]
            x_mid = x_ref[rows, :] + post
            xmid_ref[rows, :] = x_mid
            h_ref[H + c * rc0:H + (c + 1) * rc0, :] = rms(x_mid, gpre_ref).astype(BF16)
        acc_ref[...] = jnp.zeros_like(acc_ref)

    rc = tm // FFN_ROW_CHUNKS

    def up_project(c):
        rows = slice(0 if c == 0 else H + c * rc, H + (c + 1) * rc)
        h = h_ref[rows, :]
        ug_ref[rows, :] = jnp.dot(h, wg_ref[...], preferred_element_type=F32)
        uv_ref[rows, :] = jnp.dot(h, wv_ref[...], preferred_element_type=F32)

    def causal_conv(u_ref, c_ref, r0, scale):
        out = (scale * c_ref[CONV_WIDTH - 1:CONV_WIDTH, :]) * u_ref[r0:r0 + rc, :]
        for k in range(CONV_WIDTH - 1):
            d = CONV_WIDTH - 1 - k
            out = out + (scale * c_ref[k:k + 1, :]) * u_ref[r0 - d:r0 - d + rc, :]
        return out

    def gate_and_down(c):
        g = causal_conv(ug_ref, cg_ref, H + c * rc, 1.0)
        v_half = causal_conv(uv_ref, cv_ref, H + c * rc, 0.5)
        inner = g * (GELU_C + (GELU_C * 0.044715) * (g * g))
        act = (g + g * jnp.tanh(inner)) * v_half
        acc_ref[c * rc:(c + 1) * rc, :] += jnp.dot(act.astype(BF16), wd_ref[...],
                                                   preferred_element_type=F32)

    up_project(0)
    for c in range(FFN_ROW_CHUNKS):
        if c + 1 < FFN_ROW_CHUNKS:
            up_project(c + 1)
        gate_and_down(c)

    @pl.when(f == pl.num_programs(1) - 1)
    def _():
        o_ref[...] = xmid_ref[...] + rms(acc_ref[...], gpost_ref)


def _outproj_ffn(y_ret, y_nsa, x2, w_out, g_mix, g_pre, w_up, conv_w, w_down, layer, g_post, seq_len):
    T, D = x2.shape
    d_ff = w_down.shape[1]
    tm, tf, H = FFN_TILE, FFN_FTILE, FFN_HALO
    nf = d_ff // tf
    tile = lambda n: pl.BlockSpec((tm, n), lambda i, f: (i, 0))
    halo = lambda n: pl.BlockSpec((H, n), lambda i, f: (jnp.maximum(i * (tm // H) - 1, 0), 0))
    gain = pl.BlockSpec((1, D), lambda i, f: (0, 0))
    return pl.pallas_call(
        functools.partial(_ffn_kernel, tiles_per_seq=seq_len // tm),
        grid=(T // tm, nf),
        in_specs=[tile(D_RET), halo(D_RET), tile(D_NSA), halo(D_NSA), tile(D), halo(D),
                  pl.BlockSpec((None,) + w_out.shape[1:], lambda i, f: (layer, 0, 0)),
                  gain, gain,
                  pl.BlockSpec((None, D, tf), lambda i, f: (layer, 0, f)),
                  pl.BlockSpec((None, D, tf), lambda i, f: (layer, 0, nf + f)),
                  pl.BlockSpec((None, CONV_WIDTH, tf), lambda i, f: (layer, 0, f)),
                  pl.BlockSpec((None, CONV_WIDTH, tf), lambda i, f: (layer, 0, nf + f)),
                  pl.BlockSpec((None, tf, D), lambda i, f: (layer, f, 0)),
                  gain],
        out_specs=tile(D),
        out_shape=jax.ShapeDtypeStruct((T, D), F32),
        scratch_shapes=[pltpu.VMEM((tm + H, D), BF16), pltpu.VMEM((tm + H, tf), F32),
                        pltpu.VMEM((tm + H, tf), F32), pltpu.VMEM((tm, D), F32),
                        pltpu.VMEM((tm, D), F32)],
        compiler_params=pltpu.CompilerParams(dimension_semantics=("arbitrary", "arbitrary"),
                                             vmem_limit_bytes=V7X_VMEM_LIMIT),
        name="ffn",
    )(y_ret, y_ret, y_nsa, y_nsa, x2, x2, w_out, g_mix, g_pre, w_up, w_up, conv_w, conv_w, w_down, g_post)


_NSA_HEAD_ORDER = [g * NSA_GROUP + i for i in range(NSA_GROUP) for g in range(NSA_KV_HEADS)]


def _rope_tables(pos):
    inv = 1.0 / (ROPE_THETA ** (jnp.arange(0, NSA_DK, 2, dtype=F32) / NSA_DK))
    ang = pos.astype(F32)[:, None] * inv[None, :]
    c, s = jnp.cos(ang), jnp.sin(ang)
    return jnp.concatenate([c, c, c, c], axis=1), jnp.concatenate([-s, s, -s, s], axis=1)


def _prep_w_in(w):
    lead = w.shape[:-1]
    splits = np.cumsum([RET_QK, RET_QK, D_RET, D_RET, D_NSA] + [NSA_KV] * 6)
    rq, rk, rv, rg, nq, kcm, vcm, ksl, vsl, kwn, vwn, ng = jnp.split(w, [int(s) for s in splits], axis=-1)
    nq = nq.reshape(lead + (NSA_HEADS, NSA_DK))[..., np.array(_NSA_HEAD_ORDER), :].reshape(lead + (D_NSA,))
    ng = jnp.pad(ng, [(0, 0)] * len(lead) + [(0, LANES - NSA_GATES)])
    return jnp.concatenate([rq, rk, nq, ksl, kwn, rv, vsl, vwn, rg, kcm, vcm, ng], axis=-1).astype(BF16)


def _prep_w_out(w):
    layers, _, d = w.shape
    w_nsa = w[:, D_RET:].reshape(layers, NSA_HEADS, NSA_DK, d)[:, np.array(_NSA_HEAD_ORDER)]
    return jnp.concatenate([w[:, :D_RET], w_nsa.reshape(layers, D_NSA, d)], axis=1).astype(BF16)


def _prep_compress(pos, w1, w2):
    lead = w1.shape[:-2]
    half = CMP_STRIDE

    def block_diag(w, axis):
        z = jnp.zeros_like(w)
        return jnp.stack([jnp.concatenate([w, z], axis=-1), jnp.concatenate([z, w], axis=-1)], axis=axis)

    def first_layer(w_half):
        w4 = w_half.reshape(lead + (half, NSA_DK, CMP_HID))
        return block_diag(w4, -3).reshape(lead + (half * NSA_KV_HEADS * NSA_DK, -1)).astype(BF16)

    def pos_row(p_half):
        rows = jnp.broadcast_to(p_half[..., :, None, :], lead + (half, NSA_KV_HEADS, NSA_DK))
        return rows.reshape(lead + (1, -1))

    w2x = block_diag(w2, -3).reshape(lead + (NSA_KV_HEADS * CMP_HID, -1)).astype(BF16)
    n1 = half * NSA_DK
    return (pos_row(pos[..., :half, :]), pos_row(pos[..., half:, :]),
            first_layer(w1[..., :n1, :]), first_layer(w1[..., n1:, :]), w2x)


def _overlap_matrix(seq_len):
    nc = seq_len // CMP_STRIDE
    cmp_start = np.arange(nc) * CMP_STRIDE
    slc_start = np.arange(LANES) * SLC_BLOCK
    ov = ((cmp_start[:, None] < slc_start[None, :] + SLC_BLOCK)
          & (cmp_start[:, None] + CMP_BLOCK > slc_start[None, :]))
    n_cmp = (seq_len - CMP_BLOCK) // CMP_STRIDE + 1
    ov &= (np.arange(nc) < n_cmp)[:, None]
    return jnp.asarray(ov.T.astype(np.float32), dtype=BF16)


def _gate_expand_matrix():
    ex = np.zeros((LANES, 3 * D_NSA), np.float32)
    for branch in range(3):
        for p in range(D_NSA):
            head = _NSA_HEAD_ORDER[p // NSA_DK]
            ex[branch * NSA_HEADS + head, branch * D_NSA + p] = 1.0
    return jnp.asarray(ex.T, dtype=BF16)


def kernel(x, norm_mix_pre, w_in, ret_gn_w, cmp_k_pos, cmp_k_w1, cmp_k_w2, cmp_v_pos, cmp_v_w1, cmp_v_w2,
           w_out, norm_mix_post, norm_ffn_pre, ffn_w_up, ffn_conv, ffn_w_down, norm_ffn_post):
    B, S, D = x.shape
    depth = w_in.shape[0]
    assert S % SLC_KTILE == 0 and S % FFN_TILE == 0 and S // SLC_BLOCK <= LANES and S >= WINDOW + NSA_QBLOCK
    assert ffn_w_down.shape[1] % FFN_FTILE == 0

    cos, sin = _rope_tables(jnp.arange(S, dtype=jnp.int32))
    nc = S // CMP_STRIDE
    ccos, csin = _rope_tables(jnp.arange(nc, dtype=jnp.int32) * CMP_STRIDE + (CMP_BLOCK - 1))
    cmp_cos = jnp.stack([ccos, jnp.ones_like(ccos)])
    cmp_sin = jnp.stack([csin, jnp.zeros_like(csin)])
    overlap = _overlap_matrix(S)
    expand = _gate_expand_matrix()

    w_in_p = _prep_w_in(w_in)
    w_out_p = _prep_w_out(w_out)
    w_up_p = ffn_w_up.astype(BF16)
    w_down_p = ffn_w_down.astype(BF16)
    cmp_p = _prep_compress(jnp.stack([cmp_k_pos, cmp_v_pos], axis=1), jnp.stack([cmp_k_w1, cmp_v_w1], axis=1),
                           jnp.stack([cmp_k_w2, cmp_v_w2], axis=1))

    x2 = x.reshape(B * S, D)
    for l in range(depth):
        outs = _inproj(x2, norm_mix_pre[l][None], w_in_p, l, cos, sin, S)
        rq, rk, nq, ksl, kwn, rv, vsl, vwn, rg, kvc, ng = outs
        y_ret = _retention(rq, rk, rv, rg, ret_gn_w[l][None], B, S)
        kvc_cmp, kvc_cmp_t = _compress(kvc, *cmp_p, l, cmp_cos, cmp_sin, B, S)
        y_nsa = _nsa(nq, ng, kvc_cmp, kvc_cmp_t, ksl, vsl, kwn, vwn, overlap, expand, B, S)
        x2 = _outproj_ffn(y_ret, y_nsa, x2, w_out_p, norm_mix_post[l][None], norm_ffn_pre[l][None],
                          w_up_p, ffn_conv, w_down_p, l, norm_ffn_post[l][None], S)
    return x2.reshape(B, S, D)
```

```python
import functools
import math

import jax
import jax.numpy as jnp
import numpy as np
from jax import lax
from jax.experimental import pallas as pl
from jax.experimental.pallas import tpu as pltpu

F32 = jnp.float32
BF16 = jnp.bfloat16

LANES = 128
SUBLANES = 8
V7X_VMEM_LIMIT = 56 * 1024 * 1024

ROPE_THETA = 10000.0
NORM_EPS = 1e-6
GN_EPS = 1e-5
NEG = -1e30
SEL_BIG = 1e9

RET_HEADS = 4
RET_DK = 64
RET_DV = 128
RET_CHUNK = 128
NSA_HEADS = 8
NSA_KV_HEADS = 2
NSA_DK = 64
NSA_GROUP = NSA_HEADS // NSA_KV_HEADS
CMP_BLOCK = 32
CMP_STRIDE = 16
CMP_HID = 256
SLC_BLOCK = 64
SLC_TOPK = 16
SLC_LOCAL = 2
WINDOW = 512
NSA_QBLOCK = 128
NSA_TILES_PER_STEP = 4
NSA_GATES = 3 * NSA_HEADS
CONV_WIDTH = 3

NSA_Q_SCALE = NSA_DK ** -0.5 * math.log2(math.e)

D_RET = RET_HEADS * RET_DV
D_NSA = NSA_HEADS * NSA_DK
RET_QK = RET_HEADS * RET_DK
NSA_KV = NSA_KV_HEADS * NSA_DK

COL_RQ = 0
COL_RK = COL_RQ + RET_QK
COL_NQ = COL_RK + RET_QK
COL_KSL = COL_NQ + D_NSA
COL_KWN = COL_KSL + NSA_KV
ROPE_COLS = COL_KWN + NSA_KV
COL_RV = ROPE_COLS
COL_VSL = COL_RV + D_RET
COL_VWN = COL_VSL + NSA_KV
COL_RG = COL_VWN + NSA_KV
COL_KCM = COL_RG + D_RET
COL_VCM = COL_KCM + NSA_KV
COL_NG = COL_VCM + NSA_KV
IN_COLS_PAD = COL_NG + LANES

TOK_TILE = 512
RET_TILE = 512
FFN_TILE = 1024
FFN_FTILE = 256
FFN_HALO = 16
FFN_ROW_CHUNKS = 4
SLC_KTILE = 512
SLC_UNROLL = 4


GELU_C = math.sqrt(2.0 / math.pi)


def _gelu_tanh(x):
    return 0.5 * x * (1.0 + jnp.tanh(GELU_C * (x + 0.044715 * (x * x * x))))


def _rope(p, cos, sin_signed, first_half):
    half = NSA_DK // 2
    partner = jnp.where(first_half, pltpu.roll(p, LANES - half, 1), pltpu.roll(p, half, 1))
    return p * cos + partner * sin_signed


def _split_bf16(x):
    hi = x.astype(BF16)
    lo = (x - hi.astype(F32)).astype(BF16)
    return hi, lo


def _inproj_kernel(x_ref, g_ref, w_ref, cos_ref, sin_ref,
                   rq_ref, rk_ref, nq_ref, ksl_ref, kwn_ref, rv_ref, vsl_ref, vwn_ref,
                   rg_ref, kvc_ref, ng_ref, stage_ref, *, seq_len):
    tm = x_ref.shape[0]
    x = x_ref[...]
    h = (x * lax.rsqrt(jnp.mean(x * x, axis=-1, keepdims=True) + NORM_EPS) * g_ref[...]).astype(BF16)
    cos = cos_ref[...]
    sin = sin_ref[...]
    lane = lax.broadcasted_iota(jnp.int32, (tm, LANES), 1)
    first_half = (lane % NSA_DK) < NSA_DK // 2

    def proj(c0, n):
        return jnp.dot(h, w_ref[:, c0:c0 + n], preferred_element_type=F32)

    def rope_slab(p, i):
        return _rope(p[:, i * LANES:(i + 1) * LANES], cos, sin, first_half)

    def store_rq(p):
        for i in range(RET_QK // LANES):
            rq_ref[:, i * LANES:(i + 1) * LANES] = rope_slab(p, i).astype(BF16)

    def store_rk(p):
        for i in range(RET_QK // LANES):
            rk_ref[:, i * LANES:(i + 1) * LANES] = (rope_slab(p, i) * (RET_DK ** -0.5)).astype(BF16)

    def store_nq(p):
        for i in range(D_NSA // LANES):
            nq_ref[:, i * LANES:(i + 1) * LANES] = (rope_slab(p, i) * NSA_Q_SCALE).astype(BF16)

    def store_keys(p):
        ksl_ref[:, 0:LANES] = rope_slab(p, 0).astype(BF16)
        kwn_ref[...] = rope_slab(p, 1).astype(BF16)
        row = lax.broadcasted_iota(jnp.int32, (tm, LANES), 0)
        pos = (pl.program_id(0) * tm + row) % seq_len
        ksl_ref[:, LANES:2 * LANES] = jnp.where(lane == pos // SLC_BLOCK, 1.0, 0.0).astype(BF16)

    def store_rv(p):
        rv_ref[...] = p.astype(BF16)

    def store_values(p):
        low_half = lane < NSA_DK
        for v_ref, v in ((vsl_ref, p[:, 0:LANES]), (vwn_ref, p[:, LANES:2 * LANES])):
            v_ref[0, 0:LANES, :] = jnp.where(low_half, v, 1.0).T.astype(BF16)
            v_ref[0, LANES:2 * LANES, :] = jnp.where(low_half, 1.0, v).T.astype(BF16)

    def store_rg(p):
        rg_ref[...] = p

    def store_compress_inputs(p):
        for s in range(2):
            stage_ref[...] = p[:, s * LANES:(s + 1) * LANES]
            for l in range(CMP_STRIDE):
                kvc_ref[s, :, l * LANES:(l + 1) * LANES] = stage_ref[
                    pl.ds(l, tm // CMP_STRIDE, stride=CMP_STRIDE), :]

    def store_gates(p):
        ng_ref[...] = p

    stages = [(COL_RQ, RET_QK, store_rq), (COL_RK, RET_QK, store_rk), (COL_NQ, D_NSA, store_nq),
              (COL_KSL, 2 * NSA_KV, store_keys), (COL_RV, D_RET, store_rv),
              (COL_VSL, 2 * NSA_KV, store_values), (COL_RG, D_RET, store_rg),
              (COL_KCM, 2 * NSA_KV, store_compress_inputs), (COL_NG, LANES, store_gates)]
    pending = None
    for c0, n, store in stages:
        p = proj(c0, n)
        if pending is not None:
            pending[1](pending[0])
        pending = (p, store)
    pending[1](pending[0])


def _inproj(x2, gain, w, layer, cos, sin, seq_len):
    T, D = x2.shape
    tm = TOK_TILE
    nt = seq_len // tm
    tok = lambda n: pl.BlockSpec((tm, n), lambda i: (i, 0))
    tok_t = pl.BlockSpec((1, 2 * LANES, tm), lambda i: (i // nt, 0, i % nt))
    out_shape = (
        jax.ShapeDtypeStruct((T, RET_QK), BF16),
        jax.ShapeDtypeStruct((T, RET_QK), BF16),
        jax.ShapeDtypeStruct((T, D_NSA), BF16),
        jax.ShapeDtypeStruct((T, 2 * LANES), BF16),
        jax.ShapeDtypeStruct((T, LANES), BF16),
        jax.ShapeDtypeStruct((T, D_RET), BF16),
        jax.ShapeDtypeStruct((T // seq_len, 2 * LANES, seq_len), BF16),
        jax.ShapeDtypeStruct((T // seq_len, 2 * LANES, seq_len), BF16),
        jax.ShapeDtypeStruct((T, D_RET), F32),
        jax.ShapeDtypeStruct((2, T // CMP_STRIDE, CMP_STRIDE * LANES), F32),
        jax.ShapeDtypeStruct((T, LANES), F32),
    )
    out_specs = (tok(RET_QK), tok(RET_QK), tok(D_NSA), tok(2 * LANES), tok(LANES), tok(D_RET),
                 tok_t, tok_t, tok(D_RET),
                 pl.BlockSpec((2, tm // CMP_STRIDE, CMP_STRIDE * LANES), lambda i: (0, i, 0)), tok(LANES))
    return pl.pallas_call(
        functools.partial(_inproj_kernel, seq_len=seq_len),
        grid=(T // tm,),
        in_specs=[tok(D),
                  pl.BlockSpec((1, D), lambda i: (0, 0)),
                  pl.BlockSpec((None, D, IN_COLS_PAD), lambda i: (layer, 0, 0)),
                  pl.BlockSpec((tm, LANES), lambda i: (i % nt, 0)),
                  pl.BlockSpec((tm, LANES), lambda i: (i % nt, 0))],
        out_specs=out_specs,
        out_shape=out_shape,
        scratch_shapes=[pltpu.VMEM((tm, LANES), F32)],
        compiler_params=pltpu.CompilerParams(dimension_semantics=("arbitrary",),
                                             vmem_limit_bytes=V7X_VMEM_LIMIT),
        name="inproj",
    )(x2, gain, w, cos, sin)


def _retention_kernel(q_ref, k_ref, v_ref, g_ref, gnw_ref, o_ref, state_ref, vbd_ref):
    C = RET_CHUNK
    n_chunks = q_ref.shape[0] // C

    @pl.when(pl.program_id(1) == 0)
    def _():
        state_ref[...] = jnp.zeros_like(state_ref)

    vbd_ref[...] = jnp.zeros_like(vbd_ref)

    ii = lax.broadcasted_iota(jnp.int32, (C, C), 0)
    jj = lax.broadcasted_iota(jnp.int32, (C, C), 1)
    diff = (ii - jj).astype(F32)
    i_col = lax.broadcasted_iota(jnp.int32, (C, 1), 0).astype(F32)
    low_half = lax.broadcasted_iota(jnp.int32, (C, LANES), 1) < RET_DK
    low_cols = lax.broadcasted_iota(jnp.int32, (C, 2 * RET_DV), 1) < RET_DV
    own_block = ((lax.broadcasted_iota(jnp.int32, (LANES, 2 * RET_DV), 0) < RET_DK)
                 == (lax.broadcasted_iota(jnp.int32, (LANES, 2 * RET_DV), 1) < RET_DV))
    nt_dims = (((1,), (1,)), ((), ()))
    tn_dims = (((0,), (0,)), ((), ()))

    for pair in range(RET_HEADS // 2):
        lg0, lg1 = (math.log(1.0 - 2.0 ** (-5.0 - h)) for h in (2 * pair, 2 * pair + 1))
        decay = [jnp.where(diff >= 0, jnp.exp(lg * jnp.maximum(diff, 0.0)), 0.0) for lg in (lg0, lg1)]
        xi = jnp.where(low_cols, jnp.exp(lg0 * (i_col + 1.0)), jnp.exp(lg1 * (i_col + 1.0)))
        zeta = jnp.where(low_half, jnp.exp(lg0 * (C - 1.0 - i_col)), jnp.exp(lg1 * (C - 1.0 - i_col)))
        chunk_decay = jnp.where(low_cols[0:1], math.exp(lg0 * C), math.exp(lg1 * C))
        qk_cols = slice(pair * LANES, (pair + 1) * LANES)
        v_cols = slice(2 * pair * RET_DV, 2 * (pair + 1) * RET_DV)
        for c in range(n_chunks):
            rows = slice(c * C, (c + 1) * C)
            q = q_ref[rows, qk_cols]
            ks = k_ref[rows, qk_cols]
            v = v_ref[rows, v_cols]
            qf = q.astype(F32)
            q_stack = jnp.concatenate([jnp.where(low_half, qf, 0.0), jnp.where(low_half, 0.0, qf)],
                                      axis=0).astype(BF16)
            s = lax.dot_general(q_stack, ks, nt_dims, preferred_element_type=F32)
            s_pair = jnp.concatenate([s[0:C] * decay[0], s[C:2 * C] * decay[1]], axis=1).astype(BF16)
            vbd_ref[pair, 0:C, 0:RET_DV] = v[:, 0:RET_DV]
            vbd_ref[pair, C:2 * C, RET_DV:2 * RET_DV] = v[:, RET_DV:2 * RET_DV]
            o = jnp.dot(s_pair, vbd_ref[pair], preferred_element_type=F32)
            state = state_ref[pair]
            o = o + jnp.dot(q, state.astype(BF16), preferred_element_type=F32) * xi
            kz = (ks.astype(F32) * zeta).astype(BF16)
            kv = lax.dot_general(kz, v, tn_dims, preferred_element_type=F32)
            state_ref[pair] = state * chunk_decay + jnp.where(own_block, kv, 0.0)
            for e in range(2):
                cols = slice((2 * pair + e) * RET_DV, (2 * pair + e + 1) * RET_DV)
                oh = o[:, e * RET_DV:(e + 1) * RET_DV]
                mu = jnp.mean(oh, axis=-1, keepdims=True)
                var = jnp.mean(jnp.square(oh - mu), axis=-1, keepdims=True)
                on = (oh - mu) * lax.rsqrt(var + GN_EPS) * gnw_ref[:, cols]
                gate = g_ref[rows, cols]
                o_ref[rows, cols] = (gate * (1.0 / (1.0 + jnp.exp(-gate))) * on).astype(BF16)


def _retention(rq, rk, rv, rg, gn_w, batch, seq_len):
    T = rq.shape[0]
    tc = RET_TILE
    nt = seq_len // tc
    tok = lambda n: pl.BlockSpec((tc, n), lambda b, i: (b * nt + i, 0))
    return pl.pallas_call(
        _retention_kernel,
        grid=(batch, nt),
        in_specs=[tok(RET_QK), tok(RET_QK), tok(D_RET), tok(D_RET),
                  pl.BlockSpec((1, D_RET), lambda b, i: (0, 0))],
        out_specs=tok(D_RET),
        out_shape=jax.ShapeDtypeStruct((T, D_RET), BF16),
        scratch_shapes=[pltpu.VMEM((RET_HEADS // 2, LANES, 2 * RET_DV), F32),
                        pltpu.VMEM((RET_HEADS // 2, 2 * RET_CHUNK, 2 * RET_DV), BF16)],
        compiler_params=pltpu.CompilerParams(dimension_semantics=("arbitrary", "arbitrary"),
                                             vmem_limit_bytes=V7X_VMEM_LIMIT),
        name="retention",
    )(rq, rk, rv, rg, gn_w)


def _compress_kernel(x_ref, pa_ref, pb_ref, wa_ref, wb_ref, w2_ref, cos_ref, sin_ref, o_ref, ot_ref):
    ng = x_ref.shape[2]
    x = x_ref[0, 0]
    xa = (x + pa_ref[0]).astype(BF16)
    xb = (x + pb_ref[0]).astype(BF16)
    a = jnp.dot(xa, wa_ref[0], preferred_element_type=F32)
    b = jnp.dot(xb, wb_ref[0], preferred_element_type=F32)
    hid = a + pltpu.roll(b, ng - 1, 0)
    out = jnp.dot(_gelu_tanh(hid).astype(BF16), w2_ref[0], preferred_element_type=F32)
    lane = lax.broadcasted_iota(jnp.int32, out.shape, 1)
    out = _rope(out, cos_ref[0], sin_ref[0], (lane % NSA_DK) < NSA_DK // 2)
    o_ref[0, 0] = out.astype(BF16)
    ot_ref[0, 0] = out.T.astype(BF16)


def _compress(kvc, pos_a, pos_b, wa, wb, w2, layer, cos, sin, batch, seq_len):
    ng = seq_len // CMP_STRIDE
    gw = CMP_STRIDE * LANES
    x = kvc.reshape(2, batch, ng, gw)
    hid = NSA_KV_HEADS * CMP_HID
    per_kv = lambda *shape: pl.BlockSpec((1,) + shape, lambda s, b: (s,) + (0,) * len(shape))
    per_lkv = lambda *shape: pl.BlockSpec((None, 1) + shape, lambda s, b: (layer, s) + (0,) * len(shape))
    return pl.pallas_call(
        _compress_kernel,
        grid=(2, batch),
        in_specs=[pl.BlockSpec((1, 1, ng, gw), lambda s, b: (s, b, 0, 0)),
                  per_lkv(1, gw), per_lkv(1, gw), per_lkv(gw, hid), per_lkv(gw, hid), per_lkv(hid, LANES),
                  per_kv(ng, LANES), per_kv(ng, LANES)],
        out_specs=(pl.BlockSpec((1, 1, ng, LANES), lambda s, b: (s, b, 0, 0)),
                   pl.BlockSpec((1, 1, LANES, ng), lambda s, b: (s, b, 0, 0))),
        out_shape=(jax.ShapeDtypeStruct((2, batch, ng, LANES), BF16),
                   jax.ShapeDtypeStruct((2, batch, LANES, ng), BF16)),
        compiler_params=pltpu.CompilerParams(dimension_semantics=("arbitrary", "arbitrary"),
                                             vmem_limit_bytes=V7X_VMEM_LIMIT),
        name="compress",
    )(x, pos_a, pos_b, wa, wb, w2, cos, sin)


def _nsa_kernel(nq_ref, ng_ref, *refs, seq_len, topk):
    o_ref = refs[8]

    def one_tile(j, carry):
        rows = pl.ds(pl.multiple_of(j * NSA_QBLOCK, NSA_QBLOCK), NSA_QBLOCK)
        t0 = (pl.program_id(1) * NSA_TILES_PER_STEP + j) * NSA_QBLOCK
        _nsa_tile(t0, nq_ref.at[rows], ng_ref.at[rows], *refs[:8], o_ref.at[rows], *refs[9:],
                  seq_len=seq_len, topk=topk)
        return carry

    lax.fori_loop(0, NSA_TILES_PER_STEP, one_tile, 0)


def _nsa_tile(t0, nq_ref, ng_ref, kc_ref, vc_ref, ksl_ref, vsl_ref, kwn_ref, vwn_ref, ov_ref, ex_ref,
              o_ref, qaug_ref, s_ref, m_ref, acc_ref, out_ref, *, seq_len, topk):
    QB = NSA_QBLOCK
    R = NSA_GROUP
    G = NSA_KV_HEADS
    M = R * QB
    NC = kc_ref.shape[2]
    KT = SLC_KTILE
    WK = WINDOW + QB
    nt_dims = (((1,), (1,)), ((), ()))

    low_half = lax.broadcasted_iota(jnp.int32, (QB, LANES), 1) < NSA_DK
    q = nq_ref[...].astype(F32)

    def per_head(x):
        return jnp.concatenate([x] * R, axis=1)

    def normalized(acc, g):
        num, den = (acc[0:NSA_DK], acc[NSA_DK:NSA_DK + 1]) if g == 0 else (acc[NSA_DK:], acc[0:1])
        return num * (1.0 / den)

    for g in range(G):
        head_lanes = low_half if g == 0 else jnp.logical_not(low_half)
        for i in range(R):
            qaug_ref[g, i * QB:(i + 1) * QB, 0:LANES] = jnp.where(
                head_lanes, q[:, i * LANES:(i + 1) * LANES], 0.0).astype(BF16)

    ks = pl.multiple_of(jnp.clip(t0 - WINDOW, 0, seq_len - WK), QB)
    k_pos = ks + lax.broadcasted_iota(jnp.int32, (WK, QB), 0)
    t_pos = t0 + lax.broadcasted_iota(jnp.int32, (WK, QB), 1)
    win_mask = per_head((k_pos <= t_pos) & (k_pos > t_pos - WINDOW))
    jj = lax.broadcasted_iota(jnp.int32, (LANES, QB), 0)
    q_blk = (t0 + lax.broadcasted_iota(jnp.int32, (LANES, QB), 1)) // SLC_BLOCK
    valid = jj <= q_blk
    forced = (jj == 0) | (valid & (jj > q_blk - SLC_LOCAL))
    jf = jj.astype(F32)

    def pick_round(score):
        best = jnp.max(score, axis=0, keepdims=True)
        first = jnp.min(jnp.where(score == best, jf, float(LANES)), axis=0, keepdims=True)
        return jnp.where(jf == first, -jnp.inf, score)

    def slc_scores(k0, slot):
        for g in range(G):
            s_ref[slot, g] = lax.dot_general(ksl_ref[0, pl.ds(k0, KT), :], qaug_ref[g], nt_dims,
                                             preferred_element_type=F32)

    def compressed_select_window(nc):
        win_scores = [lax.dot_general(kwn_ref[0, pl.ds(ks, WK), :], qaug_ref[g, :, 0:LANES], nt_dims,
                                      preferred_element_type=F32) for g in range(G)]
        cmp_scores = [lax.dot_general(kc_ref[0, 0, 0:nc, :], qaug_ref[g, :, 0:LANES], nt_dims,
                                      preferred_element_type=F32) for g in range(G)]

        scores = []
        n_idx = lax.broadcasted_iota(jnp.int32, (nc, QB), 0)
        t_col = t0 + lax.broadcasted_iota(jnp.int32, (nc, QB), 1)
        cmp_mask = per_head(n_idx * CMP_STRIDE + (CMP_BLOCK - 1) <= t_col)
        for g in range(G):
            sc = jnp.where(cmp_mask, cmp_scores[g], NEG)
            e = jnp.exp2(sc - jnp.max(sc, axis=0, keepdims=True))
            if nc == LANES:
                e = jnp.where(cmp_mask, e, 0.0)
            p_cmp = e * (1.0 / jnp.maximum(jnp.sum(e, axis=0, keepdims=True), 1e-30))
            acc = jnp.dot(vc_ref[0, 0, :, 0:nc], p_cmp.astype(BF16), preferred_element_type=F32)
            out_ref[0, g * NSA_DK:(g + 1) * NSA_DK, :] = acc[g * NSA_DK:(g + 1) * NSA_DK]
            p_sum = p_cmp[:, 0:QB]
            for i in range(1, R):
                p_sum = p_sum + p_cmp[:, i * QB:(i + 1) * QB]
            p_hi, p_lo = _split_bf16(p_sum)
            imp = (jnp.dot(ov_ref[:, 0:nc], p_hi, preferred_element_type=F32)
                   + jnp.dot(ov_ref[:, 0:nc], p_lo, preferred_element_type=F32))
            scores.append(jnp.where(forced, -jnp.inf, jnp.where(valid, imp, -SEL_BIG)))

        for _ in range(topk - (1 + SLC_LOCAL)):
            scores = [pick_round(score) for score in scores]
        for g in range(G):
            selected = ((scores[g] == -jnp.inf) | (q_blk < topk)) & valid
            bias = jnp.where(selected, 0.0, NEG).T.astype(BF16)
            for i in range(R):
                qaug_ref[g, i * QB:(i + 1) * QB, LANES:2 * LANES] = bias
        slc_scores(0, 0)

        for g in range(G):
            sw = jnp.where(win_mask, win_scores[g], NEG)
            e_win = jnp.exp2(sw - jnp.max(sw, axis=0, keepdims=True)).astype(BF16)
            pv = jnp.dot(vwn_ref[0, g * LANES:(g + 1) * LANES, pl.ds(ks, WK)], e_win,
                         preferred_element_type=F32)
            out_ref[1, g * NSA_DK:(g + 1) * NSA_DK, :] = normalized(pv, g)

    n_visible = (t0 + QB - CMP_BLOCK) // CMP_STRIDE + 1
    n_variants = NC // LANES
    for v in range(n_variants):
        @pl.when(jnp.minimum((n_visible - 1) // LANES, n_variants - 1) == v)
        def _():
            compressed_select_window((v + 1) * LANES)

    def slc_accumulate(k0, slot, causal):
        for g in range(G):
            s = s_ref[slot, g]
            if causal:
                k_pos = k0 + lax.broadcasted_iota(jnp.int32, (KT, QB), 0)
                t_pos = t0 + lax.broadcasted_iota(jnp.int32, (KT, QB), 1)
                s = jnp.where(per_head(k_pos <= t_pos), s, NEG)
            m_run = m_ref[g]
            m_new = jnp.maximum(m_run, jnp.max(s, axis=0, keepdims=True))
            p = jnp.exp2(s - m_new).astype(BF16)
            pv = jnp.dot(vsl_ref[0, g * LANES:(g + 1) * LANES, pl.ds(k0, KT)], p,
                         preferred_element_type=F32)
            acc_ref[g] = jnp.exp2(m_run - m_new) * acc_ref[g] + pv
            m_ref[g] = m_new

    m_ref[...] = jnp.full(m_ref.shape, NEG, F32)
    acc_ref[...] = jnp.zeros_like(acc_ref)
    n_full = t0 // KT
    U = SLC_UNROLL

    def slc_run(k0, n_tiles, last_is_diagonal):
        for u in range(n_tiles):
            causal = last_is_diagonal and u == n_tiles - 1
            if not causal:
                slc_scores(k0 + (u + 1) * KT, (u + 1) % 2)
            slc_accumulate(k0 + u * KT, u % 2, causal)

    def slc_group(j, _):
        slc_run(pl.multiple_of(j * (U * KT), U * KT), U, False)
        return 0

    lax.fori_loop(0, n_full // U, slc_group, 0)
    k_rest = pl.multiple_of((n_full // U) * (U * KT), U * KT)
    for r in range(U):
        @pl.when(n_full % U == r)
        def _():
            slc_run(k_rest, r + 1, True)

    o_slc = [normalized(acc_ref[g], g) for g in range(G)]

    g_hi, g_lo = _split_bf16((1.0 / (1.0 + jnp.exp(-ng_ref[...]))).T)
    gates = (jnp.dot(ex_ref[...], g_hi, preferred_element_type=F32)
             + jnp.dot(ex_ref[...], g_lo, preferred_element_type=F32))
    branches = [out_ref[0], jnp.concatenate(o_slc, axis=0), out_ref[1]]
    for i in range(R):
        y = jnp.zeros((LANES, QB), F32)
        for b, o in enumerate(branches):
            y = y + gates[b * D_NSA + i * LANES:b * D_NSA + (i + 1) * LANES] * o[:, i * QB:(i + 1) * QB]
        o_ref[:, i * LANES:(i + 1) * LANES] = y.T.astype(BF16)


def _nsa(nq, ng, kvc_cmp, kvc_cmp_t, ksl, vsl, kwn, vwn, overlap, expand, batch, seq_len):
    T = nq.shape[0]
    QB = NSA_QBLOCK
    rows = QB * NSA_TILES_PER_STEP
    nqb = seq_len // rows
    nc = seq_len // CMP_STRIDE
    topk = min(SLC_TOPK, seq_len // SLC_BLOCK)
    tok = lambda n: pl.BlockSpec((rows, n), lambda b, i: (b * nqb + i, 0))
    seq = lambda n: pl.BlockSpec((1, seq_len, n), lambda b, i: (b, 0, 0))
    seq_t = pl.BlockSpec((1, 2 * LANES, seq_len), lambda b, i: (b, 0, 0))
    return pl.pallas_call(
        functools.partial(_nsa_kernel, seq_len=seq_len, topk=topk),
        grid=(batch, nqb),
        in_specs=[tok(D_NSA), tok(LANES),
                  pl.BlockSpec((1, 1, nc, LANES), lambda b, i: (0, b, 0, 0)),
                  pl.BlockSpec((1, 1, LANES, nc), lambda b, i: (1, b, 0, 0)),
                  seq(2 * LANES), seq_t, seq(LANES), seq_t,
                  pl.BlockSpec(overlap.shape, lambda b, i: (0, 0)),
                  pl.BlockSpec(expand.shape, lambda b, i: (0, 0))],
        out_specs=tok(D_NSA),
        out_shape=jax.ShapeDtypeStruct((T, D_NSA), BF16),
        scratch_shapes=[pltpu.VMEM((NSA_KV_HEADS, NSA_GROUP * QB, 2 * LANES), BF16),
                        pltpu.VMEM((2, NSA_KV_HEADS, SLC_KTILE, NSA_GROUP * QB), F32),
                        pltpu.VMEM((NSA_KV_HEADS, 1, NSA_GROUP * QB), F32),
                        pltpu.VMEM((NSA_KV_HEADS, LANES, NSA_GROUP * QB), F32),
                        pltpu.VMEM((2, LANES, NSA_GROUP * QB), F32)],
        compiler_params=pltpu.CompilerParams(dimension_semantics=("arbitrary", "arbitrary"),
                                             vmem_limit_bytes=V7X_VMEM_LIMIT),
        name="nsa",
    )(nq, ng, kvc_cmp, kvc_cmp_t, ksl.reshape(batch, seq_len, -1), vsl,
      kwn.reshape(batch, seq_len, -1), vwn, overlap, expand)


def _ffn_kernel(yr_ref, yrp_ref, yn_ref, ynp_ref, x_ref, xp_ref, wo_ref, gmix_ref, gpre_ref,
                wu_ref, conv_ref, wd_ref, gpost_ref, o_ref,
                h_ref, ug_ref, uv_ref, acc_ref, xmid_ref, *, tiles_per_seq):
    tm = x_ref.shape[0]
    d_ff = wd_ref.shape[0]
    tf = FFN_FTILE
    H = FFN_HALO
    rc = tm // FFN_ROW_CHUNKS

    def rms(x, gain_ref):
        return x * lax.rsqrt(jnp.mean(x * x, axis=-1, keepdims=True) + NORM_EPS) * gain_ref[...]

    keep = jnp.where(pl.program_id(0) % tiles_per_seq == 0, 0.0, 1.0)
    for c in range(FFN_ROW_CHUNKS):
        rows = slice(c * rc, (c + 1) * rc)
        y_ret, y_nsa = yr_ref[rows, :], yn_ref[rows, :]
        if c == 0:
            y_ret = jnp.concatenate([yrp_ref[...], y_ret], axis=0)
            y_nsa = jnp.concatenate([ynp_ref[...], y_nsa], axis=0)
        mix = (jnp.dot(y_ret, wo_ref[0:D_RET, :], preferred_element_type=F32)
               + jnp.dot(y_nsa, wo_ref[D_RET:D_RET + D_NSA, :], preferred_element_type=F32))
        post = rms(mix, gmix_ref)
        if c == 0:
            h_ref[0:H, :] = (rms(xp_ref[...] + post[0:H], gpre_ref) * keep).astype(BF16)
            post = post[H:H + rc]
        x_mid = x_ref[rows, :] + post
        xmid_ref[rows, :] = x_mid
        h_ref[H + c * rc:H + (c + 1) * rc, :] = rms(x_mid, gpre_ref).astype(BF16)
    acc_ref[...] = jnp.zeros_like(acc_ref)

    def hidden_tile(f, carry):
        gate_cols = pl.ds(pl.multiple_of(f * tf, tf), tf)
        value_cols = pl.ds(pl.multiple_of(d_ff + f * tf, tf), tf)

        def up_project(c):
            rows = slice(0 if c == 0 else H + c * rc, H + (c + 1) * rc)
            h = h_ref[rows, :]
            ug_ref[rows, :] = jnp.dot(h, wu_ref[:, gate_cols], preferred_element_type=F32)
            uv_ref[rows, :] = jnp.dot(h, wu_ref[:, value_cols], preferred_element_type=F32)

        def causal_conv(u_ref, cols, r0, scale):
            out = (scale * conv_ref[CONV_WIDTH - 1:CONV_WIDTH, cols]) * u_ref[r0:r0 + rc, :]
            for k in range(CONV_WIDTH - 1):
                d = CONV_WIDTH - 1 - k
                out = out + (scale * conv_ref[k:k + 1, cols]) * u_ref[r0 - d:r0 - d + rc, :]
            return out

        def gate_and_down(c):
            g = causal_conv(ug_ref, gate_cols, H + c * rc, 1.0)
            v_half = causal_conv(uv_ref, value_cols, H + c * rc, 0.5)
            inner = g * (GELU_C + (GELU_C * 0.044715) * (g * g))
            act = (g + g * jnp.tanh(inner)) * v_half
            acc_ref[c * rc:(c + 1) * rc, :] += jnp.dot(act.astype(BF16), wd_ref[gate_cols, :],
                                                       preferred_element_type=F32)

        up_project(0)
        for c in range(FFN_ROW_CHUNKS):
            if c + 1 < FFN_ROW_CHUNKS:
                up_project(c + 1)
            gate_and_down(c)
        return carry

    lax.fori_loop(0, d_ff // tf, hidden_tile, 0)
    o_ref[...] = xmid_ref[...] + rms(acc_ref[...], gpost_ref)


def _outproj_ffn(y_ret, y_nsa, x2, w_out, g_mix, g_pre, w_up, conv_w, w_down, layer, g_post, seq_len):
    T, D = x2.shape
    d_ff = w_down.shape[1]
    tm, tf, H = FFN_TILE, FFN_FTILE, FFN_HALO
    tile = lambda n: pl.BlockSpec((tm, n), lambda i: (i, 0))
    halo = lambda n: pl.BlockSpec((H, n), lambda i: (jnp.maximum(i * (tm // H) - 1, 0), 0))
    resident = lambda a: pl.BlockSpec((None,) + a.shape[1:], lambda i: (layer, 0, 0),
                                      pipeline_mode=pl.Buffered(1))
    gain = pl.BlockSpec((1, D), lambda i: (0, 0), pipeline_mode=pl.Buffered(1))
    return pl.pallas_call(
        functools.partial(_ffn_kernel, tiles_per_seq=seq_len // tm),
        grid=(T // tm,),
        in_specs=[tile(D_RET), halo(D_RET), tile(D_NSA), halo(D_NSA), tile(D), halo(D),
                  resident(w_out), gain, gain, resident(w_up), resident(conv_w), resident(w_down), gain],
        out_specs=tile(D),
        out_shape=jax.ShapeDtypeStruct((T, D), F32),
        scratch_shapes=[pltpu.VMEM((tm + H, D), BF16), pltpu.VMEM((tm + H, tf), F32),
                        pltpu.VMEM((tm + H, tf), F32), pltpu.VMEM((tm, D), F32),
                        pltpu.VMEM((tm, D), F32)],
        compiler_params=pltpu.CompilerParams(dimension_semantics=("arbitrary",),
                                             vmem_limit_bytes=V7X_VMEM_LIMIT),
        name="ffn",
    )(y_ret, y_ret, y_nsa, y_nsa, x2, x2, w_out, g_mix, g_pre, w_up, conv_w, w_down, g_post)


_NSA_HEAD_ORDER = [g * NSA_GROUP + i for i in range(NSA_GROUP) for g in range(NSA_KV_HEADS)]


def _rope_tables(pos):
    inv = 1.0 / (ROPE_THETA ** (jnp.arange(0, NSA_DK, 2, dtype=F32) / NSA_DK))
    ang = pos.astype(F32)[:, None] * inv[None, :]
    c, s = jnp.cos(ang), jnp.sin(ang)
    return jnp.concatenate([c, c, c, c], axis=1), jnp.concatenate([-s, s, -s, s], axis=1)


def _prep_w_in(w):
    lead = w.shape[:-1]
    splits = np.cumsum([RET_QK, RET_QK, D_RET, D_RET, D_NSA] + [NSA_KV] * 6)
    rq, rk, rv, rg, nq, kcm, vcm, ksl, vsl, kwn, vwn, ng = jnp.split(w, [int(s) for s in splits], axis=-1)
    nq = nq.reshape(lead + (NSA_HEADS, NSA_DK))[..., np.array(_NSA_HEAD_ORDER), :].reshape(lead + (D_NSA,))
    ng = jnp.pad(ng, [(0, 0)] * len(lead) + [(0, LANES - NSA_GATES)])
    return jnp.concatenate([rq, rk, nq, ksl, kwn, rv, vsl, vwn, rg, kcm, vcm, ng], axis=-1).astype(BF16)


def _prep_w_out(w):
    layers, _, d = w.shape
    w_nsa = w[:, D_RET:].reshape(layers, NSA_HEADS, NSA_DK, d)[:, np.array(_NSA_HEAD_ORDER)]
    return jnp.concatenate([w[:, :D_RET], w_nsa.reshape(layers, D_NSA, d)], axis=1).astype(BF16)


def _prep_compress(pos, w1, w2):
    lead = w1.shape[:-2]
    half = CMP_STRIDE

    def block_diag(w, axis):
        z = jnp.zeros_like(w)
        return jnp.stack([jnp.concatenate([w, z], axis=-1), jnp.concatenate([z, w], axis=-1)], axis=axis)

    def first_layer(w_half):
        w4 = w_half.reshape(lead + (half, NSA_DK, CMP_HID))
        return block_diag(w4, -3).reshape(lead + (half * NSA_KV_HEADS * NSA_DK, -1)).astype(BF16)

    def pos_row(p_half):
        rows = jnp.broadcast_to(p_half[..., :, None, :], lead + (half, NSA_KV_HEADS, NSA_DK))
        return rows.reshape(lead + (1, -1))

    w2x = block_diag(w2, -3).reshape(lead + (NSA_KV_HEADS * CMP_HID, -1)).astype(BF16)
    n1 = half * NSA_DK
    return (pos_row(pos[..., :half, :]), pos_row(pos[..., half:, :]),
            first_layer(w1[..., :n1, :]), first_layer(w1[..., n1:, :]), w2x)


def _overlap_matrix(seq_len):
    nc = seq_len // CMP_STRIDE
    cmp_start = np.arange(nc) * CMP_STRIDE
    slc_start = np.arange(LANES) * SLC_BLOCK
    ov = ((cmp_start[:, None] < slc_start[None, :] + SLC_BLOCK)
          & (cmp_start[:, None] + CMP_BLOCK > slc_start[None, :]))
    n_cmp = (seq_len - CMP_BLOCK) // CMP_STRIDE + 1
    ov &= (np.arange(nc) < n_cmp)[:, None]
    return jnp.asarray(ov.T.astype(np.float32), dtype=BF16)


def _gate_expand_matrix():
    ex = np.zeros((LANES, 3 * D_NSA), np.float32)
    for branch in range(3):
        for p in range(D_NSA):
            head = _NSA_HEAD_ORDER[p // NSA_DK]
            ex[branch * NSA_HEADS + head, branch * D_NSA + p] = 1.0
    return jnp.asarray(ex.T, dtype=BF16)


def kernel(x, norm_mix_pre, w_in, ret_gn_w, cmp_k_pos, cmp_k_w1, cmp_k_w2, cmp_v_pos, cmp_v_w1, cmp_v_w2,
           w_out, norm_mix_post, norm_ffn_pre, ffn_w_up, ffn_conv, ffn_w_down, norm_ffn_post):
    B, S, D = x.shape
    depth = w_in.shape[0]
    assert S % SLC_KTILE == 0 and S % FFN_TILE == 0 and S // SLC_BLOCK <= LANES and S >= WINDOW + NSA_QBLOCK
    assert ffn_w_down.shape[1] % FFN_FTILE == 0
    assert min(SLC_TOPK, S // SLC_BLOCK) > 1 + SLC_LOCAL and S % (NSA_QBLOCK * NSA_TILES_PER_STEP) == 0

    cos, sin = _rope_tables(jnp.arange(S, dtype=jnp.int32))
    nc = S // CMP_STRIDE
    ccos, csin = _rope_tables(jnp.arange(nc, dtype=jnp.int32) * CMP_STRIDE + (CMP_BLOCK - 1))
    cmp_cos = jnp.stack([ccos, jnp.ones_like(ccos)])
    cmp_sin = jnp.stack([csin, jnp.zeros_like(csin)])
    overlap = _overlap_matrix(S)
    expand = _gate_expand_matrix()

    w_in_p = _prep_w_in(w_in)
    w_out_p = _prep_w_out(w_out)
    w_up_p = ffn_w_up.astype(BF16)
    w_down_p = ffn_w_down.astype(BF16)
    cmp_p = _prep_compress(jnp.stack([cmp_k_pos, cmp_v_pos], axis=1), jnp.stack([cmp_k_w1, cmp_v_w1], axis=1),
                           jnp.stack([cmp_k_w2, cmp_v_w2], axis=1))

    x2 = x.reshape(B * S, D)
    for l in range(depth):
        outs = _inproj(x2, norm_mix_pre[l][None], w_in_p, l, cos, sin, S)
        rq, rk, nq, ksl, kwn, rv, vsl, vwn, rg, kvc, ng = outs
        y_ret = _retention(rq, rk, rv, rg, ret_gn_w[l][None], B, S)
        kvc_cmp, kvc_cmp_t = _compress(kvc, *cmp_p, l, cmp_cos, cmp_sin, B, S)
        y_nsa = _nsa(nq, ng, kvc_cmp, kvc_cmp_t, ksl, vsl, kwn, vwn, overlap, expand, B, S)
        x2 = _outproj_ffn(y_ret, y_nsa, x2, w_out_p, norm_mix_post[l][None], norm_ffn_pre[l][None],
                          w_up_p, ffn_conv, w_down_p, l, norm_ffn_post[l][None], S)
    return x2.reshape(B, S, D)
```

```python
import functools
import math

import jax
import jax.numpy as jnp
import numpy as np
from jax import lax
from jax.experimental import pallas as pl
from jax.experimental.pallas import tpu as pltpu

F32 = jnp.float32
BF16 = jnp.bfloat16

LANES = 128
SUBLANES = 8
V7X_VMEM_LIMIT = 56 * 1024 * 1024

ROPE_THETA = 10000.0
NORM_EPS = 1e-6
GN_EPS = 1e-5
NEG = -1e30
SEL_BIG = 1e9

RET_HEADS = 4
RET_DK = 64
RET_DV = 128
RET_CHUNK = 128
NSA_HEADS = 8
NSA_KV_HEADS = 2
NSA_DK = 64
NSA_GROUP = NSA_HEADS // NSA_KV_HEADS
CMP_BLOCK = 32
CMP_STRIDE = 16
CMP_HID = 256
SLC_BLOCK = 64
SLC_TOPK = 16
SLC_LOCAL = 2
WINDOW = 512
NSA_QBLOCK = 128
NSA_TILES_PER_STEP = 4
NSA_GATES = 3 * NSA_HEADS
CONV_WIDTH = 3

NSA_Q_SCALE = NSA_DK ** -0.5 * math.log2(math.e)

D_RET = RET_HEADS * RET_DV
D_NSA = NSA_HEADS * NSA_DK
RET_QK = RET_HEADS * RET_DK
NSA_KV = NSA_KV_HEADS * NSA_DK

COL_RQ = 0
COL_RK = COL_RQ + RET_QK
COL_NQ = COL_RK + RET_QK
COL_KSL = COL_NQ + D_NSA
COL_KWN = COL_KSL + NSA_KV
ROPE_COLS = COL_KWN + NSA_KV
COL_RV = ROPE_COLS
COL_VSL = COL_RV + D_RET
COL_VWN = COL_VSL + NSA_KV
COL_RG = COL_VWN + NSA_KV
COL_KCM = COL_RG + D_RET
COL_VCM = COL_KCM + NSA_KV
COL_NG = COL_VCM + NSA_KV
IN_COLS_PAD = COL_NG + LANES

TOK_TILE = 512
RET_TILE = 512
FFN_TILE = 1024
FFN_FTILE = 256
FFN_HALO = 16
FFN_ROW_CHUNKS = 4
SLC_KTILE = 512
SLC_UNROLL = 4


GELU_C = math.sqrt(2.0 / math.pi)


def _gelu_tanh(x):
    return 0.5 * x * (1.0 + jnp.tanh(GELU_C * (x + 0.044715 * (x * x * x))))


def _rope(p, cos, sin_signed, first_half):
    half = NSA_DK // 2
    partner = jnp.where(first_half, pltpu.roll(p, LANES - half, 1), pltpu.roll(p, half, 1))
    return p * cos + partner * sin_signed


def _split_bf16(x):
    hi = x.astype(BF16)
    lo = (x - hi.astype(F32)).astype(BF16)
    return hi, lo


def _inproj_kernel(x_ref, g_ref, w_ref, cos_ref, sin_ref,
                   rq_ref, rk_ref, nq_ref, ksl_ref, kwn_ref, rv_ref, vsl_ref, vwn_ref,
                   rg_ref, kvc_ref, ng_ref, stage_ref, *, seq_len):
    tm = x_ref.shape[0]
    x = x_ref[...]
    h = (x * lax.rsqrt(jnp.mean(x * x, axis=-1, keepdims=True) + NORM_EPS) * g_ref[...]).astype(BF16)
    cos = cos_ref[...]
    sin = sin_ref[...]
    lane = lax.broadcasted_iota(jnp.int32, (tm, LANES), 1)
    first_half = (lane % NSA_DK) < NSA_DK // 2

    def proj(c0, n):
        return jnp.dot(h, w_ref[:, c0:c0 + n], preferred_element_type=F32)

    def rope_slab(p, i):
        return _rope(p[:, i * LANES:(i + 1) * LANES], cos, sin, first_half)

    def store_rq(p):
        for i in range(RET_QK // LANES):
            rq_ref[:, i * LANES:(i + 1) * LANES] = rope_slab(p, i).astype(BF16)

    def store_rk(p):
        for i in range(RET_QK // LANES):
            rk_ref[:, i * LANES:(i + 1) * LANES] = (rope_slab(p, i) * (RET_DK ** -0.5)).astype(BF16)

    def store_nq(p):
        for i in range(D_NSA // LANES):
            nq_ref[:, i * LANES:(i + 1) * LANES] = (rope_slab(p, i) * NSA_Q_SCALE).astype(BF16)

    def store_keys(p):
        ksl_ref[:, 0:LANES] = rope_slab(p, 0).astype(BF16)
        kwn_ref[...] = rope_slab(p, 1).astype(BF16)
        row = lax.broadcasted_iota(jnp.int32, (tm, LANES), 0)
        pos = (pl.program_id(0) * tm + row) % seq_len
        ksl_ref[:, LANES:2 * LANES] = jnp.where(lane == pos // SLC_BLOCK, 1.0, 0.0).astype(BF16)

    def store_rv(p):
        rv_ref[...] = p.astype(BF16)

    def store_values(p):
        low_half = lane < NSA_DK
        for v_ref, v in ((vsl_ref, p[:, 0:LANES]), (vwn_ref, p[:, LANES:2 * LANES])):
            v_ref[0, 0:LANES, :] = jnp.where(low_half, v, 1.0).T.astype(BF16)
            v_ref[0, LANES:2 * LANES, :] = jnp.where(low_half, 1.0, v).T.astype(BF16)

    def store_rg(p):
        rg_ref[...] = p

    def store_compress_inputs(p):
        for s in range(2):
            stage_ref[...] = p[:, s * LANES:(s + 1) * LANES]
            for l in range(CMP_STRIDE):
                kvc_ref[s, :, l * LANES:(l + 1) * LANES] = stage_ref[
                    pl.ds(l, tm // CMP_STRIDE, stride=CMP_STRIDE), :]

    def store_gates(p):
        ng_ref[...] = p

    stages = [(COL_RQ, RET_QK, store_rq), (COL_RK, RET_QK, store_rk), (COL_NQ, D_NSA, store_nq),
              (COL_KSL, 2 * NSA_KV, store_keys), (COL_RV, D_RET, store_rv),
              (COL_VSL, 2 * NSA_KV, store_values), (COL_RG, D_RET, store_rg),
              (COL_KCM, 2 * NSA_KV, store_compress_inputs), (COL_NG, LANES, store_gates)]
    pending = None
    for c0, n, store in stages:
        p = proj(c0, n)
        if pending is not None:
            pending[1](pending[0])
        pending = (p, store)
    pending[1](pending[0])


def _inproj(x2, gain, w, layer, cos, sin, seq_len):
    T, D = x2.shape
    tm = TOK_TILE
    nt = seq_len // tm
    tok = lambda n: pl.BlockSpec((tm, n), lambda i: (i, 0))
    tok_t = pl.BlockSpec((1, 2 * LANES, tm), lambda i: (i // nt, 0, i % nt))
    out_shape = (
        jax.ShapeDtypeStruct((T, RET_QK), BF16),
        jax.ShapeDtypeStruct((T, RET_QK), BF16),
        jax.ShapeDtypeStruct((T, D_NSA), BF16),
        jax.ShapeDtypeStruct((T, 2 * LANES), BF16),
        jax.ShapeDtypeStruct((T, LANES), BF16),
        jax.ShapeDtypeStruct((T, D_RET), BF16),
        jax.ShapeDtypeStruct((T // seq_len, 2 * LANES, seq_len), BF16),
        jax.ShapeDtypeStruct((T // seq_len, 2 * LANES, seq_len), BF16),
        jax.ShapeDtypeStruct((T, D_RET), F32),
        jax.ShapeDtypeStruct((2, T // CMP_STRIDE, CMP_STRIDE * LANES), F32),
        jax.ShapeDtypeStruct((T, LANES), F32),
    )
    out_specs = (tok(RET_QK), tok(RET_QK), tok(D_NSA), tok(2 * LANES), tok(LANES), tok(D_RET),
                 tok_t, tok_t, tok(D_RET),
                 pl.BlockSpec((2, tm // CMP_STRIDE, CMP_STRIDE * LANES), lambda i: (0, i, 0)), tok(LANES))
    return pl.pallas_call(
        functools.partial(_inproj_kernel, seq_len=seq_len),
        grid=(T // tm,),
        in_specs=[tok(D),
                  pl.BlockSpec((1, D), lambda i: (0, 0)),
                  pl.BlockSpec((None, D, IN_COLS_PAD), lambda i: (layer, 0, 0)),
                  pl.BlockSpec((tm, LANES), lambda i: (i % nt, 0)),
                  pl.BlockSpec((tm, LANES), lambda i: (i % nt, 0))],
        out_specs=out_specs,
        out_shape=out_shape,
        scratch_shapes=[pltpu.VMEM((tm, LANES), F32)],
        compiler_params=pltpu.CompilerParams(dimension_semantics=("arbitrary",),
                                             vmem_limit_bytes=V7X_VMEM_LIMIT),
        name="inproj",
    )(x2, gain, w, cos, sin)


def _retention_kernel(q_ref, k_ref, v_ref, g_ref, gnw_ref, o_ref, state_ref, vbd_ref):
    C = RET_CHUNK
    n_chunks = q_ref.shape[0] // C

    @pl.when(pl.program_id(1) == 0)
    def _():
        state_ref[...] = jnp.zeros_like(state_ref)

    vbd_ref[...] = jnp.zeros_like(vbd_ref)

    ii = lax.broadcasted_iota(jnp.int32, (C, C), 0)
    jj = lax.broadcasted_iota(jnp.int32, (C, C), 1)
    diff = (ii - jj).astype(F32)
    i_col = lax.broadcasted_iota(jnp.int32, (C, 1), 0).astype(F32)
    low_half = lax.broadcasted_iota(jnp.int32, (C, LANES), 1) < RET_DK
    low_cols = lax.broadcasted_iota(jnp.int32, (C, 2 * RET_DV), 1) < RET_DV
    own_block = ((lax.broadcasted_iota(jnp.int32, (LANES, 2 * RET_DV), 0) < RET_DK)
                 == (lax.broadcasted_iota(jnp.int32, (LANES, 2 * RET_DV), 1) < RET_DV))
    nt_dims = (((1,), (1,)), ((), ()))
    tn_dims = (((0,), (0,)), ((), ()))

    for pair in range(RET_HEADS // 2):
        lg0, lg1 = (math.log(1.0 - 2.0 ** (-5.0 - h)) for h in (2 * pair, 2 * pair + 1))
        decay = [jnp.where(diff >= 0, jnp.exp(lg * jnp.maximum(diff, 0.0)), 0.0) for lg in (lg0, lg1)]
        xi = jnp.where(low_cols, jnp.exp(lg0 * (i_col + 1.0)), jnp.exp(lg1 * (i_col + 1.0)))
        zeta = jnp.where(low_half, jnp.exp(lg0 * (C - 1.0 - i_col)), jnp.exp(lg1 * (C - 1.0 - i_col)))
        chunk_decay = jnp.where(low_cols[0:1], math.exp(lg0 * C), math.exp(lg1 * C))
        qk_cols = slice(pair * LANES, (pair + 1) * LANES)
        v_cols = slice(2 * pair * RET_DV, 2 * (pair + 1) * RET_DV)
        for c in range(n_chunks):
            rows = slice(c * C, (c + 1) * C)
            q = q_ref[rows, qk_cols]
            ks = k_ref[rows, qk_cols]
            v = v_ref[rows, v_cols]
            qf = q.astype(F32)
            q_stack = jnp.concatenate([jnp.where(low_half, qf, 0.0), jnp.where(low_half, 0.0, qf)],
                                      axis=0).astype(BF16)
            s = lax.dot_general(q_stack, ks, nt_dims, preferred_element_type=F32)
            s_pair = jnp.concatenate([s[0:C] * decay[0], s[C:2 * C] * decay[1]], axis=1).astype(BF16)
            vbd_ref[pair, 0:C, 0:RET_DV] = v[:, 0:RET_DV]
            vbd_ref[pair, C:2 * C, RET_DV:2 * RET_DV] = v[:, RET_DV:2 * RET_DV]
            o = jnp.dot(s_pair, vbd_ref[pair], preferred_element_type=F32)
            state = state_ref[pair]
            o = o + jnp.dot(q, state.astype(BF16), preferred_element_type=F32) * xi
            kz = (ks.astype(F32) * zeta).astype(BF16)
            kv = lax.dot_general(kz, v, tn_dims, preferred_element_type=F32)
            state_ref[pair] = state * chunk_decay + jnp.where(own_block, kv, 0.0)
            for e in range(2):
                cols = slice((2 * pair + e) * RET_DV, (2 * pair + e + 1) * RET_DV)
                oh = o[:, e * RET_DV:(e + 1) * RET_DV]
                mu = jnp.mean(oh, axis=-1, keepdims=True)
                var = jnp.mean(jnp.square(oh - mu), axis=-1, keepdims=True)
                on = (oh - mu) * lax.rsqrt(var + GN_EPS) * gnw_ref[:, cols]
                gate = g_ref[rows, cols]
                o_ref[rows, cols] = (gate * (1.0 / (1.0 + jnp.exp(-gate))) * on).astype(BF16)


def _retention(rq, rk, rv, rg, gn_w, batch, seq_len):
    T = rq.shape[0]
    tc = RET_TILE
    nt = seq_len // tc
    tok = lambda n: pl.BlockSpec((tc, n), lambda b, i: (b * nt + i, 0))
    return pl.pallas_call(
        _retention_kernel,
        grid=(batch, nt),
        in_specs=[tok(RET_QK), tok(RET_QK), tok(D_RET), tok(D_RET),
                  pl.BlockSpec((1, D_RET), lambda b, i: (0, 0))],
        out_specs=tok(D_RET),
        out_shape=jax.ShapeDtypeStruct((T, D_RET), BF16),
        scratch_shapes=[pltpu.VMEM((RET_HEADS // 2, LANES, 2 * RET_DV), F32),
                        pltpu.VMEM((RET_HEADS // 2, 2 * RET_CHUNK, 2 * RET_DV), BF16)],
        compiler_params=pltpu.CompilerParams(dimension_semantics=("arbitrary", "arbitrary"),
                                             vmem_limit_bytes=V7X_VMEM_LIMIT),
        name="retention",
    )(rq, rk, rv, rg, gn_w)


def _compress_kernel(x_ref, pa_ref, pb_ref, wa_ref, wb_ref, w2_ref, cos_ref, sin_ref, o_ref, ot_ref):
    ng = x_ref.shape[2]
    x = x_ref[0, 0]
    xa = (x + pa_ref[0]).astype(BF16)
    xb = (x + pb_ref[0]).astype(BF16)
    a = jnp.dot(xa, wa_ref[0], preferred_element_type=F32)
    b = jnp.dot(xb, wb_ref[0], preferred_element_type=F32)
    hid = a + pltpu.roll(b, ng - 1, 0)
    out = jnp.dot(_gelu_tanh(hid).astype(BF16), w2_ref[0], preferred_element_type=F32)
    lane = lax.broadcasted_iota(jnp.int32, out.shape, 1)
    out = _rope(out, cos_ref[0], sin_ref[0], (lane % NSA_DK) < NSA_DK // 2)
    o_ref[0, 0] = out.astype(BF16)
    ot_ref[0, 0] = out.T.astype(BF16)


def _compress(kvc, pos_a, pos_b, wa, wb, w2, layer, cos, sin, batch, seq_len):
    ng = seq_len // CMP_STRIDE
    gw = CMP_STRIDE * LANES
    x = kvc.reshape(2, batch, ng, gw)
    hid = NSA_KV_HEADS * CMP_HID
    per_kv = lambda *shape: pl.BlockSpec((1,) + shape, lambda s, b: (s,) + (0,) * len(shape))
    per_lkv = lambda *shape: pl.BlockSpec((None, 1) + shape, lambda s, b: (layer, s) + (0,) * len(shape))
    return pl.pallas_call(
        _compress_kernel,
        grid=(2, batch),
        in_specs=[pl.BlockSpec((1, 1, ng, gw), lambda s, b: (s, b, 0, 0)),
                  per_lkv(1, gw), per_lkv(1, gw), per_lkv(gw, hid), per_lkv(gw, hid), per_lkv(hid, LANES),
                  per_kv(ng, LANES), per_kv(ng, LANES)],
        out_specs=(pl.BlockSpec((1, 1, ng, LANES), lambda s, b: (s, b, 0, 0)),
                   pl.BlockSpec((1, 1, LANES, ng), lambda s, b: (s, b, 0, 0))),
        out_shape=(jax.ShapeDtypeStruct((2, batch, ng, LANES), BF16),
                   jax.ShapeDtypeStruct((2, batch, LANES, ng), BF16)),
        compiler_params=pltpu.CompilerParams(dimension_semantics=("arbitrary", "arbitrary"),
                                             vmem_limit_bytes=V7X_VMEM_LIMIT),
        name="compress",
    )(x, pos_a, pos_b, wa, wb, w2, cos, sin)


def _nsa_kernel(nq_ref, ng_ref, *refs, seq_len, topk):
    o_ref = refs[8]

    def one_tile(j, carry):
        rows = pl.ds(pl.multiple_of(j * NSA_QBLOCK, NSA_QBLOCK), NSA_QBLOCK)
        t0 = (pl.program_id(1) * NSA_TILES_PER_STEP + j) * NSA_QBLOCK
        _nsa_tile(t0, nq_ref.at[rows], ng_ref.at[rows], *refs[:8], o_ref.at[rows], *refs[9:],
                  seq_len=seq_len, topk=topk)
        return carry

    lax.fori_loop(0, NSA_TILES_PER_STEP, one_tile, 0)


def _nsa_tile(t0, nq_ref, ng_ref, kc_ref, vc_ref, ksl_ref, vsl_ref, kwn_ref, vwn_ref, ov_ref, ex_ref,
              o_ref, qaug_ref, s_ref, m_ref, acc_ref, out_ref, *, seq_len, topk):
    QB = NSA_QBLOCK
    R = NSA_GROUP
    G = NSA_KV_HEADS
    M = R * QB
    NC = kc_ref.shape[2]
    KT = SLC_KTILE
    WK = WINDOW + QB
    nt_dims = (((1,), (1,)), ((), ()))

    low_half = lax.broadcasted_iota(jnp.int32, (QB, LANES), 1) < NSA_DK
    q = nq_ref[...].astype(F32)

    def per_head(x):
        return jnp.concatenate([x] * R, axis=1)

    def normalized(acc, g):
        num, den = (acc[0:NSA_DK], acc[NSA_DK:NSA_DK + 1]) if g == 0 else (acc[NSA_DK:], acc[0:1])
        return num * (1.0 / den)

    for g in range(G):
        head_lanes = low_half if g == 0 else jnp.logical_not(low_half)
        for i in range(R):
            qaug_ref[g, i * QB:(i + 1) * QB, 0:LANES] = jnp.where(
                head_lanes, q[:, i * LANES:(i + 1) * LANES], 0.0).astype(BF16)

    ks = pl.multiple_of(jnp.clip(t0 - WINDOW, 0, seq_len - WK), QB)
    k_pos = ks + lax.broadcasted_iota(jnp.int32, (WK, QB), 0)
    t_pos = t0 + lax.broadcasted_iota(jnp.int32, (WK, QB), 1)
    win_mask = per_head((k_pos <= t_pos) & (k_pos > t_pos - WINDOW))
    jj = lax.broadcasted_iota(jnp.int32, (LANES, QB), 0)
    q_blk = (t0 + lax.broadcasted_iota(jnp.int32, (LANES, QB), 1)) // SLC_BLOCK
    valid = jj <= q_blk
    forced = (jj == 0) | (valid & (jj > q_blk - SLC_LOCAL))
    jf = jj.astype(F32)

    def pick_round(score):
        best = jnp.max(score, axis=0, keepdims=True)
        first = jnp.min(jnp.where(score == best, jf, float(LANES)), axis=0, keepdims=True)
        return jnp.where(jf == first, -jnp.inf, score)

    def slc_scores(k0, slot):
        for g in range(G):
            s_ref[slot, g] = lax.dot_general(ksl_ref[0, pl.ds(k0, KT), :], qaug_ref[g], nt_dims,
                                             preferred_element_type=F32)

    def compressed_select_window(nc):
        win_scores = [lax.dot_general(kwn_ref[0, pl.ds(ks, WK), :], qaug_ref[g, :, 0:LANES], nt_dims,
                                      preferred_element_type=F32) for g in range(G)]
        cmp_scores = [lax.dot_general(kc_ref[0, 0, 0:nc, :], qaug_ref[g, :, 0:LANES], nt_dims,
                                      preferred_element_type=F32) for g in range(G)]

        scores = []
        n_idx = lax.broadcasted_iota(jnp.int32, (nc, QB), 0)
        t_col = t0 + lax.broadcasted_iota(jnp.int32, (nc, QB), 1)
        cmp_mask = per_head(n_idx * CMP_STRIDE + (CMP_BLOCK - 1) <= t_col)
        for g in range(G):
            sc = jnp.where(cmp_mask, cmp_scores[g], NEG)
            e = jnp.exp2(sc - jnp.max(sc, axis=0, keepdims=True))
            if nc == LANES:
                e = jnp.where(cmp_mask, e, 0.0)
            p_cmp = e * (1.0 / jnp.maximum(jnp.sum(e, axis=0, keepdims=True), 1e-30))
            acc = jnp.dot(vc_ref[0, 0, :, 0:nc], p_cmp.astype(BF16), preferred_element_type=F32)
            out_ref[0, g * NSA_DK:(g + 1) * NSA_DK, :] = acc[g * NSA_DK:(g + 1) * NSA_DK]
            p_sum = p_cmp[:, 0:QB]
            for i in range(1, R):
                p_sum = p_sum + p_cmp[:, i * QB:(i + 1) * QB]
            p_hi, p_lo = _split_bf16(p_sum)
            imp = (jnp.dot(ov_ref[:, 0:nc], p_hi, preferred_element_type=F32)
                   + jnp.dot(ov_ref[:, 0:nc], p_lo, preferred_element_type=F32))
            scores.append(jnp.where(forced, -jnp.inf, jnp.where(valid, imp, -SEL_BIG)))

        for _ in range(topk - (1 + SLC_LOCAL)):
            scores = [pick_round(score) for score in scores]
        for g in range(G):
            selected = ((scores[g] == -jnp.inf) | (q_blk < topk)) & valid
            bias = jnp.where(selected, 0.0, NEG).T.astype(BF16)
            for i in range(R):
                qaug_ref[g, i * QB:(i + 1) * QB, LANES:2 * LANES] = bias
        slc_scores(0, 0)

        for g in range(G):
            sw = jnp.where(win_mask, win_scores[g], NEG)
            e_win = jnp.exp2(sw - jnp.max(sw, axis=0, keepdims=True)).astype(BF16)
            pv = jnp.dot(vwn_ref[0, g * LANES:(g + 1) * LANES, pl.ds(ks, WK)], e_win,
                         preferred_element_type=F32)
            out_ref[1, g * NSA_DK:(g + 1) * NSA_DK, :] = normalized(pv, g)

    n_visible = (t0 + QB - CMP_BLOCK) // CMP_STRIDE + 1
    n_variants = NC // LANES
    for v in range(n_variants):
        @pl.when(jnp.minimum((n_visible - 1) // LANES, n_variants - 1) == v)
        def _():
            compressed_select_window((v + 1) * LANES)

    def slc_accumulate(k0, slot, causal):
        for g in range(G):
            s = s_ref[slot, g]
            if causal:
                k_pos = k0 + lax.broadcasted_iota(jnp.int32, (KT, QB), 0)
                t_pos = t0 + lax.broadcasted_iota(jnp.int32, (KT, QB), 1)
                s = jnp.where(per_head(k_pos <= t_pos), s, NEG)
            m_run = m_ref[g]
            m_new = jnp.maximum(m_run, jnp.max(s, axis=0, keepdims=True))
            p = jnp.exp2(s - m_new).astype(BF16)
            pv = jnp.dot(vsl_ref[0, g * LANES:(g + 1) * LANES, pl.ds(k0, KT)], p,
                         preferred_element_type=F32)
            acc_ref[g] = jnp.exp2(m_run - m_new) * acc_ref[g] + pv
            m_ref[g] = m_new

    m_ref[...] = jnp.full(m_ref.shape, NEG, F32)
    acc_ref[...] = jnp.zeros_like(acc_ref)
    n_full = t0 // KT
    U = SLC_UNROLL

    def slc_run(k0, n_tiles, last_is_diagonal):
        for u in range(n_tiles):
            causal = last_is_diagonal and u == n_tiles - 1
            if not causal:
                slc_scores(k0 + (u + 1) * KT, (u + 1) % 2)
            slc_accumulate(k0 + u * KT, u % 2, causal)

    def slc_group(j, _):
        slc_run(pl.multiple_of(j * (U * KT), U * KT), U, False)
        return 0

    lax.fori_loop(0, n_full // U, slc_group, 0)
    k_rest = pl.multiple_of((n_full // U) * (U * KT), U * KT)
    for r in range(U):
        @pl.when(n_full % U == r)
        def _():
            slc_run(k_rest, r + 1, True)

    o_slc = [normalized(acc_ref[g], g) for g in range(G)]

    g_hi, g_lo = _split_bf16((1.0 / (1.0 + jnp.exp(-ng_ref[...]))).T)
    gates = (jnp.dot(ex_ref[...], g_hi, preferred_element_type=F32)
             + jnp.dot(ex_ref[...], g_lo, preferred_element_type=F32))
    branches = [out_ref[0], jnp.concatenate(o_slc, axis=0), out_ref[1]]
    for i in range(R):
        y = jnp.zeros((LANES, QB), F32)
        for b, o in enumerate(branches):
            y = y + gates[b * D_NSA + i * LANES:b * D_NSA + (i + 1) * LANES] * o[:, i * QB:(i + 1) * QB]
        o_ref[:, i * LANES:(i + 1) * LANES] = y.T.astype(BF16)


def _nsa(nq, ng, kvc_cmp, kvc_cmp_t, ksl, vsl, kwn, vwn, overlap, expand, batch, seq_len):
    T = nq.shape[0]
    QB = NSA_QBLOCK
    rows = QB * NSA_TILES_PER_STEP
    nqb = seq_len // rows
    nc = seq_len // CMP_STRIDE
    topk = min(SLC_TOPK, seq_len // SLC_BLOCK)
    tok = lambda n: pl.BlockSpec((rows, n), lambda b, i: (b * nqb + i, 0))
    seq = lambda n: pl.BlockSpec((1, seq_len, n), lambda b, i: (b, 0, 0))
    seq_t = pl.BlockSpec((1, 2 * LANES, seq_len), lambda b, i: (b, 0, 0))
    return pl.pallas_call(
        functools.partial(_nsa_kernel, seq_len=seq_len, topk=topk),
        grid=(batch, nqb),
        in_specs=[tok(D_NSA), tok(LANES),
                  pl.BlockSpec((1, 1, nc, LANES), lambda b, i: (0, b, 0, 0)),
                  pl.BlockSpec((1, 1, LANES, nc), lambda b, i: (1, b, 0, 0)),
                  seq(2 * LANES), seq_t, seq(LANES), seq_t,
                  pl.BlockSpec(overlap.shape, lambda b, i: (0, 0)),
                  pl.BlockSpec(expand.shape, lambda b, i: (0, 0))],
        out_specs=tok(D_NSA),
        out_shape=jax.ShapeDtypeStruct((T, D_NSA), BF16),
        scratch_shapes=[pltpu.VMEM((NSA_KV_HEADS, NSA_GROUP * QB, 2 * LANES), BF16),
                        pltpu.VMEM((2, NSA_KV_HEADS, SLC_KTILE, NSA_GROUP * QB), F32),
                        pltpu.VMEM((NSA_KV_HEADS, 1, NSA_GROUP * QB), F32),
                        pltpu.VMEM((NSA_KV_HEADS, LANES, NSA_GROUP * QB), F32),
                        pltpu.VMEM((2, LANES, NSA_GROUP * QB), F32)],
        compiler_params=pltpu.CompilerParams(dimension_semantics=("arbitrary", "arbitrary"),
                                             vmem_limit_bytes=V7X_VMEM_LIMIT),
        name="nsa",
    )(nq, ng, kvc_cmp, kvc_cmp_t, ksl.reshape(batch, seq_len, -1), vsl,
      kwn.reshape(batch, seq_len, -1), vwn, overlap, expand)


def _ffn_kernel(yr_ref, yrp_ref, yn_ref, ynp_ref, x_ref, xp_ref, wo_ref, gmix_ref, gpre_ref,
                wu_ref, conv_ref, wd_ref, gpost_ref, o_ref,
                h_ref, ug_ref, uv_ref, acc_ref, xmid_ref, *, tiles_per_seq):
    tm = x_ref.shape[0]
    d_ff = wd_ref.shape[0]
    tf = FFN_FTILE
    H = FFN_HALO
    rc = tm // FFN_ROW_CHUNKS

    def rms(x, gain_ref):
        return x * lax.rsqrt(jnp.mean(x * x, axis=-1, keepdims=True) + NORM_EPS) * gain_ref[...]

    keep = jnp.where(pl.program_id(0) % tiles_per_seq == 0, 0.0, 1.0)
    for c in range(FFN_ROW_CHUNKS):
        rows = slice(c * rc, (c + 1) * rc)
        y_ret, y_nsa = yr_ref[rows, :], yn_ref[rows, :]
        if c == 0:
            y_ret = jnp.concatenate([yrp_ref[...], y_ret], axis=0)
            y_nsa = jnp.concatenate([ynp_ref[...], y_nsa], axis=0)
        mix = (jnp.dot(y_ret, wo_ref[0:D_RET, :], preferred_element_type=F32)
               + jnp.dot(y_nsa, wo_ref[D_RET:D_RET + D_NSA, :], preferred_element_type=F32))
        post = rms(mix, gmix_ref)
        if c == 0:
            h_ref[0:H, :] = (rms(xp_ref[...] + post[0:H], gpre_ref) * keep).astype(BF16)
            post = post[H:H + rc]
        x_mid = x_ref[rows, :] + post
        xmid_ref[rows, :] = x_mid
        h_ref[H + c * rc:H + (c + 1) * rc, :] = rms(x_mid, gpre_ref).astype(BF16)
    acc_ref[...] = jnp.zeros_like(acc_ref)

    n_tiles = d_ff // tf

    def columns(f):
        return (pl.ds(pl.multiple_of(f * tf, tf), tf), pl.ds(pl.multiple_of(d_ff + f * tf, tf), tf))

    def up_project(f, c):
        gate_cols, value_cols = columns(f)
        rows = slice(0 if c == 0 else H + c * rc, H + (c + 1) * rc)
        h = h_ref[rows, :]
        ug_ref[rows, :] = jnp.dot(h, wu_ref[:, gate_cols], preferred_element_type=F32)
        uv_ref[rows, :] = jnp.dot(h, wu_ref[:, value_cols], preferred_element_type=F32)

    def hidden_tile(f, carry):
        gate_cols, value_cols = columns(f)

        def causal_conv(u_ref, cols, r0, scale):
            out = (scale * conv_ref[CONV_WIDTH - 1:CONV_WIDTH, cols]) * u_ref[r0:r0 + rc, :]
            for k in range(CONV_WIDTH - 1):
                d = CONV_WIDTH - 1 - k
                out = out + (scale * conv_ref[k:k + 1, cols]) * u_ref[r0 - d:r0 - d + rc, :]
            return out

        def gate_and_down(c):
            g = causal_conv(ug_ref, gate_cols, H + c * rc, 1.0)
            v_half = causal_conv(uv_ref, value_cols, H + c * rc, 0.5)
            inner = g * (GELU_C + (GELU_C * 0.044715) * (g * g))
            act = (g + g * jnp.tanh(inner)) * v_half
            acc_ref[c * rc:(c + 1) * rc, :] += jnp.dot(act.astype(BF16), wd_ref[gate_cols, :],
                                                       preferred_element_type=F32)

        for c in range(FFN_ROW_CHUNKS):
            if c + 1 < FFN_ROW_CHUNKS:
                up_project(f, c + 1)
            else:
                up_project(jnp.minimum(f + 1, n_tiles - 1), 0)
            gate_and_down(c)
        return carry

    up_project(jnp.int32(0), 0)
    lax.fori_loop(0, n_tiles, hidden_tile, 0)
    o_ref[...] = xmid_ref[...] + rms(acc_ref[...], gpost_ref)


def _outproj_ffn(y_ret, y_nsa, x2, w_out, g_mix, g_pre, w_up, conv_w, w_down, layer, g_post, seq_len):
    T, D = x2.shape
    d_ff = w_down.shape[1]
    tm, tf, H = FFN_TILE, FFN_FTILE, FFN_HALO
    tile = lambda n: pl.BlockSpec((tm, n), lambda i: (i, 0))
    halo = lambda n: pl.BlockSpec((H, n), lambda i: (jnp.maximum(i * (tm // H) - 1, 0), 0))
    resident = lambda a: pl.BlockSpec((None,) + a.shape[1:], lambda i: (layer, 0, 0),
                                      pipeline_mode=pl.Buffered(1))
    gain = pl.BlockSpec((1, D), lambda i: (0, 0), pipeline_mode=pl.Buffered(1))
    return pl.pallas_call(
        functools.partial(_ffn_kernel, tiles_per_seq=seq_len // tm),
        grid=(T // tm,),
        in_specs=[tile(D_RET), halo(D_RET), tile(D_NSA), halo(D_NSA), tile(D), halo(D),
                  resident(w_out), gain, gain, resident(w_up), resident(conv_w), resident(w_down), gain],
        out_specs=tile(D),
        out_shape=jax.ShapeDtypeStruct((T, D), F32),
        scratch_shapes=[pltpu.VMEM((tm + H, D), BF16), pltpu.VMEM((tm + H, tf), F32),
                        pltpu.VMEM((tm + H, tf), F32), pltpu.VMEM((tm, D), F32),
                        pltpu.VMEM((tm, D), F32)],
        compiler_params=pltpu.CompilerParams(dimension_semantics=("arbitrary",),
                                             vmem_limit_bytes=V7X_VMEM_LIMIT),
        name="ffn",
    )(y_ret, y_ret, y_nsa, y_nsa, x2, x2, w_out, g_mix, g_pre, w_up, conv_w, w_down, g_post)


_NSA_HEAD_ORDER = [g * NSA_GROUP + i for i in range(NSA_GROUP) for g in range(NSA_KV_HEADS)]


def _rope_tables(pos):
    inv = 1.0 / (ROPE_THETA ** (jnp.arange(0, NSA_DK, 2, dtype=F32) / NSA_DK))
    ang = pos.astype(F32)[:, None] * inv[None, :]
    c, s = jnp.cos(ang), jnp.sin(ang)
    return jnp.concatenate([c, c, c, c], axis=1), jnp.concatenate([-s, s, -s, s], axis=1)


def _prep_w_in(w):
    lead = w.shape[:-1]
    splits = np.cumsum([RET_QK, RET_QK, D_RET, D_RET, D_NSA] + [NSA_KV] * 6)
    rq, rk, rv, rg, nq, kcm, vcm, ksl, vsl, kwn, vwn, ng = jnp.split(w, [int(s) for s in splits], axis=-1)
    nq = nq.reshape(lead + (NSA_HEADS, NSA_DK))[..., np.array(_NSA_HEAD_ORDER), :].reshape(lead + (D_NSA,))
    ng = jnp.pad(ng, [(0, 0)] * len(lead) + [(0, LANES - NSA_GATES)])
    return jnp.concatenate([rq, rk, nq, ksl, kwn, rv, vsl, vwn, rg, kcm, vcm, ng], axis=-1).astype(BF16)


def _prep_w_out(w):
    layers, _, d = w.shape
    w_nsa = w[:, D_RET:].reshape(layers, NSA_HEADS, NSA_DK, d)[:, np.array(_NSA_HEAD_ORDER)]
    return jnp.concatenate([w[:, :D_RET], w_nsa.reshape(layers, D_NSA, d)], axis=1).astype(BF16)


def _prep_compress(pos, w1, w2):
    lead = w1.shape[:-2]
    half = CMP_STRIDE

    def block_diag(w, axis):
        z = jnp.zeros_like(w)
        return jnp.stack([jnp.concatenate([w, z], axis=-1), jnp.concatenate([z, w], axis=-1)], axis=axis)

    def first_layer(w_half):
        w4 = w_half.reshape(lead + (half, NSA_DK, CMP_HID))
        return block_diag(w4, -3).reshape(lead + (half * NSA_KV_HEADS * NSA_DK, -1)).astype(BF16)

    def pos_row(p_half):
        rows = jnp.broadcast_to(p_half[..., :, None, :], lead + (half, NSA_KV_HEADS, NSA_DK))
        return rows.reshape(lead + (1, -1))

    w2x = block_diag(w2, -3).reshape(lead + (NSA_KV_HEADS * CMP_HID, -1)).astype(BF16)
    n1 = half * NSA_DK
    return (pos_row(pos[..., :half, :]), pos_row(pos[..., half:, :]),
            first_layer(w1[..., :n1, :]), first_layer(w1[..., n1:, :]), w2x)


def _overlap_matrix(seq_len):
    nc = seq_len // CMP_STRIDE
    cmp_start = np.arange(nc) * CMP_STRIDE
    slc_start = np.arange(LANES) * SLC_BLOCK
    ov = ((cmp_start[:, None] < slc_start[None, :] + SLC_BLOCK)
          & (cmp_start[:, None] + CMP_BLOCK > slc_start[None, :]))
    n_cmp = (seq_len - CMP_BLOCK) // CMP_STRIDE + 1
    ov &= (np.arange(nc) < n_cmp)[:, None]
    return jnp.asarray(ov.T.astype(np.float32), dtype=BF16)


def _gate_expand_matrix():
    ex = np.zeros((LANES, 3 * D_NSA), np.float32)
    for branch in range(3):
        for p in range(D_NSA):
            head = _NSA_HEAD_ORDER[p // NSA_DK]
            ex[branch * NSA_HEADS + head, branch * D_NSA + p] = 1.0
    return jnp.asarray(ex.T, dtype=BF16)


def kernel(x, norm_mix_pre, w_in, ret_gn_w, cmp_k_pos, cmp_k_w1, cmp_k_w2, cmp_v_pos, cmp_v_w1, cmp_v_w2,
           w_out, norm_mix_post, norm_ffn_pre, ffn_w_up, ffn_conv, ffn_w_down, norm_ffn_post):
    B, S, D = x.shape
    depth = w_in.shape[0]
    assert S % SLC_KTILE == 0 and S % FFN_TILE == 0 and S // SLC_BLOCK <= LANES and S >= WINDOW + NSA_QBLOCK
    assert ffn_w_down.shape[1] % FFN_FTILE == 0
    assert min(SLC_TOPK, S // SLC_BLOCK) > 1 + SLC_LOCAL and S % (NSA_QBLOCK * NSA_TILES_PER_STEP) == 0

    cos, sin = _rope_tables(jnp.arange(S, dtype=jnp.int32))
    nc = S // CMP_STRIDE
    ccos, csin = _rope_tables(jnp.arange(nc, dtype=jnp.int32) * CMP_STRIDE + (CMP_BLOCK - 1))
    cmp_cos = jnp.stack([ccos, jnp.ones_like(ccos)])
    cmp_sin = jnp.stack([csin, jnp.zeros_like(csin)])
    overlap = _overlap_matrix(S)
    expand = _gate_expand_matrix()

    w_in_p = _prep_w_in(w_in)
    w_out_p = _prep_w_out(w_out)
    w_up_p = ffn_w_up.astype(BF16)
    w_down_p = ffn_w_down.astype(BF16)
    cmp_p = _prep_compress(jnp.stack([cmp_k_pos, cmp_v_pos], axis=1), jnp.stack([cmp_k_w1, cmp_v_w1], axis=1),
                           jnp.stack([cmp_k_w2, cmp_v_w2], axis=1))

    x2 = x.reshape(B * S, D)
    for l in range(depth):
        outs = _inproj(x2, norm_mix_pre[l][None], w_in_p, l, cos, sin, S)
        rq, rk, nq, ksl, kwn, rv, vsl, vwn, rg, kvc, ng = outs
        y_ret = _retention(rq, rk, rv, rg, ret_gn_w[l][None], B, S)
        kvc_cmp, kvc_cmp_t = _compress(kvc, *cmp_p, l, cmp_cos, cmp_sin, B, S)
        y_nsa = _nsa(nq, ng, kvc_cmp, kvc_cmp_t, ksl, vsl, kwn, vwn, overlap, expand, B, S)
        x2 = _outproj_ffn(y_ret, y_nsa, x2, w_out_p, norm_mix_post[l][None], norm_ffn_pre[l][None],
                          w_up_p, ffn_conv, w_down_p, l, norm_ffn_post[l][None], S)
    return x2.reshape(B, S, D)
```

```python
import functools
import math

import jax
import jax.numpy as jnp
import numpy as np
from jax import lax
from jax.experimental import pallas as pl
from jax.experimental.pallas import tpu as pltpu

F32 = jnp.float32
BF16 = jnp.bfloat16

LANES = 128
SUBLANES = 8
V7X_VMEM_LIMIT = 56 * 1024 * 1024

ROPE_THETA = 10000.0
NORM_EPS = 1e-6
GN_EPS = 1e-5
NEG = -1e30
SEL_BIG = 1e9

RET_HEADS = 4
RET_DK = 64
RET_DV = 128
RET_CHUNK = 128
NSA_HEADS = 8
NSA_KV_HEADS = 2
NSA_DK = 64
NSA_GROUP = NSA_HEADS // NSA_KV_HEADS
CMP_BLOCK = 32
CMP_STRIDE = 16
CMP_HID = 256
SLC_BLOCK = 64
SLC_TOPK = 16
SLC_LOCAL = 2
WINDOW = 512
NSA_QBLOCK = 128
NSA_TILES_PER_STEP = 4
NSA_GATES = 3 * NSA_HEADS
CONV_WIDTH = 3

NSA_Q_SCALE = NSA_DK ** -0.5 * math.log2(math.e)

D_RET = RET_HEADS * RET_DV
D_NSA = NSA_HEADS * NSA_DK
RET_QK = RET_HEADS * RET_DK
NSA_KV = NSA_KV_HEADS * NSA_DK

COL_RQ = 0
COL_RK = COL_RQ + RET_QK
COL_NQ = COL_RK + RET_QK
COL_KSL = COL_NQ + D_NSA
COL_KWN = COL_KSL + NSA_KV
ROPE_COLS = COL_KWN + NSA_KV
COL_RV = ROPE_COLS
COL_VSL = COL_RV + D_RET
COL_VWN = COL_VSL + NSA_KV
COL_RG = COL_VWN + NSA_KV
COL_KCM = COL_RG + D_RET
COL_VCM = COL_KCM + NSA_KV
COL_NG = COL_VCM + NSA_KV
IN_COLS_PAD = COL_NG + LANES

TOK_TILE = 512
RET_TILE = 512
FFN_TILE = 1024
FFN_FTILE = 256
FFN_HALO = 16
FFN_ROW_CHUNKS = 4
SLC_KTILE = 512
SLC_UNROLL = 4


GELU_C = math.sqrt(2.0 / math.pi)


def _gelu_tanh(x):
    return 0.5 * x * (1.0 + jnp.tanh(GELU_C * (x + 0.044715 * (x * x * x))))


def _rope(p, cos, sin_signed, first_half):
    half = NSA_DK // 2
    partner = jnp.where(first_half, pltpu.roll(p, LANES - half, 1), pltpu.roll(p, half, 1))
    return p * cos + partner * sin_signed


def _split_bf16(x):
    hi = x.astype(BF16)
    lo = (x - hi.astype(F32)).astype(BF16)
    return hi, lo


def _inproj_kernel(x_ref, g_ref, w_ref, cos_ref, sin_ref,
                   rq_ref, rk_ref, nq_ref, ksl_ref, kwn_ref, rv_ref, vsl_ref, vwn_ref,
                   rg_ref, kvc_ref, ng_ref, stage_ref, *, seq_len):
    tm = x_ref.shape[0]
    x = x_ref[...]
    h = (x * lax.rsqrt(jnp.mean(x * x, axis=-1, keepdims=True) + NORM_EPS) * g_ref[...]).astype(BF16)
    cos = cos_ref[...]
    sin = sin_ref[...]
    lane = lax.broadcasted_iota(jnp.int32, (tm, LANES), 1)
    first_half = (lane % NSA_DK) < NSA_DK // 2

    def proj(c0, n):
        return jnp.dot(h, w_ref[:, c0:c0 + n], preferred_element_type=F32)

    def rope_slab(p, i):
        return _rope(p[:, i * LANES:(i + 1) * LANES], cos, sin, first_half)

    def store_rq(p):
        for i in range(RET_QK // LANES):
            rq_ref[:, i * LANES:(i + 1) * LANES] = rope_slab(p, i).astype(BF16)

    def store_rk(p):
        for i in range(RET_QK // LANES):
            rk_ref[:, i * LANES:(i + 1) * LANES] = (rope_slab(p, i) * (RET_DK ** -0.5)).astype(BF16)

    def store_nq(p):
        for i in range(D_NSA // LANES):
            nq_ref[:, i * LANES:(i + 1) * LANES] = (rope_slab(p, i) * NSA_Q_SCALE).astype(BF16)

    def store_keys(p):
        ksl_ref[:, 0:LANES] = rope_slab(p, 0).astype(BF16)
        kwn_ref[...] = rope_slab(p, 1).astype(BF16)
        row = lax.broadcasted_iota(jnp.int32, (tm, LANES), 0)
        pos = (pl.program_id(0) * tm + row) % seq_len
        ksl_ref[:, LANES:2 * LANES] = jnp.where(lane == pos // SLC_BLOCK, 1.0, 0.0).astype(BF16)

    def store_rv(p):
        rv_ref[...] = p.astype(BF16)

    def store_values(p):
        low_half = lane < NSA_DK
        for v_ref, v in ((vsl_ref, p[:, 0:LANES]), (vwn_ref, p[:, LANES:2 * LANES])):
            v_ref[0, 0:LANES, :] = jnp.where(low_half, v, 1.0).T.astype(BF16)
            v_ref[0, LANES:2 * LANES, :] = jnp.where(low_half, 1.0, v).T.astype(BF16)

    def store_rg(p):
        rg_ref[...] = p

    def store_compress_inputs(p):
        for s in range(2):
            stage_ref[...] = p[:, s * LANES:(s + 1) * LANES]
            for l in range(CMP_STRIDE):
                kvc_ref[s, :, l * LANES:(l + 1) * LANES] = stage_ref[
                    pl.ds(l, tm // CMP_STRIDE, stride=CMP_STRIDE), :]

    def store_gates(p):
        ng_ref[...] = p

    stages = [(COL_RQ, RET_QK, store_rq), (COL_RK, RET_QK, store_rk), (COL_NQ, D_NSA, store_nq),
              (COL_KSL, 2 * NSA_KV, store_keys), (COL_RV, D_RET, store_rv),
              (COL_VSL, 2 * NSA_KV, store_values), (COL_RG, D_RET, store_rg),
              (COL_KCM, 2 * NSA_KV, store_compress_inputs), (COL_NG, LANES, store_gates)]
    pending = None
    for c0, n, store in stages:
        p = proj(c0, n)
        if pending is not None:
            pending[1](pending[0])
        pending = (p, store)
    pending[1](pending[0])


def _inproj(x2, gain, w, layer, cos, sin, seq_len):
    T, D = x2.shape
    tm = TOK_TILE
    nt = seq_len // tm
    tok = lambda n: pl.BlockSpec((tm, n), lambda i: (i, 0))
    tok_t = pl.BlockSpec((1, 2 * LANES, tm), lambda i: (i // nt, 0, i % nt))
    out_shape = (
        jax.ShapeDtypeStruct((T, RET_QK), BF16),
        jax.ShapeDtypeStruct((T, RET_QK), BF16),
        jax.ShapeDtypeStruct((T, D_NSA), BF16),
        jax.ShapeDtypeStruct((T, 2 * LANES), BF16),
        jax.ShapeDtypeStruct((T, LANES), BF16),
        jax.ShapeDtypeStruct((T, D_RET), BF16),
        jax.ShapeDtypeStruct((T // seq_len, 2 * LANES, seq_len), BF16),
        jax.ShapeDtypeStruct((T // seq_len, 2 * LANES, seq_len), BF16),
        jax.ShapeDtypeStruct((T, D_RET), F32),
        jax.ShapeDtypeStruct((2, T // CMP_STRIDE, CMP_STRIDE * LANES), F32),
        jax.ShapeDtypeStruct((T, LANES), F32),
    )
    out_specs = (tok(RET_QK), tok(RET_QK), tok(D_NSA), tok(2 * LANES), tok(LANES), tok(D_RET),
                 tok_t, tok_t, tok(D_RET),
                 pl.BlockSpec((2, tm // CMP_STRIDE, CMP_STRIDE * LANES), lambda i: (0, i, 0)), tok(LANES))
    return pl.pallas_call(
        functools.partial(_inproj_kernel, seq_len=seq_len),
        grid=(T // tm,),
        in_specs=[tok(D),
                  pl.BlockSpec((1, D), lambda i: (0, 0)),
                  pl.BlockSpec((None, D, IN_COLS_PAD), lambda i: (layer, 0, 0)),
                  pl.BlockSpec((tm, LANES), lambda i: (i % nt, 0)),
                  pl.BlockSpec((tm, LANES), lambda i: (i % nt, 0))],
        out_specs=out_specs,
        out_shape=out_shape,
        scratch_shapes=[pltpu.VMEM((tm, LANES), F32)],
        compiler_params=pltpu.CompilerParams(dimension_semantics=("arbitrary",),
                                             vmem_limit_bytes=V7X_VMEM_LIMIT),
        name="inproj",
    )(x2, gain, w, cos, sin)


def _retention_kernel(q_ref, k_ref, v_ref, g_ref, gnw_ref, o_ref, state_ref, vbd_ref):
    C = RET_CHUNK
    n_chunks = q_ref.shape[0] // C

    @pl.when(pl.program_id(1) == 0)
    def _():
        state_ref[...] = jnp.zeros_like(state_ref)

    vbd_ref[...] = jnp.zeros_like(vbd_ref)

    ii = lax.broadcasted_iota(jnp.int32, (C, C), 0)
    jj = lax.broadcasted_iota(jnp.int32, (C, C), 1)
    diff = (ii - jj).astype(F32)
    i_col = lax.broadcasted_iota(jnp.int32, (C, 1), 0).astype(F32)
    low_half = lax.broadcasted_iota(jnp.int32, (C, LANES), 1) < RET_DK
    low_cols = lax.broadcasted_iota(jnp.int32, (C, 2 * RET_DV), 1) < RET_DV
    own_block = ((lax.broadcasted_iota(jnp.int32, (LANES, 2 * RET_DV), 0) < RET_DK)
                 == (lax.broadcasted_iota(jnp.int32, (LANES, 2 * RET_DV), 1) < RET_DV))
    nt_dims = (((1,), (1,)), ((), ()))
    tn_dims = (((0,), (0,)), ((), ()))

    for pair in range(RET_HEADS // 2):
        lg0, lg1 = (math.log(1.0 - 2.0 ** (-5.0 - h)) for h in (2 * pair, 2 * pair + 1))
        decay = [jnp.where(diff >= 0, jnp.exp(lg * jnp.maximum(diff, 0.0)), 0.0) for lg in (lg0, lg1)]
        xi = jnp.where(low_cols, jnp.exp(lg0 * (i_col + 1.0)), jnp.exp(lg1 * (i_col + 1.0)))
        zeta = jnp.where(low_half, jnp.exp(lg0 * (C - 1.0 - i_col)), jnp.exp(lg1 * (C - 1.0 - i_col)))
        chunk_decay = jnp.where(low_cols[0:1], math.exp(lg0 * C), math.exp(lg1 * C))
        qk_cols = slice(pair * LANES, (pair + 1) * LANES)
        v_cols = slice(2 * pair * RET_DV, 2 * (pair + 1) * RET_DV)
        for c in range(n_chunks):
            rows = slice(c * C, (c + 1) * C)
            q = q_ref[rows, qk_cols]
            ks = k_ref[rows, qk_cols]
            v = v_ref[rows, v_cols]
            qf = q.astype(F32)
            q_stack = jnp.concatenate([jnp.where(low_half, qf, 0.0), jnp.where(low_half, 0.0, qf)],
                                      axis=0).astype(BF16)
            s = lax.dot_general(q_stack, ks, nt_dims, preferred_element_type=F32)
            s_pair = jnp.concatenate([s[0:C] * decay[0], s[C:2 * C] * decay[1]], axis=1).astype(BF16)
            vbd_ref[pair, 0:C, 0:RET_DV] = v[:, 0:RET_DV]
            vbd_ref[pair, C:2 * C, RET_DV:2 * RET_DV] = v[:, RET_DV:2 * RET_DV]
            o = jnp.dot(s_pair, vbd_ref[pair], preferred_element_type=F32)
            state = state_ref[pair]
            o = o + jnp.dot(q, state.astype(BF16), preferred_element_type=F32) * xi
            kz = (ks.astype(F32) * zeta).astype(BF16)
            kv = lax.dot_general(kz, v, tn_dims, preferred_element_type=F32)
            state_ref[pair] = state * chunk_decay + jnp.where(own_block, kv, 0.0)
            for e in range(2):
                cols = slice((2 * pair + e) * RET_DV, (2 * pair + e + 1) * RET_DV)
                oh = o[:, e * RET_DV:(e + 1) * RET_DV]
                mu = jnp.mean(oh, axis=-1, keepdims=True)
                var = jnp.mean(jnp.square(oh - mu), axis=-1, keepdims=True)
                on = (oh - mu) * lax.rsqrt(var + GN_EPS) * gnw_ref[:, cols]
                gate = g_ref[rows, cols]
                o_ref[rows, cols] = (gate * (1.0 / (1.0 + jnp.exp(-gate))) * on).astype(BF16)


def _retention(rq, rk, rv, rg, gn_w, batch, seq_len):
    T = rq.shape[0]
    tc = RET_TILE
    nt = seq_len // tc
    tok = lambda n: pl.BlockSpec((tc, n), lambda b, i: (b * nt + i, 0))
    return pl.pallas_call(
        _retention_kernel,
        grid=(batch, nt),
        in_specs=[tok(RET_QK), tok(RET_QK), tok(D_RET), tok(D_RET),
                  pl.BlockSpec((1, D_RET), lambda b, i: (0, 0))],
        out_specs=tok(D_RET),
        out_shape=jax.ShapeDtypeStruct((T, D_RET), BF16),
        scratch_shapes=[pltpu.VMEM((RET_HEADS // 2, LANES, 2 * RET_DV), F32),
                        pltpu.VMEM((RET_HEADS // 2, 2 * RET_CHUNK, 2 * RET_DV), BF16)],
        compiler_params=pltpu.CompilerParams(dimension_semantics=("arbitrary", "arbitrary"),
                                             vmem_limit_bytes=V7X_VMEM_LIMIT),
        name="retention",
    )(rq, rk, rv, rg, gn_w)


def _compress_kernel(x_ref, pa_ref, pb_ref, wa_ref, wb_ref, w2_ref, cos_ref, sin_ref, o_ref, ot_ref):
    ng = x_ref.shape[2]
    x = x_ref[0, 0]
    xa = (x + pa_ref[0]).astype(BF16)
    xb = (x + pb_ref[0]).astype(BF16)
    a = jnp.dot(xa, wa_ref[0], preferred_element_type=F32)
    b = jnp.dot(xb, wb_ref[0], preferred_element_type=F32)
    hid = a + pltpu.roll(b, ng - 1, 0)
    out = jnp.dot(_gelu_tanh(hid).astype(BF16), w2_ref[0], preferred_element_type=F32)
    lane = lax.broadcasted_iota(jnp.int32, out.shape, 1)
    out = _rope(out, cos_ref[0], sin_ref[0], (lane % NSA_DK) < NSA_DK // 2)
    o_ref[0, 0] = out.astype(BF16)
    ot_ref[0, 0] = out.T.astype(BF16)


def _compress(kvc, pos_a, pos_b, wa, wb, w2, layer, cos, sin, batch, seq_len):
    ng = seq_len // CMP_STRIDE
    gw = CMP_STRIDE * LANES
    x = kvc.reshape(2, batch, ng, gw)
    hid = NSA_KV_HEADS * CMP_HID
    per_kv = lambda *shape: pl.BlockSpec((1,) + shape, lambda s, b: (s,) + (0,) * len(shape))
    per_lkv = lambda *shape: pl.BlockSpec((None, 1) + shape, lambda s, b: (layer, s) + (0,) * len(shape))
    return pl.pallas_call(
        _compress_kernel,
        grid=(2, batch),
        in_specs=[pl.BlockSpec((1, 1, ng, gw), lambda s, b: (s, b, 0, 0)),
                  per_lkv(1, gw), per_lkv(1, gw), per_lkv(gw, hid), per_lkv(gw, hid), per_lkv(hid, LANES),
                  per_kv(ng, LANES), per_kv(ng, LANES)],
        out_specs=(pl.BlockSpec((1, 1, ng, LANES), lambda s, b: (s, b, 0, 0)),
                   pl.BlockSpec((1, 1, LANES, ng), lambda s, b: (s, b, 0, 0))),
        out_shape=(jax.ShapeDtypeStruct((2, batch, ng, LANES), BF16),
                   jax.ShapeDtypeStruct((2, batch, LANES, ng), BF16)),
        compiler_params=pltpu.CompilerParams(dimension_semantics=("arbitrary", "arbitrary"),
                                             vmem_limit_bytes=V7X_VMEM_LIMIT),
        name="compress",
    )(x, pos_a, pos_b, wa, wb, w2, cos, sin)


def _nsa_kernel(nq_ref, ng_ref, *refs, seq_len, topk):
    n_in = 7
    o_ref = refs[n_in]

    def one_tile(j, carry):
        rows = pl.ds(pl.multiple_of(j * NSA_QBLOCK, NSA_QBLOCK), NSA_QBLOCK)
        t0 = (pl.program_id(1) * NSA_TILES_PER_STEP + j) * NSA_QBLOCK
        _nsa_tile(t0, nq_ref.at[rows], ng_ref.at[rows], *refs[:n_in], o_ref.at[rows], *refs[n_in + 1:],
                  seq_len=seq_len, topk=topk)
        return carry

    lax.fori_loop(0, NSA_TILES_PER_STEP, one_tile, 0)


def _nsa_tile(t0, nq_ref, ng_ref, kc_ref, vc_ref, ksl_ref, vsl_ref, kwn_ref, vwn_ref, ov_ref,
              o_ref, qaug_ref, s_ref, m_ref, acc_ref, out_ref, *, seq_len, topk):
    QB = NSA_QBLOCK
    R = NSA_GROUP
    G = NSA_KV_HEADS
    M = R * QB
    NC = kc_ref.shape[2]
    KT = SLC_KTILE
    WK = WINDOW + QB
    nt_dims = (((1,), (1,)), ((), ()))

    low_half = lax.broadcasted_iota(jnp.int32, (QB, LANES), 1) < NSA_DK
    q = nq_ref[...].astype(F32)

    def per_head(x):
        return jnp.concatenate([x] * R, axis=1)

    def normalized(acc, g):
        num, den = (acc[0:NSA_DK], acc[NSA_DK:NSA_DK + 1]) if g == 0 else (acc[NSA_DK:], acc[0:1])
        return num * (1.0 / den)

    for g in range(G):
        head_lanes = low_half if g == 0 else jnp.logical_not(low_half)
        for i in range(R):
            qaug_ref[g, i * QB:(i + 1) * QB, 0:LANES] = jnp.where(
                head_lanes, q[:, i * LANES:(i + 1) * LANES], 0.0).astype(BF16)

    ks = pl.multiple_of(jnp.clip(t0 - WINDOW, 0, seq_len - WK), QB)
    k_pos = ks + lax.broadcasted_iota(jnp.int32, (WK, QB), 0)
    t_pos = t0 + lax.broadcasted_iota(jnp.int32, (WK, QB), 1)
    win_mask = per_head((k_pos <= t_pos) & (k_pos > t_pos - WINDOW))

    def slc_scores(k0, slot):
        for g in range(G):
            s_ref[slot, g] = lax.dot_general(ksl_ref[0, pl.ds(k0, KT), :], qaug_ref[g], nt_dims,
                                             preferred_element_type=F32)

    def compressed_select_window(nc):
        win_scores = [lax.dot_general(kwn_ref[0, pl.ds(ks, WK), :], qaug_ref[g, :, 0:LANES], nt_dims,
                                      preferred_element_type=F32) for g in range(G)]
        cmp_scores = [lax.dot_general(kc_ref[0, 0, 0:nc, :], qaug_ref[g, :, 0:LANES], nt_dims,
                                      preferred_element_type=F32) for g in range(G)]

        nb = min(nc * CMP_STRIDE // SLC_BLOCK, LANES)
        jj = lax.broadcasted_iota(jnp.int32, (nb, QB), 0)
        q_blk = (t0 + lax.broadcasted_iota(jnp.int32, (nb, QB), 1)) // SLC_BLOCK
        valid = jj <= q_blk
        forced = (jj == 0) | (valid & (jj > q_blk - SLC_LOCAL))
        jf = jj.astype(F32)

        def pick_round(score):
            best = jnp.max(score, axis=0, keepdims=True)
            first = jnp.min(jnp.where(score == best, jf, float(LANES)), axis=0, keepdims=True)
            return jnp.where(jf == first, -jnp.inf, score)

        scores = []
        n_idx = lax.broadcasted_iota(jnp.int32, (nc, QB), 0)
        t_col = t0 + lax.broadcasted_iota(jnp.int32, (nc, QB), 1)
        cmp_mask = per_head(n_idx * CMP_STRIDE + (CMP_BLOCK - 1) <= t_col)
        for g in range(G):
            sc = jnp.where(cmp_mask, cmp_scores[g], NEG)
            e = jnp.exp2(sc - jnp.max(sc, axis=0, keepdims=True))
            if nc == LANES:
                e = jnp.where(cmp_mask, e, 0.0)
            p_cmp = e * (1.0 / jnp.maximum(jnp.sum(e, axis=0, keepdims=True), 1e-30))
            acc = jnp.dot(vc_ref[0, 0, :, 0:nc], p_cmp.astype(BF16), preferred_element_type=F32)
            out_ref[0, g * NSA_DK:(g + 1) * NSA_DK, :] = acc[g * NSA_DK:(g + 1) * NSA_DK]
            p_sum = p_cmp[:, 0:QB]
            for i in range(1, R):
                p_sum = p_sum + p_cmp[:, i * QB:(i + 1) * QB]
            p_hi, p_lo = _split_bf16(p_sum)
            imp = (jnp.dot(ov_ref[0:nb, 0:nc], p_hi, preferred_element_type=F32)
                   + jnp.dot(ov_ref[0:nb, 0:nc], p_lo, preferred_element_type=F32))
            scores.append(jnp.where(forced, -jnp.inf, jnp.where(valid, imp, -SEL_BIG)))

        for _ in range(topk - (1 + SLC_LOCAL)):
            scores = [pick_round(score) for score in scores]
        for g in range(G):
            selected = ((scores[g] == -jnp.inf) | (q_blk < topk)) & valid
            bias = jnp.where(selected, 0.0, NEG)
            if nb < LANES:
                bias = jnp.concatenate([bias, jnp.full((LANES - nb, QB), NEG, F32)], axis=0)
            bias = bias.T.astype(BF16)
            for i in range(R):
                qaug_ref[g, i * QB:(i + 1) * QB, LANES:2 * LANES] = bias
        slc_scores(0, 0)

        for g in range(G):
            sw = jnp.where(win_mask, win_scores[g], NEG)
            e_win = jnp.exp2(sw - jnp.max(sw, axis=0, keepdims=True)).astype(BF16)
            pv = jnp.dot(vwn_ref[0, g * LANES:(g + 1) * LANES, pl.ds(ks, WK)], e_win,
                         preferred_element_type=F32)
            out_ref[1, g * NSA_DK:(g + 1) * NSA_DK, :] = normalized(pv, g)

    n_visible = (t0 + QB - CMP_BLOCK) // CMP_STRIDE + 1
    n_variants = NC // LANES
    for v in range(n_variants):
        @pl.when(jnp.minimum((n_visible - 1) // LANES, n_variants - 1) == v)
        def _():
            compressed_select_window((v + 1) * LANES)

    def slc_accumulate(k0, slot, causal):
        for g in range(G):
            s = s_ref[slot, g]
            if causal:
                k_pos = k0 + lax.broadcasted_iota(jnp.int32, (KT, QB), 0)
                t_pos = t0 + lax.broadcasted_iota(jnp.int32, (KT, QB), 1)
                s = jnp.where(per_head(k_pos <= t_pos), s, NEG)
            m_run = m_ref[g]
            m_new = jnp.maximum(m_run, jnp.max(s, axis=0, keepdims=True))
            p = jnp.exp2(s - m_new).astype(BF16)
            pv = jnp.dot(vsl_ref[0, g * LANES:(g + 1) * LANES, pl.ds(k0, KT)], p,
                         preferred_element_type=F32)
            acc_ref[g] = jnp.exp2(m_run - m_new) * acc_ref[g] + pv
            m_ref[g] = m_new

    m_ref[...] = jnp.full(m_ref.shape, NEG, F32)
    acc_ref[...] = jnp.zeros_like(acc_ref)
    n_full = t0 // KT
    U = SLC_UNROLL

    def slc_run(k0, n_tiles, last_is_diagonal):
        for u in range(n_tiles):
            causal = last_is_diagonal and u == n_tiles - 1
            if not causal:
                slc_scores(k0 + (u + 1) * KT, (u + 1) % 2)
            slc_accumulate(k0 + u * KT, u % 2, causal)

    def slc_group(j, _):
        slc_run(pl.multiple_of(j * (U * KT), U * KT), U, False)
        return 0

    lax.fori_loop(0, n_full // U, slc_group, 0)
    k_rest = pl.multiple_of((n_full // U) * (U * KT), U * KT)
    for r in range(U):
        @pl.when(n_full % U == r)
        def _():
            slc_run(k_rest, r + 1, True)

    o_slc = [normalized(acc_ref[g], g) for g in range(G)]

    gates = (1.0 / (1.0 + jnp.exp(-ng_ref[...]))).T

    def gate_rows(branch, head):
        r = branch * NSA_HEADS + head
        return jnp.broadcast_to(gates[r:r + 1, :], (NSA_DK, QB))

    branches = [out_ref[0], jnp.concatenate(o_slc, axis=0), out_ref[1]]
    for i in range(R):
        y = jnp.zeros((LANES, QB), F32)
        for b, o in enumerate(branches):
            gate = jnp.concatenate([gate_rows(b, g * R + i) for g in range(G)], axis=0)
            y = y + gate * o[:, i * QB:(i + 1) * QB]
        o_ref[:, i * LANES:(i + 1) * LANES] = y.T.astype(BF16)


def _nsa(nq, ng, kvc_cmp, kvc_cmp_t, ksl, vsl, kwn, vwn, overlap, batch, seq_len):
    T = nq.shape[0]
    QB = NSA_QBLOCK
    rows = QB * NSA_TILES_PER_STEP
    nqb = seq_len // rows
    nc = seq_len // CMP_STRIDE
    topk = min(SLC_TOPK, seq_len // SLC_BLOCK)
    tok = lambda n: pl.BlockSpec((rows, n), lambda b, i: (b * nqb + i, 0))
    seq = lambda n: pl.BlockSpec((1, seq_len, n), lambda b, i: (b, 0, 0))
    seq_t = pl.BlockSpec((1, 2 * LANES, seq_len), lambda b, i: (b, 0, 0))
    return pl.pallas_call(
        functools.partial(_nsa_kernel, seq_len=seq_len, topk=topk),
        grid=(batch, nqb),
        in_specs=[tok(D_NSA), tok(LANES),
                  pl.BlockSpec((1, 1, nc, LANES), lambda b, i: (0, b, 0, 0)),
                  pl.BlockSpec((1, 1, LANES, nc), lambda b, i: (1, b, 0, 0)),
                  seq(2 * LANES), seq_t, seq(LANES), seq_t,
                  pl.BlockSpec(overlap.shape, lambda b, i: (0, 0))],
        out_specs=tok(D_NSA),
        out_shape=jax.ShapeDtypeStruct((T, D_NSA), BF16),
        scratch_shapes=[pltpu.VMEM((NSA_KV_HEADS, NSA_GROUP * QB, 2 * LANES), BF16),
                        pltpu.VMEM((2, NSA_KV_HEADS, SLC_KTILE, NSA_GROUP * QB), F32),
                        pltpu.VMEM((NSA_KV_HEADS, 1, NSA_GROUP * QB), F32),
                        pltpu.VMEM((NSA_KV_HEADS, LANES, NSA_GROUP * QB), F32),
                        pltpu.VMEM((2, LANES, NSA_GROUP * QB), F32)],
        compiler_params=pltpu.CompilerParams(dimension_semantics=("arbitrary", "arbitrary"),
                                             vmem_limit_bytes=V7X_VMEM_LIMIT),
        name="nsa",
    )(nq, ng, kvc_cmp, kvc_cmp_t, ksl.reshape(batch, seq_len, -1), vsl,
      kwn.reshape(batch, seq_len, -1), vwn, overlap)


def _ffn_kernel(yr_ref, yrp_ref, yn_ref, ynp_ref, x_ref, xp_ref, wo_ref, gmix_ref, gpre_ref,
                wu_ref, conv_ref, wd_ref, gpost_ref, o_ref,
                h_ref, ug_ref, uv_ref, acc_ref, xmid_ref, *, tiles_per_seq):
    tm = x_ref.shape[0]
    d_ff = wd_ref.shape[0]
    tf = FFN_FTILE
    H = FFN_HALO
    rc = tm // FFN_ROW_CHUNKS

    def rms(x, gain_ref):
        return x * lax.rsqrt(jnp.mean(x * x, axis=-1, keepdims=True) + NORM_EPS) * gain_ref[...]

    keep = jnp.where(pl.program_id(0) % tiles_per_seq == 0, 0.0, 1.0)
    for c in range(FFN_ROW_CHUNKS):
        rows = slice(c * rc, (c + 1) * rc)
        y_ret, y_nsa = yr_ref[rows, :], yn_ref[rows, :]
        if c == 0:
            y_ret = jnp.concatenate([yrp_ref[...], y_ret], axis=0)
            y_nsa = jnp.concatenate([ynp_ref[...], y_nsa], axis=0)
        mix = (jnp.dot(y_ret, wo_ref[0:D_RET, :], preferred_element_type=F32)
               + jnp.dot(y_nsa, wo_ref[D_RET:D_RET + D_NSA, :], preferred_element_type=F32))
        post = rms(mix, gmix_ref)
        if c == 0:
            h_ref[0:H, :] = (rms(xp_ref[...] + post[0:H], gpre_ref) * keep).astype(BF16)
            post = post[H:H + rc]
        x_mid = x_ref[rows, :] + post
        xmid_ref[rows, :] = x_mid
        h_ref[H + c * rc:H + (c + 1) * rc, :] = rms(x_mid, gpre_ref).astype(BF16)
    acc_ref[...] = jnp.zeros_like(acc_ref)

    n_tiles = d_ff // tf

    def columns(f):
        return (pl.ds(pl.multiple_of(f * tf, tf), tf), pl.ds(pl.multiple_of(d_ff + f * tf, tf), tf))

    def up_project(f, c):
        gate_cols, value_cols = columns(f)
        rows = slice(0 if c == 0 else H + c * rc, H + (c + 1) * rc)
        h = h_ref[rows, :]
        ug_ref[rows, :] = jnp.dot(h, wu_ref[:, gate_cols], preferred_element_type=F32)
        uv_ref[rows, :] = jnp.dot(h, wu_ref[:, value_cols], preferred_element_type=F32)

    def hidden_tile(f, carry):
        gate_cols, value_cols = columns(f)

        def causal_conv(u_ref, cols, r0, scale):
            out = (scale * conv_ref[CONV_WIDTH - 1:CONV_WIDTH, cols]) * u_ref[r0:r0 + rc, :]
            for k in range(CONV_WIDTH - 1):
                d = CONV_WIDTH - 1 - k
                out = out + (scale * conv_ref[k:k + 1, cols]) * u_ref[r0 - d:r0 - d + rc, :]
            return out

        def gate_and_down(c):
            g = causal_conv(ug_ref, gate_cols, H + c * rc, 1.0)
            v_half = causal_conv(uv_ref, value_cols, H + c * rc, 0.5)
            inner = g * (GELU_C + (GELU_C * 0.044715) * (g * g))
            act = (g + g * jnp.tanh(inner)) * v_half
            acc_ref[c * rc:(c + 1) * rc, :] += jnp.dot(act.astype(BF16), wd_ref[gate_cols, :],
                                                       preferred_element_type=F32)

        up_project(f, 0)
        for c in range(FFN_ROW_CHUNKS):
            if c + 1 < FFN_ROW_CHUNKS:
                up_project(f, c + 1)
            gate_and_down(c)
        return carry

    lax.fori_loop(0, n_tiles, hidden_tile, 0)
    o_ref[...] = xmid_ref[...] + rms(acc_ref[...], gpost_ref)


def _outproj_ffn(y_ret, y_nsa, x2, w_out, g_mix, g_pre, w_up, conv_w, w_down, layer, g_post, seq_len):
    T, D = x2.shape
    d_ff = w_down.shape[1]
    tm, tf, H = FFN_TILE, FFN_FTILE, FFN_HALO
    tile = lambda n: pl.BlockSpec((tm, n), lambda i: (i, 0))
    halo = lambda n: pl.BlockSpec((H, n), lambda i: (jnp.maximum(i * (tm // H) - 1, 0), 0))
    resident = lambda a: pl.BlockSpec((None,) + a.shape[1:], lambda i: (layer, 0, 0),
                                      pipeline_mode=pl.Buffered(1))
    gain = pl.BlockSpec((1, D), lambda i: (0, 0), pipeline_mode=pl.Buffered(1))
    return pl.pallas_call(
        functools.partial(_ffn_kernel, tiles_per_seq=seq_len // tm),
        grid=(T // tm,),
        in_specs=[tile(D_RET), halo(D_RET), tile(D_NSA), halo(D_NSA), tile(D), halo(D),
                  resident(w_out), gain, gain, resident(w_up), resident(conv_w), resident(w_down), gain],
        out_specs=tile(D),
        out_shape=jax.ShapeDtypeStruct((T, D), F32),
        scratch_shapes=[pltpu.VMEM((tm + H, D), BF16), pltpu.VMEM((tm + H, tf), F32),
                        pltpu.VMEM((tm + H, tf), F32), pltpu.VMEM((tm, D), F32),
                        pltpu.VMEM((tm, D), F32)],
        compiler_params=pltpu.CompilerParams(dimension_semantics=("arbitrary",),
                                             vmem_limit_bytes=V7X_VMEM_LIMIT),
        name="ffn",
    )(y_ret, y_ret, y_nsa, y_nsa, x2, x2, w_out, g_mix, g_pre, w_up, conv_w, w_down, g_post)


_NSA_HEAD_ORDER = [g * NSA_GROUP + i for i in range(NSA_GROUP) for g in range(NSA_KV_HEADS)]


def _rope_tables(pos):
    inv = 1.0 / (ROPE_THETA ** (jnp.arange(0, NSA_DK, 2, dtype=F32) / NSA_DK))
    ang = pos.astype(F32)[:, None] * inv[None, :]
    c, s = jnp.cos(ang), jnp.sin(ang)
    return jnp.concatenate([c, c, c, c], axis=1), jnp.concatenate([-s, s, -s, s], axis=1)


def _prep_w_in(w):
    lead = w.shape[:-1]
    splits = np.cumsum([RET_QK, RET_QK, D_RET, D_RET, D_NSA] + [NSA_KV] * 6)
    rq, rk, rv, rg, nq, kcm, vcm, ksl, vsl, kwn, vwn, ng = jnp.split(w, [int(s) for s in splits], axis=-1)
    nq = nq.reshape(lead + (NSA_HEADS, NSA_DK))[..., np.array(_NSA_HEAD_ORDER), :].reshape(lead + (D_NSA,))
    ng = jnp.pad(ng, [(0, 0)] * len(lead) + [(0, LANES - NSA_GATES)])
    return jnp.concatenate([rq, rk, nq, ksl, kwn, rv, vsl, vwn, rg, kcm, vcm, ng], axis=-1).astype(BF16)


def _prep_w_out(w):
    layers, _, d = w.shape
    w_nsa = w[:, D_RET:].reshape(layers, NSA_HEADS, NSA_DK, d)[:, np.array(_NSA_HEAD_ORDER)]
    return jnp.concatenate([w[:, :D_RET], w_nsa.reshape(layers, D_NSA, d)], axis=1).astype(BF16)


def _prep_compress(pos, w1, w2):
    lead = w1.shape[:-2]
    half = CMP_STRIDE

    def block_diag(w, axis):
        z = jnp.zeros_like(w)
        return jnp.stack([jnp.concatenate([w, z], axis=-1), jnp.concatenate([z, w], axis=-1)], axis=axis)

    def first_layer(w_half):
        w4 = w_half.reshape(lead + (half, NSA_DK, CMP_HID))
        return block_diag(w4, -3).reshape(lead + (half * NSA_KV_HEADS * NSA_DK, -1)).astype(BF16)

    def pos_row(p_half):
        rows = jnp.broadcast_to(p_half[..., :, None, :], lead + (half, NSA_KV_HEADS, NSA_DK))
        return rows.reshape(lead + (1, -1))

    w2x = block_diag(w2, -3).reshape(lead + (NSA_KV_HEADS * CMP_HID, -1)).astype(BF16)
    n1 = half * NSA_DK
    return (pos_row(pos[..., :half, :]), pos_row(pos[..., half:, :]),
            first_layer(w1[..., :n1, :]), first_layer(w1[..., n1:, :]), w2x)


def _overlap_matrix(seq_len):
    nc = seq_len // CMP_STRIDE
    cmp_start = np.arange(nc) * CMP_STRIDE
    slc_start = np.arange(LANES) * SLC_BLOCK
    ov = ((cmp_start[:, None] < slc_start[None, :] + SLC_BLOCK)
          & (cmp_start[:, None] + CMP_BLOCK > slc_start[None, :]))
    n_cmp = (seq_len - CMP_BLOCK) // CMP_STRIDE + 1
    ov &= (np.arange(nc) < n_cmp)[:, None]
    return jnp.asarray(ov.T.astype(np.float32), dtype=BF16)


def kernel(x, norm_mix_pre, w_in, ret_gn_w, cmp_k_pos, cmp_k_w1, cmp_k_w2, cmp_v_pos, cmp_v_w1, cmp_v_w2,
           w_out, norm_mix_post, norm_ffn_pre, ffn_w_up, ffn_conv, ffn_w_down, norm_ffn_post):
    B, S, D = x.shape
    depth = w_in.shape[0]
    assert S % SLC_KTILE == 0 and S % FFN_TILE == 0 and S // SLC_BLOCK <= LANES and S >= WINDOW + NSA_QBLOCK
    assert ffn_w_down.shape[1] % FFN_FTILE == 0
    assert min(SLC_TOPK, S // SLC_BLOCK) > 1 + SLC_LOCAL and S % (NSA_QBLOCK * NSA_TILES_PER_STEP) == 0

    cos, sin = _rope_tables(jnp.arange(S, dtype=jnp.int32))
    nc = S // CMP_STRIDE
    ccos, csin = _rope_tables(jnp.arange(nc, dtype=jnp.int32) * CMP_STRIDE + (CMP_BLOCK - 1))
    cmp_cos = jnp.stack([ccos, jnp.ones_like(ccos)])
    cmp_sin = jnp.stack([csin, jnp.zeros_like(csin)])
    overlap = _overlap_matrix(S)

    w_in_p = _prep_w_in(w_in)
    w_out_p = _prep_w_out(w_out)
    w_up_p = ffn_w_up.astype(BF16)
    w_down_p = ffn_w_down.astype(BF16)
    cmp_p = _prep_compress(jnp.stack([cmp_k_pos, cmp_v_pos], axis=1), jnp.stack([cmp_k_w1, cmp_v_w1], axis=1),
                           jnp.stack([cmp_k_w2, cmp_v_w2], axis=1))

    x2 = x.reshape(B * S, D)
    for l in range(depth):
        outs = _inproj(x2, norm_mix_pre[l][None], w_in_p, l, cos, sin, S)
        rq, rk, nq, ksl, kwn, rv, vsl, vwn, rg, kvc, ng = outs
        y_ret = _retention(rq, rk, rv, rg, ret_gn_w[l][None], B, S)
        kvc_cmp, kvc_cmp_t = _compress(kvc, *cmp_p, l, cmp_cos, cmp_sin, B, S)
        y_nsa = _nsa(nq, ng, kvc_cmp, kvc_cmp_t, ksl, vsl, kwn, vwn, overlap, B, S)
        x2 = _outproj_ffn(y_ret, y_nsa, x2, w_out_p, norm_mix_post[l][None], norm_ffn_pre[l][None],
                          w_up_p, ffn_conv, w_down_p, l, norm_ffn_post[l][None], S)
    return x2.reshape(B, S, D)
```

```python
import functools
import math

import jax
import jax.numpy as jnp
import numpy as np
from jax import lax
from jax.experimental import pallas as pl
from jax.experimental.pallas import tpu as pltpu

F32 = jnp.float32
BF16 = jnp.bfloat16

LANES = 128
SUBLANES = 8
V7X_VMEM_LIMIT = 56 * 1024 * 1024

ROPE_THETA = 10000.0
NORM_EPS = 1e-6
GN_EPS = 1e-5
NEG = -1e30
SEL_BIG = 1e9

RET_HEADS = 4
RET_DK = 64
RET_DV = 128
RET_CHUNK = 128
NSA_HEADS = 8
NSA_KV_HEADS = 2
NSA_DK = 64
NSA_GROUP = NSA_HEADS // NSA_KV_HEADS
CMP_BLOCK = 32
CMP_STRIDE = 16
CMP_HID = 256
SLC_BLOCK = 64
SLC_TOPK = 16
SLC_LOCAL = 2
WINDOW = 512
NSA_QBLOCK = 128
NSA_TILES_PER_STEP = 4
NSA_GATES = 3 * NSA_HEADS
CONV_WIDTH = 3

NSA_Q_SCALE = NSA_DK ** -0.5 * math.log2(math.e)

D_RET = RET_HEADS * RET_DV
D_NSA = NSA_HEADS * NSA_DK
RET_QK = RET_HEADS * RET_DK
NSA_KV = NSA_KV_HEADS * NSA_DK

COL_RQ = 0
COL_RK = COL_RQ + RET_QK
COL_NQ = COL_RK + RET_QK
COL_KSL = COL_NQ + D_NSA
COL_KWN = COL_KSL + NSA_KV
ROPE_COLS = COL_KWN + NSA_KV
COL_RV = ROPE_COLS
COL_VSL = COL_RV + D_RET
COL_VWN = COL_VSL + NSA_KV
COL_RG = COL_VWN + NSA_KV
COL_KCM = COL_RG + D_RET
COL_VCM = COL_KCM + NSA_KV
COL_NG = COL_VCM + NSA_KV
IN_COLS_PAD = COL_NG + LANES

TOK_TILE = 512
RET_TILE = 512
FFN_TILE = 1024
FFN_FTILE = 256
FFN_HALO = 16
FFN_ROW_CHUNKS = 4
SLC_KTILE = 512
SLC_UNROLL = 4


GELU_C = math.sqrt(2.0 / math.pi)


def _gelu_tanh(x):
    return 0.5 * x * (1.0 + jnp.tanh(GELU_C * (x + 0.044715 * (x * x * x))))


def _rope(p, cos, sin_signed, first_half):
    half = NSA_DK // 2
    partner = jnp.where(first_half, pltpu.roll(p, LANES - half, 1), pltpu.roll(p, half, 1))
    return p * cos + partner * sin_signed


def _inproj_kernel(x_ref, g_ref, w_ref, cos_ref, sin_ref,
                   rq_ref, rk_ref, nq_ref, ksl_ref, kwn_ref, rv_ref, vsl_ref, vwn_ref,
                   rg_ref, kvc_ref, ng_ref, stage_ref, *, seq_len):
    tm = x_ref.shape[0]
    x = x_ref[...]
    h = (x * lax.rsqrt(jnp.mean(x * x, axis=-1, keepdims=True) + NORM_EPS) * g_ref[...]).astype(BF16)
    cos = cos_ref[...]
    sin = sin_ref[...]
    lane = lax.broadcasted_iota(jnp.int32, (tm, LANES), 1)
    first_half = (lane % NSA_DK) < NSA_DK // 2

    def proj(c0, n):
        return jnp.dot(h, w_ref[:, c0:c0 + n], preferred_element_type=F32)

    def rope_slab(p, i):
        return _rope(p[:, i * LANES:(i + 1) * LANES], cos, sin, first_half)

    def store_rq(p):
        for i in range(RET_QK // LANES):
            rq_ref[:, i * LANES:(i + 1) * LANES] = rope_slab(p, i).astype(BF16)

    def store_rk(p):
        for i in range(RET_QK // LANES):
            rk_ref[:, i * LANES:(i + 1) * LANES] = (rope_slab(p, i) * (RET_DK ** -0.5)).astype(BF16)

    def store_nq(p):
        for i in range(D_NSA // LANES):
            nq_ref[:, i * LANES:(i + 1) * LANES] = (rope_slab(p, i) * NSA_Q_SCALE).astype(BF16)

    def store_keys(p):
        ksl_ref[:, 0:LANES] = rope_slab(p, 0).astype(BF16)
        kwn_ref[...] = rope_slab(p, 1).astype(BF16)
        row = lax.broadcasted_iota(jnp.int32, (tm, LANES), 0)
        pos = (pl.program_id(0) * tm + row) % seq_len
        ksl_ref[:, LANES:2 * LANES] = jnp.where(lane == pos // SLC_BLOCK, 1.0, 0.0).astype(BF16)

    def store_rv(p):
        rv_ref[...] = p.astype(BF16)

    def store_values(p):
        low_half = lane < NSA_DK
        for v_ref, v in ((vsl_ref, p[:, 0:LANES]), (vwn_ref, p[:, LANES:2 * LANES])):
            v_ref[0, 0:LANES, :] = jnp.where(low_half, v, 1.0).T.astype(BF16)
            v_ref[0, LANES:2 * LANES, :] = jnp.where(low_half, 1.0, v).T.astype(BF16)

    def store_rg(p):
        rg_ref[...] = p

    def store_compress_inputs(p):
        for s in range(2):
            stage_ref[...] = p[:, s * LANES:(s + 1) * LANES]
            for l in range(CMP_STRIDE):
                kvc_ref[s, :, l * LANES:(l + 1) * LANES] = stage_ref[
                    pl.ds(l, tm // CMP_STRIDE, stride=CMP_STRIDE), :]

    def store_gates(p):
        ng_ref[...] = p

    stages = [(COL_RQ, RET_QK, store_rq), (COL_RK, RET_QK, store_rk), (COL_NQ, D_NSA, store_nq),
              (COL_KSL, 2 * NSA_KV, store_keys), (COL_RV, D_RET, store_rv),
              (COL_VSL, 2 * NSA_KV, store_values), (COL_RG, D_RET, store_rg),
              (COL_KCM, 2 * NSA_KV, store_compress_inputs), (COL_NG, LANES, store_gates)]
    pending = None
    for c0, n, store in stages:
        p = proj(c0, n)
        if pending is not None:
            pending[1](pending[0])
        pending = (p, store)
    pending[1](pending[0])


def _inproj(x2, gain, w, layer, cos, sin, seq_len):
    T, D = x2.shape
    tm = TOK_TILE
    nt = seq_len // tm
    tok = lambda n: pl.BlockSpec((tm, n), lambda i: (i, 0))
    tok_t = pl.BlockSpec((1, 2 * LANES, tm), lambda i: (i // nt, 0, i % nt))
    out_shape = (
        jax.ShapeDtypeStruct((T, RET_QK), BF16),
        jax.ShapeDtypeStruct((T, RET_QK), BF16),
        jax.ShapeDtypeStruct((T, D_NSA), BF16),
        jax.ShapeDtypeStruct((T, 2 * LANES), BF16),
        jax.ShapeDtypeStruct((T, LANES), BF16),
        jax.ShapeDtypeStruct((T, D_RET), BF16),
        jax.ShapeDtypeStruct((T // seq_len, 2 * LANES, seq_len), BF16),
        jax.ShapeDtypeStruct((T // seq_len, 2 * LANES, seq_len), BF16),
        jax.ShapeDtypeStruct((T, D_RET), F32),
        jax.ShapeDtypeStruct((2, T // CMP_STRIDE, CMP_STRIDE * LANES), F32),
        jax.ShapeDtypeStruct((T, LANES), F32),
    )
    out_specs = (tok(RET_QK), tok(RET_QK), tok(D_NSA), tok(2 * LANES), tok(LANES), tok(D_RET),
                 tok_t, tok_t, tok(D_RET),
                 pl.BlockSpec((2, tm // CMP_STRIDE, CMP_STRIDE * LANES), lambda i: (0, i, 0)), tok(LANES))
    return pl.pallas_call(
        functools.partial(_inproj_kernel, seq_len=seq_len),
        grid=(T // tm,),
        in_specs=[tok(D),
                  pl.BlockSpec((1, D), lambda i: (0, 0)),
                  pl.BlockSpec((None, D, IN_COLS_PAD), lambda i: (layer, 0, 0)),
                  pl.BlockSpec((tm, LANES), lambda i: (i % nt, 0)),
                  pl.BlockSpec((tm, LANES), lambda i: (i % nt, 0))],
        out_specs=out_specs,
        out_shape=out_shape,
        scratch_shapes=[pltpu.VMEM((tm, LANES), F32)],
        compiler_params=pltpu.CompilerParams(dimension_semantics=("arbitrary",),
                                             vmem_limit_bytes=V7X_VMEM_LIMIT),
        name="inproj",
    )(x2, gain, w, cos, sin)


def _retention_kernel(q_ref, k_ref, v_ref, g_ref, gnw_ref, o_ref, state_ref, vbd_ref):
    C = RET_CHUNK
    n_chunks = q_ref.shape[0] // C

    @pl.when(pl.program_id(1) == 0)
    def _():
        state_ref[...] = jnp.zeros_like(state_ref)

    vbd_ref[...] = jnp.zeros_like(vbd_ref)

    ii = lax.broadcasted_iota(jnp.int32, (C, C), 0)
    jj = lax.broadcasted_iota(jnp.int32, (C, C), 1)
    diff = (ii - jj).astype(F32)
    i_col = lax.broadcasted_iota(jnp.int32, (C, 1), 0).astype(F32)
    low_half = lax.broadcasted_iota(jnp.int32, (C, LANES), 1) < RET_DK
    low_cols = lax.broadcasted_iota(jnp.int32, (C, 2 * RET_DV), 1) < RET_DV
    own_block = ((lax.broadcasted_iota(jnp.int32, (LANES, 2 * RET_DV), 0) < RET_DK)
                 == (lax.broadcasted_iota(jnp.int32, (LANES, 2 * RET_DV), 1) < RET_DV))
    nt_dims = (((1,), (1,)), ((), ()))
    tn_dims = (((0,), (0,)), ((), ()))

    for pair in range(RET_HEADS // 2):
        lg0, lg1 = (math.log(1.0 - 2.0 ** (-5.0 - h)) for h in (2 * pair, 2 * pair + 1))
        decay = [jnp.where(diff >= 0, jnp.exp(lg * jnp.maximum(diff, 0.0)), 0.0) for lg in (lg0, lg1)]
        xi = jnp.where(low_cols, jnp.exp(lg0 * (i_col + 1.0)), jnp.exp(lg1 * (i_col + 1.0)))
        zeta = jnp.where(low_half, jnp.exp(lg0 * (C - 1.0 - i_col)), jnp.exp(lg1 * (C - 1.0 - i_col)))
        chunk_decay = jnp.where(low_cols[0:1], math.exp(lg0 * C), math.exp(lg1 * C))
        qk_cols = slice(pair * LANES, (pair + 1) * LANES)
        v_cols = slice(2 * pair * RET_DV, 2 * (pair + 1) * RET_DV)
        for c in range(n_chunks):
            rows = slice(c * C, (c + 1) * C)
            q = q_ref[rows, qk_cols]
            ks = k_ref[rows, qk_cols]
            v = v_ref[rows, v_cols]
            qf = q.astype(F32)
            q_stack = jnp.concatenate([jnp.where(low_half, qf, 0.0), jnp.where(low_half, 0.0, qf)],
                                      axis=0).astype(BF16)
            s = lax.dot_general(q_stack, ks, nt_dims, preferred_element_type=F32)
            s_pair = jnp.concatenate([s[0:C] * decay[0], s[C:2 * C] * decay[1]], axis=1).astype(BF16)
            vbd_ref[pair, 0:C, 0:RET_DV] = v[:, 0:RET_DV]
            vbd_ref[pair, C:2 * C, RET_DV:2 * RET_DV] = v[:, RET_DV:2 * RET_DV]
            o = jnp.dot(s_pair, vbd_ref[pair], preferred_element_type=F32)
            state = state_ref[pair]
            o = o + jnp.dot(q, state.astype(BF16), preferred_element_type=F32) * xi
            kz = (ks.astype(F32) * zeta).astype(BF16)
            kv = lax.dot_general(kz, v, tn_dims, preferred_element_type=F32)
            state_ref[pair] = state * chunk_decay + jnp.where(own_block, kv, 0.0)
            for e in range(2):
                cols = slice((2 * pair + e) * RET_DV, (2 * pair + e + 1) * RET_DV)
                oh = o[:, e * RET_DV:(e + 1) * RET_DV]
                mu = jnp.mean(oh, axis=-1, keepdims=True)
                var = jnp.mean(jnp.square(oh - mu), axis=-1, keepdims=True)
                on = (oh - mu) * lax.rsqrt(var + GN_EPS) * gnw_ref[:, cols]
                gate = g_ref[rows, cols]
                o_ref[rows, cols] = (gate * (1.0 / (1.0 + jnp.exp(-gate))) * on).astype(BF16)


def _retention(rq, rk, rv, rg, gn_w, batch, seq_len):
    T = rq.shape[0]
    tc = RET_TILE
    nt = seq_len // tc
    tok = lambda n: pl.BlockSpec((tc, n), lambda b, i: (b * nt + i, 0))
    return pl.pallas_call(
        _retention_kernel,
        grid=(batch, nt),
        in_specs=[tok(RET_QK), tok(RET_QK), tok(D_RET), tok(D_RET),
                  pl.BlockSpec((1, D_RET), lambda b, i: (0, 0))],
        out_specs=tok(D_RET),
        out_shape=jax.ShapeDtypeStruct((T, D_RET), BF16),
        scratch_shapes=[pltpu.VMEM((RET_HEADS // 2, LANES, 2 * RET_DV), F32),
                        pltpu.VMEM((RET_HEADS // 2, 2 * RET_CHUNK, 2 * RET_DV), BF16)],
        compiler_params=pltpu.CompilerParams(dimension_semantics=("arbitrary", "arbitrary"),
                                             vmem_limit_bytes=V7X_VMEM_LIMIT),
        name="retention",
    )(rq, rk, rv, rg, gn_w)


def _compress_kernel(x_ref, pa_ref, pb_ref, wa_ref, wb_ref, w2_ref, cos_ref, sin_ref, o_ref, ot_ref):
    ng = x_ref.shape[2]
    x = x_ref[0, 0]
    xa = (x + pa_ref[0]).astype(BF16)
    xb = (x + pb_ref[0]).astype(BF16)
    a = jnp.dot(xa, wa_ref[0], preferred_element_type=F32)
    b = jnp.dot(xb, wb_ref[0], preferred_element_type=F32)
    hid = a + pltpu.roll(b, ng - 1, 0)
    out = jnp.dot(_gelu_tanh(hid).astype(BF16), w2_ref[0], preferred_element_type=F32)
    lane = lax.broadcasted_iota(jnp.int32, out.shape, 1)
    out = _rope(out, cos_ref[0], sin_ref[0], (lane % NSA_DK) < NSA_DK // 2)
    o_ref[0, 0] = out.astype(BF16)
    ot_ref[0, 0] = out.T.astype(BF16)


def _compress(kvc, pos_a, pos_b, wa, wb, w2, layer, cos, sin, batch, seq_len):
    ng = seq_len // CMP_STRIDE
    gw = CMP_STRIDE * LANES
    x = kvc.reshape(2, batch, ng, gw)
    hid = NSA_KV_HEADS * CMP_HID
    per_kv = lambda *shape: pl.BlockSpec((1,) + shape, lambda s, b: (s,) + (0,) * len(shape))
    per_lkv = lambda *shape: pl.BlockSpec((None, 1) + shape, lambda s, b: (layer, s) + (0,) * len(shape))
    return pl.pallas_call(
        _compress_kernel,
        grid=(2, batch),
        in_specs=[pl.BlockSpec((1, 1, ng, gw), lambda s, b: (s, b, 0, 0)),
                  per_lkv(1, gw), per_lkv(1, gw), per_lkv(gw, hid), per_lkv(gw, hid), per_lkv(hid, LANES),
                  per_kv(ng, LANES), per_kv(ng, LANES)],
        out_specs=(pl.BlockSpec((1, 1, ng, LANES), lambda s, b: (s, b, 0, 0)),
                   pl.BlockSpec((1, 1, LANES, ng), lambda s, b: (s, b, 0, 0))),
        out_shape=(jax.ShapeDtypeStruct((2, batch, ng, LANES), BF16),
                   jax.ShapeDtypeStruct((2, batch, LANES, ng), BF16)),
        compiler_params=pltpu.CompilerParams(dimension_semantics=("arbitrary", "arbitrary"),
                                             vmem_limit_bytes=V7X_VMEM_LIMIT),
        name="compress",
    )(x, pos_a, pos_b, wa, wb, w2, cos, sin)


def _nsa_kernel(nq_ref, ng_ref, *refs, seq_len, topk):
    n_in = 6
    o_ref = refs[n_in]

    def one_tile(j, carry):
        rows = pl.ds(pl.multiple_of(j * NSA_QBLOCK, NSA_QBLOCK), NSA_QBLOCK)
        t0 = (pl.program_id(1) * NSA_TILES_PER_STEP + j) * NSA_QBLOCK
        _nsa_tile(t0, nq_ref.at[rows], ng_ref.at[rows], *refs[:n_in], o_ref.at[rows], *refs[n_in + 1:],
                  seq_len=seq_len, topk=topk)
        return carry

    lax.fori_loop(0, NSA_TILES_PER_STEP, one_tile, 0)


def _nsa_tile(t0, nq_ref, ng_ref, kc_ref, vc_ref, ksl_ref, vsl_ref, kwn_ref, vwn_ref,
              o_ref, qaug_ref, s_ref, m_ref, acc_ref, out_ref, psum_ref, *, seq_len, topk):
    QB = NSA_QBLOCK
    R = NSA_GROUP
    G = NSA_KV_HEADS
    M = R * QB
    NC = kc_ref.shape[2]
    KT = SLC_KTILE
    WK = WINDOW + QB
    nt_dims = (((1,), (1,)), ((), ()))

    low_half = lax.broadcasted_iota(jnp.int32, (QB, LANES), 1) < NSA_DK
    q = nq_ref[...].astype(F32)

    def per_head(x):
        return jnp.concatenate([x] * R, axis=1)

    def normalized(acc, g):
        num, den = (acc[0:NSA_DK], acc[NSA_DK:NSA_DK + 1]) if g == 0 else (acc[NSA_DK:], acc[0:1])
        return num * (1.0 / den)

    for g in range(G):
        head_lanes = low_half if g == 0 else jnp.logical_not(low_half)
        for i in range(R):
            qaug_ref[g, i * QB:(i + 1) * QB, 0:LANES] = jnp.where(
                head_lanes, q[:, i * LANES:(i + 1) * LANES], 0.0).astype(BF16)

    ks = pl.multiple_of(jnp.clip(t0 - WINDOW, 0, seq_len - WK), QB)
    k_pos = ks + lax.broadcasted_iota(jnp.int32, (WK, QB), 0)
    t_pos = t0 + lax.broadcasted_iota(jnp.int32, (WK, QB), 1)
    win_mask = per_head((k_pos <= t_pos) & (k_pos > t_pos - WINDOW))

    def slc_scores(k0, slot):
        for g in range(G):
            s_ref[slot, g] = lax.dot_general(ksl_ref[0, pl.ds(k0, KT), :], qaug_ref[g], nt_dims,
                                             preferred_element_type=F32)

    def compressed_select_window(nc):
        win_scores = [lax.dot_general(kwn_ref[0, pl.ds(ks, WK), :], qaug_ref[g, :, 0:LANES], nt_dims,
                                      preferred_element_type=F32) for g in range(G)]
        cmp_scores = [lax.dot_general(kc_ref[0, 0, 0:nc, :], qaug_ref[g, :, 0:LANES], nt_dims,
                                      preferred_element_type=F32) for g in range(G)]

        nb = min(nc * CMP_STRIDE // SLC_BLOCK, LANES)
        jj = lax.broadcasted_iota(jnp.int32, (nb, QB), 0)
        q_blk = (t0 + lax.broadcasted_iota(jnp.int32, (nb, QB), 1)) // SLC_BLOCK
        valid = jj <= q_blk
        forced = (jj == 0) | (valid & (jj > q_blk - SLC_LOCAL))
        jf = jj.astype(F32)

        def pick_round(score):
            best = jnp.max(score, axis=0, keepdims=True)
            first = jnp.min(jnp.where(score == best, jf, float(LANES)), axis=0, keepdims=True)
            return jnp.where(jf == first, -jnp.inf, score)

        scores = []
        n_idx = lax.broadcasted_iota(jnp.int32, (nc, QB), 0)
        t_col = t0 + lax.broadcasted_iota(jnp.int32, (nc, QB), 1)
        cmp_mask = per_head(n_idx * CMP_STRIDE + (CMP_BLOCK - 1) <= t_col)
        for g in range(G):
            sc = jnp.where(cmp_mask, cmp_scores[g], NEG)
            e = jnp.exp2(sc - jnp.max(sc, axis=0, keepdims=True))
            if nc == LANES:
                e = jnp.where(cmp_mask, e, 0.0)
            p_cmp = e * (1.0 / jnp.maximum(jnp.sum(e, axis=0, keepdims=True), 1e-30))
            acc = jnp.dot(vc_ref[0, 0, :, 0:nc], p_cmp.astype(BF16), preferred_element_type=F32)
            out_ref[0, g * NSA_DK:(g + 1) * NSA_DK, :] = acc[g * NSA_DK:(g + 1) * NSA_DK]
            p_sum = p_cmp[:, 0:QB]
            for i in range(1, R):
                p_sum = p_sum + p_cmp[:, i * QB:(i + 1) * QB]
            psum_ref[g, 0:SUBLANES, :] = jnp.zeros((SUBLANES, QB), F32)
            psum_ref[g, SUBLANES:SUBLANES + nc, :] = p_sum
            ratio, lead = SLC_BLOCK // CMP_STRIDE, CMP_BLOCK // CMP_STRIDE - 1
            imp = psum_ref[g, pl.ds(SUBLANES - lead, nb, stride=ratio), :]
            for d in range(1 - lead, ratio):
                imp = imp + psum_ref[g, pl.ds(SUBLANES + d, nb, stride=ratio), :]
            scores.append(jnp.where(forced, -jnp.inf, jnp.where(valid, imp, -SEL_BIG)))

        for _ in range(topk - (1 + SLC_LOCAL)):
            scores = [pick_round(score) for score in scores]
        for g in range(G):
            selected = ((scores[g] == -jnp.inf) | (q_blk < topk)) & valid
            bias = jnp.where(selected, 0.0, NEG)
            if nb < LANES:
                bias = jnp.concatenate([bias, jnp.full((LANES - nb, QB), NEG, F32)], axis=0)
            bias = bias.T.astype(BF16)
            for i in range(R):
                qaug_ref[g, i * QB:(i + 1) * QB, LANES:2 * LANES] = bias
        slc_scores(0, 0)

        for g in range(G):
            sw = jnp.where(win_mask, win_scores[g], NEG)
            e_win = jnp.exp2(sw - jnp.max(sw, axis=0, keepdims=True)).astype(BF16)
            pv = jnp.dot(vwn_ref[0, g * LANES:(g + 1) * LANES, pl.ds(ks, WK)], e_win,
                         preferred_element_type=F32)
            out_ref[1, g * NSA_DK:(g + 1) * NSA_DK, :] = normalized(pv, g)

    n_visible = (t0 + QB - CMP_BLOCK) // CMP_STRIDE + 1
    n_variants = NC // LANES
    for v in range(n_variants):
        @pl.when(jnp.minimum((n_visible - 1) // LANES, n_variants - 1) == v)
        def _():
            compressed_select_window((v + 1) * LANES)

    def slc_accumulate(k0, slot, causal):
        for g in range(G):
            s = s_ref[slot, g]
            if causal:
                k_pos = k0 + lax.broadcasted_iota(jnp.int32, (KT, QB), 0)
                t_pos = t0 + lax.broadcasted_iota(jnp.int32, (KT, QB), 1)
                s = jnp.where(per_head(k_pos <= t_pos), s, NEG)
            m_run = m_ref[g]
            m_new = jnp.maximum(m_run, jnp.max(s, axis=0, keepdims=True))
            p = jnp.exp2(s - m_new).astype(BF16)
            pv = jnp.dot(vsl_ref[0, g * LANES:(g + 1) * LANES, pl.ds(k0, KT)], p,
                         preferred_element_type=F32)
            acc_ref[g] = jnp.exp2(m_run - m_new) * acc_ref[g] + pv
            m_ref[g] = m_new

    m_ref[...] = jnp.full(m_ref.shape, NEG, F32)
    acc_ref[...] = jnp.zeros_like(acc_ref)
    n_full = t0 // KT
    U = SLC_UNROLL

    def slc_run(k0, n_tiles, last_is_diagonal):
        for u in range(n_tiles):
            causal = last_is_diagonal and u == n_tiles - 1
            if not causal:
                slc_scores(k0 + (u + 1) * KT, (u + 1) % 2)
            slc_accumulate(k0 + u * KT, u % 2, causal)

    def slc_group(j, _):
        slc_run(pl.multiple_of(j * (U * KT), U * KT), U, False)
        return 0

    lax.fori_loop(0, n_full // U, slc_group, 0)
    k_rest = pl.multiple_of((n_full // U) * (U * KT), U * KT)
    for r in range(U):
        @pl.when(n_full % U == r)
        def _():
            slc_run(k_rest, r + 1, True)

    o_slc = [normalized(acc_ref[g], g) for g in range(G)]

    gates = (1.0 / (1.0 + jnp.exp(-ng_ref[...]))).T

    def gate_rows(branch, head):
        r = branch * NSA_HEADS + head
        return jnp.broadcast_to(gates[r:r + 1, :], (NSA_DK, QB))

    branches = [out_ref[0], jnp.concatenate(o_slc, axis=0), out_ref[1]]
    for i in range(R):
        y = jnp.zeros((LANES, QB), F32)
        for b, o in enumerate(branches):
            gate = jnp.concatenate([gate_rows(b, g * R + i) for g in range(G)], axis=0)
            y = y + gate * o[:, i * QB:(i + 1) * QB]
        o_ref[:, i * LANES:(i + 1) * LANES] = y.T.astype(BF16)


def _nsa(nq, ng, kvc_cmp, kvc_cmp_t, ksl, vsl, kwn, vwn, batch, seq_len):
    T = nq.shape[0]
    QB = NSA_QBLOCK
    rows = QB * NSA_TILES_PER_STEP
    nqb = seq_len // rows
    nc = seq_len // CMP_STRIDE
    topk = min(SLC_TOPK, seq_len // SLC_BLOCK)
    tok = lambda n: pl.BlockSpec((rows, n), lambda b, i: (b * nqb + i, 0))
    seq = lambda n: pl.BlockSpec((1, seq_len, n), lambda b, i: (b, 0, 0))
    seq_t = pl.BlockSpec((1, 2 * LANES, seq_len), lambda b, i: (b, 0, 0))
    return pl.pallas_call(
        functools.partial(_nsa_kernel, seq_len=seq_len, topk=topk),
        grid=(batch, nqb),
        in_specs=[tok(D_NSA), tok(LANES),
                  pl.BlockSpec((1, 1, nc, LANES), lambda b, i: (0, b, 0, 0)),
                  pl.BlockSpec((1, 1, LANES, nc), lambda b, i: (1, b, 0, 0)),
                  seq(2 * LANES), seq_t, seq(LANES), seq_t],
        out_specs=tok(D_NSA),
        out_shape=jax.ShapeDtypeStruct((T, D_NSA), BF16),
        scratch_shapes=[pltpu.VMEM((NSA_KV_HEADS, NSA_GROUP * QB, 2 * LANES), BF16),
                        pltpu.VMEM((2, NSA_KV_HEADS, SLC_KTILE, NSA_GROUP * QB), F32),
                        pltpu.VMEM((NSA_KV_HEADS, 1, NSA_GROUP * QB), F32),
                        pltpu.VMEM((NSA_KV_HEADS, LANES, NSA_GROUP * QB), F32),
                        pltpu.VMEM((2, LANES, NSA_GROUP * QB), F32),
                        pltpu.VMEM((NSA_KV_HEADS, nc + SUBLANES, QB), F32)],
        compiler_params=pltpu.CompilerParams(dimension_semantics=("arbitrary", "arbitrary"),
                                             vmem_limit_bytes=V7X_VMEM_LIMIT),
        name="nsa",
    )(nq, ng, kvc_cmp, kvc_cmp_t, ksl.reshape(batch, seq_len, -1), vsl,
      kwn.reshape(batch, seq_len, -1), vwn)


def _ffn_kernel(yr_ref, yrp_ref, yn_ref, ynp_ref, x_ref, xp_ref, wo_ref, gmix_ref, gpre_ref,
                wu_ref, conv_ref, wd_ref, gpost_ref, o_ref,
                h_ref, ug_ref, uv_ref, acc_ref, xmid_ref, *, tiles_per_seq):
    tm = x_ref.shape[0]
    d_ff = wd_ref.shape[0]
    tf = FFN_FTILE
    H = FFN_HALO
    rc = tm // FFN_ROW_CHUNKS

    def rms(x, gain_ref):
        return x * lax.rsqrt(jnp.mean(x * x, axis=-1, keepdims=True) + NORM_EPS) * gain_ref[...]

    keep = jnp.where(pl.program_id(0) % tiles_per_seq == 0, 0.0, 1.0)
    for c in range(FFN_ROW_CHUNKS):
        rows = slice(c * rc, (c + 1) * rc)
        y_ret, y_nsa = yr_ref[rows, :], yn_ref[rows, :]
        if c == 0:
            y_ret = jnp.concatenate([yrp_ref[...], y_ret], axis=0)
            y_nsa = jnp.concatenate([ynp_ref[...], y_nsa], axis=0)
        mix = (jnp.dot(y_ret, wo_ref[0:D_RET, :], preferred_element_type=F32)
               + jnp.dot(y_nsa, wo_ref[D_RET:D_RET + D_NSA, :], preferred_element_type=F32))
        post = rms(mix, gmix_ref)
        if c == 0:
            h_ref[0:H, :] = (rms(xp_ref[...] + post[0:H], gpre_ref) * keep).astype(BF16)
            post = post[H:H + rc]
        x_mid = x_ref[rows, :] + post
        xmid_ref[rows, :] = x_mid
        h_ref[H + c * rc:H + (c + 1) * rc, :] = rms(x_mid, gpre_ref).astype(BF16)
    acc_ref[...] = jnp.zeros_like(acc_ref)

    n_tiles = d_ff // tf

    def columns(f):
        return (pl.ds(pl.multiple_of(f * tf, tf), tf), pl.ds(pl.multiple_of(d_ff + f * tf, tf), tf))

    def up_project(f, c):
        gate_cols, value_cols = columns(f)
        rows = slice(0 if c == 0 else H + c * rc, H + (c + 1) * rc)
        h = h_ref[rows, :]
        ug_ref[rows, :] = jnp.dot(h, wu_ref[:, gate_cols], preferred_element_type=F32)
        uv_ref[rows, :] = jnp.dot(h, wu_ref[:, value_cols], preferred_element_type=F32)

    def hidden_tile(f, carry):
        gate_cols, value_cols = columns(f)

        def causal_conv(u_ref, cols, r0, scale):
            out = (scale * conv_ref[CONV_WIDTH - 1:CONV_WIDTH, cols]) * u_ref[r0:r0 + rc, :]
            for k in range(CONV_WIDTH - 1):
                d = CONV_WIDTH - 1 - k
                out = out + (scale * conv_ref[k:k + 1, cols]) * u_ref[r0 - d:r0 - d + rc, :]
            return out

        def gate_and_down(c):
            g = causal_conv(ug_ref, gate_cols, H + c * rc, 1.0)
            v_half = causal_conv(uv_ref, value_cols, H + c * rc, 0.5)
            inner = g * (GELU_C + (GELU_C * 0.044715) * (g * g))
            act = (g + g * jnp.tanh(inner)) * v_half
            acc_ref[c * rc:(c + 1) * rc, :] += jnp.dot(act.astype(BF16), wd_ref[gate_cols, :],
                                                       preferred_element_type=F32)

        up_project(f, 0)
        for c in range(FFN_ROW_CHUNKS):
            if c + 1 < FFN_ROW_CHUNKS:
                up_project(f, c + 1)
            gate_and_down(c)
        return carry

    lax.fori_loop(0, n_tiles, hidden_tile, 0)
    o_ref[...] = xmid_ref[...] + rms(acc_ref[...], gpost_ref)


def _outproj_ffn(y_ret, y_nsa, x2, w_out, g_mix, g_pre, w_up, conv_w, w_down, layer, g_post, seq_len):
    T, D = x2.shape
    d_ff = w_down.shape[1]
    tm, tf, H = FFN_TILE, FFN_FTILE, FFN_HALO
    tile = lambda n: pl.BlockSpec((tm, n), lambda i: (i, 0))
    halo = lambda n: pl.BlockSpec((H, n), lambda i: (jnp.maximum(i * (tm // H) - 1, 0), 0))
    resident = lambda a: pl.BlockSpec((None,) + a.shape[1:], lambda i: (layer, 0, 0),
                                      pipeline_mode=pl.Buffered(1))
    gain = pl.BlockSpec((1, D), lambda i: (0, 0), pipeline_mode=pl.Buffered(1))
    return pl.pallas_call(
        functools.partial(_ffn_kernel, tiles_per_seq=seq_len // tm),
        grid=(T // tm,),
        in_specs=[tile(D_RET), halo(D_RET), tile(D_NSA), halo(D_NSA), tile(D), halo(D),
                  resident(w_out), gain, gain, resident(w_up), resident(conv_w), resident(w_down), gain],
        out_specs=tile(D),
        out_shape=jax.ShapeDtypeStruct((T, D), F32),
        scratch_shapes=[pltpu.VMEM((tm + H, D), BF16), pltpu.VMEM((tm + H, tf), F32),
                        pltpu.VMEM((tm + H, tf), F32), pltpu.VMEM((tm, D), F32),
                        pltpu.VMEM((tm, D), F32)],
        compiler_params=pltpu.CompilerParams(dimension_semantics=("arbitrary",),
                                             vmem_limit_bytes=V7X_VMEM_LIMIT),
        name="ffn",
    )(y_ret, y_ret, y_nsa, y_nsa, x2, x2, w_out, g_mix, g_pre, w_up, conv_w, w_down, g_post)


_NSA_HEAD_ORDER = [g * NSA_GROUP + i for i in range(NSA_GROUP) for g in range(NSA_KV_HEADS)]


def _rope_tables(pos):
    inv = 1.0 / (ROPE_THETA ** (jnp.arange(0, NSA_DK, 2, dtype=F32) / NSA_DK))
    ang = pos.astype(F32)[:, None] * inv[None, :]
    c, s = jnp.cos(ang), jnp.sin(ang)
    return jnp.concatenate([c, c, c, c], axis=1), jnp.concatenate([-s, s, -s, s], axis=1)


def _prep_w_in(w):
    lead = w.shape[:-1]
    splits = np.cumsum([RET_QK, RET_QK, D_RET, D_RET, D_NSA] + [NSA_KV] * 6)
    rq, rk, rv, rg, nq, kcm, vcm, ksl, vsl, kwn, vwn, ng = jnp.split(w, [int(s) for s in splits], axis=-1)
    nq = nq.reshape(lead + (NSA_HEADS, NSA_DK))[..., np.array(_NSA_HEAD_ORDER), :].reshape(lead + (D_NSA,))
    ng = jnp.pad(ng, [(0, 0)] * len(lead) + [(0, LANES - NSA_GATES)])
    return jnp.concatenate([rq, rk, nq, ksl, kwn, rv, vsl, vwn, rg, kcm, vcm, ng], axis=-1).astype(BF16)


def _prep_w_out(w):
    layers, _, d = w.shape
    w_nsa = w[:, D_RET:].reshape(layers, NSA_HEADS, NSA_DK, d)[:, np.array(_NSA_HEAD_ORDER)]
    return jnp.concatenate([w[:, :D_RET], w_nsa.reshape(layers, D_NSA, d)], axis=1).astype(BF16)


def _prep_compress(pos, w1, w2):
    lead = w1.shape[:-2]
    half = CMP_STRIDE

    def block_diag(w, axis):
        z = jnp.zeros_like(w)
        return jnp.stack([jnp.concatenate([w, z], axis=-1), jnp.concatenate([z, w], axis=-1)], axis=axis)

    def first_layer(w_half):
        w4 = w_half.reshape(lead + (half, NSA_DK, CMP_HID))
        return block_diag(w4, -3).reshape(lead + (half * NSA_KV_HEADS * NSA_DK, -1)).astype(BF16)

    def pos_row(p_half):
        rows = jnp.broadcast_to(p_half[..., :, None, :], lead + (half, NSA_KV_HEADS, NSA_DK))
        return rows.reshape(lead + (1, -1))

    w2x = block_diag(w2, -3).reshape(lead + (NSA_KV_HEADS * CMP_HID, -1)).astype(BF16)
    n1 = half * NSA_DK
    return (pos_row(pos[..., :half, :]), pos_row(pos[..., half:, :]),
            first_layer(w1[..., :n1, :]), first_layer(w1[..., n1:, :]), w2x)


def kernel(x, norm_mix_pre, w_in, ret_gn_w, cmp_k_pos, cmp_k_w1, cmp_k_w2, cmp_v_pos, cmp_v_w1, cmp_v_w2,
           w_out, norm_mix_post, norm_ffn_pre, ffn_w_up, ffn_conv, ffn_w_down, norm_ffn_post):
    B, S, D = x.shape
    depth = w_in.shape[0]
    assert S % SLC_KTILE == 0 and S % FFN_TILE == 0 and S // SLC_BLOCK <= LANES and S >= WINDOW + NSA_QBLOCK
    assert ffn_w_down.shape[1] % FFN_FTILE == 0
    assert min(SLC_TOPK, S // SLC_BLOCK) > 1 + SLC_LOCAL and S % (NSA_QBLOCK * NSA_TILES_PER_STEP) == 0

    cos, sin = _rope_tables(jnp.arange(S, dtype=jnp.int32))
    nc = S // CMP_STRIDE
    ccos, csin = _rope_tables(jnp.arange(nc, dtype=jnp.int32) * CMP_STRIDE + (CMP_BLOCK - 1))
    cmp_cos = jnp.stack([ccos, jnp.ones_like(ccos)])
    cmp_sin = jnp.stack([csin, jnp.zeros_like(csin)])

    w_in_p = _prep_w_in(w_in)
    w_out_p = _prep_w_out(w_out)
    w_up_p = ffn_w_up.astype(BF16)
    w_down_p = ffn_w_down.astype(BF16)
    cmp_p = _prep_compress(jnp.stack([cmp_k_pos, cmp_v_pos], axis=1), jnp.stack([cmp_k_w1, cmp_v_w1], axis=1),
                           jnp.stack([cmp_k_w2, cmp_v_w2], axis=1))

    x2 = x.reshape(B * S, D)
    for l in range(depth):
        outs = _inproj(x2, norm_mix_pre[l][None], w_in_p, l, cos, sin, S)
        rq, rk, nq, ksl, kwn, rv, vsl, vwn, rg, kvc, ng = outs
        y_ret = _retention(rq, rk, rv, rg, ret_gn_w[l][None], B, S)
        kvc_cmp, kvc_cmp_t = _compress(kvc, *cmp_p, l, cmp_cos, cmp_sin, B, S)
        y_nsa = _nsa(nq, ng, kvc_cmp, kvc_cmp_t, ksl, vsl, kwn, vwn, B, S)
        x2 = _outproj_ffn(y_ret, y_nsa, x2, w_out_p, norm_mix_post[l][None], norm_ffn_pre[l][None],
                          w_up_p, ffn_conv, w_down_p, l, norm_ffn_post[l][None], S)
    return x2.reshape(B, S, D)
```

```python
import functools
import math

import jax
import jax.numpy as jnp
import numpy as np
from jax import lax
from jax.experimental import pallas as pl
from jax.experimental.pallas import tpu as pltpu

F32 = jnp.float32
BF16 = jnp.bfloat16

LANES = 128
SUBLANES = 8
V7X_VMEM_LIMIT = 56 * 1024 * 1024

ROPE_THETA = 10000.0
NORM_EPS = 1e-6
GN_EPS = 1e-5
NEG = -1e30
SEL_BIG = 1e9

RET_HEADS = 4
RET_DK = 64
RET_DV = 128
RET_CHUNK = 128
NSA_HEADS = 8
NSA_KV_HEADS = 2
NSA_DK = 64
NSA_GROUP = NSA_HEADS // NSA_KV_HEADS
CMP_BLOCK = 32
CMP_STRIDE = 16
CMP_HID = 256
SLC_BLOCK = 64
SLC_TOPK = 16
SLC_LOCAL = 2
WINDOW = 512
NSA_QBLOCK = 128
NSA_TILES_PER_STEP = 4
NSA_GATES = 3 * NSA_HEADS
CONV_WIDTH = 3

NSA_Q_SCALE = NSA_DK ** -0.5 * math.log2(math.e)

D_RET = RET_HEADS * RET_DV
D_NSA = NSA_HEADS * NSA_DK
RET_QK = RET_HEADS * RET_DK
NSA_KV = NSA_KV_HEADS * NSA_DK

COL_RQ = 0
COL_RK = COL_RQ + RET_QK
COL_NQ = COL_RK + RET_QK
COL_KSL = COL_NQ + D_NSA
COL_KWN = COL_KSL + NSA_KV
ROPE_COLS = COL_KWN + NSA_KV
COL_RV = ROPE_COLS
COL_VSL = COL_RV + D_RET
COL_VWN = COL_VSL + NSA_KV
COL_RG = COL_VWN + NSA_KV
COL_KCM = COL_RG + D_RET
COL_VCM = COL_KCM + NSA_KV
COL_NG = COL_VCM + NSA_KV
IN_COLS_PAD = COL_NG + LANES

TOK_TILE = 512
RET_TILE = 512
FFN_TILE = 1024
FFN_FTILE = 256
FFN_HALO = 16
FFN_ROW_CHUNKS = 4
SLC_KTILE = 512
SLC_UNROLL = 4


GELU_C = math.sqrt(2.0 / math.pi)


def _gelu_tanh(x):
    return 0.5 * x * (1.0 + jnp.tanh(GELU_C * (x + 0.044715 * (x * x * x))))


def _rope(p, cos, sin_signed, first_half):
    half = NSA_DK // 2
    partner = jnp.where(first_half, pltpu.roll(p, LANES - half, 1), pltpu.roll(p, half, 1))
    return p * cos + partner * sin_signed


def _inproj_kernel(x_ref, g_ref, w_ref, cos_ref, sin_ref,
                   rq_ref, rk_ref, nq_ref, ksl_ref, kwn_ref, rv_ref, vsl_ref, vwn_ref,
                   rg_ref, kvc_ref, ng_ref, stage_ref, *, seq_len):
    tm = x_ref.shape[0]
    x = x_ref[...]
    h = (x * lax.rsqrt(jnp.mean(x * x, axis=-1, keepdims=True) + NORM_EPS) * g_ref[...]).astype(BF16)
    cos = cos_ref[...]
    sin = sin_ref[...]
    lane = lax.broadcasted_iota(jnp.int32, (tm, LANES), 1)
    first_half = (lane % NSA_DK) < NSA_DK // 2

    def proj(c0, n):
        return jnp.dot(h, w_ref[:, c0:c0 + n], preferred_element_type=F32)

    def rope_slab(p, i):
        return _rope(p[:, i * LANES:(i + 1) * LANES], cos, sin, first_half)

    def store_rq(p):
        for i in range(RET_QK // LANES):
            rq_ref[:, i * LANES:(i + 1) * LANES] = rope_slab(p, i).astype(BF16)

    def store_rk(p):
        for i in range(RET_QK // LANES):
            rk_ref[:, i * LANES:(i + 1) * LANES] = (rope_slab(p, i) * (RET_DK ** -0.5)).astype(BF16)

    def store_nq(p):
        for i in range(D_NSA // LANES):
            nq_ref[0, i * LANES:(i + 1) * LANES, :] = (rope_slab(p, i) * NSA_Q_SCALE).T.astype(BF16)

    def store_keys(p):
        ksl_ref[:, 0:LANES] = rope_slab(p, 0).astype(BF16)
        kwn_ref[...] = rope_slab(p, 1).astype(BF16)
        row = lax.broadcasted_iota(jnp.int32, (tm, LANES), 0)
        pos = (pl.program_id(0) * tm + row) % seq_len
        ksl_ref[:, LANES:2 * LANES] = jnp.where(lane == pos // SLC_BLOCK, 1.0, 0.0).astype(BF16)

    def store_rv(p):
        rv_ref[...] = p.astype(BF16)

    def store_values(p):
        low_half = lane < NSA_DK
        for v_ref, v in ((vsl_ref, p[:, 0:LANES]), (vwn_ref, p[:, LANES:2 * LANES])):
            v_ref[0, 0:LANES, :] = jnp.where(low_half, v, 1.0).T.astype(BF16)
            v_ref[0, LANES:2 * LANES, :] = jnp.where(low_half, 1.0, v).T.astype(BF16)

    def store_rg(p):
        rg_ref[...] = p

    def store_compress_inputs(p):
        for s in range(2):
            stage_ref[...] = p[:, s * LANES:(s + 1) * LANES]
            for l in range(CMP_STRIDE):
                kvc_ref[s, :, l * LANES:(l + 1) * LANES] = stage_ref[
                    pl.ds(l, tm // CMP_STRIDE, stride=CMP_STRIDE), :]

    def store_gates(p):
        ng_ref[...] = p

    stages = [(COL_RQ, RET_QK, store_rq), (COL_RK, RET_QK, store_rk), (COL_NQ, D_NSA, store_nq),
              (COL_KSL, 2 * NSA_KV, store_keys), (COL_RV, D_RET, store_rv),
              (COL_VSL, 2 * NSA_KV, store_values), (COL_RG, D_RET, store_rg),
              (COL_KCM, 2 * NSA_KV, store_compress_inputs), (COL_NG, LANES, store_gates)]
    pending = None
    for c0, n, store in stages:
        p = proj(c0, n)
        if pending is not None:
            pending[1](pending[0])
        pending = (p, store)
    pending[1](pending[0])


def _inproj(x2, gain, w, layer, cos, sin, seq_len):
    T, D = x2.shape
    tm = TOK_TILE
    nt = seq_len // tm
    tok = lambda n: pl.BlockSpec((tm, n), lambda i: (i, 0))
    tok_t = pl.BlockSpec((1, 2 * LANES, tm), lambda i: (i // nt, 0, i % nt))
    out_shape = (
        jax.ShapeDtypeStruct((T, RET_QK), BF16),
        jax.ShapeDtypeStruct((T, RET_QK), BF16),
        jax.ShapeDtypeStruct((T // seq_len, D_NSA, seq_len), BF16),
        jax.ShapeDtypeStruct((T, 2 * LANES), BF16),
        jax.ShapeDtypeStruct((T, LANES), BF16),
        jax.ShapeDtypeStruct((T, D_RET), BF16),
        jax.ShapeDtypeStruct((T // seq_len, 2 * LANES, seq_len), BF16),
        jax.ShapeDtypeStruct((T // seq_len, 2 * LANES, seq_len), BF16),
        jax.ShapeDtypeStruct((T, D_RET), F32),
        jax.ShapeDtypeStruct((2, T // CMP_STRIDE, CMP_STRIDE * LANES), F32),
        jax.ShapeDtypeStruct((T, LANES), F32),
    )
    out_specs = (tok(RET_QK), tok(RET_QK), pl.BlockSpec((1, D_NSA, tm), lambda i: (i // nt, 0, i % nt)),
                 tok(2 * LANES), tok(LANES), tok(D_RET),
                 tok_t, tok_t, tok(D_RET),
                 pl.BlockSpec((2, tm // CMP_STRIDE, CMP_STRIDE * LANES), lambda i: (0, i, 0)), tok(LANES))
    return pl.pallas_call(
        functools.partial(_inproj_kernel, seq_len=seq_len),
        grid=(T // tm,),
        in_specs=[tok(D),
                  pl.BlockSpec((1, D), lambda i: (0, 0)),
                  pl.BlockSpec((None, D, IN_COLS_PAD), lambda i: (layer, 0, 0)),
                  pl.BlockSpec((tm, LANES), lambda i: (i % nt, 0)),
                  pl.BlockSpec((tm, LANES), lambda i: (i % nt, 0))],
        out_specs=out_specs,
        out_shape=out_shape,
        scratch_shapes=[pltpu.VMEM((tm, LANES), F32)],
        compiler_params=pltpu.CompilerParams(dimension_semantics=("arbitrary",),
                                             vmem_limit_bytes=V7X_VMEM_LIMIT),
        name="inproj",
    )(x2, gain, w, cos, sin)


def _retention_kernel(q_ref, k_ref, v_ref, g_ref, gnw_ref, o_ref, state_ref, vbd_ref):
    C = RET_CHUNK
    n_chunks = q_ref.shape[0] // C

    @pl.when(pl.program_id(1) == 0)
    def _():
        state_ref[...] = jnp.zeros_like(state_ref)

    vbd_ref[...] = jnp.zeros_like(vbd_ref)

    ii = lax.broadcasted_iota(jnp.int32, (C, C), 0)
    jj = lax.broadcasted_iota(jnp.int32, (C, C), 1)
    diff = (ii - jj).astype(F32)
    i_col = lax.broadcasted_iota(jnp.int32, (C, 1), 0).astype(F32)
    low_half = lax.broadcasted_iota(jnp.int32, (C, LANES), 1) < RET_DK
    low_cols = lax.broadcasted_iota(jnp.int32, (C, 2 * RET_DV), 1) < RET_DV
    own_block = ((lax.broadcasted_iota(jnp.int32, (LANES, 2 * RET_DV), 0) < RET_DK)
                 == (lax.broadcasted_iota(jnp.int32, (LANES, 2 * RET_DV), 1) < RET_DV))
    nt_dims = (((1,), (1,)), ((), ()))
    tn_dims = (((0,), (0,)), ((), ()))

    for pair in range(RET_HEADS // 2):
        lg0, lg1 = (math.log(1.0 - 2.0 ** (-5.0 - h)) for h in (2 * pair, 2 * pair + 1))
        decay = [jnp.where(diff >= 0, jnp.exp(lg * jnp.maximum(diff, 0.0)), 0.0) for lg in (lg0, lg1)]
        xi = jnp.where(low_cols, jnp.exp(lg0 * (i_col + 1.0)), jnp.exp(lg1 * (i_col + 1.0)))
        zeta = jnp.where(low_half, jnp.exp(lg0 * (C - 1.0 - i_col)), jnp.exp(lg1 * (C - 1.0 - i_col)))
        chunk_decay = jnp.where(low_cols[0:1], math.exp(lg0 * C), math.exp(lg1 * C))
        qk_cols = slice(pair * LANES, (pair + 1) * LANES)
        v_cols = slice(2 * pair * RET_DV, 2 * (pair + 1) * RET_DV)
        for c in range(n_chunks):
            rows = slice(c * C, (c + 1) * C)
            q = q_ref[rows, qk_cols]
            ks = k_ref[rows, qk_cols]
            v = v_ref[rows, v_cols]
            qf = q.astype(F32)
            q_stack = jnp.concatenate([jnp.where(low_half, qf, 0.0), jnp.where(low_half, 0.0, qf)],
                                      axis=0).astype(BF16)
            s = lax.dot_general(q_stack, ks, nt_dims, preferred_element_type=F32)
            s_pair = jnp.concatenate([s[0:C] * decay[0], s[C:2 * C] * decay[1]], axis=1).astype(BF16)
            vbd_ref[pair, 0:C, 0:RET_DV] = v[:, 0:RET_DV]
            vbd_ref[pair, C:2 * C, RET_DV:2 * RET_DV] = v[:, RET_DV:2 * RET_DV]
            o = jnp.dot(s_pair, vbd_ref[pair], preferred_element_type=F32)
            state = state_ref[pair]
            o = o + jnp.dot(q, state.astype(BF16), preferred_element_type=F32) * xi
            kz = (ks.astype(F32) * zeta).astype(BF16)
            kv = lax.dot_general(kz, v, tn_dims, preferred_element_type=F32)
            state_ref[pair] = state * chunk_decay + jnp.where(own_block, kv, 0.0)
            for e in range(2):
                cols = slice((2 * pair + e) * RET_DV, (2 * pair + e + 1) * RET_DV)
                oh = o[:, e * RET_DV:(e + 1) * RET_DV]
                mu = jnp.mean(oh, axis=-1, keepdims=True)
                var = jnp.mean(jnp.square(oh - mu), axis=-1, keepdims=True)
                on = (oh - mu) * lax.rsqrt(var + GN_EPS) * gnw_ref[:, cols]
                gate = g_ref[rows, cols]
                o_ref[rows, cols] = (gate * (1.0 / (1.0 + jnp.exp(-gate))) * on).astype(BF16)


def _retention(rq, rk, rv, rg, gn_w, batch, seq_len):
    T = rq.shape[0]
    tc = RET_TILE
    nt = seq_len // tc
    tok = lambda n: pl.BlockSpec((tc, n), lambda b, i: (b * nt + i, 0))
    return pl.pallas_call(
        _retention_kernel,
        grid=(batch, nt),
        in_specs=[tok(RET_QK), tok(RET_QK), tok(D_RET), tok(D_RET),
                  pl.BlockSpec((1, D_RET), lambda b, i: (0, 0))],
        out_specs=tok(D_RET),
        out_shape=jax.ShapeDtypeStruct((T, D_RET), BF16),
        scratch_shapes=[pltpu.VMEM((RET_HEADS // 2, LANES, 2 * RET_DV), F32),
                        pltpu.VMEM((RET_HEADS // 2, 2 * RET_CHUNK, 2 * RET_DV), BF16)],
        compiler_params=pltpu.CompilerParams(dimension_semantics=("arbitrary", "arbitrary"),
                                             vmem_limit_bytes=V7X_VMEM_LIMIT),
        name="retention",
    )(rq, rk, rv, rg, gn_w)


def _compress_kernel(x_ref, pa_ref, pb_ref, wa_ref, wb_ref, w2_ref, cos_ref, sin_ref, o_ref, ot_ref):
    ng = x_ref.shape[2]
    x = x_ref[0, 0]
    xa = (x + pa_ref[0]).astype(BF16)
    xb = (x + pb_ref[0]).astype(BF16)
    a = jnp.dot(xa, wa_ref[0], preferred_element_type=F32)
    b = jnp.dot(xb, wb_ref[0], preferred_element_type=F32)
    hid = a + pltpu.roll(b, ng - 1, 0)
    out = jnp.dot(_gelu_tanh(hid).astype(BF16), w2_ref[0], preferred_element_type=F32)
    lane = lax.broadcasted_iota(jnp.int32, out.shape, 1)
    out = _rope(out, cos_ref[0], sin_ref[0], (lane % NSA_DK) < NSA_DK // 2)
    o_ref[0, 0] = out.astype(BF16)
    ot_ref[0, 0] = out.T.astype(BF16)


def _compress(kvc, pos_a, pos_b, wa, wb, w2, layer, cos, sin, batch, seq_len):
    ng = seq_len // CMP_STRIDE
    gw = CMP_STRIDE * LANES
    x = kvc.reshape(2, batch, ng, gw)
    hid = NSA_KV_HEADS * CMP_HID
    per_kv = lambda *shape: pl.BlockSpec((1,) + shape, lambda s, b: (s,) + (0,) * len(shape))
    per_lkv = lambda *shape: pl.BlockSpec((None, 1) + shape, lambda s, b: (layer, s) + (0,) * len(shape))
    return pl.pallas_call(
        _compress_kernel,
        grid=(2, batch),
        in_specs=[pl.BlockSpec((1, 1, ng, gw), lambda s, b: (s, b, 0, 0)),
                  per_lkv(1, gw), per_lkv(1, gw), per_lkv(gw, hid), per_lkv(gw, hid), per_lkv(hid, LANES),
                  per_kv(ng, LANES), per_kv(ng, LANES)],
        out_specs=(pl.BlockSpec((1, 1, ng, LANES), lambda s, b: (s, b, 0, 0)),
                   pl.BlockSpec((1, 1, LANES, ng), lambda s, b: (s, b, 0, 0))),
        out_shape=(jax.ShapeDtypeStruct((2, batch, ng, LANES), BF16),
                   jax.ShapeDtypeStruct((2, batch, LANES, ng), BF16)),
        compiler_params=pltpu.CompilerParams(dimension_semantics=("arbitrary", "arbitrary"),
                                             vmem_limit_bytes=V7X_VMEM_LIMIT),
        name="compress",
    )(x, pos_a, pos_b, wa, wb, w2, cos, sin)


def _nsa_kernel(nq_ref, ng_ref, *refs, seq_len, topk):
    n_in = 6
    o_ref = refs[n_in]

    def one_tile(j, carry):
        rows = pl.ds(pl.multiple_of(j * NSA_QBLOCK, NSA_QBLOCK), NSA_QBLOCK)
        t0 = (pl.program_id(1) * NSA_TILES_PER_STEP + j) * NSA_QBLOCK
        _nsa_tile(t0, nq_ref[0, :, rows], ng_ref.at[rows], *refs[:n_in], o_ref.at[rows], *refs[n_in + 1:],
                  seq_len=seq_len, topk=topk)
        return carry

    lax.fori_loop(0, NSA_TILES_PER_STEP, one_tile, 0)


def _nsa_tile(t0, q_t, ng_ref, kc_ref, vc_ref, ksl_ref, vsl_ref, kwn_ref, vwn_ref,
              o_ref, qaug_ref, s_ref, m_ref, acc_ref, out_ref, psum_ref, *, seq_len, topk):
    QB = NSA_QBLOCK
    R = NSA_GROUP
    G = NSA_KV_HEADS
    M = R * QB
    NC = kc_ref.shape[2]
    KT = SLC_KTILE
    WK = WINDOW + QB
    def per_head(x):
        return jnp.concatenate([x] * R, axis=1)

    def normalized(acc, g):
        num, den = (acc[0:NSA_DK], acc[NSA_DK:NSA_DK + 1]) if g == 0 else (acc[NSA_DK:], acc[0:1])
        return num * (1.0 / den)

    for i in range(R):
        for g in range(G):
            own = slice(i * LANES + g * NSA_DK, i * LANES + (g + 1) * NSA_DK)
            other = slice((1 - g) * NSA_DK, (2 - g) * NSA_DK)
            qaug_ref[g, g * NSA_DK:(g + 1) * NSA_DK, i * QB:(i + 1) * QB] = q_t[own, :]
            qaug_ref[g, other, i * QB:(i + 1) * QB] = jnp.zeros((NSA_DK, QB), BF16)

    ks = pl.multiple_of(jnp.clip(t0 - WINDOW, 0, seq_len - WK), QB)
    k_pos = ks + lax.broadcasted_iota(jnp.int32, (WK, QB), 0)
    t_pos = t0 + lax.broadcasted_iota(jnp.int32, (WK, QB), 1)
    win_mask = per_head((k_pos <= t_pos) & (k_pos > t_pos - WINDOW))

    def slc_scores(k0, slot):
        for g in range(G):
            s_ref[slot, g] = jnp.dot(ksl_ref[0, pl.ds(k0, KT), :], qaug_ref[g],
                                     preferred_element_type=F32)

    def compressed_select_window(nc):
        win_scores = [jnp.dot(kwn_ref[0, pl.ds(ks, WK), :], qaug_ref[g, 0:LANES, :],
                              preferred_element_type=F32) for g in range(G)]
        cmp_scores = [jnp.dot(kc_ref[0, 0, 0:nc, :], qaug_ref[g, 0:LANES, :],
                              preferred_element_type=F32) for g in range(G)]

        nb = min(nc * CMP_STRIDE // SLC_BLOCK, LANES)
        jj = lax.broadcasted_iota(jnp.int32, (nb, QB), 0)
        q_blk = (t0 + lax.broadcasted_iota(jnp.int32, (nb, QB), 1)) // SLC_BLOCK
        valid = jj <= q_blk
        forced = (jj == 0) | (valid & (jj > q_blk - SLC_LOCAL))
        jf = jj.astype(F32)

        def pick_round(score):
            best = jnp.max(score, axis=0, keepdims=True)
            first = jnp.min(jnp.where(score == best, jf, float(LANES)), axis=0, keepdims=True)
            return jnp.where(jf == first, -jnp.inf, score)

        scores = []
        n_idx = lax.broadcasted_iota(jnp.int32, (nc, QB), 0)
        t_col = t0 + lax.broadcasted_iota(jnp.int32, (nc, QB), 1)
        cmp_mask = per_head(n_idx * CMP_STRIDE + (CMP_BLOCK - 1) <= t_col)
        for g in range(G):
            sc = jnp.where(cmp_mask, cmp_scores[g], NEG)
            e = jnp.exp2(sc - jnp.max(sc, axis=0, keepdims=True))
            if nc == LANES:
                e = jnp.where(cmp_mask, e, 0.0)
            p_cmp = e * (1.0 / jnp.maximum(jnp.sum(e, axis=0, keepdims=True), 1e-30))
            acc = jnp.dot(vc_ref[0, 0, :, 0:nc], p_cmp.astype(BF16), preferred_element_type=F32)
            out_ref[0, g * NSA_DK:(g + 1) * NSA_DK, :] = acc[g * NSA_DK:(g + 1) * NSA_DK]
            p_sum = p_cmp[:, 0:QB]
            for i in range(1, R):
                p_sum = p_sum + p_cmp[:, i * QB:(i + 1) * QB]
            psum_ref[g, 0:SUBLANES, :] = jnp.zeros((SUBLANES, QB), F32)
            psum_ref[g, SUBLANES:SUBLANES + nc, :] = p_sum
            ratio, lead = SLC_BLOCK // CMP_STRIDE, CMP_BLOCK // CMP_STRIDE - 1
            imp = psum_ref[g, pl.ds(SUBLANES - lead, nb, stride=ratio), :]
            for d in range(1 - lead, ratio):
                imp = imp + psum_ref[g, pl.ds(SUBLANES + d, nb, stride=ratio), :]
            scores.append(jnp.where(forced, -jnp.inf, jnp.where(valid, imp, -SEL_BIG)))

        for _ in range(topk - (1 + SLC_LOCAL)):
            scores = [pick_round(score) for score in scores]
        for g in range(G):
            selected = ((scores[g] == -jnp.inf) | (q_blk < topk)) & valid
            bias = jnp.where(selected, 0.0, NEG)
            if nb < LANES:
                bias = jnp.concatenate([bias, jnp.full((LANES - nb, QB), NEG, F32)], axis=0)
            qaug_ref[g, LANES:2 * LANES, :] = per_head(bias.astype(BF16))
        slc_scores(0, 0)

        for g in range(G):
            sw = jnp.where(win_mask, win_scores[g], NEG)
            e_win = jnp.exp2(sw - jnp.max(sw, axis=0, keepdims=True)).astype(BF16)
            pv = jnp.dot(vwn_ref[0, g * LANES:(g + 1) * LANES, pl.ds(ks, WK)], e_win,
                         preferred_element_type=F32)
            out_ref[1, g * NSA_DK:(g + 1) * NSA_DK, :] = normalized(pv, g)

    n_visible = (t0 + QB - CMP_BLOCK) // CMP_STRIDE + 1
    n_variants = NC // LANES
    for v in range(n_variants):
        @pl.when(jnp.minimum((n_visible - 1) // LANES, n_variants - 1) == v)
        def _():
            compressed_select_window((v + 1) * LANES)

    def slc_accumulate(k0, slot, causal):
        for g in range(G):
            s = s_ref[slot, g]
            if causal:
                k_pos = k0 + lax.broadcasted_iota(jnp.int32, (KT, QB), 0)
                t_pos = t0 + lax.broadcasted_iota(jnp.int32, (KT, QB), 1)
                s = jnp.where(per_head(k_pos <= t_pos), s, NEG)
            m_run = m_ref[g]
            m_new = jnp.maximum(m_run, jnp.max(s, axis=0, keepdims=True))
            p = jnp.exp2(s - m_new).astype(BF16)
            pv = jnp.dot(vsl_ref[0, g * LANES:(g + 1) * LANES, pl.ds(k0, KT)], p,
                         preferred_element_type=F32)
            acc_ref[g] = jnp.exp2(m_run - m_new) * acc_ref[g] + pv
            m_ref[g] = m_new

    m_ref[...] = jnp.full(m_ref.shape, NEG, F32)
    acc_ref[...] = jnp.zeros_like(acc_ref)
    n_full = t0 // KT
    U = SLC_UNROLL

    def slc_run(k0, n_tiles, last_is_diagonal):
        for u in range(n_tiles):
            causal = last_is_diagonal and u == n_tiles - 1
            if not causal:
                slc_scores(k0 + (u + 1) * KT, (u + 1) % 2)
            slc_accumulate(k0 + u * KT, u % 2, causal)

    def slc_group(j, _):
        slc_run(pl.multiple_of(j * (U * KT), U * KT), U, False)
        return 0

    lax.fori_loop(0, n_full // U, slc_group, 0)
    k_rest = pl.multiple_of((n_full // U) * (U * KT), U * KT)
    for r in range(U):
        @pl.when(n_full % U == r)
        def _():
            slc_run(k_rest, r + 1, True)

    o_slc = [normalized(acc_ref[g], g) for g in range(G)]

    gates = (1.0 / (1.0 + jnp.exp(-ng_ref[...]))).T

    def gate_rows(branch, head):
        r = branch * NSA_HEADS + head
        return jnp.broadcast_to(gates[r:r + 1, :], (NSA_DK, QB))

    branches = [out_ref[0], jnp.concatenate(o_slc, axis=0), out_ref[1]]
    for i in range(R):
        y = jnp.zeros((LANES, QB), F32)
        for b, o in enumerate(branches):
            gate = jnp.concatenate([gate_rows(b, g * R + i) for g in range(G)], axis=0)
            y = y + gate * o[:, i * QB:(i + 1) * QB]
        o_ref[:, i * LANES:(i + 1) * LANES] = y.T.astype(BF16)


def _nsa(nq, ng, kvc_cmp, kvc_cmp_t, ksl, vsl, kwn, vwn, batch, seq_len):
    T = batch * seq_len
    QB = NSA_QBLOCK
    rows = QB * NSA_TILES_PER_STEP
    nqb = seq_len // rows
    nc = seq_len // CMP_STRIDE
    topk = min(SLC_TOPK, seq_len // SLC_BLOCK)
    tok = lambda n: pl.BlockSpec((rows, n), lambda b, i: (b * nqb + i, 0))
    seq = lambda n: pl.BlockSpec((1, seq_len, n), lambda b, i: (b, 0, 0))
    seq_t = pl.BlockSpec((1, 2 * LANES, seq_len), lambda b, i: (b, 0, 0))
    return pl.pallas_call(
        functools.partial(_nsa_kernel, seq_len=seq_len, topk=topk),
        grid=(batch, nqb),
        in_specs=[pl.BlockSpec((1, D_NSA, rows), lambda b, i: (b, 0, i)), tok(LANES),
                  pl.BlockSpec((1, 1, nc, LANES), lambda b, i: (0, b, 0, 0)),
                  pl.BlockSpec((1, 1, LANES, nc), lambda b, i: (1, b, 0, 0)),
                  seq(2 * LANES), seq_t, seq(LANES), seq_t],
        out_specs=tok(D_NSA),
        out_shape=jax.ShapeDtypeStruct((T, D_NSA), BF16),
        scratch_shapes=[pltpu.VMEM((NSA_KV_HEADS, 2 * LANES, NSA_GROUP * QB), BF16),
                        pltpu.VMEM((2, NSA_KV_HEADS, SLC_KTILE, NSA_GROUP * QB), F32),
                        pltpu.VMEM((NSA_KV_HEADS, 1, NSA_GROUP * QB), F32),
                        pltpu.VMEM((NSA_KV_HEADS, LANES, NSA_GROUP * QB), F32),
                        pltpu.VMEM((2, LANES, NSA_GROUP * QB), F32),
                        pltpu.VMEM((NSA_KV_HEADS, nc + SUBLANES, QB), F32)],
        compiler_params=pltpu.CompilerParams(dimension_semantics=("arbitrary", "arbitrary"),
                                             vmem_limit_bytes=V7X_VMEM_LIMIT),
        name="nsa",
    )(nq, ng, kvc_cmp, kvc_cmp_t, ksl.reshape(batch, seq_len, -1), vsl,
      kwn.reshape(batch, seq_len, -1), vwn)


def _ffn_kernel(yr_ref, yrp_ref, yn_ref, ynp_ref, x_ref, xp_ref, wo_ref, gmix_ref, gpre_ref,
                wu_ref, conv_ref, wd_ref, gpost_ref, o_ref,
                h_ref, ug_ref, uv_ref, acc_ref, xmid_ref, *, tiles_per_seq):
    tm = x_ref.shape[0]
    d_ff = wd_ref.shape[0]
    tf = FFN_FTILE
    H = FFN_HALO
    rc = tm // FFN_ROW_CHUNKS

    def rms(x, gain_ref):
        return x * lax.rsqrt(jnp.mean(x * x, axis=-1, keepdims=True) + NORM_EPS) * gain_ref[...]

    keep = jnp.where(pl.program_id(0) % tiles_per_seq == 0, 0.0, 1.0)
    for c in range(FFN_ROW_CHUNKS):
        rows = slice(c * rc, (c + 1) * rc)
        y_ret, y_nsa = yr_ref[rows, :], yn_ref[rows, :]
        if c == 0:
            y_ret = jnp.concatenate([yrp_ref[...], y_ret], axis=0)
            y_nsa = jnp.concatenate([ynp_ref[...], y_nsa], axis=0)
        mix = (jnp.dot(y_ret, wo_ref[0:D_RET, :], preferred_element_type=F32)
               + jnp.dot(y_nsa, wo_ref[D_RET:D_RET + D_NSA, :], preferred_element_type=F32))
        post = rms(mix, gmix_ref)
        if c == 0:
            h_ref[0:H, :] = (rms(xp_ref[...] + post[0:H], gpre_ref) * keep).astype(BF16)
            post = post[H:H + rc]
        x_mid = x_ref[rows, :] + post
        xmid_ref[rows, :] = x_mid
        h_ref[H + c * rc:H + (c + 1) * rc, :] = rms(x_mid, gpre_ref).astype(BF16)
    acc_ref[...] = jnp.zeros_like(acc_ref)

    n_tiles = d_ff // tf

    def columns(f):
        return (pl.ds(pl.multiple_of(f * tf, tf), tf), pl.ds(pl.multiple_of(d_ff + f * tf, tf), tf))

    def up_project(f, c):
        gate_cols, value_cols = columns(f)
        rows = slice(0 if c == 0 else H + c * rc, H + (c + 1) * rc)
        h = h_ref[rows, :]
        ug_ref[rows, :] = jnp.dot(h, wu_ref[:, gate_cols], preferred_element_type=F32)
        uv_ref[rows, :] = jnp.dot(h, wu_ref[:, value_cols], preferred_element_type=F32)

    def hidden_tile(f, carry):
        gate_cols, value_cols = columns(f)

        def causal_conv(u_ref, cols, r0, scale):
            out = (scale * conv_ref[CONV_WIDTH - 1:CONV_WIDTH, cols]) * u_ref[r0:r0 + rc, :]
            for k in range(CONV_WIDTH - 1):
                d = CONV_WIDTH - 1 - k
                out = out + (scale * conv_ref[k:k + 1, cols]) * u_ref[r0 - d:r0 - d + rc, :]
            return out

        def gate_and_down(c):
            g = causal_conv(ug_ref, gate_cols, H + c * rc, 1.0)
            v_half = causal_conv(uv_ref, value_cols, H + c * rc, 0.5)
            inner = g * (GELU_C + (GELU_C * 0.044715) * (g * g))
            act = (g + g * jnp.tanh(inner)) * v_half
            acc_ref[c * rc:(c + 1) * rc, :] += jnp.dot(act.astype(BF16), wd_ref[gate_cols, :],
                                                       preferred_element_type=F32)

        up_project(f, 0)
        for c in range(FFN_ROW_CHUNKS):
            if c + 1 < FFN_ROW_CHUNKS:
                up_project(f, c + 1)
            gate_and_down(c)
        return carry

    lax.fori_loop(0, n_tiles, hidden_tile, 0)
    o_ref[...] = xmid_ref[...] + rms(acc_ref[...], gpost_ref)


def _outproj_ffn(y_ret, y_nsa, x2, w_out, g_mix, g_pre, w_up, conv_w, w_down, layer, g_post, seq_len):
    T, D = x2.shape
    d_ff = w_down.shape[1]
    tm, tf, H = FFN_TILE, FFN_FTILE, FFN_HALO
    tile = lambda n: pl.BlockSpec((tm, n), lambda i: (i, 0))
    halo = lambda n: pl.BlockSpec((H, n), lambda i: (jnp.maximum(i * (tm // H) - 1, 0), 0))
    resident = lambda a: pl.BlockSpec((None,) + a.shape[1:], lambda i: (layer, 0, 0),
                                      pipeline_mode=pl.Buffered(1))
    gain = pl.BlockSpec((1, D), lambda i: (0, 0), pipeline_mode=pl.Buffered(1))
    return pl.pallas_call(
        functools.partial(_ffn_kernel, tiles_per_seq=seq_len // tm),
        grid=(T // tm,),
        in_specs=[tile(D_RET), halo(D_RET), tile(D_NSA), halo(D_NSA), tile(D), halo(D),
                  resident(w_out), gain, gain, resident(w_up), resident(conv_w), resident(w_down), gain],
        out_specs=tile(D),
        out_shape=jax.ShapeDtypeStruct((T, D), F32),
        scratch_shapes=[pltpu.VMEM((tm + H, D), BF16), pltpu.VMEM((tm + H, tf), F32),
                        pltpu.VMEM((tm + H, tf), F32), pltpu.VMEM((tm, D), F32),
                        pltpu.VMEM((tm, D), F32)],
        compiler_params=pltpu.CompilerParams(dimension_semantics=("arbitrary",),
                                             vmem_limit_bytes=V7X_VMEM_LIMIT),
        name="ffn",
    )(y_ret, y_ret, y_nsa, y_nsa, x2, x2, w_out, g_mix, g_pre, w_up, conv_w, w_down, g_post)


_NSA_HEAD_ORDER = [g * NSA_GROUP + i for i in range(NSA_GROUP) for g in range(NSA_KV_HEADS)]


def _rope_tables(pos):
    inv = 1.0 / (ROPE_THETA ** (jnp.arange(0, NSA_DK, 2, dtype=F32) / NSA_DK))
    ang = pos.astype(F32)[:, None] * inv[None, :]
    c, s = jnp.cos(ang), jnp.sin(ang)
    return jnp.concatenate([c, c, c, c], axis=1), jnp.concatenate([-s, s, -s, s], axis=1)


def _prep_w_in(w):
    lead = w.shape[:-1]
    splits = np.cumsum([RET_QK, RET_QK, D_RET, D_RET, D_NSA] + [NSA_KV] * 6)
    rq, rk, rv, rg, nq, kcm, vcm, ksl, vsl, kwn, vwn, ng = jnp.split(w, [int(s) for s in splits], axis=-1)
    nq = nq.reshape(lead + (NSA_HEADS, NSA_DK))[..., np.array(_NSA_HEAD_ORDER), :].reshape(lead + (D_NSA,))
    ng = jnp.pad(ng, [(0, 0)] * len(lead) + [(0, LANES - NSA_GATES)])
    return jnp.concatenate([rq, rk, nq, ksl, kwn, rv, vsl, vwn, rg, kcm, vcm, ng], axis=-1).astype(BF16)


def _prep_w_out(w):
    layers, _, d = w.shape
    w_nsa = w[:, D_RET:].reshape(layers, NSA_HEADS, NSA_DK, d)[:, np.array(_NSA_HEAD_ORDER)]
    return jnp.concatenate([w[:, :D_RET], w_nsa.reshape(layers, D_NSA, d)], axis=1).astype(BF16)


def _prep_compress(pos, w1, w2):
    lead = w1.shape[:-2]
    half = CMP_STRIDE

    def block_diag(w, axis):
        z = jnp.zeros_like(w)
        return jnp.stack([jnp.concatenate([w, z], axis=-1), jnp.concatenate([z, w], axis=-1)], axis=axis)

    def first_layer(w_half):
        w4 = w_half.reshape(lead + (half, NSA_DK, CMP_HID))
        return block_diag(w4, -3).reshape(lead + (half * NSA_KV_HEADS * NSA_DK, -1)).astype(BF16)

    def pos_row(p_half):
        rows = jnp.broadcast_to(p_half[..., :, None, :], lead + (half, NSA_KV_HEADS, NSA_DK))
        return rows.reshape(lead + (1, -1))

    w2x = block_diag(w2, -3).reshape(lead + (NSA_KV_HEADS * CMP_HID, -1)).astype(BF16)
    n1 = half * NSA_DK
    return (pos_row(pos[..., :half, :]), pos_row(pos[..., half:, :]),
            first_layer(w1[..., :n1, :]), first_layer(w1[..., n1:, :]), w2x)


def kernel(x, norm_mix_pre, w_in, ret_gn_w, cmp_k_pos, cmp_k_w1, cmp_k_w2, cmp_v_pos, cmp_v_w1, cmp_v_w2,
           w_out, norm_mix_post, norm_ffn_pre, ffn_w_up, ffn_conv, ffn_w_down, norm_ffn_post):
    B, S, D = x.shape
    depth = w_in.shape[0]
    assert S % SLC_KTILE == 0 and S % FFN_TILE == 0 and S // SLC_BLOCK <= LANES and S >= WINDOW + NSA_QBLOCK
    assert ffn_w_down.shape[1] % FFN_FTILE == 0
    assert min(SLC_TOPK, S // SLC_BLOCK) > 1 + SLC_LOCAL and S % (NSA_QBLOCK * NSA_TILES_PER_STEP) == 0

    cos, sin = _rope_tables(jnp.arange(S, dtype=jnp.int32))
    nc = S // CMP_STRIDE
    ccos, csin = _rope_tables(jnp.arange(nc, dtype=jnp.int32) * CMP_STRIDE + (CMP_BLOCK - 1))
    cmp_cos = jnp.stack([ccos, jnp.ones_like(ccos)])
    cmp_sin = jnp.stack([csin, jnp.zeros_like(csin)])

    w_in_p = _prep_w_in(w_in)
    w_out_p = _prep_w_out(w_out)
    w_up_p = ffn_w_up.astype(BF16)
    w_down_p = ffn_w_down.astype(BF16)
    cmp_p = _prep_compress(jnp.stack([cmp_k_pos, cmp_v_pos], axis=1), jnp.stack([cmp_k_w1, cmp_v_w1], axis=1),
                           jnp.stack([cmp_k_w2, cmp_v_w2], axis=1))

    x2 = x.reshape(B * S, D)
    for l in range(depth):
        outs = _inproj(x2, norm_mix_pre[l][None], w_in_p, l, cos, sin, S)
        rq, rk, nq, ksl, kwn, rv, vsl, vwn, rg, kvc, ng = outs
        y_ret = _retention(rq, rk, rv, rg, ret_gn_w[l][None], B, S)
        kvc_cmp, kvc_cmp_t = _compress(kvc, *cmp_p, l, cmp_cos, cmp_sin, B, S)
        y_nsa = _nsa(nq, ng, kvc_cmp, kvc_cmp_t, ksl, vsl, kwn, vwn, B, S)
        x2 = _outproj_ffn(y_ret, y_nsa, x2, w_out_p, norm_mix_post[l][None], norm_ffn_pre[l][None],
                          w_up_p, ffn_conv, w_down_p, l, norm_ffn_post[l][None], S)
    return x2.reshape(B, S, D)
```

```python
import functools
import math

import jax
import jax.numpy as jnp
import numpy as np
from jax import lax
from jax.experimental import pallas as pl
from jax.experimental.pallas import tpu as pltpu

F32 = jnp.float32
BF16 = jnp.bfloat16

LANES = 128
SUBLANES = 8
V7X_VMEM_LIMIT = 56 * 1024 * 1024

ROPE_THETA = 10000.0
NORM_EPS = 1e-6
GN_EPS = 1e-5
NEG = -1e30
SEL_BIG = 1e9

RET_HEADS = 4
RET_DK = 64
RET_DV = 128
RET_CHUNK = 128
NSA_HEADS = 8
NSA_KV_HEADS = 2
NSA_DK = 64
NSA_GROUP = NSA_HEADS // NSA_KV_HEADS
CMP_BLOCK = 32
CMP_STRIDE = 16
CMP_HID = 256
SLC_BLOCK = 64
SLC_TOPK = 16
SLC_LOCAL = 2
WINDOW = 512
NSA_QBLOCK = 128
NSA_TILES_PER_STEP = 4
NSA_GATES = 3 * NSA_HEADS
CONV_WIDTH = 3

NSA_Q_SCALE = NSA_DK ** -0.5 * math.log2(math.e)

D_RET = RET_HEADS * RET_DV
D_NSA = NSA_HEADS * NSA_DK
RET_QK = RET_HEADS * RET_DK
NSA_KV = NSA_KV_HEADS * NSA_DK

COL_RQ = 0
COL_RK = COL_RQ + RET_QK
COL_NQ = COL_RK + RET_QK
COL_KSL = COL_NQ + D_NSA
COL_KWN = COL_KSL + NSA_KV
ROPE_COLS = COL_KWN + NSA_KV
COL_RV = ROPE_COLS
COL_VSL = COL_RV + D_RET
COL_VWN = COL_VSL + NSA_KV
COL_RG = COL_VWN + NSA_KV
COL_KCM = COL_RG + D_RET
COL_VCM = COL_KCM + NSA_KV
COL_NG = COL_VCM + NSA_KV
IN_COLS_PAD = COL_NG + LANES

TOK_TILE = 512
RET_TILE = 512
FFN_TILE = 1024
FFN_FTILE = 256
FFN_HALO = 16
FFN_ROW_CHUNKS = 4
SLC_KTILE = 512
SLC_UNROLL = 4


GELU_C = math.sqrt(2.0 / math.pi)


def _gelu_tanh(x):
    return 0.5 * x * (1.0 + jnp.tanh(GELU_C * (x + 0.044715 * (x * x * x))))


def _rope(p, cos, sin_signed, first_half):
    half = NSA_DK // 2
    partner = jnp.where(first_half, pltpu.roll(p, LANES - half, 1), pltpu.roll(p, half, 1))
    return p * cos + partner * sin_signed


def _inproj_kernel(x_ref, g_ref, w_ref, cos_ref, sin_ref,
                   rq_ref, rk_ref, nq_ref, ksl_ref, kwn_ref, rv_ref, vsl_ref, vwn_ref,
                   rg_ref, kvc_ref, ng_ref, stage_ref, *, seq_len):
    tm = x_ref.shape[0]
    x = x_ref[...]
    h = (x * lax.rsqrt(jnp.mean(x * x, axis=-1, keepdims=True) + NORM_EPS) * g_ref[...]).astype(BF16)
    cos = cos_ref[...]
    sin = sin_ref[...]
    lane = lax.broadcasted_iota(jnp.int32, (tm, LANES), 1)
    first_half = (lane % NSA_DK) < NSA_DK // 2

    def proj(c0, n):
        return jnp.dot(h, w_ref[:, c0:c0 + n], preferred_element_type=F32)

    def rope_slab(p, i):
        return _rope(p[:, i * LANES:(i + 1) * LANES], cos, sin, first_half)

    def store_rq(p):
        for i in range(RET_QK // LANES):
            rq_ref[:, i * LANES:(i + 1) * LANES] = rope_slab(p, i).astype(BF16)

    def store_rk(p):
        for i in range(RET_QK // LANES):
            rk_ref[0, i * LANES:(i + 1) * LANES, :] = (rope_slab(p, i) * (RET_DK ** -0.5)).T.astype(BF16)

    def store_nq(p):
        for i in range(D_NSA // LANES):
            nq_ref[:, i * LANES:(i + 1) * LANES] = (rope_slab(p, i) * NSA_Q_SCALE).astype(BF16)

    def store_keys(p):
        ksl_ref[:, 0:LANES] = rope_slab(p, 0).astype(BF16)
        kwn_ref[...] = rope_slab(p, 1).astype(BF16)
        row = lax.broadcasted_iota(jnp.int32, (tm, LANES), 0)
        pos = (pl.program_id(0) * tm + row) % seq_len
        ksl_ref[:, LANES:2 * LANES] = jnp.where(lane == pos // SLC_BLOCK, 1.0, 0.0).astype(BF16)

    def store_rv(p):
        rv_ref[...] = p.astype(BF16)

    def store_values(p):
        low_half = lane < NSA_DK
        for v_ref, v in ((vsl_ref, p[:, 0:LANES]), (vwn_ref, p[:, LANES:2 * LANES])):
            v_ref[0, 0:LANES, :] = jnp.where(low_half, v, 1.0).T.astype(BF16)
            v_ref[0, LANES:2 * LANES, :] = jnp.where(low_half, 1.0, v).T.astype(BF16)

    def store_rg(p):
        rg_ref[...] = p

    def store_compress_inputs(p):
        for s in range(2):
            stage_ref[...] = p[:, s * LANES:(s + 1) * LANES]
            for l in range(CMP_STRIDE):
                kvc_ref[s, :, l * LANES:(l + 1) * LANES] = stage_ref[
                    pl.ds(l, tm // CMP_STRIDE, stride=CMP_STRIDE), :]

    def store_gates(p):
        ng_ref[...] = p

    stages = [(COL_RQ, RET_QK, store_rq), (COL_RK, RET_QK, store_rk), (COL_NQ, D_NSA, store_nq),
              (COL_KSL, 2 * NSA_KV, store_keys), (COL_RV, D_RET, store_rv),
              (COL_VSL, 2 * NSA_KV, store_values), (COL_RG, D_RET, store_rg),
              (COL_KCM, 2 * NSA_KV, store_compress_inputs), (COL_NG, LANES, store_gates)]
    pending = None
    for c0, n, store in stages:
        p = proj(c0, n)
        if pending is not None:
            pending[1](pending[0])
        pending = (p, store)
    pending[1](pending[0])


def _inproj(x2, gain, w, layer, cos, sin, seq_len):
    T, D = x2.shape
    tm = TOK_TILE
    nt = seq_len // tm
    tok = lambda n: pl.BlockSpec((tm, n), lambda i: (i, 0))
    tok_t = pl.BlockSpec((1, 2 * LANES, tm), lambda i: (i // nt, 0, i % nt))
    out_shape = (
        jax.ShapeDtypeStruct((T, RET_QK), BF16),
        jax.ShapeDtypeStruct((T // seq_len, RET_QK, seq_len), BF16),
        jax.ShapeDtypeStruct((T, D_NSA), BF16),
        jax.ShapeDtypeStruct((T, 2 * LANES), BF16),
        jax.ShapeDtypeStruct((T, LANES), BF16),
        jax.ShapeDtypeStruct((T, D_RET), BF16),
        jax.ShapeDtypeStruct((T // seq_len, 2 * LANES, seq_len), BF16),
        jax.ShapeDtypeStruct((T // seq_len, 2 * LANES, seq_len), BF16),
        jax.ShapeDtypeStruct((T, D_RET), F32),
        jax.ShapeDtypeStruct((2, T // CMP_STRIDE, CMP_STRIDE * LANES), F32),
        jax.ShapeDtypeStruct((T, LANES), F32),
    )
    out_specs = (tok(RET_QK), pl.BlockSpec((1, RET_QK, tm), lambda i: (i // nt, 0, i % nt)),
                 tok(D_NSA), tok(2 * LANES), tok(LANES), tok(D_RET),
                 tok_t, tok_t, tok(D_RET),
                 pl.BlockSpec((2, tm // CMP_STRIDE, CMP_STRIDE * LANES), lambda i: (0, i, 0)), tok(LANES))
    return pl.pallas_call(
        functools.partial(_inproj_kernel, seq_len=seq_len),
        grid=(T // tm,),
        in_specs=[tok(D),
                  pl.BlockSpec((1, D), lambda i: (0, 0)),
                  pl.BlockSpec((None, D, IN_COLS_PAD), lambda i: (layer, 0, 0)),
                  pl.BlockSpec((tm, LANES), lambda i: (i % nt, 0)),
                  pl.BlockSpec((tm, LANES), lambda i: (i % nt, 0))],
        out_specs=out_specs,
        out_shape=out_shape,
        scratch_shapes=[pltpu.VMEM((tm, LANES), F32)],
        compiler_params=pltpu.CompilerParams(dimension_semantics=("arbitrary",),
                                             vmem_limit_bytes=V7X_VMEM_LIMIT),
        name="inproj",
    )(x2, gain, w, cos, sin)


def _retention_kernel(q_ref, k_ref, v_ref, g_ref, gnw_ref, o_ref, state_ref, vbd_ref):
    C = RET_CHUNK
    n_chunks = q_ref.shape[0] // C

    @pl.when(pl.program_id(1) == 0)
    def _():
        state_ref[...] = jnp.zeros_like(state_ref)

    vbd_ref[...] = jnp.zeros_like(vbd_ref)

    ii = lax.broadcasted_iota(jnp.int32, (C, C), 0)
    jj = lax.broadcasted_iota(jnp.int32, (C, C), 1)
    diff = (ii - jj).astype(F32)
    i_col = lax.broadcasted_iota(jnp.int32, (C, 1), 0).astype(F32)
    low_half = lax.broadcasted_iota(jnp.int32, (C, LANES), 1) < RET_DK
    low_cols = lax.broadcasted_iota(jnp.int32, (C, 2 * RET_DV), 1) < RET_DV
    own_block = ((lax.broadcasted_iota(jnp.int32, (LANES, 2 * RET_DV), 0) < RET_DK)
                 == (lax.broadcasted_iota(jnp.int32, (LANES, 2 * RET_DV), 1) < RET_DV))
    low_rows = lax.broadcasted_iota(jnp.int32, (LANES, C), 0) < RET_DK
    j_row = lax.broadcasted_iota(jnp.int32, (1, C), 1).astype(F32)

    for pair in range(RET_HEADS // 2):
        lg0, lg1 = (math.log(1.0 - 2.0 ** (-5.0 - h)) for h in (2 * pair, 2 * pair + 1))
        decay = [jnp.where(diff >= 0, jnp.exp(lg * jnp.maximum(diff, 0.0)), 0.0) for lg in (lg0, lg1)]
        xi = jnp.where(low_cols, jnp.exp(lg0 * (i_col + 1.0)), jnp.exp(lg1 * (i_col + 1.0)))
        zeta = jnp.where(low_rows, jnp.exp(lg0 * (C - 1.0 - j_row)), jnp.exp(lg1 * (C - 1.0 - j_row)))
        chunk_decay = jnp.where(low_cols[0:1], math.exp(lg0 * C), math.exp(lg1 * C))
        qk_cols = slice(pair * LANES, (pair + 1) * LANES)
        v_cols = slice(2 * pair * RET_DV, 2 * (pair + 1) * RET_DV)
        for c in range(n_chunks):
            rows = slice(c * C, (c + 1) * C)
            q = q_ref[rows, qk_cols]
            ks_t = k_ref[0, qk_cols, rows]
            v = v_ref[rows, v_cols]
            qf = q.astype(F32)
            q_stack = jnp.concatenate([jnp.where(low_half, qf, 0.0), jnp.where(low_half, 0.0, qf)],
                                      axis=0).astype(BF16)
            s = jnp.dot(q_stack, ks_t, preferred_element_type=F32)
            s_pair = jnp.concatenate([s[0:C] * decay[0], s[C:2 * C] * decay[1]], axis=1).astype(BF16)
            vbd_ref[pair, 0:C, 0:RET_DV] = v[:, 0:RET_DV]
            vbd_ref[pair, C:2 * C, RET_DV:2 * RET_DV] = v[:, RET_DV:2 * RET_DV]
            o = jnp.dot(s_pair, vbd_ref[pair], preferred_element_type=F32)
            state = state_ref[pair]
            o = o + jnp.dot(q, state.astype(BF16), preferred_element_type=F32) * xi
            kz_t = (ks_t.astype(F32) * zeta).astype(BF16)
            kv = jnp.dot(kz_t, v, preferred_element_type=F32)
            state_ref[pair] = state * chunk_decay + jnp.where(own_block, kv, 0.0)
            for e in range(2):
                cols = slice((2 * pair + e) * RET_DV, (2 * pair + e + 1) * RET_DV)
                oh = o[:, e * RET_DV:(e + 1) * RET_DV]
                mu = jnp.mean(oh, axis=-1, keepdims=True)
                var = jnp.mean(jnp.square(oh - mu), axis=-1, keepdims=True)
                on = (oh - mu) * lax.rsqrt(var + GN_EPS) * gnw_ref[:, cols]
                gate = g_ref[rows, cols]
                o_ref[rows, cols] = (gate * (1.0 / (1.0 + jnp.exp(-gate))) * on).astype(BF16)


def _retention(rq, rk, rv, rg, gn_w, batch, seq_len):
    T = rq.shape[0]
    tc = RET_TILE
    nt = seq_len // tc
    tok = lambda n: pl.BlockSpec((tc, n), lambda b, i: (b * nt + i, 0))
    return pl.pallas_call(
        _retention_kernel,
        grid=(batch, nt),
        in_specs=[tok(RET_QK), pl.BlockSpec((1, RET_QK, tc), lambda b, i: (b, 0, i)), tok(D_RET), tok(D_RET),
                  pl.BlockSpec((1, D_RET), lambda b, i: (0, 0))],
        out_specs=tok(D_RET),
        out_shape=jax.ShapeDtypeStruct((T, D_RET), BF16),
        scratch_shapes=[pltpu.VMEM((RET_HEADS // 2, LANES, 2 * RET_DV), F32),
                        pltpu.VMEM((RET_HEADS // 2, 2 * RET_CHUNK, 2 * RET_DV), BF16)],
        compiler_params=pltpu.CompilerParams(dimension_semantics=("arbitrary", "arbitrary"),
                                             vmem_limit_bytes=V7X_VMEM_LIMIT),
        name="retention",
    )(rq, rk, rv, rg, gn_w)


def _compress_kernel(x_ref, pa_ref, pb_ref, wa_ref, wb_ref, w2_ref, cos_ref, sin_ref, o_ref, ot_ref):
    ng = x_ref.shape[2]
    x = x_ref[0, 0]
    xa = (x + pa_ref[0]).astype(BF16)
    xb = (x + pb_ref[0]).astype(BF16)
    a = jnp.dot(xa, wa_ref[0], preferred_element_type=F32)
    b = jnp.dot(xb, wb_ref[0], preferred_element_type=F32)
    hid = a + pltpu.roll(b, ng - 1, 0)
    out = jnp.dot(_gelu_tanh(hid).astype(BF16), w2_ref[0], preferred_element_type=F32)
    lane = lax.broadcasted_iota(jnp.int32, out.shape, 1)
    out = _rope(out, cos_ref[0], sin_ref[0], (lane % NSA_DK) < NSA_DK // 2)
    o_ref[0, 0] = out.astype(BF16)
    ot_ref[0, 0] = out.T.astype(BF16)


def _compress(kvc, pos_a, pos_b, wa, wb, w2, layer, cos, sin, batch, seq_len):
    ng = seq_len // CMP_STRIDE
    gw = CMP_STRIDE * LANES
    x = kvc.reshape(2, batch, ng, gw)
    hid = NSA_KV_HEADS * CMP_HID
    per_kv = lambda *shape: pl.BlockSpec((1,) + shape, lambda s, b: (s,) + (0,) * len(shape))
    per_lkv = lambda *shape: pl.BlockSpec((None, 1) + shape, lambda s, b: (layer, s) + (0,) * len(shape))
    return pl.pallas_call(
        _compress_kernel,
        grid=(2, batch),
        in_specs=[pl.BlockSpec((1, 1, ng, gw), lambda s, b: (s, b, 0, 0)),
                  per_lkv(1, gw), per_lkv(1, gw), per_lkv(gw, hid), per_lkv(gw, hid), per_lkv(hid, LANES),
                  per_kv(ng, LANES), per_kv(ng, LANES)],
        out_specs=(pl.BlockSpec((1, 1, ng, LANES), lambda s, b: (s, b, 0, 0)),
                   pl.BlockSpec((1, 1, LANES, ng), lambda s, b: (s, b, 0, 0))),
        out_shape=(jax.ShapeDtypeStruct((2, batch, ng, LANES), BF16),
                   jax.ShapeDtypeStruct((2, batch, LANES, ng), BF16)),
        compiler_params=pltpu.CompilerParams(dimension_semantics=("arbitrary", "arbitrary"),
                                             vmem_limit_bytes=V7X_VMEM_LIMIT),
        name="compress",
    )(x, pos_a, pos_b, wa, wb, w2, cos, sin)


def _nsa_kernel(nq_ref, ng_ref, *refs, seq_len, topk):
    n_in = 6
    o_ref = refs[n_in]

    def one_tile(j, carry):
        rows = pl.ds(pl.multiple_of(j * NSA_QBLOCK, NSA_QBLOCK), NSA_QBLOCK)
        t0 = (pl.program_id(1) * NSA_TILES_PER_STEP + j) * NSA_QBLOCK
        _nsa_tile(t0, nq_ref.at[rows], ng_ref.at[rows], *refs[:n_in], o_ref.at[rows], *refs[n_in + 1:],
                  seq_len=seq_len, topk=topk)
        return carry

    lax.fori_loop(0, NSA_TILES_PER_STEP, one_tile, 0)


def _nsa_tile(t0, nq_ref, ng_ref, kc_ref, vc_ref, ksl_ref, vsl_ref, kwn_ref, vwn_ref,
              o_ref, qaug_ref, s_ref, m_ref, acc_ref, out_ref, psum_ref, *, seq_len, topk):
    QB = NSA_QBLOCK
    R = NSA_GROUP
    G = NSA_KV_HEADS
    M = R * QB
    NC = kc_ref.shape[2]
    KT = SLC_KTILE
    WK = WINDOW + QB
    q = nq_ref[...].astype(F32)

    def per_head(x):
        return jnp.concatenate([x] * R, axis=1)

    def normalized(acc, g):
        num, den = (acc[0:NSA_DK], acc[NSA_DK:NSA_DK + 1]) if g == 0 else (acc[NSA_DK:], acc[0:1])
        return num * (1.0 / den)

    low_rows = lax.broadcasted_iota(jnp.int32, (LANES, QB), 0) < NSA_DK
    for i in range(R):
        q_t = q[:, i * LANES:(i + 1) * LANES].T
        for g in range(G):
            own = low_rows if g == 0 else jnp.logical_not(low_rows)
            qaug_ref[g, 0:LANES, i * QB:(i + 1) * QB] = jnp.where(own, q_t, 0.0).astype(BF16)

    ks = pl.multiple_of(jnp.clip(t0 - WINDOW, 0, seq_len - WK), QB)
    k_pos = ks + lax.broadcasted_iota(jnp.int32, (WK, QB), 0)
    t_pos = t0 + lax.broadcasted_iota(jnp.int32, (WK, QB), 1)
    win_mask = per_head((k_pos <= t_pos) & (k_pos > t_pos - WINDOW))

    def slc_scores(k0, slot):
        for g in range(G):
            s_ref[slot, g] = jnp.dot(ksl_ref[0, pl.ds(k0, KT), :], qaug_ref[g],
                                     preferred_element_type=F32)

    def compressed_select_window(nc):
        win_scores = [jnp.dot(kwn_ref[0, pl.ds(ks, WK), :], qaug_ref[g, 0:LANES, :],
                              preferred_element_type=F32) for g in range(G)]
        cmp_scores = [jnp.dot(kc_ref[0, 0, 0:nc, :], qaug_ref[g, 0:LANES, :],
                              preferred_element_type=F32) for g in range(G)]

        nb = min(nc * CMP_STRIDE // SLC_BLOCK, LANES)
        jj = lax.broadcasted_iota(jnp.int32, (nb, QB), 0)
        q_blk = (t0 + lax.broadcasted_iota(jnp.int32, (nb, QB), 1)) // SLC_BLOCK
        valid = jj <= q_blk
        forced = (jj == 0) | (valid & (jj > q_blk - SLC_LOCAL))
        jf = jj.astype(F32)

        def pick_round(score):
            best = jnp.max(score, axis=0, keepdims=True)
            first = jnp.min(jnp.where(score == best, jf, float(LANES)), axis=0, keepdims=True)
            return jnp.where(jf == first, -jnp.inf, score)

        scores = []
        n_idx = lax.broadcasted_iota(jnp.int32, (nc, QB), 0)
        t_col = t0 + lax.broadcasted_iota(jnp.int32, (nc, QB), 1)
        cmp_mask = per_head(n_idx * CMP_STRIDE + (CMP_BLOCK - 1) <= t_col)
        for g in range(G):
            sc = jnp.where(cmp_mask, cmp_scores[g], NEG)
            e = jnp.exp2(sc - jnp.max(sc, axis=0, keepdims=True))
            if nc == LANES:
                e = jnp.where(cmp_mask, e, 0.0)
            p_cmp = e * (1.0 / jnp.maximum(jnp.sum(e, axis=0, keepdims=True), 1e-30))
            acc = jnp.dot(vc_ref[0, 0, :, 0:nc], p_cmp.astype(BF16), preferred_element_type=F32)
            out_ref[0, g * NSA_DK:(g + 1) * NSA_DK, :] = acc[g * NSA_DK:(g + 1) * NSA_DK]
            p_sum = p_cmp[:, 0:QB]
            for i in range(1, R):
                p_sum = p_sum + p_cmp[:, i * QB:(i + 1) * QB]
            psum_ref[g, 0:SUBLANES, :] = jnp.zeros((SUBLANES, QB), F32)
            psum_ref[g, SUBLANES:SUBLANES + nc, :] = p_sum
            ratio, lead = SLC_BLOCK // CMP_STRIDE, CMP_BLOCK // CMP_STRIDE - 1
            imp = psum_ref[g, pl.ds(SUBLANES - lead, nb, stride=ratio), :]
            for d in range(1 - lead, ratio):
                imp = imp + psum_ref[g, pl.ds(SUBLANES + d, nb, stride=ratio), :]
            scores.append(jnp.where(forced, -jnp.inf, jnp.where(valid, imp, -SEL_BIG)))

        for _ in range(topk - (1 + SLC_LOCAL)):
            scores = [pick_round(score) for score in scores]
        for g in range(G):
            selected = ((scores[g] == -jnp.inf) | (q_blk < topk)) & valid
            bias = jnp.where(selected, 0.0, NEG)
            if nb < LANES:
                bias = jnp.concatenate([bias, jnp.full((LANES - nb, QB), NEG, F32)], axis=0)
            qaug_ref[g, LANES:2 * LANES, :] = per_head(bias.astype(BF16))
        slc_scores(0, 0)

        for g in range(G):
            sw = jnp.where(win_mask, win_scores[g], NEG)
            e_win = jnp.exp2(sw - jnp.max(sw, axis=0, keepdims=True)).astype(BF16)
            pv = jnp.dot(vwn_ref[0, g * LANES:(g + 1) * LANES, pl.ds(ks, WK)], e_win,
                         preferred_element_type=F32)
            out_ref[1, g * NSA_DK:(g + 1) * NSA_DK, :] = normalized(pv, g)

    n_visible = (t0 + QB - CMP_BLOCK) // CMP_STRIDE + 1
    n_variants = NC // LANES
    for v in range(n_variants):
        @pl.when(jnp.minimum((n_visible - 1) // LANES, n_variants - 1) == v)
        def _():
            compressed_select_window((v + 1) * LANES)

    def slc_accumulate(k0, slot, causal):
        for g in range(G):
            s = s_ref[slot, g]
            if causal:
                k_pos = k0 + lax.broadcasted_iota(jnp.int32, (KT, QB), 0)
                t_pos = t0 + lax.broadcasted_iota(jnp.int32, (KT, QB), 1)
                s = jnp.where(per_head(k_pos <= t_pos), s, NEG)
            m_run = m_ref[g]
            m_new = jnp.maximum(m_run, jnp.max(s, axis=0, keepdims=True))
            p = jnp.exp2(s - m_new).astype(BF16)
            pv = jnp.dot(vsl_ref[0, g * LANES:(g + 1) * LANES, pl.ds(k0, KT)], p,
                         preferred_element_type=F32)
            acc_ref[g] = jnp.exp2(m_run - m_new) * acc_ref[g] + pv
            m_ref[g] = m_new

    m_ref[...] = jnp.full(m_ref.shape, NEG, F32)
    acc_ref[...] = jnp.zeros_like(acc_ref)
    n_full = t0 // KT
    U = SLC_UNROLL

    def slc_run(k0, n_tiles, last_is_diagonal):
        for u in range(n_tiles):
            causal = last_is_diagonal and u == n_tiles - 1
            if not causal:
                slc_scores(k0 + (u + 1) * KT, (u + 1) % 2)
            slc_accumulate(k0 + u * KT, u % 2, causal)

    def slc_group(j, _):
        slc_run(pl.multiple_of(j * (U * KT), U * KT), U, False)
        return 0

    lax.fori_loop(0, n_full // U, slc_group, 0)
    k_rest = pl.multiple_of((n_full // U) * (U * KT), U * KT)
    for r in range(U):
        @pl.when(n_full % U == r)
        def _():
            slc_run(k_rest, r + 1, True)

    o_slc = [normalized(acc_ref[g], g) for g in range(G)]

    gates = (1.0 / (1.0 + jnp.exp(-ng_ref[...]))).T

    def gate_rows(branch, head):
        r = branch * NSA_HEADS + head
        return jnp.broadcast_to(gates[r:r + 1, :], (NSA_DK, QB))

    branches = [out_ref[0], jnp.concatenate(o_slc, axis=0), out_ref[1]]
    for i in range(R):
        y = jnp.zeros((LANES, QB), F32)
        for b, o in enumerate(branches):
            gate = jnp.concatenate([gate_rows(b, g * R + i) for g in range(G)], axis=0)
            y = y + gate * o[:, i * QB:(i + 1) * QB]
        o_ref[:, i * LANES:(i + 1) * LANES] = y.T.astype(BF16)


def _nsa(nq, ng, kvc_cmp, kvc_cmp_t, ksl, vsl, kwn, vwn, batch, seq_len):
    T = nq.shape[0]
    QB = NSA_QBLOCK
    rows = QB * NSA_TILES_PER_STEP
    nqb = seq_len // rows
    nc = seq_len // CMP_STRIDE
    topk = min(SLC_TOPK, seq_len // SLC_BLOCK)
    tok = lambda n: pl.BlockSpec((rows, n), lambda b, i: (b * nqb + i, 0))
    seq = lambda n: pl.BlockSpec((1, seq_len, n), lambda b, i: (b, 0, 0))
    seq_t = pl.BlockSpec((1, 2 * LANES, seq_len), lambda b, i: (b, 0, 0))
    return pl.pallas_call(
        functools.partial(_nsa_kernel, seq_len=seq_len, topk=topk),
        grid=(batch, nqb),
        in_specs=[tok(D_NSA), tok(LANES),
                  pl.BlockSpec((1, 1, nc, LANES), lambda b, i: (0, b, 0, 0)),
                  pl.BlockSpec((1, 1, LANES, nc), lambda b, i: (1, b, 0, 0)),
                  seq(2 * LANES), seq_t, seq(LANES), seq_t],
        out_specs=tok(D_NSA),
        out_shape=jax.ShapeDtypeStruct((T, D_NSA), BF16),
        scratch_shapes=[pltpu.VMEM((NSA_KV_HEADS, 2 * LANES, NSA_GROUP * QB), BF16),
                        pltpu.VMEM((2, NSA_KV_HEADS, SLC_KTILE, NSA_GROUP * QB), F32),
                        pltpu.VMEM((NSA_KV_HEADS, 1, NSA_GROUP * QB), F32),
                        pltpu.VMEM((NSA_KV_HEADS, LANES, NSA_GROUP * QB), F32),
                        pltpu.VMEM((2, LANES, NSA_GROUP * QB), F32),
                        pltpu.VMEM((NSA_KV_HEADS, nc + SUBLANES, QB), F32)],
        compiler_params=pltpu.CompilerParams(dimension_semantics=("arbitrary", "arbitrary"),
                                             vmem_limit_bytes=V7X_VMEM_LIMIT),
        name="nsa",
    )(nq, ng, kvc_cmp, kvc_cmp_t, ksl.reshape(batch, seq_len, -1), vsl,
      kwn.reshape(batch, seq_len, -1), vwn)


def _ffn_kernel(yr_ref, yrp_ref, yn_ref, ynp_ref, x_ref, xp_ref, wo_ref, gmix_ref, gpre_ref,
                wu_ref, conv_ref, wd_ref, gpost_ref, o_ref,
                h_ref, ug_ref, uv_ref, acc_ref, xmid_ref, *, tiles_per_seq):
    tm = x_ref.shape[0]
    d_ff = wd_ref.shape[0]
    tf = FFN_FTILE
    H = FFN_HALO
    rc = tm // FFN_ROW_CHUNKS

    def rms(x, gain_ref):
        return x * lax.rsqrt(jnp.mean(x * x, axis=-1, keepdims=True) + NORM_EPS) * gain_ref[...]

    keep = jnp.where(pl.program_id(0) % tiles_per_seq == 0, 0.0, 1.0)
    for c in range(FFN_ROW_CHUNKS):
        rows = slice(c * rc, (c + 1) * rc)
        y_ret, y_nsa = yr_ref[rows, :], yn_ref[rows, :]
        if c == 0:
            y_ret = jnp.concatenate([yrp_ref[...], y_ret], axis=0)
            y_nsa = jnp.concatenate([ynp_ref[...], y_nsa], axis=0)
        mix = (jnp.dot(y_ret, wo_ref[0:D_RET, :], preferred_element_type=F32)
               + jnp.dot(y_nsa, wo_ref[D_RET:D_RET + D_NSA, :], preferred_element_type=F32))
        post = rms(mix, gmix_ref)
        if c == 0:
            h_ref[0:H, :] = (rms(xp_ref[...] + post[0:H], gpre_ref) * keep).astype(BF16)
            post = post[H:H + rc]
        x_mid = x_ref[rows, :] + post
        xmid_ref[rows, :] = x_mid
        h_ref[H + c * rc:H + (c + 1) * rc, :] = rms(x_mid, gpre_ref).astype(BF16)
    acc_ref[...] = jnp.zeros_like(acc_ref)

    n_tiles = d_ff // tf

    def columns(f):
        return (pl.ds(pl.multiple_of(f * tf, tf), tf), pl.ds(pl.multiple_of(d_ff + f * tf, tf), tf))

    def up_project(f, c):
        gate_cols, value_cols = columns(f)
        rows = slice(0 if c == 0 else H + c * rc, H + (c + 1) * rc)
        h = h_ref[rows, :]
        ug_ref[rows, :] = jnp.dot(h, wu_ref[:, gate_cols], preferred_element_type=F32)
        uv_ref[rows, :] = jnp.dot(h, wu_ref[:, value_cols], preferred_element_type=F32)

    def hidden_tile(f, carry):
        gate_cols, value_cols = columns(f)

        def causal_conv(u_ref, cols, r0, scale):
            out = (scale * conv_ref[CONV_WIDTH - 1:CONV_WIDTH, cols]) * u_ref[r0:r0 + rc, :]
            for k in range(CONV_WIDTH - 1):
                d = CONV_WIDTH - 1 - k
                out = out + (scale * conv_ref[k:k + 1, cols]) * u_ref[r0 - d:r0 - d + rc, :]
            return out

        def gate_and_down(c):
            g = causal_conv(ug_ref, gate_cols, H + c * rc, 1.0)
            v_half = causal_conv(uv_ref, value_cols, H + c * rc, 0.5)
            inner = g * (GELU_C + (GELU_C * 0.044715) * (g * g))
            act = (g + g * jnp.tanh(inner)) * v_half
            acc_ref[c * rc:(c + 1) * rc, :] += jnp.dot(act.astype(BF16), wd_ref[gate_cols, :],
                                                       preferred_element_type=F32)

        up_project(f, 0)
        for c in range(FFN_ROW_CHUNKS):
            if c + 1 < FFN_ROW_CHUNKS:
                up_project(f, c + 1)
            gate_and_down(c)
        return carry

    lax.fori_loop(0, n_tiles, hidden_tile, 0)
    o_ref[...] = xmid_ref[...] + rms(acc_ref[...], gpost_ref)


def _outproj_ffn(y_ret, y_nsa, x2, w_out, g_mix, g_pre, w_up, conv_w, w_down, layer, g_post, seq_len):
    T, D = x2.shape
    d_ff = w_down.shape[1]
    tm, tf, H = FFN_TILE, FFN_FTILE, FFN_HALO
    tile = lambda n: pl.BlockSpec((tm, n), lambda i: (i, 0))
    halo = lambda n: pl.BlockSpec((H, n), lambda i: (jnp.maximum(i * (tm // H) - 1, 0), 0))
    resident = lambda a: pl.BlockSpec((None,) + a.shape[1:], lambda i: (layer, 0, 0),
                                      pipeline_mode=pl.Buffered(1))
    gain = pl.BlockSpec((1, D), lambda i: (0, 0), pipeline_mode=pl.Buffered(1))
    return pl.pallas_call(
        functools.partial(_ffn_kernel, tiles_per_seq=seq_len // tm),
        grid=(T // tm,),
        in_specs=[tile(D_RET), halo(D_RET), tile(D_NSA), halo(D_NSA), tile(D), halo(D),
                  resident(w_out), gain, gain, resident(w_up), resident(conv_w), resident(w_down), gain],
        out_specs=tile(D),
        out_shape=jax.ShapeDtypeStruct((T, D), F32),
        scratch_shapes=[pltpu.VMEM((tm + H, D), BF16), pltpu.VMEM((tm + H, tf), F32),
                        pltpu.VMEM((tm + H, tf), F32), pltpu.VMEM((tm, D), F32),
                        pltpu.VMEM((tm, D), F32)],
        compiler_params=pltpu.CompilerParams(dimension_semantics=("arbitrary",),
                                             vmem_limit_bytes=V7X_VMEM_LIMIT),
        name="ffn",
    )(y_ret, y_ret, y_nsa, y_nsa, x2, x2, w_out, g_mix, g_pre, w_up, conv_w, w_down, g_post)


_NSA_HEAD_ORDER = [g * NSA_GROUP + i for i in range(NSA_GROUP) for g in range(NSA_KV_HEADS)]


def _rope_tables(pos):
    inv = 1.0 / (ROPE_THETA ** (jnp.arange(0, NSA_DK, 2, dtype=F32) / NSA_DK))
    ang = pos.astype(F32)[:, None] * inv[None, :]
    c, s = jnp.cos(ang), jnp.sin(ang)
    return jnp.concatenate([c, c, c, c], axis=1), jnp.concatenate([-s, s, -s, s], axis=1)


def _prep_w_in(w):
    lead = w.shape[:-1]
    splits = np.cumsum([RET_QK, RET_QK, D_RET, D_RET, D_NSA] + [NSA_KV] * 6)
    rq, rk, rv, rg, nq, kcm, vcm, ksl, vsl, kwn, vwn, ng = jnp.split(w, [int(s) for s in splits], axis=-1)
    nq = nq.reshape(lead + (NSA_HEADS, NSA_DK))[..., np.array(_NSA_HEAD_ORDER), :].reshape(lead + (D_NSA,))
    ng = jnp.pad(ng, [(0, 0)] * len(lead) + [(0, LANES - NSA_GATES)])
    return jnp.concatenate([rq, rk, nq, ksl, kwn, rv, vsl, vwn, rg, kcm, vcm, ng], axis=-1).astype(BF16)


def _prep_w_out(w):
    layers, _, d = w.shape
    w_nsa = w[:, D_RET:].reshape(layers, NSA_HEADS, NSA_DK, d)[:, np.array(_NSA_HEAD_ORDER)]
    return jnp.concatenate([w[:, :D_RET], w_nsa.reshape(layers, D_NSA, d)], axis=1).astype(BF16)


def _prep_compress(pos, w1, w2):
    lead = w1.shape[:-2]
    half = CMP_STRIDE

    def block_diag(w, axis):
        z = jnp.zeros_like(w)
        return jnp.stack([jnp.concatenate([w, z], axis=-1), jnp.concatenate([z, w], axis=-1)], axis=axis)

    def first_layer(w_half):
        w4 = w_half.reshape(lead + (half, NSA_DK, CMP_HID))
        return block_diag(w4, -3).reshape(lead + (half * NSA_KV_HEADS * NSA_DK, -1)).astype(BF16)

    def pos_row(p_half):
        rows = jnp.broadcast_to(p_half[..., :, None, :], lead + (half, NSA_KV_HEADS, NSA_DK))
        return rows.reshape(lead + (1, -1))

    w2x = block_diag(w2, -3).reshape(lead + (NSA_KV_HEADS * CMP_HID, -1)).astype(BF16)
    n1 = half * NSA_DK
    return (pos_row(pos[..., :half, :]), pos_row(pos[..., half:, :]),
            first_layer(w1[..., :n1, :]), first_layer(w1[..., n1:, :]), w2x)


def kernel(x, norm_mix_pre, w_in, ret_gn_w, cmp_k_pos, cmp_k_w1, cmp_k_w2, cmp_v_pos, cmp_v_w1, cmp_v_w2,
           w_out, norm_mix_post, norm_ffn_pre, ffn_w_up, ffn_conv, ffn_w_down, norm_ffn_post):
    B, S, D = x.shape
    depth = w_in.shape[0]
    assert S % SLC_KTILE == 0 and S % FFN_TILE == 0 and S // SLC_BLOCK <= LANES and S >= WINDOW + NSA_QBLOCK
    assert ffn_w_down.shape[1] % FFN_FTILE == 0
    assert min(SLC_TOPK, S // SLC_BLOCK) > 1 + SLC_LOCAL and S % (NSA_QBLOCK * NSA_TILES_PER_STEP) == 0

    cos, sin = _rope_tables(jnp.arange(S, dtype=jnp.int32))
    nc = S // CMP_STRIDE
    ccos, csin = _rope_tables(jnp.arange(nc, dtype=jnp.int32) * CMP_STRIDE + (CMP_BLOCK - 1))
    cmp_cos = jnp.stack([ccos, jnp.ones_like(ccos)])
    cmp_sin = jnp.stack([csin, jnp.zeros_like(csin)])

    w_in_p = _prep_w_in(w_in)
    w_out_p = _prep_w_out(w_out)
    w_up_p = ffn_w_up.astype(BF16)
    w_down_p = ffn_w_down.astype(BF16)
    cmp_p = _prep_compress(jnp.stack([cmp_k_pos, cmp_v_pos], axis=1), jnp.stack([cmp_k_w1, cmp_v_w1], axis=1),
                           jnp.stack([cmp_k_w2, cmp_v_w2], axis=1))

    x2 = x.reshape(B * S, D)
    for l in range(depth):
        outs = _inproj(x2, norm_mix_pre[l][None], w_in_p, l, cos, sin, S)
        rq, rk, nq, ksl, kwn, rv, vsl, vwn, rg, kvc, ng = outs
        y_ret = _retention(rq, rk, rv, rg, ret_gn_w[l][None], B, S)
        kvc_cmp, kvc_cmp_t = _compress(kvc, *cmp_p, l, cmp_cos, cmp_sin, B, S)
        y_nsa = _nsa(nq, ng, kvc_cmp, kvc_cmp_t, ksl, vsl, kwn, vwn, B, S)
        x2 = _outproj_ffn(y_ret, y_nsa, x2, w_out_p, norm_mix_post[l][None], norm_ffn_pre[l][None],
                          w_up_p, ffn_conv, w_down_p, l, norm_ffn_post[l][None], S)
    return x2.reshape(B, S, D)
```

```python
import functools
import math

import jax
import jax.numpy as jnp
import numpy as np
from jax import lax
from jax.experimental import pallas as pl
from jax.experimental.pallas import tpu as pltpu

F32 = jnp.float32
BF16 = jnp.bfloat16

LANES = 128
SUBLANES = 8
V7X_VMEM_LIMIT = 56 * 1024 * 1024

ROPE_THETA = 10000.0
NORM_EPS = 1e-6
GN_EPS = 1e-5
NEG = -1e30
SEL_BIG = 1e9

RET_HEADS = 4
RET_DK = 64
RET_DV = 128
RET_CHUNK = 128
NSA_HEADS = 8
NSA_KV_HEADS = 2
NSA_DK = 64
NSA_GROUP = NSA_HEADS // NSA_KV_HEADS
CMP_BLOCK = 32
CMP_STRIDE = 16
CMP_HID = 256
SLC_BLOCK = 64
SLC_TOPK = 16
SLC_LOCAL = 2
WINDOW = 512
NSA_QBLOCK = 128
NSA_TILES_PER_STEP = 16
NSA_GATES = 3 * NSA_HEADS
CONV_WIDTH = 3

NSA_Q_SCALE = NSA_DK ** -0.5 * math.log2(math.e)

D_RET = RET_HEADS * RET_DV
D_NSA = NSA_HEADS * NSA_DK
RET_QK = RET_HEADS * RET_DK
NSA_KV = NSA_KV_HEADS * NSA_DK

COL_RQ = 0
COL_RK = COL_RQ + RET_QK
COL_NQ = COL_RK + RET_QK
COL_KSL = COL_NQ + D_NSA
COL_KWN = COL_KSL + NSA_KV
ROPE_COLS = COL_KWN + NSA_KV
COL_RV = ROPE_COLS
COL_VSL = COL_RV + D_RET
COL_VWN = COL_VSL + NSA_KV
COL_RG = COL_VWN + NSA_KV
COL_KCM = COL_RG + D_RET
COL_VCM = COL_KCM + NSA_KV
COL_NG = COL_VCM + NSA_KV
IN_COLS_PAD = COL_NG + LANES

TOK_TILE = 512
RET_TILE = 512
FFN_TILE = 1024
FFN_FTILE = 256
FFN_HALO = 16
FFN_ROW_CHUNKS = 4
SLC_KTILE = 512
SLC_UNROLL = 4


GELU_C = math.sqrt(2.0 / math.pi)


def _gelu_tanh(x):
    return 0.5 * x * (1.0 + jnp.tanh(GELU_C * (x + 0.044715 * (x * x * x))))


def _rope(p, cos, sin_signed, first_half):
    half = NSA_DK // 2
    partner = jnp.where(first_half, pltpu.roll(p, LANES - half, 1), pltpu.roll(p, half, 1))
    return p * cos + partner * sin_signed


def _inproj_kernel(x_ref, g_ref, w_ref, cos_ref, sin_ref,
                   rq_ref, rk_ref, nq_ref, ksl_ref, kwn_ref, rv_ref, vsl_ref, vwn_ref,
                   rg_ref, kvc_ref, ng_ref, stage_ref, *, seq_len):
    tm = x_ref.shape[0]
    x = x_ref[...]
    h = (x * lax.rsqrt(jnp.mean(x * x, axis=-1, keepdims=True) + NORM_EPS) * g_ref[...]).astype(BF16)
    cos = cos_ref[...]
    sin = sin_ref[...]
    lane = lax.broadcasted_iota(jnp.int32, (tm, LANES), 1)
    first_half = (lane % NSA_DK) < NSA_DK // 2

    def proj(c0, n):
        return jnp.dot(h, w_ref[:, c0:c0 + n], preferred_element_type=F32)

    def rope_slab(p, i):
        return _rope(p[:, i * LANES:(i + 1) * LANES], cos, sin, first_half)

    def store_rq(p):
        for i in range(RET_QK // LANES):
            rq_ref[:, i * LANES:(i + 1) * LANES] = rope_slab(p, i).astype(BF16)

    def store_rk(p):
        for i in range(RET_QK // LANES):
            rk_ref[:, i * LANES:(i + 1) * LANES] = (rope_slab(p, i) * (RET_DK ** -0.5)).astype(BF16)

    def store_nq(p):
        for i in range(D_NSA // LANES):
            nq_ref[:, i * LANES:(i + 1) * LANES] = (rope_slab(p, i) * NSA_Q_SCALE).astype(BF16)

    def store_keys(p):
        ksl_ref[:, 0:LANES] = rope_slab(p, 0).astype(BF16)
        kwn_ref[...] = rope_slab(p, 1).astype(BF16)
        row = lax.broadcasted_iota(jnp.int32, (tm, LANES), 0)
        pos = (pl.program_id(0) * tm + row) % seq_len
        ksl_ref[:, LANES:2 * LANES] = jnp.where(lane == pos // SLC_BLOCK, 1.0, 0.0).astype(BF16)

    def store_rv(p):
        rv_ref[...] = p.astype(BF16)

    def store_values(p):
        low_half = lane < NSA_DK
        for v_ref, v in ((vsl_ref, p[:, 0:LANES]), (vwn_ref, p[:, LANES:2 * LANES])):
            v_ref[0, 0:LANES, :] = jnp.where(low_half, v, 1.0).T.astype(BF16)
            v_ref[0, LANES:2 * LANES, :] = jnp.where(low_half, 1.0, v).T.astype(BF16)

    def store_rg(p):
        rg_ref[...] = p

    def store_compress_inputs(p):
        for s in range(2):
            stage_ref[...] = p[:, s * LANES:(s + 1) * LANES]
            for l in range(CMP_STRIDE):
                kvc_ref[s, :, l * LANES:(l + 1) * LANES] = stage_ref[
                    pl.ds(l, tm // CMP_STRIDE, stride=CMP_STRIDE), :]

    def store_gates(p):
        ng_ref[...] = p

    stages = [(COL_RQ, RET_QK, store_rq), (COL_RK, RET_QK, store_rk), (COL_NQ, D_NSA, store_nq),
              (COL_KSL, 2 * NSA_KV, store_keys), (COL_RV, D_RET, store_rv),
              (COL_VSL, 2 * NSA_KV, store_values), (COL_RG, D_RET, store_rg),
              (COL_KCM, 2 * NSA_KV, store_compress_inputs), (COL_NG, LANES, store_gates)]
    pending = None
    for c0, n, store in stages:
        p = proj(c0, n)
        if pending is not None:
            pending[1](pending[0])
        pending = (p, store)
    pending[1](pending[0])


def _inproj(x2, gain, w, layer, cos, sin, seq_len):
    T, D = x2.shape
    tm = TOK_TILE
    nt = seq_len // tm
    tok = lambda n: pl.BlockSpec((tm, n), lambda i: (i, 0))
    tok_t = pl.BlockSpec((1, 2 * LANES, tm), lambda i: (i // nt, 0, i % nt))
    out_shape = (
        jax.ShapeDtypeStruct((T, RET_QK), BF16),
        jax.ShapeDtypeStruct((T, RET_QK), BF16),
        jax.ShapeDtypeStruct((T, D_NSA), BF16),
        jax.ShapeDtypeStruct((T, 2 * LANES), BF16),
        jax.ShapeDtypeStruct((T, LANES), BF16),
        jax.ShapeDtypeStruct((T, D_RET), BF16),
        jax.ShapeDtypeStruct((T // seq_len, 2 * LANES, seq_len), BF16),
        jax.ShapeDtypeStruct((T // seq_len, 2 * LANES, seq_len), BF16),
        jax.ShapeDtypeStruct((T, D_RET), F32),
        jax.ShapeDtypeStruct((2, T // CMP_STRIDE, CMP_STRIDE * LANES), F32),
        jax.ShapeDtypeStruct((T, LANES), F32),
    )
    out_specs = (tok(RET_QK), tok(RET_QK), tok(D_NSA), tok(2 * LANES), tok(LANES), tok(D_RET),
                 tok_t, tok_t, tok(D_RET),
                 pl.BlockSpec((2, tm // CMP_STRIDE, CMP_STRIDE * LANES), lambda i: (0, i, 0)), tok(LANES))
    return pl.pallas_call(
        functools.partial(_inproj_kernel, seq_len=seq_len),
        grid=(T // tm,),
        in_specs=[tok(D),
                  pl.BlockSpec((1, D), lambda i: (0, 0)),
                  pl.BlockSpec((None, D, IN_COLS_PAD), lambda i: (layer, 0, 0)),
                  pl.BlockSpec((tm, LANES), lambda i: (i % nt, 0)),
                  pl.BlockSpec((tm, LANES), lambda i: (i % nt, 0))],
        out_specs=out_specs,
        out_shape=out_shape,
        scratch_shapes=[pltpu.VMEM((tm, LANES), F32)],
        compiler_params=pltpu.CompilerParams(dimension_semantics=("arbitrary",),
                                             vmem_limit_bytes=V7X_VMEM_LIMIT),
        name="inproj",
    )(x2, gain, w, cos, sin)


def _retention_kernel(q_ref, k_ref, v_ref, g_ref, gnw_ref, o_ref, state_ref, vbd_ref):
    C = RET_CHUNK
    n_chunks = q_ref.shape[0] // C

    @pl.when(pl.program_id(1) == 0)
    def _():
        state_ref[...] = jnp.zeros_like(state_ref)

    vbd_ref[...] = jnp.zeros_like(vbd_ref)

    ii = lax.broadcasted_iota(jnp.int32, (C, C), 0)
    jj = lax.broadcasted_iota(jnp.int32, (C, C), 1)
    diff = (ii - jj).astype(F32)
    i_col = lax.broadcasted_iota(jnp.int32, (C, 1), 0).astype(F32)
    low_half = lax.broadcasted_iota(jnp.int32, (C, LANES), 1) < RET_DK
    low_cols = lax.broadcasted_iota(jnp.int32, (C, 2 * RET_DV), 1) < RET_DV
    own_block = ((lax.broadcasted_iota(jnp.int32, (LANES, 2 * RET_DV), 0) < RET_DK)
                 == (lax.broadcasted_iota(jnp.int32, (LANES, 2 * RET_DV), 1) < RET_DV))
    nt_dims = (((1,), (1,)), ((), ()))
    tn_dims = (((0,), (0,)), ((), ()))

    for pair in range(RET_HEADS // 2):
        lg0, lg1 = (math.log(1.0 - 2.0 ** (-5.0 - h)) for h in (2 * pair, 2 * pair + 1))
        decay = [jnp.where(diff >= 0, jnp.exp(lg * jnp.maximum(diff, 0.0)), 0.0) for lg in (lg0, lg1)]
        xi = jnp.where(low_cols, jnp.exp(lg0 * (i_col + 1.0)), jnp.exp(lg1 * (i_col + 1.0)))
        zeta = jnp.where(low_half, jnp.exp(lg0 * (C - 1.0 - i_col)), jnp.exp(lg1 * (C - 1.0 - i_col)))
        chunk_decay = jnp.where(low_cols[0:1], math.exp(lg0 * C), math.exp(lg1 * C))
        qk_cols = slice(pair * LANES, (pair + 1) * LANES)
        v_cols = slice(2 * pair * RET_DV, 2 * (pair + 1) * RET_DV)
        for c in range(n_chunks):
            rows = slice(c * C, (c + 1) * C)
            q = q_ref[rows, qk_cols]
            ks = k_ref[rows, qk_cols]
            v = v_ref[rows, v_cols]
            qf = q.astype(F32)
            q_stack = jnp.concatenate([jnp.where(low_half, qf, 0.0), jnp.where(low_half, 0.0, qf)],
                                      axis=0).astype(BF16)
            s = lax.dot_general(q_stack, ks, nt_dims, preferred_element_type=F32)
            s_pair = jnp.concatenate([s[0:C] * decay[0], s[C:2 * C] * decay[1]], axis=1).astype(BF16)
            vbd_ref[pair, 0:C, 0:RET_DV] = v[:, 0:RET_DV]
            vbd_ref[pair, C:2 * C, RET_DV:2 * RET_DV] = v[:, RET_DV:2 * RET_DV]
            o = jnp.dot(s_pair, vbd_ref[pair], preferred_element_type=F32)
            state = state_ref[pair]
            o = o + jnp.dot(q, state.astype(BF16), preferred_element_type=F32) * xi
            kz = (ks.astype(F32) * zeta).astype(BF16)
            kv = lax.dot_general(kz, v, tn_dims, preferred_element_type=F32)
            state_ref[pair] = state * chunk_decay + jnp.where(own_block, kv, 0.0)
            for e in range(2):
                cols = slice((2 * pair + e) * RET_DV, (2 * pair + e + 1) * RET_DV)
                oh = o[:, e * RET_DV:(e + 1) * RET_DV]
                mu = jnp.mean(oh, axis=-1, keepdims=True)
                var = jnp.mean(jnp.square(oh - mu), axis=-1, keepdims=True)
                on = (oh - mu) * lax.rsqrt(var + GN_EPS) * gnw_ref[:, cols]
                gate = g_ref[rows, cols]
                o_ref[rows, cols] = (gate * (1.0 / (1.0 + jnp.exp(-gate))) * on).astype(BF16)


def _retention(rq, rk, rv, rg, gn_w, batch, seq_len):
    T = rq.shape[0]
    tc = RET_TILE
    nt = seq_len // tc
    tok = lambda n: pl.BlockSpec((tc, n), lambda b, i: (b * nt + i, 0))
    return pl.pallas_call(
        _retention_kernel,
        grid=(batch, nt),
        in_specs=[tok(RET_QK), tok(RET_QK), tok(D_RET), tok(D_RET),
                  pl.BlockSpec((1, D_RET), lambda b, i: (0, 0))],
        out_specs=tok(D_RET),
        out_shape=jax.ShapeDtypeStruct((T, D_RET), BF16),
        scratch_shapes=[pltpu.VMEM((RET_HEADS // 2, LANES, 2 * RET_DV), F32),
                        pltpu.VMEM((RET_HEADS // 2, 2 * RET_CHUNK, 2 * RET_DV), BF16)],
        compiler_params=pltpu.CompilerParams(dimension_semantics=("arbitrary", "arbitrary"),
                                             vmem_limit_bytes=V7X_VMEM_LIMIT),
        name="retention",
    )(rq, rk, rv, rg, gn_w)


def _compress_kernel(x_ref, pa_ref, pb_ref, wa_ref, wb_ref, w2_ref, cos_ref, sin_ref, o_ref, ot_ref):
    ng = x_ref.shape[2]
    x = x_ref[0, 0]
    xa = (x + pa_ref[0]).astype(BF16)
    xb = (x + pb_ref[0]).astype(BF16)
    a = jnp.dot(xa, wa_ref[0], preferred_element_type=F32)
    b = jnp.dot(xb, wb_ref[0], preferred_element_type=F32)
    hid = a + pltpu.roll(b, ng - 1, 0)
    out = jnp.dot(_gelu_tanh(hid).astype(BF16), w2_ref[0], preferred_element_type=F32)
    lane = lax.broadcasted_iota(jnp.int32, out.shape, 1)
    out = _rope(out, cos_ref[0], sin_ref[0], (lane % NSA_DK) < NSA_DK // 2)
    o_ref[0, 0] = out.astype(BF16)
    ot_ref[0, 0] = out.T.astype(BF16)


def _compress(kvc, pos_a, pos_b, wa, wb, w2, layer, cos, sin, batch, seq_len):
    ng = seq_len // CMP_STRIDE
    gw = CMP_STRIDE * LANES
    x = kvc.reshape(2, batch, ng, gw)
    hid = NSA_KV_HEADS * CMP_HID
    per_kv = lambda *shape: pl.BlockSpec((1,) + shape, lambda s, b: (s,) + (0,) * len(shape))
    per_lkv = lambda *shape: pl.BlockSpec((None, 1) + shape, lambda s, b: (layer, s) + (0,) * len(shape))
    return pl.pallas_call(
        _compress_kernel,
        grid=(2, batch),
        in_specs=[pl.BlockSpec((1, 1, ng, gw), lambda s, b: (s, b, 0, 0)),
                  per_lkv(1, gw), per_lkv(1, gw), per_lkv(gw, hid), per_lkv(gw, hid), per_lkv(hid, LANES),
                  per_kv(ng, LANES), per_kv(ng, LANES)],
        out_specs=(pl.BlockSpec((1, 1, ng, LANES), lambda s, b: (s, b, 0, 0)),
                   pl.BlockSpec((1, 1, LANES, ng), lambda s, b: (s, b, 0, 0))),
        out_shape=(jax.ShapeDtypeStruct((2, batch, ng, LANES), BF16),
                   jax.ShapeDtypeStruct((2, batch, LANES, ng), BF16)),
        compiler_params=pltpu.CompilerParams(dimension_semantics=("arbitrary", "arbitrary"),
                                             vmem_limit_bytes=V7X_VMEM_LIMIT),
        name="compress",
    )(x, pos_a, pos_b, wa, wb, w2, cos, sin)


def _nsa_kernel(nq_ref, ng_ref, *refs, seq_len, topk):
    n_in = 6
    o_ref = refs[n_in]

    def one_tile(j, carry):
        rows = pl.ds(pl.multiple_of(j * NSA_QBLOCK, NSA_QBLOCK), NSA_QBLOCK)
        t0 = (pl.program_id(1) * NSA_TILES_PER_STEP + j) * NSA_QBLOCK
        _nsa_tile(t0, nq_ref.at[rows], ng_ref.at[rows], *refs[:n_in], o_ref.at[rows], *refs[n_in + 1:],
                  seq_len=seq_len, topk=topk)
        return carry

    lax.fori_loop(0, NSA_TILES_PER_STEP, one_tile, 0)


def _nsa_tile(t0, nq_ref, ng_ref, kc_ref, vc_ref, ksl_ref, vsl_ref, kwn_ref, vwn_ref,
              o_ref, qaug_ref, s_ref, m_ref, acc_ref, out_ref, psum_ref, *, seq_len, topk):
    QB = NSA_QBLOCK
    R = NSA_GROUP
    G = NSA_KV_HEADS
    M = R * QB
    NC = kc_ref.shape[2]
    KT = SLC_KTILE
    WK = WINDOW + QB
    q = nq_ref[...].astype(F32)

    def per_head(x):
        return jnp.concatenate([x] * R, axis=1)

    def normalized(acc, g):
        num, den = (acc[0:NSA_DK], acc[NSA_DK:NSA_DK + 1]) if g == 0 else (acc[NSA_DK:], acc[0:1])
        return num * (1.0 / den)

    low_rows = lax.broadcasted_iota(jnp.int32, (LANES, QB), 0) < NSA_DK
    for i in range(R):
        q_t = q[:, i * LANES:(i + 1) * LANES].T
        for g in range(G):
            own = low_rows if g == 0 else jnp.logical_not(low_rows)
            qaug_ref[g, 0:LANES, i * QB:(i + 1) * QB] = jnp.where(own, q_t, 0.0).astype(BF16)

    ks = pl.multiple_of(jnp.clip(t0 - WINDOW, 0, seq_len - WK), QB)
    k_pos = ks + lax.broadcasted_iota(jnp.int32, (WK, QB), 0)
    t_pos = t0 + lax.broadcasted_iota(jnp.int32, (WK, QB), 1)
    win_mask = per_head((k_pos <= t_pos) & (k_pos > t_pos - WINDOW))

    def slc_scores(k0, slot):
        for g in range(G):
            s_ref[slot, g] = jnp.dot(ksl_ref[0, pl.ds(k0, KT), :], qaug_ref[g],
                                     preferred_element_type=F32)

    def compressed_select_window(nc):
        win_scores = [jnp.dot(kwn_ref[0, pl.ds(ks, WK), :], qaug_ref[g, 0:LANES, :],
                              preferred_element_type=F32) for g in range(G)]
        cmp_scores = [jnp.dot(kc_ref[0, 0, 0:nc, :], qaug_ref[g, 0:LANES, :],
                              preferred_element_type=F32) for g in range(G)]

        nb = min(nc * CMP_STRIDE // SLC_BLOCK, LANES)
        jj = lax.broadcasted_iota(jnp.int32, (nb, QB), 0)
        q_blk = (t0 + lax.broadcasted_iota(jnp.int32, (nb, QB), 1)) // SLC_BLOCK
        valid = jj <= q_blk
        forced = (jj == 0) | (valid & (jj > q_blk - SLC_LOCAL))
        jf = jj.astype(F32)

        def pick_round(score):
            best = jnp.max(score, axis=0, keepdims=True)
            first = jnp.min(jnp.where(score == best, jf, float(LANES)), axis=0, keepdims=True)
            return jnp.where(jf == first, -jnp.inf, score)

        scores = []
        n_idx = lax.broadcasted_iota(jnp.int32, (nc, QB), 0)
        t_col = t0 + lax.broadcasted_iota(jnp.int32, (nc, QB), 1)
        cmp_mask = per_head(n_idx * CMP_STRIDE + (CMP_BLOCK - 1) <= t_col)
        for g in range(G):
            sc = jnp.where(cmp_mask, cmp_scores[g], NEG)
            e = jnp.exp2(sc - jnp.max(sc, axis=0, keepdims=True))
            if nc == LANES:
                e = jnp.where(cmp_mask, e, 0.0)
            p_cmp = e * (1.0 / jnp.maximum(jnp.sum(e, axis=0, keepdims=True), 1e-30))
            acc = jnp.dot(vc_ref[0, 0, :, 0:nc], p_cmp.astype(BF16), preferred_element_type=F32)
            out_ref[0, g * NSA_DK:(g + 1) * NSA_DK, :] = acc[g * NSA_DK:(g + 1) * NSA_DK]
            p_sum = p_cmp[:, 0:QB]
            for i in range(1, R):
                p_sum = p_sum + p_cmp[:, i * QB:(i + 1) * QB]
            psum_ref[g, 0:SUBLANES, :] = jnp.zeros((SUBLANES, QB), F32)
            psum_ref[g, SUBLANES:SUBLANES + nc, :] = p_sum
            ratio, lead = SLC_BLOCK // CMP_STRIDE, CMP_BLOCK // CMP_STRIDE - 1
            imp = psum_ref[g, pl.ds(SUBLANES - lead, nb, stride=ratio), :]
            for d in range(1 - lead, ratio):
                imp = imp + psum_ref[g, pl.ds(SUBLANES + d, nb, stride=ratio), :]
            scores.append(jnp.where(forced, -jnp.inf, jnp.where(valid, imp, -SEL_BIG)))

        for _ in range(topk - (1 + SLC_LOCAL)):
            scores = [pick_round(score) for score in scores]
        for g in range(G):
            selected = ((scores[g] == -jnp.inf) | (q_blk < topk)) & valid
            bias = jnp.where(selected, 0.0, NEG)
            if nb < LANES:
                bias = jnp.concatenate([bias, jnp.full((LANES - nb, QB), NEG, F32)], axis=0)
            qaug_ref[g, LANES:2 * LANES, :] = per_head(bias.astype(BF16))
        slc_scores(0, 0)

        for g in range(G):
            sw = jnp.where(win_mask, win_scores[g], NEG)
            e_win = jnp.exp2(sw - jnp.max(sw, axis=0, keepdims=True)).astype(BF16)
            pv = jnp.dot(vwn_ref[0, g * LANES:(g + 1) * LANES, pl.ds(ks, WK)], e_win,
                         preferred_element_type=F32)
            out_ref[1, g * NSA_DK:(g + 1) * NSA_DK, :] = normalized(pv, g)

    n_visible = (t0 + QB - CMP_BLOCK) // CMP_STRIDE + 1
    n_variants = NC // LANES
    for v in range(n_variants):
        @pl.when(jnp.minimum((n_visible - 1) // LANES, n_variants - 1) == v)
        def _():
            compressed_select_window((v + 1) * LANES)

    def slc_accumulate(k0, slot, causal):
        for g in range(G):
            s = s_ref[slot, g]
            if causal:
                k_pos = k0 + lax.broadcasted_iota(jnp.int32, (KT, QB), 0)
                t_pos = t0 + lax.broadcasted_iota(jnp.int32, (KT, QB), 1)
                s = jnp.where(per_head(k_pos <= t_pos), s, NEG)
            m_run = m_ref[g]
            m_new = jnp.maximum(m_run, jnp.max(s, axis=0, keepdims=True))
            p = jnp.exp2(s - m_new).astype(BF16)
            pv = jnp.dot(vsl_ref[0, g * LANES:(g + 1) * LANES, pl.ds(k0, KT)], p,
                         preferred_element_type=F32)
            acc_ref[g] = jnp.exp2(m_run - m_new) * acc_ref[g] + pv
            m_ref[g] = m_new

    m_ref[...] = jnp.full(m_ref.shape, NEG, F32)
    acc_ref[...] = jnp.zeros_like(acc_ref)
    n_full = t0 // KT
    U = SLC_UNROLL

    def slc_run(k0, n_tiles, last_is_diagonal):
        for u in range(n_tiles):
            causal = last_is_diagonal and u == n_tiles - 1
            if not causal:
                slc_scores(k0 + (u + 1) * KT, (u + 1) % 2)
            slc_accumulate(k0 + u * KT, u % 2, causal)

    def slc_group(j, _):
        slc_run(pl.multiple_of(j * (U * KT), U * KT), U, False)
        return 0

    lax.fori_loop(0, n_full // U, slc_group, 0)
    k_rest = pl.multiple_of((n_full // U) * (U * KT), U * KT)
    for r in range(U):
        @pl.when(n_full % U == r)
        def _():
            slc_run(k_rest, r + 1, True)

    o_slc = [normalized(acc_ref[g], g) for g in range(G)]

    gates = (1.0 / (1.0 + jnp.exp(-ng_ref[...]))).T

    def gate_rows(branch, head):
        r = branch * NSA_HEADS + head
        return jnp.broadcast_to(gates[r:r + 1, :], (NSA_DK, QB))

    branches = [out_ref[0], jnp.concatenate(o_slc, axis=0), out_ref[1]]
    for i in range(R):
        y = jnp.zeros((LANES, QB), F32)
        for b, o in enumerate(branches):
            gate = jnp.concatenate([gate_rows(b, g * R + i) for g in range(G)], axis=0)
            y = y + gate * o[:, i * QB:(i + 1) * QB]
        o_ref[:, i * LANES:(i + 1) * LANES] = y.T.astype(BF16)


def _nsa(nq, ng, kvc_cmp, kvc_cmp_t, ksl, vsl, kwn, vwn, batch, seq_len):
    T = nq.shape[0]
    QB = NSA_QBLOCK
    rows = QB * NSA_TILES_PER_STEP
    nqb = seq_len // rows
    nc = seq_len // CMP_STRIDE
    topk = min(SLC_TOPK, seq_len // SLC_BLOCK)
    tok = lambda n: pl.BlockSpec((rows, n), lambda b, i: (b * nqb + i, 0))
    seq = lambda n: pl.BlockSpec((1, seq_len, n), lambda b, i: (b, 0, 0))
    seq_t = pl.BlockSpec((1, 2 * LANES, seq_len), lambda b, i: (b, 0, 0))
    return pl.pallas_call(
        functools.partial(_nsa_kernel, seq_len=seq_len, topk=topk),
        grid=(batch, nqb),
        in_specs=[tok(D_NSA), tok(LANES),
                  pl.BlockSpec((1, 1, nc, LANES), lambda b, i: (0, b, 0, 0)),
                  pl.BlockSpec((1, 1, LANES, nc), lambda b, i: (1, b, 0, 0)),
                  seq(2 * LANES), seq_t, seq(LANES), seq_t],
        out_specs=tok(D_NSA),
        out_shape=jax.ShapeDtypeStruct((T, D_NSA), BF16),
        scratch_shapes=[pltpu.VMEM((NSA_KV_HEADS, 2 * LANES, NSA_GROUP * QB), BF16),
                        pltpu.VMEM((2, NSA_KV_HEADS, SLC_KTILE, NSA_GROUP * QB), F32),
                        pltpu.VMEM((NSA_KV_HEADS, 1, NSA_GROUP * QB), F32),
                        pltpu.VMEM((NSA_KV_HEADS, LANES, NSA_GROUP * QB), F32),
                        pltpu.VMEM((2, LANES, NSA_GROUP * QB), F32),
                        pltpu.VMEM((NSA_KV_HEADS, nc + SUBLANES, QB), F32)],
        compiler_params=pltpu.CompilerParams(dimension_semantics=("arbitrary", "arbitrary"),
                                             vmem_limit_bytes=V7X_VMEM_LIMIT),
        name="nsa",
    )(nq, ng, kvc_cmp, kvc_cmp_t, ksl.reshape(batch, seq_len, -1), vsl,
      kwn.reshape(batch, seq_len, -1), vwn)


def _ffn_kernel(yr_ref, yrp_ref, yn_ref, ynp_ref, x_ref, xp_ref, wo_ref, gmix_ref, gpre_ref,
                wu_ref, conv_ref, wd_ref, gpost_ref, o_ref,
                h_ref, ug_ref, uv_ref, acc_ref, xmid_ref, *, tiles_per_seq):
    tm = x_ref.shape[0]
    d_ff = wd_ref.shape[0]
    tf = FFN_FTILE
    H = FFN_HALO
    rc = tm // FFN_ROW_CHUNKS

    def rms(x, gain_ref):
        return x * lax.rsqrt(jnp.mean(x * x, axis=-1, keepdims=True) + NORM_EPS) * gain_ref[...]

    keep = jnp.where(pl.program_id(0) % tiles_per_seq == 0, 0.0, 1.0)
    for c in range(FFN_ROW_CHUNKS):
        rows = slice(c * rc, (c + 1) * rc)
        y_ret, y_nsa = yr_ref[rows, :], yn_ref[rows, :]
        if c == 0:
            y_ret = jnp.concatenate([yrp_ref[...], y_ret], axis=0)
            y_nsa = jnp.concatenate([ynp_ref[...], y_nsa], axis=0)
        mix = (jnp.dot(y_ret, wo_ref[0:D_RET, :], preferred_element_type=F32)
               + jnp.dot(y_nsa, wo_ref[D_RET:D_RET + D_NSA, :], preferred_element_type=F32))
        post = rms(mix, gmix_ref)
        if c == 0:
            h_ref[0:H, :] = (rms(xp_ref[...] + post[0:H], gpre_ref) * keep).astype(BF16)
            post = post[H:H + rc]
        x_mid = x_ref[rows, :] + post
        xmid_ref[rows, :] = x_mid
        h_ref[H + c * rc:H + (c + 1) * rc, :] = rms(x_mid, gpre_ref).astype(BF16)
    acc_ref[...] = jnp.zeros_like(acc_ref)

    n_tiles = d_ff // tf

    def columns(f):
        return (pl.ds(pl.multiple_of(f * tf, tf), tf), pl.ds(pl.multiple_of(d_ff + f * tf, tf), tf))

    def up_project(f, c):
        gate_cols, value_cols = columns(f)
        rows = slice(0 if c == 0 else H + c * rc, H + (c + 1) * rc)
        h = h_ref[rows, :]
        ug_ref[rows, :] = jnp.dot(h, wu_ref[:, gate_cols], preferred_element_type=F32)
        uv_ref[rows, :] = jnp.dot(h, wu_ref[:, value_cols], preferred_element_type=F32)

    def hidden_tile(f, carry):
        gate_cols, value_cols = columns(f)

        def causal_conv(u_ref, cols, r0, scale):
            out = (scale * conv_ref[CONV_WIDTH - 1:CONV_WIDTH, cols]) * u_ref[r0:r0 + rc, :]
            for k in range(CONV_WIDTH - 1):
                d = CONV_WIDTH - 1 - k
                out = out + (scale * conv_ref[k:k + 1, cols]) * u_ref[r0 - d:r0 - d + rc, :]
            return out

        def gate_and_down(c):
            g = causal_conv(ug_ref, gate_cols, H + c * rc, 1.0)
            v_half = causal_conv(uv_ref, value_cols, H + c * rc, 0.5)
            inner = g * (GELU_C + (GELU_C * 0.044715) * (g * g))
            act = (g + g * jnp.tanh(inner)) * v_half
            acc_ref[c * rc:(c + 1) * rc, :] += jnp.dot(act.astype(BF16), wd_ref[gate_cols, :],
                                                       preferred_element_type=F32)

        up_project(f, 0)
        for c in range(FFN_ROW_CHUNKS):
            if c + 1 < FFN_ROW_CHUNKS:
                up_project(f, c + 1)
            gate_and_down(c)
        return carry

    lax.fori_loop(0, n_tiles, hidden_tile, 0)
    o_ref[...] = xmid_ref[...] + rms(acc_ref[...], gpost_ref)


def _outproj_ffn(y_ret, y_nsa, x2, w_out, g_mix, g_pre, w_up, conv_w, w_down, layer, g_post, seq_len):
    T, D = x2.shape
    d_ff = w_down.shape[1]
    tm, tf, H = FFN_TILE, FFN_FTILE, FFN_HALO
    tile = lambda n: pl.BlockSpec((tm, n), lambda i: (i, 0))
    halo = lambda n: pl.BlockSpec((H, n), lambda i: (jnp.maximum(i * (tm // H) - 1, 0), 0))
    resident = lambda a: pl.BlockSpec((None,) + a.shape[1:], lambda i: (layer, 0, 0),
                                      pipeline_mode=pl.Buffered(1))
    gain = pl.BlockSpec((1, D), lambda i: (0, 0), pipeline_mode=pl.Buffered(1))
    return pl.pallas_call(
        functools.partial(_ffn_kernel, tiles_per_seq=seq_len // tm),
        grid=(T // tm,),
        in_specs=[tile(D_RET), halo(D_RET), tile(D_NSA), halo(D_NSA), tile(D), halo(D),
                  resident(w_out), gain, gain, resident(w_up), resident(conv_w), resident(w_down), gain],
        out_specs=tile(D),
        out_shape=jax.ShapeDtypeStruct((T, D), F32),
        scratch_shapes=[pltpu.VMEM((tm + H, D), BF16), pltpu.VMEM((tm + H, tf), F32),
                        pltpu.VMEM((tm + H, tf), F32), pltpu.VMEM((tm, D), F32),
                        pltpu.VMEM((tm, D), F32)],
        compiler_params=pltpu.CompilerParams(dimension_semantics=("arbitrary",),
                                             vmem_limit_bytes=V7X_VMEM_LIMIT),
        name="ffn",
    )(y_ret, y_ret, y_nsa, y_nsa, x2, x2, w_out, g_mix, g_pre, w_up, conv_w, w_down, g_post)


_NSA_HEAD_ORDER = [g * NSA_GROUP + i for i in range(NSA_GROUP) for g in range(NSA_KV_HEADS)]


def _rope_tables(pos):
    inv = 1.0 / (ROPE_THETA ** (jnp.arange(0, NSA_DK, 2, dtype=F32) / NSA_DK))
    ang = pos.astype(F32)[:, None] * inv[None, :]
    c, s = jnp.cos(ang), jnp.sin(ang)
    return jnp.concatenate([c, c, c, c], axis=1), jnp.concatenate([-s, s, -s, s], axis=1)


def _prep_w_in(w):
    lead = w.shape[:-1]
    splits = np.cumsum([RET_QK, RET_QK, D_RET, D_RET, D_NSA] + [NSA_KV] * 6)
    rq, rk, rv, rg, nq, kcm, vcm, ksl, vsl, kwn, vwn, ng = jnp.split(w, [int(s) for s in splits], axis=-1)
    nq = nq.reshape(lead + (NSA_HEADS, NSA_DK))[..., np.array(_NSA_HEAD_ORDER), :].reshape(lead + (D_NSA,))
    ng = jnp.pad(ng, [(0, 0)] * len(lead) + [(0, LANES - NSA_GATES)])
    return jnp.concatenate([rq, rk, nq, ksl, kwn, rv, vsl, vwn, rg, kcm, vcm, ng], axis=-1).astype(BF16)


def _prep_w_out(w):
    layers, _, d = w.shape
    w_nsa = w[:, D_RET:].reshape(layers, NSA_HEADS, NSA_DK, d)[:, np.array(_NSA_HEAD_ORDER)]
    return jnp.concatenate([w[:, :D_RET], w_nsa.reshape(layers, D_NSA, d)], axis=1).astype(BF16)


def _prep_compress(pos, w1, w2):
    lead = w1.shape[:-2]
    half = CMP_STRIDE

    def block_diag(w, axis):
        z = jnp.zeros_like(w)
        return jnp.stack([jnp.concatenate([w, z], axis=-1), jnp.concatenate([z, w], axis=-1)], axis=axis)

    def first_layer(w_half):
        w4 = w_half.reshape(lead + (half, NSA_DK, CMP_HID))
        return block_diag(w4, -3).reshape(lead + (half * NSA_KV_HEADS * NSA_DK, -1)).astype(BF16)

    def pos_row(p_half):
        rows = jnp.broadcast_to(p_half[..., :, None, :], lead + (half, NSA_KV_HEADS, NSA_DK))
        return rows.reshape(lead + (1, -1))

    w2x = block_diag(w2, -3).reshape(lead + (NSA_KV_HEADS * CMP_HID, -1)).astype(BF16)
    n1 = half * NSA_DK
    return (pos_row(pos[..., :half, :]), pos_row(pos[..., half:, :]),
            first_layer(w1[..., :n1, :]), first_layer(w1[..., n1:, :]), w2x)


def kernel(x, norm_mix_pre, w_in, ret_gn_w, cmp_k_pos, cmp_k_w1, cmp_k_w2, cmp_v_pos, cmp_v_w1, cmp_v_w2,
           w_out, norm_mix_post, norm_ffn_pre, ffn_w_up, ffn_conv, ffn_w_down, norm_ffn_post):
    B, S, D = x.shape
    depth = w_in.shape[0]
    assert S % SLC_KTILE == 0 and S % FFN_TILE == 0 and S // SLC_BLOCK <= LANES and S >= WINDOW + NSA_QBLOCK
    assert ffn_w_down.shape[1] % FFN_FTILE == 0
    assert min(SLC_TOPK, S // SLC_BLOCK) > 1 + SLC_LOCAL and S % (NSA_QBLOCK * NSA_TILES_PER_STEP) == 0

    cos, sin = _rope_tables(jnp.arange(S, dtype=jnp.int32))
    nc = S // CMP_STRIDE
    ccos, csin = _rope_tables(jnp.arange(nc, dtype=jnp.int32) * CMP_STRIDE + (CMP_BLOCK - 1))
    cmp_cos = jnp.stack([ccos, jnp.ones_like(ccos)])
    cmp_sin = jnp.stack([csin, jnp.zeros_like(csin)])

    w_in_p = _prep_w_in(w_in)
    w_out_p = _prep_w_out(w_out)
    w_up_p = ffn_w_up.astype(BF16)
    w_down_p = ffn_w_down.astype(BF16)
    cmp_p = _prep_compress(jnp.stack([cmp_k_pos, cmp_v_pos], axis=1), jnp.stack([cmp_k_w1, cmp_v_w1], axis=1),
                           jnp.stack([cmp_k_w2, cmp_v_w2], axis=1))

    x2 = x.reshape(B * S, D)
    for l in range(depth):
        outs = _inproj(x2, norm_mix_pre[l][None], w_in_p, l, cos, sin, S)
        rq, rk, nq, ksl, kwn, rv, vsl, vwn, rg, kvc, ng = outs
        y_ret = _retention(rq, rk, rv, rg, ret_gn_w[l][None], B, S)
        kvc_cmp, kvc_cmp_t = _compress(kvc, *cmp_p, l, cmp_cos, cmp_sin, B, S)
        y_nsa = _nsa(nq, ng, kvc_cmp, kvc_cmp_t, ksl, vsl, kwn, vwn, B, S)
        x2 = _outproj_ffn(y_ret, y_nsa, x2, w_out_p, norm_mix_post[l][None], norm_ffn_pre[l][None],
                          w_up_p, ffn_conv, w_down_p, l, norm_ffn_post[l][None], S)
    return x2.reshape(B, S, D)
```

```python
import functools
import math

import jax
import jax.numpy as jnp
import numpy as np
from jax import lax
from jax.experimental import pallas as pl
from jax.experimental.pallas import tpu as pltpu

F32 = jnp.float32
BF16 = jnp.bfloat16

LANES = 128
SUBLANES = 8
V7X_VMEM_LIMIT = 56 * 1024 * 1024

ROPE_THETA = 10000.0
NORM_EPS = 1e-6
GN_EPS = 1e-5
NEG = -1e30
SEL_BIG = 1e9

RET_HEADS = 4
RET_DK = 64
RET_DV = 128
RET_CHUNK = 128
NSA_HEADS = 8
NSA_KV_HEADS = 2
NSA_DK = 64
NSA_GROUP = NSA_HEADS // NSA_KV_HEADS
CMP_BLOCK = 32
CMP_STRIDE = 16
CMP_HID = 256
SLC_BLOCK = 64
SLC_TOPK = 16
SLC_LOCAL = 2
WINDOW = 512
NSA_QBLOCK = 128
NSA_TILES_PER_STEP = 4
NSA_GATES = 3 * NSA_HEADS
CONV_WIDTH = 3

NSA_Q_SCALE = NSA_DK ** -0.5 * math.log2(math.e)

D_RET = RET_HEADS * RET_DV
D_NSA = NSA_HEADS * NSA_DK
RET_QK = RET_HEADS * RET_DK
NSA_KV = NSA_KV_HEADS * NSA_DK

COL_RQ = 0
COL_RK = COL_RQ + RET_QK
COL_NQ = COL_RK + RET_QK
COL_KSL = COL_NQ + D_NSA
COL_KWN = COL_KSL + NSA_KV
ROPE_COLS = COL_KWN + NSA_KV
COL_RV = ROPE_COLS
COL_VSL = COL_RV + D_RET
COL_VWN = COL_VSL + NSA_KV
COL_RG = COL_VWN + NSA_KV
COL_KCM = COL_RG + D_RET
COL_VCM = COL_KCM + NSA_KV
COL_NG = COL_VCM + NSA_KV
IN_COLS_PAD = COL_NG + LANES

TOK_TILE = 512
RET_TILE = 1024
FFN_TILE = 1024
FFN_FTILE = 256
FFN_HALO = 16
FFN_ROW_CHUNKS = 4
SLC_KTILE = 512
SLC_UNROLL = 4


GELU_C = math.sqrt(2.0 / math.pi)


def _gelu_tanh(x):
    return 0.5 * x * (1.0 + jnp.tanh(GELU_C * (x + 0.044715 * (x * x * x))))


def _rope(p, cos, sin_signed, first_half):
    half = NSA_DK // 2
    partner = jnp.where(first_half, pltpu.roll(p, LANES - half, 1), pltpu.roll(p, half, 1))
    return p * cos + partner * sin_signed


def _inproj_kernel(x_ref, g_ref, w_ref, cos_ref, sin_ref,
                   rq_ref, rk_ref, nq_ref, ksl_ref, kwn_ref, rv_ref, vsl_ref, vwn_ref,
                   rg_ref, kvc_ref, ng_ref, stage_ref, *, seq_len):
    tm = x_ref.shape[0]
    x = x_ref[...]
    h = (x * lax.rsqrt(jnp.mean(x * x, axis=-1, keepdims=True) + NORM_EPS) * g_ref[...]).astype(BF16)
    cos = cos_ref[...]
    sin = sin_ref[...]
    lane = lax.broadcasted_iota(jnp.int32, (tm, LANES), 1)
    first_half = (lane % NSA_DK) < NSA_DK // 2

    def proj(c0, n):
        return jnp.dot(h, w_ref[:, c0:c0 + n], preferred_element_type=F32)

    def rope_slab(p, i):
        return _rope(p[:, i * LANES:(i + 1) * LANES], cos, sin, first_half)

    def store_rq(p):
        for i in range(RET_QK // LANES):
            rq_ref[:, i * LANES:(i + 1) * LANES] = rope_slab(p, i).astype(BF16)

    def store_rk(p):
        for i in range(RET_QK // LANES):
            rk_ref[:, i * LANES:(i + 1) * LANES] = (rope_slab(p, i) * (RET_DK ** -0.5)).astype(BF16)

    def store_nq(p):
        for i in range(D_NSA // LANES):
            nq_ref[:, i * LANES:(i + 1) * LANES] = (rope_slab(p, i) * NSA_Q_SCALE).astype(BF16)

    def store_keys(p):
        ksl_ref[:, 0:LANES] = rope_slab(p, 0).astype(BF16)
        kwn_ref[...] = rope_slab(p, 1).astype(BF16)
        row = lax.broadcasted_iota(jnp.int32, (tm, LANES), 0)
        pos = (pl.program_id(0) * tm + row) % seq_len
        ksl_ref[:, LANES:2 * LANES] = jnp.where(lane == pos // SLC_BLOCK, 1.0, 0.0).astype(BF16)

    def store_rv(p):
        rv_ref[...] = p.astype(BF16)

    def store_values(p):
        low_half = lane < NSA_DK
        for v_ref, v in ((vsl_ref, p[:, 0:LANES]), (vwn_ref, p[:, LANES:2 * LANES])):
            v_ref[0, 0:LANES, :] = jnp.where(low_half, v, 1.0).T.astype(BF16)
            v_ref[0, LANES:2 * LANES, :] = jnp.where(low_half, 1.0, v).T.astype(BF16)

    def store_rg(p):
        rg_ref[...] = p

    def store_compress_inputs(p):
        for s in range(2):
            stage_ref[...] = p[:, s * LANES:(s + 1) * LANES]
            for l in range(CMP_STRIDE):
                kvc_ref[s, :, l * LANES:(l + 1) * LANES] = stage_ref[
                    pl.ds(l, tm // CMP_STRIDE, stride=CMP_STRIDE), :]

    def store_gates(p):
        ng_ref[...] = p

    stages = [(COL_RQ, RET_QK, store_rq), (COL_RK, RET_QK, store_rk), (COL_NQ, D_NSA, store_nq),
              (COL_KSL, 2 * NSA_KV, store_keys), (COL_RV, D_RET, store_rv),
              (COL_VSL, 2 * NSA_KV, store_values), (COL_RG, D_RET, store_rg),
              (COL_KCM, 2 * NSA_KV, store_compress_inputs), (COL_NG, LANES, store_gates)]
    pending = None
    for c0, n, store in stages:
        p = proj(c0, n)
        if pending is not None:
            pending[1](pending[0])
        pending = (p, store)
    pending[1](pending[0])


def _inproj(x2, gain, w, layer, cos, sin, seq_len):
    T, D = x2.shape
    tm = TOK_TILE
    nt = seq_len // tm
    tok = lambda n: pl.BlockSpec((tm, n), lambda i: (i, 0))
    tok_t = pl.BlockSpec((1, 2 * LANES, tm), lambda i: (i // nt, 0, i % nt))
    out_shape = (
        jax.ShapeDtypeStruct((T, RET_QK), BF16),
        jax.ShapeDtypeStruct((T, RET_QK), BF16),
        jax.ShapeDtypeStruct((T, D_NSA), BF16),
        jax.ShapeDtypeStruct((T, 2 * LANES), BF16),
        jax.ShapeDtypeStruct((T, LANES), BF16),
        jax.ShapeDtypeStruct((T, D_RET), BF16),
        jax.ShapeDtypeStruct((T // seq_len, 2 * LANES, seq_len), BF16),
        jax.ShapeDtypeStruct((T // seq_len, 2 * LANES, seq_len), BF16),
        jax.ShapeDtypeStruct((T, D_RET), F32),
        jax.ShapeDtypeStruct((2, T // CMP_STRIDE, CMP_STRIDE * LANES), F32),
        jax.ShapeDtypeStruct((T, LANES), F32),
    )
    out_specs = (tok(RET_QK), tok(RET_QK), tok(D_NSA), tok(2 * LANES), tok(LANES), tok(D_RET),
                 tok_t, tok_t, tok(D_RET),
                 pl.BlockSpec((2, tm // CMP_STRIDE, CMP_STRIDE * LANES), lambda i: (0, i, 0)), tok(LANES))
    return pl.pallas_call(
        functools.partial(_inproj_kernel, seq_len=seq_len),
        grid=(T // tm,),
        in_specs=[tok(D),
                  pl.BlockSpec((1, D), lambda i: (0, 0)),
                  pl.BlockSpec((None, D, IN_COLS_PAD), lambda i: (layer, 0, 0)),
                  pl.BlockSpec((tm, LANES), lambda i: (i % nt, 0)),
                  pl.BlockSpec((tm, LANES), lambda i: (i % nt, 0))],
        out_specs=out_specs,
        out_shape=out_shape,
        scratch_shapes=[pltpu.VMEM((tm, LANES), F32)],
        compiler_params=pltpu.CompilerParams(dimension_semantics=("arbitrary",),
                                             vmem_limit_bytes=V7X_VMEM_LIMIT),
        name="inproj",
    )(x2, gain, w, cos, sin)


def _retention_kernel(q_ref, k_ref, v_ref, g_ref, gnw_ref, o_ref, state_ref, vbd_ref):
    C = RET_CHUNK
    n_chunks = q_ref.shape[0] // C

    @pl.when(pl.program_id(1) == 0)
    def _():
        state_ref[...] = jnp.zeros_like(state_ref)

    vbd_ref[...] = jnp.zeros_like(vbd_ref)

    ii = lax.broadcasted_iota(jnp.int32, (C, C), 0)
    jj = lax.broadcasted_iota(jnp.int32, (C, C), 1)
    diff = (ii - jj).astype(F32)
    i_col = lax.broadcasted_iota(jnp.int32, (C, 1), 0).astype(F32)
    low_half = lax.broadcasted_iota(jnp.int32, (C, LANES), 1) < RET_DK
    low_cols = lax.broadcasted_iota(jnp.int32, (C, 2 * RET_DV), 1) < RET_DV
    own_block = ((lax.broadcasted_iota(jnp.int32, (LANES, 2 * RET_DV), 0) < RET_DK)
                 == (lax.broadcasted_iota(jnp.int32, (LANES, 2 * RET_DV), 1) < RET_DV))
    nt_dims = (((1,), (1,)), ((), ()))
    tn_dims = (((0,), (0,)), ((), ()))

    for pair in range(RET_HEADS // 2):
        lg0, lg1 = (math.log(1.0 - 2.0 ** (-5.0 - h)) for h in (2 * pair, 2 * pair + 1))
        decay = [jnp.where(diff >= 0, jnp.exp(lg * jnp.maximum(diff, 0.0)), 0.0) for lg in (lg0, lg1)]
        xi = jnp.where(low_cols, jnp.exp(lg0 * (i_col + 1.0)), jnp.exp(lg1 * (i_col + 1.0)))
        zeta = jnp.where(low_half, jnp.exp(lg0 * (C - 1.0 - i_col)), jnp.exp(lg1 * (C - 1.0 - i_col)))
        chunk_decay = jnp.where(low_cols[0:1], math.exp(lg0 * C), math.exp(lg1 * C))
        qk_cols = slice(pair * LANES, (pair + 1) * LANES)
        v_cols = slice(2 * pair * RET_DV, 2 * (pair + 1) * RET_DV)
        for c in range(n_chunks):
            rows = slice(c * C, (c + 1) * C)
            q = q_ref[rows, qk_cols]
            ks = k_ref[rows, qk_cols]
            v = v_ref[rows, v_cols]
            qf = q.astype(F32)
            q_stack = jnp.concatenate([jnp.where(low_half, qf, 0.0), jnp.where(low_half, 0.0, qf)],
                                      axis=0).astype(BF16)
            s = lax.dot_general(q_stack, ks, nt_dims, preferred_element_type=F32)
            s_pair = jnp.concatenate([s[0:C] * decay[0], s[C:2 * C] * decay[1]], axis=1).astype(BF16)
            vbd_ref[pair, c, 0:C, 0:RET_DV] = v[:, 0:RET_DV]
            vbd_ref[pair, c, C:2 * C, RET_DV:2 * RET_DV] = v[:, RET_DV:2 * RET_DV]
            o = jnp.dot(s_pair, vbd_ref[pair, c], preferred_element_type=F32)
            state = state_ref[pair]
            o = o + jnp.dot(q, state.astype(BF16), preferred_element_type=F32) * xi
            kz = (ks.astype(F32) * zeta).astype(BF16)
            kv = lax.dot_general(kz, v, tn_dims, preferred_element_type=F32)
            state_ref[pair] = state * chunk_decay + jnp.where(own_block, kv, 0.0)
            for e in range(2):
                cols = slice((2 * pair + e) * RET_DV, (2 * pair + e + 1) * RET_DV)
                oh = o[:, e * RET_DV:(e + 1) * RET_DV]
                mu = jnp.mean(oh, axis=-1, keepdims=True)
                var = jnp.mean(jnp.square(oh - mu), axis=-1, keepdims=True)
                on = (oh - mu) * lax.rsqrt(var + GN_EPS) * gnw_ref[:, cols]
                gate = g_ref[rows, cols]
                o_ref[rows, cols] = (gate * (1.0 / (1.0 + jnp.exp(-gate))) * on).astype(BF16)


def _retention(rq, rk, rv, rg, gn_w, batch, seq_len):
    T = rq.shape[0]
    tc = RET_TILE
    nt = seq_len // tc
    tok = lambda n: pl.BlockSpec((tc, n), lambda b, i: (b * nt + i, 0))
    return pl.pallas_call(
        _retention_kernel,
        grid=(batch, nt),
        in_specs=[tok(RET_QK), tok(RET_QK), tok(D_RET), tok(D_RET),
                  pl.BlockSpec((1, D_RET), lambda b, i: (0, 0))],
        out_specs=tok(D_RET),
        out_shape=jax.ShapeDtypeStruct((T, D_RET), BF16),
        scratch_shapes=[pltpu.VMEM((RET_HEADS // 2, LANES, 2 * RET_DV), F32),
                        pltpu.VMEM((RET_HEADS // 2, tc // RET_CHUNK, 2 * RET_CHUNK, 2 * RET_DV), BF16)],
        compiler_params=pltpu.CompilerParams(dimension_semantics=("arbitrary", "arbitrary"),
                                             vmem_limit_bytes=V7X_VMEM_LIMIT),
        name="retention",
    )(rq, rk, rv, rg, gn_w)


def _compress_kernel(x_ref, pa_ref, pb_ref, wa_ref, wb_ref, w2_ref, cos_ref, sin_ref, o_ref, ot_ref):
    ng = x_ref.shape[2]
    x = x_ref[0, 0]
    xa = (x + pa_ref[0]).astype(BF16)
    xb = (x + pb_ref[0]).astype(BF16)
    a = jnp.dot(xa, wa_ref[0], preferred_element_type=F32)
    b = jnp.dot(xb, wb_ref[0], preferred_element_type=F32)
    hid = a + pltpu.roll(b, ng - 1, 0)
    out = jnp.dot(_gelu_tanh(hid).astype(BF16), w2_ref[0], preferred_element_type=F32)
    lane = lax.broadcasted_iota(jnp.int32, out.shape, 1)
    out = _rope(out, cos_ref[0], sin_ref[0], (lane % NSA_DK) < NSA_DK // 2)
    o_ref[0, 0] = out.astype(BF16)
    ot_ref[0, 0] = out.T.astype(BF16)


def _compress(kvc, pos_a, pos_b, wa, wb, w2, layer, cos, sin, batch, seq_len):
    ng = seq_len // CMP_STRIDE
    gw = CMP_STRIDE * LANES
    x = kvc.reshape(2, batch, ng, gw)
    hid = NSA_KV_HEADS * CMP_HID
    per_kv = lambda *shape: pl.BlockSpec((1,) + shape, lambda s, b: (s,) + (0,) * len(shape))
    per_lkv = lambda *shape: pl.BlockSpec((None, 1) + shape, lambda s, b: (layer, s) + (0,) * len(shape))
    return pl.pallas_call(
        _compress_kernel,
        grid=(2, batch),
        in_specs=[pl.BlockSpec((1, 1, ng, gw), lambda s, b: (s, b, 0, 0)),
                  per_lkv(1, gw), per_lkv(1, gw), per_lkv(gw, hid), per_lkv(gw, hid), per_lkv(hid, LANES),
                  per_kv(ng, LANES), per_kv(ng, LANES)],
        out_specs=(pl.BlockSpec((1, 1, ng, LANES), lambda s, b: (s, b, 0, 0)),
                   pl.BlockSpec((1, 1, LANES, ng), lambda s, b: (s, b, 0, 0))),
        out_shape=(jax.ShapeDtypeStruct((2, batch, ng, LANES), BF16),
                   jax.ShapeDtypeStruct((2, batch, LANES, ng), BF16)),
        compiler_params=pltpu.CompilerParams(dimension_semantics=("arbitrary", "arbitrary"),
                                             vmem_limit_bytes=V7X_VMEM_LIMIT),
        name="compress",
    )(x, pos_a, pos_b, wa, wb, w2, cos, sin)


def _nsa_kernel(nq_ref, ng_ref, *refs, seq_len, topk):
    n_in = 6
    o_ref = refs[n_in]

    def one_tile(j, carry):
        rows = pl.ds(pl.multiple_of(j * NSA_QBLOCK, NSA_QBLOCK), NSA_QBLOCK)
        t0 = (pl.program_id(1) * NSA_TILES_PER_STEP + j) * NSA_QBLOCK
        _nsa_tile(t0, nq_ref.at[rows], ng_ref.at[rows], *refs[:n_in], o_ref.at[rows], *refs[n_in + 1:],
                  seq_len=seq_len, topk=topk)
        return carry

    lax.fori_loop(0, NSA_TILES_PER_STEP, one_tile, 0)


def _nsa_tile(t0, nq_ref, ng_ref, kc_ref, vc_ref, ksl_ref, vsl_ref, kwn_ref, vwn_ref,
              o_ref, qaug_ref, s_ref, m_ref, acc_ref, out_ref, psum_ref, *, seq_len, topk):
    QB = NSA_QBLOCK
    R = NSA_GROUP
    G = NSA_KV_HEADS
    M = R * QB
    NC = kc_ref.shape[2]
    KT = SLC_KTILE
    WK = WINDOW + QB
    q = nq_ref[...].astype(F32)

    def per_head(x):
        return jnp.concatenate([x] * R, axis=1)

    def normalized(acc, g):
        num, den = (acc[0:NSA_DK], acc[NSA_DK:NSA_DK + 1]) if g == 0 else (acc[NSA_DK:], acc[0:1])
        return num * (1.0 / den)

    low_rows = lax.broadcasted_iota(jnp.int32, (LANES, QB), 0) < NSA_DK
    for i in range(R):
        q_t = q[:, i * LANES:(i + 1) * LANES].T
        for g in range(G):
            own = low_rows if g == 0 else jnp.logical_not(low_rows)
            qaug_ref[g, 0:LANES, i * QB:(i + 1) * QB] = jnp.where(own, q_t, 0.0).astype(BF16)

    ks = pl.multiple_of(jnp.clip(t0 - WINDOW, 0, seq_len - WK), QB)
    k_pos = ks + lax.broadcasted_iota(jnp.int32, (WK, QB), 0)
    t_pos = t0 + lax.broadcasted_iota(jnp.int32, (WK, QB), 1)
    win_mask = per_head((k_pos <= t_pos) & (k_pos > t_pos - WINDOW))

    def slc_scores(k0, slot):
        for g in range(G):
            s_ref[slot, g] = jnp.dot(ksl_ref[0, pl.ds(k0, KT), :], qaug_ref[g],
                                     preferred_element_type=F32)

    def compressed_select_window(nc):
        win_scores = [jnp.dot(kwn_ref[0, pl.ds(ks, WK), :], qaug_ref[g, 0:LANES, :],
                              preferred_element_type=F32) for g in range(G)]
        cmp_scores = [jnp.dot(kc_ref[0, 0, 0:nc, :], qaug_ref[g, 0:LANES, :],
                              preferred_element_type=F32) for g in range(G)]

        nb = min(nc * CMP_STRIDE // SLC_BLOCK, LANES)
        jj = lax.broadcasted_iota(jnp.int32, (nb, QB), 0)
        q_blk = (t0 + lax.broadcasted_iota(jnp.int32, (nb, QB), 1)) // SLC_BLOCK
        valid = jj <= q_blk
        forced = (jj == 0) | (valid & (jj > q_blk - SLC_LOCAL))
        jf = jj.astype(F32)

        def pick_round(score):
            best = jnp.max(score, axis=0, keepdims=True)
            first = jnp.min(jnp.where(score == best, jf, float(LANES)), axis=0, keepdims=True)
            return jnp.where(jf == first, -jnp.inf, score)

        scores = []
        n_idx = lax.broadcasted_iota(jnp.int32, (nc, QB), 0)
        t_col = t0 + lax.broadcasted_iota(jnp.int32, (nc, QB), 1)
        cmp_mask = per_head(n_idx * CMP_STRIDE + (CMP_BLOCK - 1) <= t_col)
        for g in range(G):
            sc = jnp.where(cmp_mask, cmp_scores[g], NEG)
            e = jnp.exp2(sc - jnp.max(sc, axis=0, keepdims=True))
            if nc == LANES:
                e = jnp.where(cmp_mask, e, 0.0)
            p_cmp = e * (1.0 / jnp.maximum(jnp.sum(e, axis=0, keepdims=True), 1e-30))
            acc = jnp.dot(vc_ref[0, 0, :, 0:nc], p_cmp.astype(BF16), preferred_element_type=F32)
            out_ref[0, g * NSA_DK:(g + 1) * NSA_DK, :] = acc[g * NSA_DK:(g + 1) * NSA_DK]
            p_sum = p_cmp[:, 0:QB]
            for i in range(1, R):
                p_sum = p_sum + p_cmp[:, i * QB:(i + 1) * QB]
            psum_ref[g, 0:SUBLANES, :] = jnp.zeros((SUBLANES, QB), F32)
            psum_ref[g, SUBLANES:SUBLANES + nc, :] = p_sum
            ratio, lead = SLC_BLOCK // CMP_STRIDE, CMP_BLOCK // CMP_STRIDE - 1
            imp = psum_ref[g, pl.ds(SUBLANES - lead, nb, stride=ratio), :]
            for d in range(1 - lead, ratio):
                imp = imp + psum_ref[g, pl.ds(SUBLANES + d, nb, stride=ratio), :]
            scores.append(jnp.where(forced, -jnp.inf, jnp.where(valid, imp, -SEL_BIG)))

        for _ in range(topk - (1 + SLC_LOCAL)):
            scores = [pick_round(score) for score in scores]
        for g in range(G):
            selected = ((scores[g] == -jnp.inf) | (q_blk < topk)) & valid
            bias = jnp.where(selected, 0.0, NEG)
            if nb < LANES:
                bias = jnp.concatenate([bias, jnp.full((LANES - nb, QB), NEG, F32)], axis=0)
            qaug_ref[g, LANES:2 * LANES, :] = per_head(bias.astype(BF16))
        slc_scores(0, 0)

        for g in range(G):
            sw = jnp.where(win_mask, win_scores[g], NEG)
            e_win = jnp.exp2(sw - jnp.max(sw, axis=0, keepdims=True)).astype(BF16)
            pv = jnp.dot(vwn_ref[0, g * LANES:(g + 1) * LANES, pl.ds(ks, WK)], e_win,
                         preferred_element_type=F32)
            out_ref[1, g * NSA_DK:(g + 1) * NSA_DK, :] = normalized(pv, g)

    n_visible = (t0 + QB - CMP_BLOCK) // CMP_STRIDE + 1
    n_variants = NC // LANES
    for v in range(n_variants):
        @pl.when(jnp.minimum((n_visible - 1) // LANES, n_variants - 1) == v)
        def _():
            compressed_select_window((v + 1) * LANES)

    def slc_accumulate(k0, slot, causal):
        for g in range(G):
            s = s_ref[slot, g]
            if causal:
                k_pos = k0 + lax.broadcasted_iota(jnp.int32, (KT, QB), 0)
                t_pos = t0 + lax.broadcasted_iota(jnp.int32, (KT, QB), 1)
                s = jnp.where(per_head(k_pos <= t_pos), s, NEG)
            m_run = m_ref[g]
            m_new = jnp.maximum(m_run, jnp.max(s, axis=0, keepdims=True))
            p = jnp.exp2(s - m_new).astype(BF16)
            pv = jnp.dot(vsl_ref[0, g * LANES:(g + 1) * LANES, pl.ds(k0, KT)], p,
                         preferred_element_type=F32)
            acc_ref[g] = jnp.exp2(m_run - m_new) * acc_ref[g] + pv
            m_ref[g] = m_new

    m_ref[...] = jnp.full(m_ref.shape, NEG, F32)
    acc_ref[...] = jnp.zeros_like(acc_ref)
    n_full = t0 // KT
    U = SLC_UNROLL

    def slc_run(k0, n_tiles, last_is_diagonal):
        for u in range(n_tiles):
            causal = last_is_diagonal and u == n_tiles - 1
            if not causal:
                slc_scores(k0 + (u + 1) * KT, (u + 1) % 2)
            slc_accumulate(k0 + u * KT, u % 2, causal)

    def slc_group(j, _):
        slc_run(pl.multiple_of(j * (U * KT), U * KT), U, False)
        return 0

    lax.fori_loop(0, n_full // U, slc_group, 0)
    k_rest = pl.multiple_of((n_full // U) * (U * KT), U * KT)
    for r in range(U):
        @pl.when(n_full % U == r)
        def _():
            slc_run(k_rest, r + 1, True)

    o_slc = [normalized(acc_ref[g], g) for g in range(G)]

    gates = (1.0 / (1.0 + jnp.exp(-ng_ref[...]))).T

    def gate_rows(branch, head):
        r = branch * NSA_HEADS + head
        return jnp.broadcast_to(gates[r:r + 1, :], (NSA_DK, QB))

    branches = [out_ref[0], jnp.concatenate(o_slc, axis=0), out_ref[1]]
    for i in range(R):
        y = jnp.zeros((LANES, QB), F32)
        for b, o in enumerate(branches):
            gate = jnp.concatenate([gate_rows(b, g * R + i) for g in range(G)], axis=0)
            y = y + gate * o[:, i * QB:(i + 1) * QB]
        o_ref[:, i * LANES:(i + 1) * LANES] = y.T.astype(BF16)


def _nsa(nq, ng, kvc_cmp, kvc_cmp_t, ksl, vsl, kwn, vwn, batch, seq_len):
    T = nq.shape[0]
    QB = NSA_QBLOCK
    rows = QB * NSA_TILES_PER_STEP
    nqb = seq_len // rows
    nc = seq_len // CMP_STRIDE
    topk = min(SLC_TOPK, seq_len // SLC_BLOCK)
    tok = lambda n: pl.BlockSpec((rows, n), lambda b, i: (b * nqb + i, 0))
    seq = lambda n: pl.BlockSpec((1, seq_len, n), lambda b, i: (b, 0, 0))
    seq_t = pl.BlockSpec((1, 2 * LANES, seq_len), lambda b, i: (b, 0, 0))
    return pl.pallas_call(
        functools.partial(_nsa_kernel, seq_len=seq_len, topk=topk),
        grid=(batch, nqb),
        in_specs=[tok(D_NSA), tok(LANES),
                  pl.BlockSpec((1, 1, nc, LANES), lambda b, i: (0, b, 0, 0)),
                  pl.BlockSpec((1, 1, LANES, nc), lambda b, i: (1, b, 0, 0)),
                  seq(2 * LANES), seq_t, seq(LANES), seq_t],
        out_specs=tok(D_NSA),
        out_shape=jax.ShapeDtypeStruct((T, D_NSA), BF16),
        scratch_shapes=[pltpu.VMEM((NSA_KV_HEADS, 2 * LANES, NSA_GROUP * QB), BF16),
                        pltpu.VMEM((2, NSA_KV_HEADS, SLC_KTILE, NSA_GROUP * QB), F32),
                        pltpu.VMEM((NSA_KV_HEADS, 1, NSA_GROUP * QB), F32),
                        pltpu.VMEM((NSA_KV_HEADS, LANES, NSA_GROUP * QB), F32),
                        pltpu.VMEM((2, LANES, NSA_GROUP * QB), F32),
                        pltpu.VMEM((NSA_KV_HEADS, nc + SUBLANES, QB), F32)],
        compiler_params=pltpu.CompilerParams(dimension_semantics=("arbitrary", "arbitrary"),
                                             vmem_limit_bytes=V7X_VMEM_LIMIT),
        name="nsa",
    )(nq, ng, kvc_cmp, kvc_cmp_t, ksl.reshape(batch, seq_len, -1), vsl,
      kwn.reshape(batch, seq_len, -1), vwn)


def _ffn_kernel(yr_ref, yrp_ref, yn_ref, ynp_ref, x_ref, xp_ref, wo_ref, gmix_ref, gpre_ref,
                wu_ref, conv_ref, wd_ref, gpost_ref, o_ref,
                h_ref, ug_ref, uv_ref, acc_ref, xmid_ref, *, tiles_per_seq):
    tm = x_ref.shape[0]
    d_ff = wd_ref.shape[0]
    tf = FFN_FTILE
    H = FFN_HALO
    rc = tm // FFN_ROW_CHUNKS

    def rms(x, gain_ref):
        return x * lax.rsqrt(jnp.mean(x * x, axis=-1, keepdims=True) + NORM_EPS) * gain_ref[...]

    keep = jnp.where(pl.program_id(0) % tiles_per_seq == 0, 0.0, 1.0)
    for c in range(FFN_ROW_CHUNKS):
        rows = slice(c * rc, (c + 1) * rc)
        y_ret, y_nsa = yr_ref[rows, :], yn_ref[rows, :]
        if c == 0:
            y_ret = jnp.concatenate([yrp_ref[...], y_ret], axis=0)
            y_nsa = jnp.concatenate([ynp_ref[...], y_nsa], axis=0)
        mix = (jnp.dot(y_ret, wo_ref[0:D_RET, :], preferred_element_type=F32)
               + jnp.dot(y_nsa, wo_ref[D_RET:D_RET + D_NSA, :], preferred_element_type=F32))
        post = rms(mix, gmix_ref)
        if c == 0:
            h_ref[0:H, :] = (rms(xp_ref[...] + post[0:H], gpre_ref) * keep).astype(BF16)
            post = post[H:H + rc]
        x_mid = x_ref[rows, :] + post
        xmid_ref[rows, :] = x_mid
        h_ref[H + c * rc:H + (c + 1) * rc, :] = rms(x_mid, gpre_ref).astype(BF16)
    acc_ref[...] = jnp.zeros_like(acc_ref)

    n_tiles = d_ff // tf

    def columns(f):
        return (pl.ds(pl.multiple_of(f * tf, tf), tf), pl.ds(pl.multiple_of(d_ff + f * tf, tf), tf))

    def up_project(f, c):
        gate_cols, value_cols = columns(f)
        rows = slice(0 if c == 0 else H + c * rc, H + (c + 1) * rc)
        h = h_ref[rows, :]
        ug_ref[rows, :] = jnp.dot(h, wu_ref[:, gate_cols], preferred_element_type=F32)
        uv_ref[rows, :] = jnp.dot(h, wu_ref[:, value_cols], preferred_element_type=F32)

    def hidden_tile(f, carry):
        gate_cols, value_cols = columns(f)

        def causal_conv(u_ref, cols, r0, scale):
            out = (scale * conv_ref[CONV_WIDTH - 1:CONV_WIDTH, cols]) * u_ref[r0:r0 + rc, :]
            for k in range(CONV_WIDTH - 1):
                d = CONV_WIDTH - 1 - k
                out = out + (scale * conv_ref[k:k + 1, cols]) * u_ref[r0 - d:r0 - d + rc, :]
            return out

        def gate_and_down(c):
            g = causal_conv(ug_ref, gate_cols, H + c * rc, 1.0)
            v_half = causal_conv(uv_ref, value_cols, H + c * rc, 0.5)
            inner = g * (GELU_C + (GELU_C * 0.044715) * (g * g))
            act = (g + g * jnp.tanh(inner)) * v_half
            acc_ref[c * rc:(c + 1) * rc, :] += jnp.dot(act.astype(BF16), wd_ref[gate_cols, :],
                                                       preferred_element_type=F32)

        up_project(f, 0)
        for c in range(FFN_ROW_CHUNKS):
            if c + 1 < FFN_ROW_CHUNKS:
                up_project(f, c + 1)
            gate_and_down(c)
        return carry

    lax.fori_loop(0, n_tiles, hidden_tile, 0)
    o_ref[...] = xmid_ref[...] + rms(acc_ref[...], gpost_ref)


def _outproj_ffn(y_ret, y_nsa, x2, w_out, g_mix, g_pre, w_up, conv_w, w_down, layer, g_post, seq_len):
    T, D = x2.shape
    d_ff = w_down.shape[1]
    tm, tf, H = FFN_TILE, FFN_FTILE, FFN_HALO
    tile = lambda n: pl.BlockSpec((tm, n), lambda i: (i, 0))
    halo = lambda n: pl.BlockSpec((H, n), lambda i: (jnp.maximum(i * (tm // H) - 1, 0), 0))
    resident = lambda a: pl.BlockSpec((None,) + a.shape[1:], lambda i: (layer, 0, 0),
                                      pipeline_mode=pl.Buffered(1))
    gain = pl.BlockSpec((1, D), lambda i: (0, 0), pipeline_mode=pl.Buffered(1))
    return pl.pallas_call(
        functools.partial(_ffn_kernel, tiles_per_seq=seq_len // tm),
        grid=(T // tm,),
        in_specs=[tile(D_RET), halo(D_RET), tile(D_NSA), halo(D_NSA), tile(D), halo(D),
                  resident(w_out), gain, gain, resident(w_up), resident(conv_w), resident(w_down), gain],
        out_specs=tile(D),
        out_shape=jax.ShapeDtypeStruct((T, D), F32),
        scratch_shapes=[pltpu.VMEM((tm + H, D), BF16), pltpu.VMEM((tm + H, tf), F32),
                        pltpu.VMEM((tm + H, tf), F32), pltpu.VMEM((tm, D), F32),
                        pltpu.VMEM((tm, D), F32)],
        compiler_params=pltpu.CompilerParams(dimension_semantics=("arbitrary",),
                                             vmem_limit_bytes=V7X_VMEM_LIMIT),
        name="ffn",
    )(y_ret, y_ret, y_nsa, y_nsa, x2, x2, w_out, g_mix, g_pre, w_up, conv_w, w_down, g_post)


_NSA_HEAD_ORDER = [g * NSA_GROUP + i for i in range(NSA_GROUP) for g in range(NSA_KV_HEADS)]


def _rope_tables(pos):
    inv = 1.0 / (ROPE_THETA ** (jnp.arange(0, NSA_DK, 2, dtype=F32) / NSA_DK))
    ang = pos.astype(F32)[:, None] * inv[None, :]
    c, s = jnp.cos(ang), jnp.sin(ang)
    return jnp.concatenate([c, c, c, c], axis=1), jnp.concatenate([-s, s, -s, s], axis=1)


def _prep_w_in(w):
    lead = w.shape[:-1]
    splits = np.cumsum([RET_QK, RET_QK, D_RET, D_RET, D_NSA] + [NSA_KV] * 6)
    rq, rk, rv, rg, nq, kcm, vcm, ksl, vsl, kwn, vwn, ng = jnp.split(w, [int(s) for s in splits], axis=-1)
    nq = nq.reshape(lead + (NSA_HEADS, NSA_DK))[..., np.array(_NSA_HEAD_ORDER), :].reshape(lead + (D_NSA,))
    ng = jnp.pad(ng, [(0, 0)] * len(lead) + [(0, LANES - NSA_GATES)])
    return jnp.concatenate([rq, rk, nq, ksl, kwn, rv, vsl, vwn, rg, kcm, vcm, ng], axis=-1).astype(BF16)


def _prep_w_out(w):
    layers, _, d = w.shape
    w_nsa = w[:, D_RET:].reshape(layers, NSA_HEADS, NSA_DK, d)[:, np.array(_NSA_HEAD_ORDER)]
    return jnp.concatenate([w[:, :D_RET], w_nsa.reshape(layers, D_NSA, d)], axis=1).astype(BF16)


def _prep_compress(pos, w1, w2):
    lead = w1.shape[:-2]
    half = CMP_STRIDE

    def block_diag(w, axis):
        z = jnp.zeros_like(w)
        return jnp.stack([jnp.concatenate([w, z], axis=-1), jnp.concatenate([z, w], axis=-1)], axis=axis)

    def first_layer(w_half):
        w4 = w_half.reshape(lead + (half, NSA_DK, CMP_HID))
        return block_diag(w4, -3).reshape(lead + (half * NSA_KV_HEADS * NSA_DK, -1)).astype(BF16)

    def pos_row(p_half):
        rows = jnp.broadcast_to(p_half[..., :, None, :], lead + (half, NSA_KV_HEADS, NSA_DK))
        return rows.reshape(lead + (1, -1))

    w2x = block_diag(w2, -3).reshape(lead + (NSA_KV_HEADS * CMP_HID, -1)).astype(BF16)
    n1 = half * NSA_DK
    return (pos_row(pos[..., :half, :]), pos_row(pos[..., half:, :]),
            first_layer(w1[..., :n1, :]), first_layer(w1[..., n1:, :]), w2x)


def kernel(x, norm_mix_pre, w_in, ret_gn_w, cmp_k_pos, cmp_k_w1, cmp_k_w2, cmp_v_pos, cmp_v_w1, cmp_v_w2,
           w_out, norm_mix_post, norm_ffn_pre, ffn_w_up, ffn_conv, ffn_w_down, norm_ffn_post):
    B, S, D = x.shape
    depth = w_in.shape[0]
    assert S % SLC_KTILE == 0 and S % FFN_TILE == 0 and S // SLC_BLOCK <= LANES and S >= WINDOW + NSA_QBLOCK
    assert ffn_w_down.shape[1] % FFN_FTILE == 0
    assert min(SLC_TOPK, S // SLC_BLOCK) > 1 + SLC_LOCAL and S % (NSA_QBLOCK * NSA_TILES_PER_STEP) == 0

    cos, sin = _rope_tables(jnp.arange(S, dtype=jnp.int32))
    nc = S // CMP_STRIDE
    ccos, csin = _rope_tables(jnp.arange(nc, dtype=jnp.int32) * CMP_STRIDE + (CMP_BLOCK - 1))
    cmp_cos = jnp.stack([ccos, jnp.ones_like(ccos)])
    cmp_sin = jnp.stack([csin, jnp.zeros_like(csin)])

    w_in_p = _prep_w_in(w_in)
    w_out_p = _prep_w_out(w_out)
    w_up_p = ffn_w_up.astype(BF16)
    w_down_p = ffn_w_down.astype(BF16)
    cmp_p = _prep_compress(jnp.stack([cmp_k_pos, cmp_v_pos], axis=1), jnp.stack([cmp_k_w1, cmp_v_w1], axis=1),
                           jnp.stack([cmp_k_w2, cmp_v_w2], axis=1))

    x2 = x.reshape(B * S, D)
    for l in range(depth):
        outs = _inproj(x2, norm_mix_pre[l][None], w_in_p, l, cos, sin, S)
        rq, rk, nq, ksl, kwn, rv, vsl, vwn, rg, kvc, ng = outs
        y_ret = _retention(rq, rk, rv, rg, ret_gn_w[l][None], B, S)
        kvc_cmp, kvc_cmp_t = _compress(kvc, *cmp_p, l, cmp_cos, cmp_sin, B, S)
        y_nsa = _nsa(nq, ng, kvc_cmp, kvc_cmp_t, ksl, vsl, kwn, vwn, B, S)
        x2 = _outproj_ffn(y_ret, y_nsa, x2, w_out_p, norm_mix_post[l][None], norm_ffn_pre[l][None],
                          w_up_p, ffn_conv, w_down_p, l, norm_ffn_post[l][None], S)
    return x2.reshape(B, S, D)
```

```python
import functools
import math

import jax
import jax.numpy as jnp
import numpy as np
from jax import lax
from jax.experimental import pallas as pl
from jax.experimental.pallas import tpu as pltpu

F32 = jnp.float32
BF16 = jnp.bfloat16

LANES = 128
SUBLANES = 8
V7X_VMEM_LIMIT = 56 * 1024 * 1024

ROPE_THETA = 10000.0
NORM_EPS = 1e-6
GN_EPS = 1e-5
NEG = -1e30
SEL_BIG = 1e9

RET_HEADS = 4
RET_DK = 64
RET_DV = 128
RET_CHUNK = 128
NSA_HEADS = 8
NSA_KV_HEADS = 2
NSA_DK = 64
NSA_GROUP = NSA_HEADS // NSA_KV_HEADS
CMP_BLOCK = 32
CMP_STRIDE = 16
CMP_HID = 256
SLC_BLOCK = 64
SLC_TOPK = 16
SLC_LOCAL = 2
WINDOW = 512
NSA_QBLOCK = 128
NSA_TILES_PER_STEP = 4
NSA_GATES = 3 * NSA_HEADS
CONV_WIDTH = 3

NSA_Q_SCALE = NSA_DK ** -0.5 * math.log2(math.e)

D_RET = RET_HEADS * RET_DV
D_NSA = NSA_HEADS * NSA_DK
RET_QK = RET_HEADS * RET_DK
NSA_KV = NSA_KV_HEADS * NSA_DK

COL_RQ = 0
COL_RK = COL_RQ + RET_QK
COL_NQ = COL_RK + RET_QK
COL_KSL = COL_NQ + D_NSA
COL_KWN = COL_KSL + NSA_KV
ROPE_COLS = COL_KWN + NSA_KV
COL_RV = ROPE_COLS
COL_VSL = COL_RV + D_RET
COL_VWN = COL_VSL + NSA_KV
COL_RG = COL_VWN + NSA_KV
COL_KCM = COL_RG + D_RET
COL_VCM = COL_KCM + NSA_KV
COL_NG = COL_VCM + NSA_KV
IN_COLS_PAD = COL_NG + LANES

TOK_TILE = 512
RET_TILE = 1024
FFN_TILE = 1024
FFN_FTILE = 256
FFN_HALO = 16
FFN_ROW_CHUNKS = 4
SLC_KTILE = 512
SLC_UNROLL = 4


GELU_C = math.sqrt(2.0 / math.pi)


def _gelu_tanh(x):
    return 0.5 * x * (1.0 + jnp.tanh(GELU_C * (x + 0.044715 * (x * x * x))))


def _rope(p, cos, sin_signed, first_half):
    half = NSA_DK // 2
    partner = jnp.where(first_half, pltpu.roll(p, LANES - half, 1), pltpu.roll(p, half, 1))
    return p * cos + partner * sin_signed


def _inproj_kernel(x_ref, g_ref, w_ref, cos_ref, sin_ref,
                   rq_ref, rk_ref, nq_ref, ksl_ref, kwn_ref, rv_ref, vsl_ref, vwn_ref,
                   rg_ref, kvc_ref, ng_ref, stage_ref, *, seq_len):
    tm = x_ref.shape[0]
    x = x_ref[...]
    h = (x * lax.rsqrt(jnp.mean(x * x, axis=-1, keepdims=True) + NORM_EPS) * g_ref[...]).astype(BF16)
    cos = cos_ref[...]
    sin = sin_ref[...]
    lane = lax.broadcasted_iota(jnp.int32, (tm, LANES), 1)
    first_half = (lane % NSA_DK) < NSA_DK // 2

    def proj(c0, n):
        return jnp.dot(h, w_ref[:, c0:c0 + n], preferred_element_type=F32)

    def rope_slab(p, i):
        return _rope(p[:, i * LANES:(i + 1) * LANES], cos, sin, first_half)

    def store_rq(p):
        for i in range(RET_QK // LANES):
            rq_ref[:, i * LANES:(i + 1) * LANES] = rope_slab(p, i).astype(BF16)

    def store_rk(p):
        for i in range(RET_QK // LANES):
            rk_ref[:, i * LANES:(i + 1) * LANES] = (rope_slab(p, i) * (RET_DK ** -0.5)).astype(BF16)

    def store_nq(p):
        for i in range(D_NSA // LANES):
            nq_ref[:, i * LANES:(i + 1) * LANES] = (rope_slab(p, i) * NSA_Q_SCALE).astype(BF16)

    def store_keys(p):
        ksl_ref[:, 0:LANES] = rope_slab(p, 0).astype(BF16)
        kwn_ref[...] = rope_slab(p, 1).astype(BF16)
        row = lax.broadcasted_iota(jnp.int32, (tm, LANES), 0)
        pos = (pl.program_id(0) * tm + row) % seq_len
        ksl_ref[:, LANES:2 * LANES] = jnp.where(lane == pos // SLC_BLOCK, 1.0, 0.0).astype(BF16)

    def store_rv(p):
        rv_ref[...] = p.astype(BF16)

    def store_values(p):
        low_half = lane < NSA_DK
        for v_ref, v in ((vsl_ref, p[:, 0:LANES]), (vwn_ref, p[:, LANES:2 * LANES])):
            v_ref[0, 0:LANES, :] = jnp.where(low_half, v, 1.0).T.astype(BF16)
            v_ref[0, LANES:2 * LANES, :] = jnp.where(low_half, 1.0, v).T.astype(BF16)

    def store_rg(p):
        rg_ref[...] = p

    def store_compress_inputs(p):
        for s in range(2):
            stage_ref[...] = p[:, s * LANES:(s + 1) * LANES]
            for l in range(CMP_STRIDE):
                kvc_ref[s, :, l * LANES:(l + 1) * LANES] = stage_ref[
                    pl.ds(l, tm // CMP_STRIDE, stride=CMP_STRIDE), :]

    def store_gates(p):
        ng_ref[...] = p

    stages = [(COL_RQ, RET_QK, store_rq), (COL_RK, RET_QK, store_rk), (COL_NQ, D_NSA, store_nq),
              (COL_KSL, 2 * NSA_KV, store_keys), (COL_RV, D_RET, store_rv),
              (COL_VSL, 2 * NSA_KV, store_values), (COL_RG, D_RET, store_rg),
              (COL_KCM, 2 * NSA_KV, store_compress_inputs), (COL_NG, LANES, store_gates)]
    pending = None
    for c0, n, store in stages:
        p = proj(c0, n)
        if pending is not None:
            pending[1](pending[0])
        pending = (p, store)
    pending[1](pending[0])


def _inproj(x2, gain, w, layer, cos, sin, seq_len):
    T, D = x2.shape
    tm = TOK_TILE
    nt = seq_len // tm
    tok = lambda n: pl.BlockSpec((tm, n), lambda i: (i, 0))
    tok_t = pl.BlockSpec((1, 2 * LANES, tm), lambda i: (i // nt, 0, i % nt))
    out_shape = (
        jax.ShapeDtypeStruct((T, RET_QK), BF16),
        jax.ShapeDtypeStruct((T, RET_QK), BF16),
        jax.ShapeDtypeStruct((T, D_NSA), BF16),
        jax.ShapeDtypeStruct((T, 2 * LANES), BF16),
        jax.ShapeDtypeStruct((T, LANES), BF16),
        jax.ShapeDtypeStruct((T, D_RET), BF16),
        jax.ShapeDtypeStruct((T // seq_len, 2 * LANES, seq_len), BF16),
        jax.ShapeDtypeStruct((T // seq_len, 2 * LANES, seq_len), BF16),
        jax.ShapeDtypeStruct((T, D_RET), F32),
        jax.ShapeDtypeStruct((2, T // CMP_STRIDE, CMP_STRIDE * LANES), F32),
        jax.ShapeDtypeStruct((T, LANES), F32),
    )
    out_specs = (tok(RET_QK), tok(RET_QK), tok(D_NSA), tok(2 * LANES), tok(LANES), tok(D_RET),
                 tok_t, tok_t, tok(D_RET),
                 pl.BlockSpec((2, tm // CMP_STRIDE, CMP_STRIDE * LANES), lambda i: (0, i, 0)), tok(LANES))
    return pl.pallas_call(
        functools.partial(_inproj_kernel, seq_len=seq_len),
        grid=(T // tm,),
        in_specs=[tok(D),
                  pl.BlockSpec((1, D), lambda i: (0, 0)),
                  pl.BlockSpec((None, D, IN_COLS_PAD), lambda i: (layer, 0, 0)),
                  pl.BlockSpec((tm, LANES), lambda i: (i % nt, 0)),
                  pl.BlockSpec((tm, LANES), lambda i: (i % nt, 0))],
        out_specs=out_specs,
        out_shape=out_shape,
        scratch_shapes=[pltpu.VMEM((tm, LANES), F32)],
        compiler_params=pltpu.CompilerParams(dimension_semantics=("arbitrary",),
                                             vmem_limit_bytes=V7X_VMEM_LIMIT),
        name="inproj",
    )(x2, gain, w, cos, sin)


def _retention_kernel(q_ref, k_ref, v_ref, g_ref, gnw_ref, o_ref, state_ref, vbd_ref):
    C = RET_CHUNK
    n_chunks = q_ref.shape[0] // C

    @pl.when(pl.program_id(1) == 0)
    def _():
        state_ref[...] = jnp.zeros_like(state_ref)

    vbd_ref[...] = jnp.zeros_like(vbd_ref)

    ii = lax.broadcasted_iota(jnp.int32, (C, C), 0)
    jj = lax.broadcasted_iota(jnp.int32, (C, C), 1)
    diff = (ii - jj).astype(F32)
    i_col = lax.broadcasted_iota(jnp.int32, (C, 1), 0).astype(F32)
    low_half = lax.broadcasted_iota(jnp.int32, (C, LANES), 1) < RET_DK
    low_cols = lax.broadcasted_iota(jnp.int32, (C, 2 * RET_DV), 1) < RET_DV
    own_block = ((lax.broadcasted_iota(jnp.int32, (LANES, 2 * RET_DV), 0) < RET_DK)
                 == (lax.broadcasted_iota(jnp.int32, (LANES, 2 * RET_DV), 1) < RET_DV))
    nt_dims = (((1,), (1,)), ((), ()))
    tn_dims = (((0,), (0,)), ((), ()))

    for pair in range(RET_HEADS // 2):
        lg0, lg1 = (math.log(1.0 - 2.0 ** (-5.0 - h)) for h in (2 * pair, 2 * pair + 1))
        decay = [jnp.where(diff >= 0, jnp.exp(lg * jnp.maximum(diff, 0.0)), 0.0) for lg in (lg0, lg1)]
        xi = jnp.where(low_cols, jnp.exp(lg0 * (i_col + 1.0)), jnp.exp(lg1 * (i_col + 1.0)))
        zeta = jnp.where(low_half, jnp.exp(lg0 * (C - 1.0 - i_col)), jnp.exp(lg1 * (C - 1.0 - i_col)))
        chunk_decay = jnp.where(low_cols[0:1], math.exp(lg0 * C), math.exp(lg1 * C))
        qk_cols = slice(pair * LANES, (pair + 1) * LANES)
        v_cols = slice(2 * pair * RET_DV, 2 * (pair + 1) * RET_DV)
        for c in range(n_chunks):
            rows = slice(c * C, (c + 1) * C)
            q = q_ref[rows, qk_cols]
            ks = k_ref[rows, qk_cols]
            v = v_ref[rows, v_cols]
            qf = q.astype(F32)
            q_stack = jnp.concatenate([jnp.where(low_half, qf, 0.0), jnp.where(low_half, 0.0, qf)],
                                      axis=0).astype(BF16)
            s = lax.dot_general(q_stack, ks, nt_dims, preferred_element_type=F32)
            s_pair = jnp.concatenate([s[0:C] * decay[0], s[C:2 * C] * decay[1]], axis=1).astype(BF16)
            vbd_ref[pair, c, 0:C, 0:RET_DV] = v[:, 0:RET_DV]
            vbd_ref[pair, c, C:2 * C, RET_DV:2 * RET_DV] = v[:, RET_DV:2 * RET_DV]
            o = jnp.dot(s_pair, vbd_ref[pair, c], preferred_element_type=F32)
            state = state_ref[pair]
            o = o + jnp.dot(q, state.astype(BF16), preferred_element_type=F32) * xi
            kz = (ks.astype(F32) * zeta).astype(BF16)
            kv = lax.dot_general(kz, v, tn_dims, preferred_element_type=F32)
            state_ref[pair] = state * chunk_decay + jnp.where(own_block, kv, 0.0)
            for e in range(2):
                cols = slice((2 * pair + e) * RET_DV, (2 * pair + e + 1) * RET_DV)
                oh = o[:, e * RET_DV:(e + 1) * RET_DV]
                mu = jnp.mean(oh, axis=-1, keepdims=True)
                var = jnp.mean(jnp.square(oh - mu), axis=-1, keepdims=True)
                on = (oh - mu) * lax.rsqrt(var + GN_EPS) * gnw_ref[:, cols]
                gate = g_ref[rows, cols]
                o_ref[rows, cols] = (gate * (1.0 / (1.0 + jnp.exp(-gate))) * on).astype(BF16)


def _retention(rq, rk, rv, rg, gn_w, batch, seq_len):
    T = rq.shape[0]
    tc = RET_TILE
    nt = seq_len // tc
    tok = lambda n: pl.BlockSpec((tc, n), lambda b, i: (b * nt + i, 0))
    return pl.pallas_call(
        _retention_kernel,
        grid=(batch, nt),
        in_specs=[tok(RET_QK), tok(RET_QK), tok(D_RET), tok(D_RET),
                  pl.BlockSpec((1, D_RET), lambda b, i: (0, 0))],
        out_specs=tok(D_RET),
        out_shape=jax.ShapeDtypeStruct((T, D_RET), BF16),
        scratch_shapes=[pltpu.VMEM((RET_HEADS // 2, LANES, 2 * RET_DV), F32),
                        pltpu.VMEM((RET_HEADS // 2, tc // RET_CHUNK, 2 * RET_CHUNK, 2 * RET_DV), BF16)],
        compiler_params=pltpu.CompilerParams(dimension_semantics=("arbitrary", "arbitrary"),
                                             vmem_limit_bytes=V7X_VMEM_LIMIT),
        name="retention",
    )(rq, rk, rv, rg, gn_w)


def _compress_kernel(x_ref, pa_ref, pb_ref, wa_ref, wb_ref, w2_ref, cos_ref, sin_ref, o_ref, ot_ref):
    ng = x_ref.shape[2]
    x = x_ref[0, 0]
    xa = (x + pa_ref[0]).astype(BF16)
    xb = (x + pb_ref[0]).astype(BF16)
    a = jnp.dot(xa, wa_ref[0], preferred_element_type=F32)
    b = jnp.dot(xb, wb_ref[0], preferred_element_type=F32)
    hid = a + pltpu.roll(b, ng - 1, 0)
    out = jnp.dot(_gelu_tanh(hid).astype(BF16), w2_ref[0], preferred_element_type=F32)
    lane = lax.broadcasted_iota(jnp.int32, out.shape, 1)
    out = _rope(out, cos_ref[0], sin_ref[0], (lane % NSA_DK) < NSA_DK // 2)
    o_ref[0, 0] = out.astype(BF16)
    ot_ref[0, 0] = out.T.astype(BF16)


def _compress(kvc, pos_a, pos_b, wa, wb, w2, layer, cos, sin, batch, seq_len):
    ng = seq_len // CMP_STRIDE
    gw = CMP_STRIDE * LANES
    x = kvc.reshape(2, batch, ng, gw)
    hid = NSA_KV_HEADS * CMP_HID
    per_kv = lambda *shape: pl.BlockSpec((1,) + shape, lambda s, b: (s,) + (0,) * len(shape))
    per_lkv = lambda *shape: pl.BlockSpec((None, 1) + shape, lambda s, b: (layer, s) + (0,) * len(shape))
    return pl.pallas_call(
        _compress_kernel,
        grid=(2, batch),
        in_specs=[pl.BlockSpec((1, 1, ng, gw), lambda s, b: (s, b, 0, 0)),
                  per_lkv(1, gw), per_lkv(1, gw), per_lkv(gw, hid), per_lkv(gw, hid), per_lkv(hid, LANES),
                  per_kv(ng, LANES), per_kv(ng, LANES)],
        out_specs=(pl.BlockSpec((1, 1, ng, LANES), lambda s, b: (s, b, 0, 0)),
                   pl.BlockSpec((1, 1, LANES, ng), lambda s, b: (s, b, 0, 0))),
        out_shape=(jax.ShapeDtypeStruct((2, batch, ng, LANES), BF16),
                   jax.ShapeDtypeStruct((2, batch, LANES, ng), BF16)),
        compiler_params=pltpu.CompilerParams(dimension_semantics=("arbitrary", "arbitrary"),
                                             vmem_limit_bytes=V7X_VMEM_LIMIT),
        name="compress",
    )(x, pos_a, pos_b, wa, wb, w2, cos, sin)


def _nsa_kernel(nq_ref, ng_ref, *refs, seq_len, topk):
    n_in = 6
    o_ref = refs[n_in]

    def one_tile(j, carry):
        rows = pl.ds(pl.multiple_of(j * NSA_QBLOCK, NSA_QBLOCK), NSA_QBLOCK)
        t0 = (pl.program_id(1) * NSA_TILES_PER_STEP + j) * NSA_QBLOCK
        _nsa_tile(t0, nq_ref.at[rows], ng_ref.at[rows], *refs[:n_in], o_ref.at[rows],
                  refs[n_in + 1].at[j], *refs[n_in + 2:], seq_len=seq_len, topk=topk)
        return carry

    lax.fori_loop(0, NSA_TILES_PER_STEP, one_tile, 0)


def _nsa_tile(t0, nq_ref, ng_ref, kc_ref, vc_ref, ksl_ref, vsl_ref, kwn_ref, vwn_ref,
              o_ref, qaug_ref, s_ref, m_ref, acc_ref, out_ref, psum_ref, *, seq_len, topk):
    QB = NSA_QBLOCK
    R = NSA_GROUP
    G = NSA_KV_HEADS
    M = R * QB
    NC = kc_ref.shape[2]
    KT = SLC_KTILE
    WK = WINDOW + QB
    q = nq_ref[...].astype(F32)

    def per_head(x):
        return jnp.concatenate([x] * R, axis=1)

    def normalized(acc, g):
        num, den = (acc[0:NSA_DK], acc[NSA_DK:NSA_DK + 1]) if g == 0 else (acc[NSA_DK:], acc[0:1])
        return num * (1.0 / den)

    low_rows = lax.broadcasted_iota(jnp.int32, (LANES, QB), 0) < NSA_DK
    for i in range(R):
        q_t = q[:, i * LANES:(i + 1) * LANES].T
        for g in range(G):
            own = low_rows if g == 0 else jnp.logical_not(low_rows)
            qaug_ref[g, 0:LANES, i * QB:(i + 1) * QB] = jnp.where(own, q_t, 0.0).astype(BF16)

    ks = pl.multiple_of(jnp.clip(t0 - WINDOW, 0, seq_len - WK), QB)
    k_pos = ks + lax.broadcasted_iota(jnp.int32, (WK, QB), 0)
    t_pos = t0 + lax.broadcasted_iota(jnp.int32, (WK, QB), 1)
    win_mask = per_head((k_pos <= t_pos) & (k_pos > t_pos - WINDOW))

    def slc_scores(k0, slot):
        for g in range(G):
            s_ref[slot, g] = jnp.dot(ksl_ref[0, pl.ds(k0, KT), :], qaug_ref[g],
                                     preferred_element_type=F32)

    def compressed_select_window(nc):
        win_scores = [jnp.dot(kwn_ref[0, pl.ds(ks, WK), :], qaug_ref[g, 0:LANES, :],
                              preferred_element_type=F32) for g in range(G)]
        cmp_scores = [jnp.dot(kc_ref[0, 0, 0:nc, :], qaug_ref[g, 0:LANES, :],
                              preferred_element_type=F32) for g in range(G)]

        nb = min(nc * CMP_STRIDE // SLC_BLOCK, LANES)
        jj = lax.broadcasted_iota(jnp.int32, (nb, QB), 0)
        q_blk = (t0 + lax.broadcasted_iota(jnp.int32, (nb, QB), 1)) // SLC_BLOCK
        valid = jj <= q_blk
        forced = (jj == 0) | (valid & (jj > q_blk - SLC_LOCAL))
        jf = jj.astype(F32)

        def pick_round(score):
            best = jnp.max(score, axis=0, keepdims=True)
            first = jnp.min(jnp.where(score == best, jf, float(LANES)), axis=0, keepdims=True)
            return jnp.where(jf == first, -jnp.inf, score)

        scores = []
        n_idx = lax.broadcasted_iota(jnp.int32, (nc, QB), 0)
        t_col = t0 + lax.broadcasted_iota(jnp.int32, (nc, QB), 1)
        cmp_mask = per_head(n_idx * CMP_STRIDE + (CMP_BLOCK - 1) <= t_col)
        for g in range(G):
            sc = jnp.where(cmp_mask, cmp_scores[g], NEG)
            e = jnp.exp2(sc - jnp.max(sc, axis=0, keepdims=True))
            if nc == LANES:
                e = jnp.where(cmp_mask, e, 0.0)
            p_cmp = e * (1.0 / jnp.maximum(jnp.sum(e, axis=0, keepdims=True), 1e-30))
            acc = jnp.dot(vc_ref[0, 0, :, 0:nc], p_cmp.astype(BF16), preferred_element_type=F32)
            out_ref[0, g * NSA_DK:(g + 1) * NSA_DK, :] = acc[g * NSA_DK:(g + 1) * NSA_DK]
            p_sum = p_cmp[:, 0:QB]
            for i in range(1, R):
                p_sum = p_sum + p_cmp[:, i * QB:(i + 1) * QB]
            psum_ref[g, 0:SUBLANES, :] = jnp.zeros((SUBLANES, QB), F32)
            psum_ref[g, SUBLANES:SUBLANES + nc, :] = p_sum
            ratio, lead = SLC_BLOCK // CMP_STRIDE, CMP_BLOCK // CMP_STRIDE - 1
            imp = psum_ref[g, pl.ds(SUBLANES - lead, nb, stride=ratio), :]
            for d in range(1 - lead, ratio):
                imp = imp + psum_ref[g, pl.ds(SUBLANES + d, nb, stride=ratio), :]
            scores.append(jnp.where(forced, -jnp.inf, jnp.where(valid, imp, -SEL_BIG)))

        for _ in range(topk - (1 + SLC_LOCAL)):
            scores = [pick_round(score) for score in scores]
        for g in range(G):
            selected = ((scores[g] == -jnp.inf) | (q_blk < topk)) & valid
            bias = jnp.where(selected, 0.0, NEG)
            if nb < LANES:
                bias = jnp.concatenate([bias, jnp.full((LANES - nb, QB), NEG, F32)], axis=0)
            qaug_ref[g, LANES:2 * LANES, :] = per_head(bias.astype(BF16))
        slc_scores(0, 0)

        for g in range(G):
            sw = jnp.where(win_mask, win_scores[g], NEG)
            e_win = jnp.exp2(sw - jnp.max(sw, axis=0, keepdims=True)).astype(BF16)
            pv = jnp.dot(vwn_ref[0, g * LANES:(g + 1) * LANES, pl.ds(ks, WK)], e_win,
                         preferred_element_type=F32)
            out_ref[1, g * NSA_DK:(g + 1) * NSA_DK, :] = normalized(pv, g)

    n_visible = (t0 + QB - CMP_BLOCK) // CMP_STRIDE + 1
    n_variants = NC // LANES
    for v in range(n_variants):
        @pl.when(jnp.minimum((n_visible - 1) // LANES, n_variants - 1) == v)
        def _():
            compressed_select_window((v + 1) * LANES)

    def slc_accumulate(k0, slot, causal):
        for g in range(G):
            s = s_ref[slot, g]
            if causal:
                k_pos = k0 + lax.broadcasted_iota(jnp.int32, (KT, QB), 0)
                t_pos = t0 + lax.broadcasted_iota(jnp.int32, (KT, QB), 1)
                s = jnp.where(per_head(k_pos <= t_pos), s, NEG)
            m_run = m_ref[g]
            m_new = jnp.maximum(m_run, jnp.max(s, axis=0, keepdims=True))
            p = jnp.exp2(s - m_new).astype(BF16)
            pv = jnp.dot(vsl_ref[0, g * LANES:(g + 1) * LANES, pl.ds(k0, KT)], p,
                         preferred_element_type=F32)
            acc_ref[g] = jnp.exp2(m_run - m_new) * acc_ref[g] + pv
            m_ref[g] = m_new

    m_ref[...] = jnp.full(m_ref.shape, NEG, F32)
    acc_ref[...] = jnp.zeros_like(acc_ref)
    n_full = t0 // KT
    U = SLC_UNROLL

    def slc_run(k0, n_tiles, last_is_diagonal):
        for u in range(n_tiles):
            causal = last_is_diagonal and u == n_tiles - 1
            if not causal:
                slc_scores(k0 + (u + 1) * KT, (u + 1) % 2)
            slc_accumulate(k0 + u * KT, u % 2, causal)

    def slc_group(j, _):
        slc_run(pl.multiple_of(j * (U * KT), U * KT), U, False)
        return 0

    lax.fori_loop(0, n_full // U, slc_group, 0)
    k_rest = pl.multiple_of((n_full // U) * (U * KT), U * KT)
    for r in range(U):
        @pl.when(n_full % U == r)
        def _():
            slc_run(k_rest, r + 1, True)

    o_slc = [normalized(acc_ref[g], g) for g in range(G)]

    gates = (1.0 / (1.0 + jnp.exp(-ng_ref[...]))).T

    def gate_rows(branch, head):
        r = branch * NSA_HEADS + head
        return jnp.broadcast_to(gates[r:r + 1, :], (NSA_DK, QB))

    branches = [out_ref[0], jnp.concatenate(o_slc, axis=0), out_ref[1]]
    for i in range(R):
        y = jnp.zeros((LANES, QB), F32)
        for b, o in enumerate(branches):
            gate = jnp.concatenate([gate_rows(b, g * R + i) for g in range(G)], axis=0)
            y = y + gate * o[:, i * QB:(i + 1) * QB]
        o_ref[:, i * LANES:(i + 1) * LANES] = y.T.astype(BF16)


def _nsa(nq, ng, kvc_cmp, kvc_cmp_t, ksl, vsl, kwn, vwn, batch, seq_len):
    T = nq.shape[0]
    QB = NSA_QBLOCK
    rows = QB * NSA_TILES_PER_STEP
    nqb = seq_len // rows
    nc = seq_len // CMP_STRIDE
    topk = min(SLC_TOPK, seq_len // SLC_BLOCK)
    tok = lambda n: pl.BlockSpec((rows, n), lambda b, i: (b * nqb + i, 0))
    seq = lambda n: pl.BlockSpec((1, seq_len, n), lambda b, i: (b, 0, 0))
    seq_t = pl.BlockSpec((1, 2 * LANES, seq_len), lambda b, i: (b, 0, 0))
    return pl.pallas_call(
        functools.partial(_nsa_kernel, seq_len=seq_len, topk=topk),
        grid=(batch, nqb),
        in_specs=[tok(D_NSA), tok(LANES),
                  pl.BlockSpec((1, 1, nc, LANES), lambda b, i: (0, b, 0, 0)),
                  pl.BlockSpec((1, 1, LANES, nc), lambda b, i: (1, b, 0, 0)),
                  seq(2 * LANES), seq_t, seq(LANES), seq_t],
        out_specs=tok(D_NSA),
        out_shape=jax.ShapeDtypeStruct((T, D_NSA), BF16),
        scratch_shapes=[pltpu.VMEM((NSA_TILES_PER_STEP, NSA_KV_HEADS, 2 * LANES, NSA_GROUP * QB), BF16),
                        pltpu.VMEM((2, NSA_KV_HEADS, SLC_KTILE, NSA_GROUP * QB), F32),
                        pltpu.VMEM((NSA_KV_HEADS, 1, NSA_GROUP * QB), F32),
                        pltpu.VMEM((NSA_KV_HEADS, LANES, NSA_GROUP * QB), F32),
                        pltpu.VMEM((2, LANES, NSA_GROUP * QB), F32),
                        pltpu.VMEM((NSA_KV_HEADS, nc + SUBLANES, QB), F32)],
        compiler_params=pltpu.CompilerParams(dimension_semantics=("arbitrary", "arbitrary"),
                                             vmem_limit_bytes=V7X_VMEM_LIMIT),
        name="nsa",
    )(nq, ng, kvc_cmp, kvc_cmp_t, ksl.reshape(batch, seq_len, -1), vsl,
      kwn.reshape(batch, seq_len, -1), vwn)


def _ffn_kernel(yr_ref, yrp_ref, yn_ref, ynp_ref, x_ref, xp_ref, wo_ref, gmix_ref, gpre_ref,
                wu_ref, conv_ref, wd_ref, gpost_ref, o_ref,
                h_ref, ug_ref, uv_ref, acc_ref, xmid_ref, *, tiles_per_seq):
    tm = x_ref.shape[0]
    d_ff = wd_ref.shape[0]
    tf = FFN_FTILE
    H = FFN_HALO
    rc = tm // FFN_ROW_CHUNKS

    def rms(x, gain_ref):
        return x * lax.rsqrt(jnp.mean(x * x, axis=-1, keepdims=True) + NORM_EPS) * gain_ref[...]

    keep = jnp.where(pl.program_id(0) % tiles_per_seq == 0, 0.0, 1.0)
    for c in range(FFN_ROW_CHUNKS):
        rows = slice(c * rc, (c + 1) * rc)
        y_ret, y_nsa = yr_ref[rows, :], yn_ref[rows, :]
        if c == 0:
            y_ret = jnp.concatenate([yrp_ref[...], y_ret], axis=0)
            y_nsa = jnp.concatenate([ynp_ref[...], y_nsa], axis=0)
        mix = (jnp.dot(y_ret, wo_ref[0:D_RET, :], preferred_element_type=F32)
               + jnp.dot(y_nsa, wo_ref[D_RET:D_RET + D_NSA, :], preferred_element_type=F32))
        post = rms(mix, gmix_ref)
        if c == 0:
            h_ref[0:H, :] = (rms(xp_ref[...] + post[0:H], gpre_ref) * keep).astype(BF16)
            post = post[H:H + rc]
        x_mid = x_ref[rows, :] + post
        xmid_ref[rows, :] = x_mid
        h_ref[H + c * rc:H + (c + 1) * rc, :] = rms(x_mid, gpre_ref).astype(BF16)
    acc_ref[...] = jnp.zeros_like(acc_ref)

    n_tiles = d_ff // tf

    def columns(f):
        return (pl.ds(pl.multiple_of(f * tf, tf), tf), pl.ds(pl.multiple_of(d_ff + f * tf, tf), tf))

    def up_project(f, c):
        gate_cols, value_cols = columns(f)
        rows = slice(0 if c == 0 else H + c * rc, H + (c + 1) * rc)
        h = h_ref[rows, :]
        ug_ref[rows, :] = jnp.dot(h, wu_ref[:, gate_cols], preferred_element_type=F32)
        uv_ref[rows, :] = jnp.dot(h, wu_ref[:, value_cols], preferred_element_type=F32)

    def hidden_tile(f, carry):
        gate_cols, value_cols = columns(f)

        def causal_conv(u_ref, cols, r0, scale):
            out = (scale * conv_ref[CONV_WIDTH - 1:CONV_WIDTH, cols]) * u_ref[r0:r0 + rc, :]
            for k in range(CONV_WIDTH - 1):
                d = CONV_WIDTH - 1 - k
                out = out + (scale * conv_ref[k:k + 1, cols]) * u_ref[r0 - d:r0 - d + rc, :]
            return out

        def gate_and_down(c):
            g = causal_conv(ug_ref, gate_cols, H + c * rc, 1.0)
            v_half = causal_conv(uv_ref, value_cols, H + c * rc, 0.5)
            inner = g * (GELU_C + (GELU_C * 0.044715) * (g * g))
            act = (g + g * jnp.tanh(inner)) * v_half
            acc_ref[c * rc:(c + 1) * rc, :] += jnp.dot(act.astype(BF16), wd_ref[gate_cols, :],
                                                       preferred_element_type=F32)

        up_project(f, 0)
        for c in range(FFN_ROW_CHUNKS):
            if c + 1 < FFN_ROW_CHUNKS:
                up_project(f, c + 1)
            gate_and_down(c)
        return carry

    lax.fori_loop(0, n_tiles, hidden_tile, 0)
    o_ref[...] = xmid_ref[...] + rms(acc_ref[...], gpost_ref)


def _outproj_ffn(y_ret, y_nsa, x2, w_out, g_mix, g_pre, w_up, conv_w, w_down, layer, g_post, seq_len):
    T, D = x2.shape
    d_ff = w_down.shape[1]
    tm, tf, H = FFN_TILE, FFN_FTILE, FFN_HALO
    tile = lambda n: pl.BlockSpec((tm, n), lambda i: (i, 0))
    halo = lambda n: pl.BlockSpec((H, n), lambda i: (jnp.maximum(i * (tm // H) - 1, 0), 0))
    resident = lambda a: pl.BlockSpec((None,) + a.shape[1:], lambda i: (layer, 0, 0),
                                      pipeline_mode=pl.Buffered(1))
    gain = pl.BlockSpec((1, D), lambda i: (0, 0), pipeline_mode=pl.Buffered(1))
    return pl.pallas_call(
        functools.partial(_ffn_kernel, tiles_per_seq=seq_len // tm),
        grid=(T // tm,),
        in_specs=[tile(D_RET), halo(D_RET), tile(D_NSA), halo(D_NSA), tile(D), halo(D),
                  resident(w_out), gain, gain, resident(w_up), resident(conv_w), resident(w_down), gain],
        out_specs=tile(D),
        out_shape=jax.ShapeDtypeStruct((T, D), F32),
        scratch_shapes=[pltpu.VMEM((tm + H, D), BF16), pltpu.VMEM((tm + H, tf), F32),
                        pltpu.VMEM((tm + H, tf), F32), pltpu.VMEM((tm, D), F32),
                        pltpu.VMEM((tm, D), F32)],
        compiler_params=pltpu.CompilerParams(dimension_semantics=("arbitrary",),
                                             vmem_limit_bytes=V7X_VMEM_LIMIT),
        name="ffn",
    )(y_ret, y_ret, y_nsa, y_nsa, x2, x2, w_out, g_mix, g_pre, w_up, conv_w, w_down, g_post)


_NSA_HEAD_ORDER = [g * NSA_GROUP + i for i in range(NSA_GROUP) for g in range(NSA_KV_HEADS)]


def _rope_tables(pos):
    inv = 1.0 / (ROPE_THETA ** (jnp.arange(0, NSA_DK, 2, dtype=F32) / NSA_DK))
    ang = pos.astype(F32)[:, None] * inv[None, :]
    c, s = jnp.cos(ang), jnp.sin(ang)
    return jnp.concatenate([c, c, c, c], axis=1), jnp.concatenate([-s, s, -s, s], axis=1)


def _prep_w_in(w):
    lead = w.shape[:-1]
    splits = np.cumsum([RET_QK, RET_QK, D_RET, D_RET, D_NSA] + [NSA_KV] * 6)
    rq, rk, rv, rg, nq, kcm, vcm, ksl, vsl, kwn, vwn, ng = jnp.split(w, [int(s) for s in splits], axis=-1)
    nq = nq.reshape(lead + (NSA_HEADS, NSA_DK))[..., np.array(_NSA_HEAD_ORDER), :].reshape(lead + (D_NSA,))
    ng = jnp.pad(ng, [(0, 0)] * len(lead) + [(0, LANES - NSA_GATES)])
    return jnp.concatenate([rq, rk, nq, ksl, kwn, rv, vsl, vwn, rg, kcm, vcm, ng], axis=-1).astype(BF16)


def _prep_w_out(w):
    layers, _, d = w.shape
    w_nsa = w[:, D_RET:].reshape(layers, NSA_HEADS, NSA_DK, d)[:, np.array(_NSA_HEAD_ORDER)]
    return jnp.concatenate([w[:, :D_RET], w_nsa.reshape(layers, D_NSA, d)], axis=1).astype(BF16)


def _prep_compress(pos, w1, w2):
    lead = w1.shape[:-2]
    half = CMP_STRIDE

    def block_diag(w, axis):
        z = jnp.zeros_like(w)
        return jnp.stack([jnp.concatenate([w, z], axis=-1), jnp.concatenate([z, w], axis=-1)], axis=axis)

    def first_layer(w_half):
        w4 = w_half.reshape(lead + (half, NSA_DK, CMP_HID))
        return block_diag(w4, -3).reshape(lead + (half * NSA_KV_HEADS * NSA_DK, -1)).astype(BF16)

    def pos_row(p_half):
        rows = jnp.broadcast_to(p_half[..., :, None, :], lead + (half, NSA_KV_HEADS, NSA_DK))
        return rows.reshape(lead + (1, -1))

    w2x = block_diag(w2, -3).reshape(lead + (NSA_KV_HEADS * CMP_HID, -1)).astype(BF16)
    n1 = half * NSA_DK
    return (pos_row(pos[..., :half, :]), pos_row(pos[..., half:, :]),
            first_layer(w1[..., :n1, :]), first_layer(w1[..., n1:, :]), w2x)


def kernel(x, norm_mix_pre, w_in, ret_gn_w, cmp_k_pos, cmp_k_w1, cmp_k_w2, cmp_v_pos, cmp_v_w1, cmp_v_w2,
           w_out, norm_mix_post, norm_ffn_pre, ffn_w_up, ffn_conv, ffn_w_down, norm_ffn_post):
    B, S, D = x.shape
    depth = w_in.shape[0]
    assert S % SLC_KTILE == 0 and S % FFN_TILE == 0 and S // SLC_BLOCK <= LANES and S >= WINDOW + NSA_QBLOCK
    assert ffn_w_down.shape[1] % FFN_FTILE == 0
    assert min(SLC_TOPK, S // SLC_BLOCK) > 1 + SLC_LOCAL and S % (NSA_QBLOCK * NSA_TILES_PER_STEP) == 0

    cos, sin = _rope_tables(jnp.arange(S, dtype=jnp.int32))
    nc = S // CMP_STRIDE
    ccos, csin = _rope_tables(jnp.arange(nc, dtype=jnp.int32) * CMP_STRIDE + (CMP_BLOCK - 1))
    cmp_cos = jnp.stack([ccos, jnp.ones_like(ccos)])
    cmp_sin = jnp.stack([csin, jnp.zeros_like(csin)])

    w_in_p = _prep_w_in(w_in)
    w_out_p = _prep_w_out(w_out)
    w_up_p = ffn_w_up.astype(BF16)
    w_down_p = ffn_w_down.astype(BF16)
    cmp_p = _prep_compress(jnp.stack([cmp_k_pos, cmp_v_pos], axis=1), jnp.stack([cmp_k_w1, cmp_v_w1], axis=1),
                           jnp.stack([cmp_k_w2, cmp_v_w2], axis=1))

    x2 = x.reshape(B * S, D)
    for l in range(depth):
        outs = _inproj(x2, norm_mix_pre[l][None], w_in_p, l, cos, sin, S)
        rq, rk, nq, ksl, kwn, rv, vsl, vwn, rg, kvc, ng = outs
        y_ret = _retention(rq, rk, rv, rg, ret_gn_w[l][None], B, S)
        kvc_cmp, kvc_cmp_t = _compress(kvc, *cmp_p, l, cmp_cos, cmp_sin, B, S)
        y_nsa = _nsa(nq, ng, kvc_cmp, kvc_cmp_t, ksl, vsl, kwn, vwn, B, S)
        x2 = _outproj_ffn(y_ret, y_nsa, x2, w_out_p, norm_mix_post[l][None], norm_ffn_pre[l][None],
                          w_up_p, ffn_conv, w_down_p, l, norm_ffn_post[l][None], S)
    return x2.reshape(B, S, D)
```
